```python
import jax, jax.numpy as jnp
from jax import lax
import numpy as np

D_MODEL = 1024
BATCH = 8
SEQ = 2048
DEPTH = 1
DEC_BATCH = 128
DEC_SEQ = 4
PAST_LEN = 16384
PAGE_SIZE = 128

GLA_HEADS = 4
GLA_DK = D_MODEL // 8
GLA_DV = D_MODEL // 4
GLA_KW = GLA_HEADS * GLA_DK
GLA_VW = GLA_HEADS * GLA_DV
GLA_GATE_RANK = 16
GLA_GATE_TAU = 16.0
GLA_CHUNK = 64
GLA_NORM_EPS = 1e-5

RW_HEAD = 64
RW_HEADS = D_MODEL // RW_HEAD
RW_W = RW_HEADS * RW_HEAD
RW_DECAY_RANK = 64
RW_A_RANK = 64
RW_GATE_RANK = 128
RW_SHIFT_W = 3 * RW_W + RW_DECAY_RANK + RW_A_RANK + RW_GATE_RANK
RW_GN_EPS = 64e-5

O_GQ = 0
O_GK = O_GQ + GLA_KW
O_GV = O_GK + GLA_KW
O_GG = O_GV + GLA_VW
O_GA = O_GG + GLA_VW
O_RW = O_GA + GLA_GATE_RANK
O_MG = O_RW + RW_SHIFT_W
PROJ_W = O_MG + 2 * D_MODEL

R_R = 0
R_K = RW_W
R_V = 2 * RW_W
R_WL = 3 * RW_W
R_AL = R_WL + RW_DECAY_RANK
R_GL = R_AL + RW_A_RANK

PEER_HEADS = 8
PEER_NKEYS = 128
PEER_N = PEER_NKEYS * PEER_NKEYS
PEER_QHALF = 128
PEER_TOPK = 16
PEER_BLOCK = 128

NORM_EPS = 1e-6

kernel_name = 'hybrid_gla_rwkv7_peer_step'


def rmsnorm(x, g):
    xf = x.astype(jnp.float32)
    y = xf * lax.rsqrt(jnp.mean(xf * xf, axis=-1, keepdims=True) + NORM_EPS)
    return (y * g).astype(x.dtype)


def gla_chunked(q, k, v, la, S0):
    B, T, H, DK = q.shape
    DV = v.shape[-1]
    C = min(GLA_CHUNK, T)
    n = -(-T // C)
    pad = n * C - T

    def prep(a):
        a = jnp.pad(a, ((0, 0), (0, pad), (0, 0), (0, 0)))
        return a.reshape(B, n, C, H, a.shape[-1]).transpose(1, 0, 3, 2, 4)

    qc, kc, vc, lc = prep(q), prep(k), prep(v), prep(la)
    mask = jnp.tril(jnp.ones((C, C), dtype=bool))

    def step(S, inp):
        qi, ki, vi, li = inp
        b = jnp.cumsum(li, axis=2)
        bl = b[:, :, -1:, :]
        qe = qi * jnp.exp(b)
        A = jnp.einsum('bhcd,bhsd->bhcs', qe, ki * jnp.exp(-b))
        A = jnp.where(mask, A, 0.0)
        o = jnp.einsum('bhcd,bhde->bhce', qe, S) + jnp.einsum('bhcs,bhse->bhce', A, vi)
        S = jnp.exp(bl[:, :, 0, :])[..., None] * S + jnp.einsum('bhsd,bhse->bhde', ki * jnp.exp(bl - b), vi)
        return S, o

    S, o = lax.scan(step, S0, (qc, kc, vc, lc))
    o = o.transpose(1, 0, 3, 2, 4).reshape(B, n * C, H, DV)[:, :T]
    return o, S


def rwkv7_scan(r, w, k, v, a, b, S0):
    seq = tuple(jnp.moveaxis(z, 1, 0) for z in (r, w, k, v, a, b))

    def step(S, inp):
        r_, w_, k_, v_, a_, b_ = inp
        sa = jnp.einsum('bhvk,bhk->bhv', S, a_)
        S = S * w_[:, :, None, :] + sa[..., :, None] * b_[..., None, :] + v_[..., :, None] * k_[..., None, :]
        return S, jnp.einsum('bhvk,bhk->bhv', S, r_)

    S, o = lax.scan(step, S0, seq)
    return jnp.moveaxis(o, 0, 1), S


def mixer(h, S_gla0, S_rw0, shift0, lp):
    B, T, _ = h.shape
    f32 = jnp.float32
    p = jnp.einsum('btd,de->bte', h, lp['w_in'])

    q = p[..., O_GQ:O_GK].reshape(B, T, GLA_HEADS, GLA_DK) * (GLA_DK ** -0.5)
    kg = p[..., O_GK:O_GV].reshape(B, T, GLA_HEADS, GLA_DK)
    vg = p[..., O_GV:O_GG].reshape(B, T, GLA_HEADS, GLA_DV)
    gg = p[..., O_GG:O_GA]
    z = p[..., O_GA:O_RW] @ lp['w_gla_gate'] + lp['b_gla_gate']
    la = (jax.nn.log_sigmoid(z.astype(f32)) / GLA_GATE_TAU).reshape(B, T, GLA_HEADS, GLA_DK)
    og, S_gla = gla_chunked(q.astype(f32), kg.astype(f32), vg.astype(f32), la, S_gla0.astype(f32))
    og = og * lax.rsqrt(jnp.mean(og * og, axis=-1, keepdims=True) + GLA_NORM_EPS)
    og = og.reshape(B, T, GLA_VW).astype(h.dtype) * lp['g_gla_norm'] * jax.nn.silu(gg)
    br_gla = og @ lp['w_b_gla']

    pr = p[..., O_RW:O_MG]
    prev = jnp.concatenate([shift0[:, None, :].astype(pr.dtype), pr[:, :-1]], axis=1)
    pm = pr + (prev - pr) * lp['mu_shift']
    r = pm[..., R_R:R_K]
    kr = pm[..., R_K:R_V]
    vr = pm[..., R_V:R_WL]
    wl = pm[..., R_WL:R_AL]
    al = pm[..., R_AL:R_GL]
    gl = pm[..., R_GL:]
    wlog = -jax.nn.softplus(-(lp['rw_w0'] + jnp.tanh(wl) @ lp['rw_w2']).astype(f32)) - 0.5
    decay = jnp.exp(-jnp.exp(wlog))
    a = jax.nn.sigmoid(lp['rw_a0'] + al @ lp['rw_a2'])
    gate = jax.nn.sigmoid(gl) @ lp['rw_g2']

    def hs(t):
        return t.reshape(B, T, RW_HEADS, RW_HEAD).astype(f32)

    kk = hs(kr * lp['rw_k_k'])
    kk = kk / jnp.maximum(jnp.sqrt(jnp.sum(kk * kk, axis=-1, keepdims=True)), 1e-12)
    kmod = hs(kr * (1.0 + (a - 1.0) * lp['rw_k_a']))
    rh, vh = hs(r), hs(vr)
    o_r, S_rw = rwkv7_scan(rh, hs(decay), kmod, vh, -kk, kk * hs(a), S_rw0.astype(f32))
    mu = jnp.mean(o_r, axis=-1, keepdims=True)
    var = jnp.mean((o_r - mu) ** 2, axis=-1, keepdims=True)
    o_r = ((o_r - mu) * lax.rsqrt(var + RW_GN_EPS)).reshape(B, T, RW_W) * lp['rw_gn_w'] + lp['rw_gn_b']
    bonus = jnp.sum(rh * kmod * lp['rw_r_k'], axis=-1, keepdims=True) * vh
    o_r = (o_r + bonus.reshape(B, T, RW_W)) * gate
    br_rw = o_r.astype(h.dtype) @ lp['w_b_rw']

    gates = jax.nn.sigmoid(p[..., O_MG:])
    merged = gates[..., :D_MODEL] * br_gla + gates[..., D_MODEL:] * br_rw
    y = merged @ lp['w_out']
    return y, S_gla.astype(S_gla0.dtype), S_rw.astype(S_rw0.dtype), pr[:, -1]


def peer(h, lp):
    B, T, D = h.shape
    M = B * T
    xt = h.reshape(M, D)
    q = (xt @ lp['peer_wq']).reshape(M, PEER_HEADS, 2, PEER_QHALF).astype(jnp.float32)
    s = jnp.einsum('mhpd,hpnd->mhpn', q, lp['peer_keys'].astype(jnp.float32))
    s1, i1 = lax.top_k(s[:, :, 0], PEER_TOPK)
    s2, i2 = lax.top_k(s[:, :, 1], PEER_TOPK)
    cand = (s1[..., :, None] + s2[..., None, :]).reshape(M, PEER_HEADS, PEER_TOPK * PEER_TOPK)
    cidx = (i1[..., :, None] * PEER_NKEYS + i2[..., None, :]).reshape(M, PEER_HEADS, PEER_TOPK * PEER_TOPK)
    top, j = lax.top_k(cand, PEER_TOPK)
    idx = jnp.take_along_axis(cidx, j, axis=-1).reshape(M, PEER_HEADS * PEER_TOPK)
    wts = jax.nn.softmax(top, axis=-1).reshape(M, PEER_HEADS * PEER_TOPK)
    nb = -(-M // PEER_BLOCK)
    pad = nb * PEER_BLOCK - M
    xb = jnp.pad(xt, ((0, pad), (0, 0))).reshape(nb, PEER_BLOCK, D)
    ib = jnp.pad(idx, ((0, pad), (0, 0))).reshape(nb, PEER_BLOCK, -1)
    wb = jnp.pad(wts, ((0, pad), (0, 0))).reshape(nb, PEER_BLOCK, -1)
    pu, pv = lp['peer_u'], lp['peer_v']

    def blk(args):
        xx, ii, ww = args
        u = jnp.take(pu, ii, axis=0)
        act = jax.nn.gelu(jnp.einsum('md,mkd->mk', xx, u), approximate=False)
        vv = jnp.take(pv, ii, axis=0)
        return jnp.einsum('mk,mkd->md', (act * ww).astype(xx.dtype), vv)

    out = lax.map(blk, (xb, ib, wb)).reshape(nb * PEER_BLOCK, D)[:M]
    return out.reshape(B, T, D)


def block(x, c, S_gla0, S_rw0, shift0, lp):
    mod = jax.nn.silu(c) @ lp['w_ada'] + lp['b_ada']
    shift1, scale1, gate1, shift2, scale2, gate2 = jnp.split(mod[:, None, :], 6, axis=-1)
    h = rmsnorm(x, lp['g_pre_mix']) * (1.0 + scale1) + shift1
    y, S_gla, S_rw, shift_new = mixer(h, S_gla0, S_rw0, shift0, lp)
    x = x + gate1 * rmsnorm(y, lp['g_post_mix'])
    h2 = rmsnorm(x, lp['g_pre_ffn']) * (1.0 + scale2) + shift2
    y2 = peer(h2, lp)
    x = x + gate2 * rmsnorm(y2, lp['g_post_ffn'])
    return x, S_gla, S_rw, shift_new


def setup_inputs(seed: int = 0) -> dict:
    key = jax.random.key(seed)
    ks = iter(jax.random.split(key, 48))
    f32 = jnp.float32

    def nrm(shape, scale):
        return jax.random.normal(next(ks), shape, f32) * scale

    def gain(shape):
        return 1.0 + 0.05 * jax.random.normal(next(ks), shape, f32)

    L, D = DEPTH, D_MODEL
    return {
        'x_prompt': nrm((BATCH, SEQ, D), 1.0),
        'x_sample': nrm((DEC_BATCH, DEC_SEQ, D), 1.0),
        'c_prompt': nrm((BATCH, D), 1.0),
        'c_sample': nrm((DEC_BATCH, D), 1.0),
        'state_gla': nrm((L, DEC_BATCH, GLA_HEADS, GLA_DK, GLA_DV), 0.5),
        'state_rwkv': nrm((L, DEC_BATCH, RW_HEADS, RW_HEAD, RW_HEAD), 0.5),
        'state_shift': nrm((L, DEC_BATCH, RW_SHIFT_W), 1.0),
        'w_ada': nrm((L, D, 6 * D), 0.5 * D ** -0.5),
        'b_ada': nrm((L, 6 * D), 0.01),
        'g_pre_mix': gain((L, D)),
        'g_post_mix': gain((L, D)),
        'g_pre_ffn': gain((L, D)),
        'g_post_ffn': gain((L, D)),
        'w_in': nrm((L, D, PROJ_W), D ** -0.5),
        'w_gla_gate': nrm((L, GLA_GATE_RANK, GLA_KW), GLA_GATE_RANK ** -0.5),
        'b_gla_gate': nrm((L, GLA_KW), 0.1),
        'g_gla_norm': gain((L, GLA_VW)),
        'mu_shift': jax.random.uniform(next(ks), (L, RW_SHIFT_W), f32),
        'rw_w0': jax.random.uniform(next(ks), (L, RW_W), f32, -6.0, -1.0),
        'rw_w2': nrm((L, RW_DECAY_RANK, RW_W), 0.1 * RW_DECAY_RANK ** -0.5),
        'rw_a0': nrm((L, RW_W), 0.1),
        'rw_a2': nrm((L, RW_A_RANK, RW_W), 0.5 * RW_A_RANK ** -0.5),
        'rw_g2': nrm((L, RW_GATE_RANK, RW_W), RW_GATE_RANK ** -0.5),
        'rw_k_k': 0.85 + nrm((L, RW_W), 0.05),
        'rw_k_a': 1.0 + nrm((L, RW_W), 0.05),
        'rw_r_k': nrm((L, RW_HEADS, RW_HEAD), 0.1),
        'rw_gn_w': gain((L, RW_W)),
        'rw_gn_b': nrm((L, RW_W), 0.01),
        'w_b_gla': nrm((L, GLA_VW, D), GLA_VW ** -0.5),
        'w_b_rw': nrm((L, RW_W, D), RW_W ** -0.5),
        'w_out': nrm((L, D, D), D ** -0.5),
        'peer_wq': nrm((L, D, PEER_HEADS * 2 * PEER_QHALF), D ** -0.5),
        'peer_keys': nrm((L, PEER_HEADS, 2, PEER_NKEYS, PEER_QHALF), PEER_QHALF ** -0.5),
        'peer_u': nrm((L, PEER_N, D), D ** -0.5),
        'peer_v': nrm((L, PEER_N, D), D ** -0.5),
    }


def reference(x_prompt, x_sample, c_prompt, c_sample, state_gla, state_rwkv, state_shift,
              w_ada, b_ada, g_pre_mix, g_post_mix, g_pre_ffn, g_post_ffn, w_in, w_gla_gate, b_gla_gate,
              g_gla_norm, mu_shift, rw_w0, rw_w2, rw_a0, rw_a2, rw_g2, rw_k_k, rw_k_a, rw_r_k, rw_gn_w,
              rw_gn_b, w_b_gla, w_b_rw, w_out, peer_wq, peer_keys, peer_u, peer_v):
    xp, xs = x_prompt, x_sample
    gla_p, rw_p, sh_p, gla_s, rw_s, sh_s = [], [], [], [], [], []
    for l in range(DEPTH):
        lp = dict(w_ada=w_ada[l], b_ada=b_ada[l], g_pre_mix=g_pre_mix[l], g_post_mix=g_post_mix[l],
                  g_pre_ffn=g_pre_ffn[l], g_post_ffn=g_post_ffn[l], w_in=w_in[l], w_gla_gate=w_gla_gate[l],
                  b_gla_gate=b_gla_gate[l], g_gla_norm=g_gla_norm[l], mu_shift=mu_shift[l], rw_w0=rw_w0[l],
                  rw_w2=rw_w2[l], rw_a0=rw_a0[l], rw_a2=rw_a2[l], rw_g2=rw_g2[l], rw_k_k=rw_k_k[l],
                  rw_k_a=rw_k_a[l], rw_r_k=rw_r_k[l], rw_gn_w=rw_gn_w[l], rw_gn_b=rw_gn_b[l],
                  w_b_gla=w_b_gla[l], w_b_rw=w_b_rw[l], w_out=w_out[l], peer_wq=peer_wq[l],
                  peer_keys=peer_keys[l], peer_u=peer_u[l], peer_v=peer_v[l])
        Bp = xp.shape[0]
        zg = jnp.zeros((Bp, GLA_HEADS, GLA_DK, GLA_DV), xp.dtype)
        zr = jnp.zeros((Bp, RW_HEADS, RW_HEAD, RW_HEAD), xp.dtype)
        zs = jnp.zeros((Bp, RW_SHIFT_W), xp.dtype)
        xp, g1, r1, s1 = block(xp, c_prompt, zg, zr, zs, lp)
        xs, g2, r2, s2 = block(xs, c_sample, state_gla[l], state_rwkv[l], state_shift[l], lp)
        gla_p.append(g1)
        rw_p.append(r1)
        sh_p.append(s1)
        gla_s.append(g2)
        rw_s.append(r2)
        sh_s.append(s2)
    return (xp, xs, jnp.stack(gla_p), jnp.stack(rw_p), jnp.stack(sh_p), jnp.stack(gla_s), jnp.stack(rw_s), jnp.stack(sh_s))
```

```python
import functools

import jax
import jax.numpy as jnp
from jax import lax
from jax.experimental import pallas as pl
from jax.experimental.pallas import tpu as pltpu

F32 = jnp.float32
BF16 = jnp.bfloat16

D_MODEL = 1024
GLA_HEADS = 4
GLA_DK = 128
GLA_DV = 256
GLA_GATE_RANK = 16
GLA_GATE_TAU = 16.0
GLA_NORM_EPS = 1e-5
RW_HEAD = 64
RW_HEADS = 16
RW_GN_EPS = 64e-5
RW_SHIFT_W = 3328
PEER_HEADS = 8
PEER_NKEYS = 128
PEER_TOPK = 16
NORM_EPS = 1e-6

COL_R, COL_K, COL_V = 0, 1024, 2048
COL_MG = 3072
COL_GG = 5120
COL_GV = 6144
COL_GQ = 7168
COL_GK = 7680
COL_LORA = 8192
COL_GA = 8448
PROJ_PAD_W = 8704
PROJ_TILE_N = 2176

CHUNK = 64
TOKEN_TILE = 512
MERGE_TILE = 256
PEER_EXPERT_TILE = 1024
VMEM_LIMIT = 56 * 1024 * 1024

NEG_INF = float("-inf")


def _mm(a, b):
    return jnp.dot(a.astype(BF16), b.astype(BF16), preferred_element_type=F32)


def _mm_nt(a, b):
    return lax.dot_general(a.astype(BF16), b.astype(BF16), (((1,), (1,)), ((), ())),
                           preferred_element_type=F32)


def _mm_tn(a, b):
    return lax.dot_general(a.astype(BF16), b.astype(BF16), (((0,), (0,)), ((), ())),
                           preferred_element_type=F32)


def _split(x):
    hi = x.astype(BF16)
    lo = (x - hi.astype(F32)).astype(BF16)
    return hi, lo


def _mm_exact_lhs(a, x):
    hi, lo = _split(x)
    a = a.astype(BF16)
    return (jnp.dot(a, hi, preferred_element_type=F32)
            + jnp.dot(a, lo, preferred_element_type=F32))


def _mm_exact_rhs(x, a):
    hi, lo = _split(x)
    a = a.astype(BF16)
    return (jnp.dot(hi, a, preferred_element_type=F32)
            + jnp.dot(lo, a, preferred_element_type=F32))


def _softplus(x):
    return jnp.maximum(x, 0.0) + jnp.log1p(jnp.exp(-jnp.abs(x)))


def _rms(x, g):
    return x * lax.rsqrt(jnp.mean(x * x, axis=-1, keepdims=True) + NORM_EPS) * g


def _params(sem):
    return pltpu.CompilerParams(dimension_semantics=sem, vmem_limit_bytes=VMEM_LIMIT)


def _mod_kernel(c_ref, w_ref, b_ref, o_ref):
    c = c_ref[...]
    o_ref[...] = _mm(c * jax.nn.sigmoid(c), w_ref[...]) + b_ref[...]


def _modulation(c, w_ada, b_ada):
    rows = c.shape[0]
    n = w_ada.shape[1]
    tn = 1536
    return pl.pallas_call(
        _mod_kernel,
        grid=(n // tn,),
        in_specs=[pl.BlockSpec((rows, D_MODEL), lambda j: (0, 0)),
                  pl.BlockSpec((D_MODEL, tn), lambda j: (0, j)),
                  pl.BlockSpec((1, tn), lambda j: (0, j))],
        out_specs=pl.BlockSpec((rows, tn), lambda j: (0, j)),
        out_shape=jax.ShapeDtypeStruct((rows, n), F32),
        compiler_params=_params(("arbitrary",)),
        name="adaln_mod",
    )(c, w_ada, b_ada.reshape(1, n))


class _Group:
    def __init__(self, batch, seq, valid):
        self.B, self.T, self.Tv = batch, seq, valid
        self.M = batch * seq
        self.tm = min(TOKEN_TILE, self.M)
        self.per_batch_mod = seq % self.tm == 0

    def mod_operand(self, mod):
        if self.per_batch_mod:
            return mod.reshape(self.B, 1, 6 * D_MODEL)
        return jnp.repeat(mod, self.T, axis=0)

    def mod_spec(self, col, tm=None):
        tm, T = tm or self.tm, self.T
        if self.per_batch_mod:
            return pl.BlockSpec((None, 1, D_MODEL), lambda i, *_: ((i * tm) // T, 0, col))
        return pl.BlockSpec((tm, D_MODEL), lambda i, *_: (i, col))


def _proj_kernel(x_ref, shift_ref, scale_ref, g_ref, w_ref, p_ref, h_scr):
    @pl.when(pl.program_id(1) == 0)
    def _():
        h = _rms(x_ref[...], g_ref[...]) * (1.0 + scale_ref[...]) + shift_ref[...]
        h_scr[...] = h.astype(BF16)

    p_ref[...] = jnp.dot(h_scr[...], w_ref[...], preferred_element_type=F32)


def _projection(grp, x2d, modop, g_pre, w_proj):
    tm = grp.tm
    return pl.pallas_call(
        _proj_kernel,
        grid=(grp.M // tm, PROJ_PAD_W // PROJ_TILE_N),
        in_specs=[pl.BlockSpec((tm, D_MODEL), lambda i, j: (i, 0)),
                  grp.mod_spec(0), grp.mod_spec(1),
                  pl.BlockSpec((1, D_MODEL), lambda i, j: (0, 0)),
                  pl.BlockSpec((D_MODEL, PROJ_TILE_N), lambda i, j: (0, j))],
        out_specs=pl.BlockSpec((tm, PROJ_TILE_N), lambda i, j: (i, j)),
        out_shape=jax.ShapeDtypeStruct((grp.M, PROJ_PAD_W), F32),
        scratch_shapes=[pltpu.VMEM((tm, D_MODEL), BF16)],
        compiler_params=_params(("arbitrary", "arbitrary")),
        name="in_proj",
    )(x2d, modop, modop, g_pre, w_proj)


def _gla_kernel(q_ref, k_ref, v_ref, ga_ref, wg_ref, bg_ref, s0_ref, og_ref, sout_ref, st_scr,
                *, chunk, n_chunks, tile, valid):
    t = pl.program_id(2)

    @pl.when(t == 0)
    def _():
        st_scr[...] = s0_ref[...].T

    row = lax.broadcasted_iota(jnp.int32, (chunk, chunk), 0)
    col = lax.broadcasted_iota(jnp.int32, (chunk, chunk), 1)
    causal = row >= col
    tril = jnp.where(causal, 1.0, 0.0).astype(BF16)
    for c in range(n_chunks):
        sl = slice(c * chunk, (c + 1) * chunk)
        q = q_ref[sl, :] * (GLA_DK ** -0.5)
        k = k_ref[sl, :]
        v = v_ref[sl, :]
        z = _mm(ga_ref[sl, :], wg_ref[...]) + bg_ref[...]
        la = -_softplus(-z) / GLA_GATE_TAU
        if valid is not None:
            pos = t * tile + c * chunk + lax.broadcasted_iota(jnp.int32, (chunk, 1), 0)
            ok = pos < valid
            la = jnp.where(ok, la, 0.0)
            k = jnp.where(ok, k, 0.0)
            v = jnp.where(ok, v, 0.0)
        b = _mm_exact_lhs(tril, la)
        bl = b[chunk - 1:chunk, :]
        qe = q * jnp.exp(b)
        ke = k * jnp.exp(-b)
        a = jnp.where(causal, _mm_nt(qe, ke), 0.0)
        st = st_scr[...]
        o = _mm_nt(qe, st) + _mm(a, v)
        kd = k * jnp.exp(bl - b)
        st_scr[...] = st * jnp.exp(bl) + _mm_tn(v, kd)
        og_ref[sl, :] = o * lax.rsqrt(jnp.mean(o * o, axis=-1, keepdims=True) + GLA_NORM_EPS)

    @pl.when(t == pl.num_programs(2) - 1)
    def _():
        sout_ref[...] = st_scr[...].T


def _gla(grp, p3, wg_pad, bg, s0):
    B, T = grp.B, grp.T
    tile = min(T, 4 * CHUNK)
    chunk = min(CHUNK, tile)
    kern = functools.partial(_gla_kernel, chunk=chunk, n_chunks=tile // chunk, tile=tile,
                             valid=None if grp.Tv == T else grp.Tv)
    q0, k0, v0, a0 = COL_GQ // GLA_DK, COL_GK // GLA_DK, COL_GV // GLA_DV, COL_GA // 128
    return pl.pallas_call(
        kern,
        grid=(B, GLA_HEADS, T // tile),
        in_specs=[pl.BlockSpec((None, tile, GLA_DK), lambda b, h, t: (b, t, q0 + h)),
                  pl.BlockSpec((None, tile, GLA_DK), lambda b, h, t: (b, t, k0 + h)),
                  pl.BlockSpec((None, tile, GLA_DV), lambda b, h, t: (b, t, v0 + h)),
                  pl.BlockSpec((None, tile, 128), lambda b, h, t: (b, t, a0)),
                  pl.BlockSpec((128, GLA_DK), lambda b, h, t: (0, h)),
                  pl.BlockSpec((1, GLA_DK), lambda b, h, t: (0, h)),
                  pl.BlockSpec((None, None, GLA_DK, GLA_DV), lambda b, h, t: (b, h, 0, 0))],
        out_specs=[pl.BlockSpec((None, tile, GLA_DV), lambda b, h, t: (b, t, h)),
                   pl.BlockSpec((None, None, GLA_DK, GLA_DV), lambda b, h, t: (b, h, 0, 0))],
        out_shape=[jax.ShapeDtypeStruct((B, T, GLA_HEADS * GLA_DV), F32),
                   jax.ShapeDtypeStruct((B, GLA_HEADS, GLA_DK, GLA_DV), F32)],
        scratch_shapes=[pltpu.VMEM((GLA_DV, GLA_DK), F32)],
        compiler_params=_params(("arbitrary", "arbitrary", "arbitrary")),
        name="gla_scan",
    )(p3, p3, p3, p3, wg_pad, bg, s0)


def _rwprep_kernel(r_ref, k_ref, v_ref, l_ref, pr_ref, pk_ref, pv_ref, pl_ref,
                   s0_ref, s0l_ref, mu_ref, mul_ref, w0_ref, w2_ref, a0_ref, a2_ref, g2_ref,
                   kk_ref, ka_ref, bd_ref,
                   ro_ref, ko_ref, vo_ref, an_ref, bb_ref, lw_ref, gate_ref,
                   *, tile, valid):
    t = pl.program_id(1)
    first = lax.broadcasted_iota(jnp.int32, (tile, 1), 0) == 0

    def shifted(cur_ref, prev_ref, s0, mu):
        cur = cur_ref[...]
        carry = jnp.where(t == 0, s0, prev_ref[7:8, :])
        prev = jnp.where(first, carry, pltpu.roll(cur, 1, 0))
        return cur + (prev - cur) * mu

    r = shifted(r_ref, pr_ref, s0_ref[:, 0:1024], mu_ref[:, 0:1024])
    k = shifted(k_ref, pk_ref, s0_ref[:, 1024:2048], mu_ref[:, 1024:2048])
    v = shifted(v_ref, pv_ref, s0_ref[:, 2048:3072], mu_ref[:, 2048:3072])
    lo = shifted(l_ref, pl_ref, s0l_ref[...], mul_ref[...])

    wpre = w0_ref[...] + _mm(jnp.tanh(lo), w2_ref[...])
    wlog = -_softplus(-wpre) - 0.5
    lw = -jnp.exp(wlog)
    a = jax.nn.sigmoid(a0_ref[...] + _mm(lo, a2_ref[...]))
    gate = _mm(jax.nn.sigmoid(lo), g2_ref[...])

    kk = k * kk_ref[...]
    ss = _mm_exact_rhs(kk * kk, bd_ref[...])
    kk = kk / jnp.maximum(jnp.sqrt(ss), 1e-12)
    kmod = k * (1.0 + (a - 1.0) * ka_ref[...])
    an = -kk
    bb = kk * a
    if valid is not None:
        ok = (t * tile + lax.broadcasted_iota(jnp.int32, (tile, 1), 0)) < valid
        lw = jnp.where(ok, lw, 0.0)
        an = jnp.where(ok, an, 0.0)
        bb = jnp.where(ok, bb, 0.0)
        kmod = jnp.where(ok, kmod, 0.0)
        v = jnp.where(ok, v, 0.0)
    ro_ref[...] = r
    ko_ref[...] = kmod
    vo_ref[...] = v
    an_ref[...] = an
    bb_ref[...] = bb
    lw_ref[...] = lw
    gate_ref[...] = gate


def _rwkv_prep(grp, p3, sh_rkv, sh_lora, wts):
    B, T = grp.B, grp.T
    tile = min(T, 256)
    tb = tile // 8
    W = RW_HEADS * RW_HEAD

    def cur(width, blk):
        return pl.BlockSpec((None, tile, width), lambda b, t: (b, t, blk))

    def prev(width, blk):
        return pl.BlockSpec((None, 8, width), lambda b, t: (b, jnp.maximum(t * tb - 1, 0), blk))

    def full(shape):
        return pl.BlockSpec(shape, lambda b, t: (0,) * len(shape))

    out_spec = pl.BlockSpec((None, tile, W), lambda b, t: (b, t, 0))
    kern = functools.partial(_rwprep_kernel, tile=tile, valid=None if grp.Tv == T else grp.Tv)
    lora_blk = COL_LORA // 256
    return pl.pallas_call(
        kern,
        grid=(B, T // tile),
        in_specs=[cur(W, 0), cur(W, 1), cur(W, 2), cur(256, lora_blk),
                  prev(W, 0), prev(W, 1), prev(W, 2), prev(256, lora_blk),
                  pl.BlockSpec((None, 1, 3 * W), lambda b, t: (b, 0, 0)),
                  pl.BlockSpec((None, 1, 256), lambda b, t: (b, 0, 0)),
                  full((1, 3 * W)), full((1, 256)),
                  full((1, W)), full((256, W)), full((1, W)), full((256, W)), full((256, W)),
                  full((1, W)), full((1, W)), full((W, W))],
        out_specs=[out_spec] * 7,
        out_shape=[jax.ShapeDtypeStruct((B, T, W), F32)] * 7,
        compiler_params=_params(("arbitrary", "arbitrary")),
        name="rwkv_prep",
    )(p3, p3, p3, p3, p3, p3, p3, p3, sh_rkv, sh_lora,
      wts["mu_rkv"], wts["mu_lora"], wts["rw_w0"], wts["w2p"], wts["rw_a0"], wts["a2p"], wts["g2p"],
      wts["rw_k_k"], wts["rw_k_a"], wts["head_ones"])


def _rwkv_chunk(r, k, v, an, bb, lw, s, *, chunk, strict, incl, tril, eye):
    cum = _mm_exact_lhs(tril, lw)
    w_incl = jnp.exp(cum)
    w_prev = jnp.exp(cum - lw)
    w_inv = jnp.exp(-cum)
    at = an * w_prev
    rt = r * w_incl
    bt = bb * w_inv
    kt = k * w_inv
    l_ab = jnp.where(strict, _mm_nt(at, bt), 0.0)
    l_ak = jnp.where(strict, _mm_nt(at, kt), 0.0)
    m_rb = jnp.where(incl, _mm_nt(rt, bt), 0.0)
    m_rk = jnp.where(incl, _mm_nt(rt, kt), 0.0)
    rhs = _mm_nt(at, s) + _mm(l_ak, v)
    inv = eye + l_ab
    pw = l_ab
    steps = chunk.bit_length() - 2
    for _ in range(steps):
        pw = _mm(pw, pw)
        inv = inv + _mm(inv, pw)
    u = _mm(inv, rhs)
    o = _mm_nt(rt, s) + _mm(m_rb, u) + _mm(m_rk, v)
    s_new = (s + _mm_tn(u, bt) + _mm_tn(v, kt)) * w_incl[chunk - 1:chunk, :]
    return o, s_new


def _rwkv_kernel(r_ref, k_ref, v_ref, an_ref, bb_ref, lw_ref, gw_ref, gb_ref, rk_ref, s0_ref,
                 o_ref, sout_ref, s_scr, *, chunk, n_chunks):
    t = pl.program_id(2)

    @pl.when(t == 0)
    def _():
        s_scr[...] = s0_ref[...]

    row = lax.broadcasted_iota(jnp.int32, (chunk, chunk), 0)
    col = lax.broadcasted_iota(jnp.int32, (chunk, chunk), 1)
    strict = row > col
    incl = row >= col
    tril = jnp.where(incl, 1.0, 0.0).astype(BF16)
    eye = jnp.where(row == col, 1.0, 0.0)
    for j in range(2):
        hs = slice(j * RW_HEAD, (j + 1) * RW_HEAD)
        s = s_scr[j]
        gw = gw_ref[:, hs]
        gb = gb_ref[:, hs]
        rk = rk_ref[:, hs]
        for c in range(n_chunks):
            sl = slice(c * chunk, (c + 1) * chunk)
            r = r_ref[sl, hs]
            k = k_ref[sl, hs]
            v = v_ref[sl, hs]
            o, s = _rwkv_chunk(r, k, v, an_ref[sl, hs], bb_ref[sl, hs], lw_ref[sl, hs], s,
                               chunk=chunk, strict=strict, incl=incl, tril=tril, eye=eye)
            mu = jnp.mean(o, axis=-1, keepdims=True)
            oc = o - mu
            var = jnp.mean(oc * oc, axis=-1, keepdims=True)
            on = oc * lax.rsqrt(var + RW_GN_EPS) * gw + gb
            bonus = jnp.sum(r * k * rk, axis=-1, keepdims=True) * v
            o_ref[sl, hs] = on + bonus
        s_scr[j] = s

    @pl.when(t == pl.num_programs(2) - 1)
    def _():
        sout_ref[...] = s_scr[...]


def _rwkv(grp, seqs, gn_w, gn_b, r_k, s0):
    B, T = grp.B, grp.T
    tile = min(T, 2 * CHUNK)
    chunk = min(CHUNK, tile)
    W = RW_HEADS * RW_HEAD
    seq_spec = pl.BlockSpec((None, tile, 128), lambda b, h, t: (b, t, h))
    vec_spec = pl.BlockSpec((1, 128), lambda b, h, t: (0, h))
    st_spec = pl.BlockSpec((None, 2, RW_HEAD, RW_HEAD), lambda b, h, t: (b, h, 0, 0))
    kern = functools.partial(_rwkv_kernel, chunk=chunk, n_chunks=tile // chunk)
    return pl.pallas_call(
        kern,
        grid=(B, RW_HEADS // 2, T // tile),
        in_specs=[seq_spec] * 6 + [vec_spec] * 3 + [st_spec],
        out_specs=[seq_spec, st_spec],
        out_shape=[jax.ShapeDtypeStruct((B, T, W), F32),
                   jax.ShapeDtypeStruct((B, RW_HEADS, RW_HEAD, RW_HEAD), F32)],
        scratch_shapes=[pltpu.VMEM((2, RW_HEAD, RW_HEAD), F32)],
        compiler_params=_params(("arbitrary", "arbitrary", "arbitrary")),
        name="rwkv_scan",
    )(*seqs, gn_w, gn_b, r_k, s0)


def _merge_kernel(og_ref, gg_ref, orw_ref, gate_ref, mga_ref, mgb_ref, x_ref,
                  gate1_ref, shift2_ref, scale2_ref,
                  ggn_ref, gpost_ref, gpre_ref, wbg_ref, wbr_ref, wout_ref, wq_ref,
                  x1_ref, h2_ref, qp_ref):
    gg = gg_ref[...]
    gla_in = og_ref[...] * ggn_ref[...] * (gg * jax.nn.sigmoid(gg))
    br_gla = _mm(gla_in, wbg_ref[...])
    br_rw = _mm(orw_ref[...] * gate_ref[...], wbr_ref[...])
    merged = jax.nn.sigmoid(mga_ref[...]) * br_gla + jax.nn.sigmoid(mgb_ref[...]) * br_rw
    y = _mm(merged, wout_ref[...])
    x1 = x_ref[...] + gate1_ref[...] * _rms(y, gpost_ref[...])
    x1_ref[...] = x1
    h2 = (_rms(x1, gpre_ref[...]) * (1.0 + scale2_ref[...]) + shift2_ref[...]).astype(BF16)
    h2_ref[...] = h2
    qp_ref[...] = jnp.dot(h2, wq_ref[...], preferred_element_type=F32)


def _merge(grp, og, p2, orw, gate_rw, x2d, modop, wts):
    tm, M = min(grp.tm, MERGE_TILE), grp.M
    W = D_MODEL

    def tok(blk):
        return pl.BlockSpec((tm, W), lambda i: (i, blk))

    def full(shape):
        return pl.BlockSpec(shape, lambda i: (0,) * len(shape))

    return pl.pallas_call(
        _merge_kernel,
        grid=(M // tm,),
        in_specs=[tok(0), tok(COL_GG // W), tok(0), tok(0), tok(COL_MG // W), tok(COL_MG // W + 1), tok(0),
                  grp.mod_spec(2, tm), grp.mod_spec(3, tm), grp.mod_spec(4, tm),
                  full((1, W)), full((1, W)), full((1, W)),
                  full((W, W)), full((W, W)), full((W, W)), full((W, 2 * W))],
        out_specs=[tok(0), tok(0), pl.BlockSpec((tm, 2 * W), lambda i: (i, 0))],
        out_shape=[jax.ShapeDtypeStruct((M, W), F32), jax.ShapeDtypeStruct((M, W), BF16),
                   jax.ShapeDtypeStruct((M, 2 * W), F32)],
        compiler_params=_params(("arbitrary",)),
        name="merge",
    )(og, p2, orw, gate_rw, p2, p2, x2d, modop, modop, modop,
      wts["g_gla_norm"], wts["g_post_mix"], wts["g_pre_ffn"],
      wts["w_b_gla"], wts["w_b_rw"], wts["w_out"], wts["peer_wq"])


def _top_values(s, n):
    rows = lax.broadcasted_iota(jnp.int32, (n, s.shape[1]), 0)
    vals = jnp.zeros((n, s.shape[1]), F32)
    for i in range(n):
        m = jnp.max(s, axis=0, keepdims=True)
        vals = jnp.where(rows == i, m, vals)
        s = jnp.where(s == m, NEG_INF, s)
    return vals


def _select_kernel(qp_ref, keys_ref, s2_ref, e2_ref, al_ref, th_ref):
    tm = qp_ref.shape[0]
    for h in range(PEER_HEADS):
        q1 = qp_ref[:, (2 * h) * 128:(2 * h + 1) * 128]
        q2 = qp_ref[:, (2 * h + 1) * 128:(2 * h + 2) * 128]
        s1 = _mm_nt(keys_ref[2 * h], q1)
        s2 = _mm_nt(keys_ref[2 * h + 1], q2)
        v1 = _top_values(s1, PEER_TOPK)
        v2 = _top_values(s2, PEER_TOPK)
        cands = [v1[a:a + 1, :] + v2 for a in range(PEER_TOPK)]
        cmax = v1[0:1, :] + v2[0:1, :]
        z = jnp.zeros((1, tm), F32)
        m = cmax
        for _ in range(PEER_TOPK):
            best = cands[0]
            for c in cands[1:]:
                best = jnp.maximum(best, c)
            m = jnp.max(best, axis=0, keepdims=True)
            z = z + jnp.exp(m - cmax)
            cands = [jnp.where(c == m, NEG_INF, c) for c in cands]
        s2_ref[h] = s2
        e2_ref[h] = jnp.exp(s2 - v2[0:1, :])
        al_ref[h] = jnp.exp(s1 - v1[0:1, :]) / z
        th_ref[h] = m - s1


def _peer_select(grp, qp, keys):
    tm, M = grp.tm, grp.M
    out_spec = pl.BlockSpec((PEER_HEADS, PEER_NKEYS, tm), lambda i: (0, 0, i))
    return pl.pallas_call(
        _select_kernel,
        grid=(M // tm,),
        in_specs=[pl.BlockSpec((tm, 2 * D_MODEL), lambda i: (i, 0)),
                  pl.BlockSpec((2 * PEER_HEADS, PEER_NKEYS, 128), lambda i: (0, 0, 0))],
        out_specs=[out_spec] * 4,
        out_shape=[jax.ShapeDtypeStruct((PEER_HEADS, PEER_NKEYS, M), F32)] * 4,
        compiler_params=_params(("arbitrary",)),
        name="peer_select",
    )(qp, keys)


def _peer_kernel(h2_ref, u_ref, v_ref, s2_ref, e2_ref, al_ref, th_ref, x1_ref, gate2_ref, gpost_ref,
                 o_ref, acc_ref, y_scr):
    e = pl.program_id(1)

    @pl.when(e == 0)
    def _():
        acc_ref[...] = jnp.zeros_like(acc_ref)

    act = lax.dot_general(u_ref[...], h2_ref[...], (((1,), (1,)), ((), ())),
                          preferred_element_type=F32)
    n_sub = PEER_EXPERT_TILE // PEER_NKEYS
    for j in range(n_sub):
        rows = slice(j * PEER_NKEYS, (j + 1) * PEER_NKEYS)
        a = act[rows, :]
        g = 0.5 * a * (1.0 + lax.erf(a * 0.7071067811865476))
        w = jnp.zeros_like(a)
        for h in range(PEER_HEADS):
            w = w + al_ref[h, j:j + 1, :] * jnp.where(s2_ref[h] >= th_ref[h, j:j + 1, :], e2_ref[h], 0.0)
        y_scr[rows, :] = (g * w).astype(BF16)
    acc_ref[...] += lax.dot_general(y_scr[...], v_ref[...], (((0,), (0,)), ((), ())),
                                    preferred_element_type=F32)

    @pl.when(e == pl.num_programs(1) - 1)
    def _():
        o_ref[...] = x1_ref[...] + gate2_ref[...] * _rms(acc_ref[...], gpost_ref[...])


def _peer(grp, h2, sel, x1, modop, g_post, pu, pv):
    tm, M = grp.tm, grp.M
    te = PEER_EXPERT_TILE
    n_sub = te // PEER_NKEYS
    s2, e2, al, th = sel
    wide = pl.BlockSpec((PEER_HEADS, PEER_NKEYS, tm), lambda i, e: (0, 0, i))
    rows = pl.BlockSpec((PEER_HEADS, n_sub, tm), lambda i, e: (0, e, i))
    tok = pl.BlockSpec((tm, D_MODEL), lambda i, e: (i, 0))
    return pl.pallas_call(
        _peer_kernel,
        grid=(M // tm, pu.shape[0] // te),
        in_specs=[tok,
                  pl.BlockSpec((te, D_MODEL), lambda i, e: (e, 0)),
                  pl.BlockSpec((te, D_MODEL), lambda i, e: (e, 0)),
                  wide, wide, rows, rows, tok, grp.mod_spec(5),
                  pl.BlockSpec((1, D_MODEL), lambda i, e: (0, 0))],
        out_specs=tok,
        out_shape=jax.ShapeDtypeStruct((M, D_MODEL), F32),
        scratch_shapes=[pltpu.VMEM((tm, D_MODEL), F32), pltpu.VMEM((te, tm), BF16)],
        compiler_params=_params(("arbitrary", "arbitrary")),
        name="peer_dense",
    )(h2, pu, pv, s2, e2, al, th, x1, modop, g_post)


def _prepare_weights(w_in, w_gla_gate, b_gla_gate, g_gla_norm, mu_shift, rw_w0, rw_w2, rw_a0, rw_a2,
                     rw_g2, rw_k_k, rw_k_a, rw_r_k, rw_gn_w, rw_gn_b, w_b_gla, w_b_rw, w_out,
                     peer_wq, peer_keys, peer_u, peer_v, g_pre_mix, g_post_mix, g_pre_ffn, g_post_ffn):
    W = RW_HEADS * RW_HEAD
    rw = w_in[:, 3088:6416]
    w_proj = jnp.concatenate(
        [rw[:, :3 * W], w_in[:, 6416:8464], w_in[:, 2048:3072], w_in[:, 1024:2048], w_in[:, 0:1024],
         rw[:, 3 * W:], w_in[:, 3072:3088], jnp.zeros((D_MODEL, PROJ_PAD_W - 8464), F32)],
        axis=1).astype(BF16)
    head = jnp.arange(W) // RW_HEAD
    zeros64 = jnp.zeros((64, W), F32)
    return dict(
        w_proj=w_proj,
        wg_pad=jnp.concatenate([w_gla_gate, jnp.zeros((128 - GLA_GATE_RANK, 512), F32)], axis=0),
        bg=b_gla_gate.reshape(1, -1),
        g_gla_norm=g_gla_norm.reshape(1, -1),
        mu_rkv=mu_shift[:3 * W].reshape(1, -1),
        mu_lora=mu_shift[3 * W:].reshape(1, -1),
        rw_w0=rw_w0.reshape(1, -1),
        w2p=jnp.concatenate([rw_w2, jnp.zeros((192, W), F32)], axis=0),
        rw_a0=rw_a0.reshape(1, -1),
        a2p=jnp.concatenate([zeros64, rw_a2, jnp.zeros((128, W), F32)], axis=0),
        g2p=jnp.concatenate([jnp.zeros((128, W), F32), rw_g2], axis=0),
        rw_k_k=rw_k_k.reshape(1, -1),
        rw_k_a=rw_k_a.reshape(1, -1),
        rw_r_k=rw_r_k.reshape(1, -1),
        rw_gn_w=rw_gn_w.reshape(1, -1),
        rw_gn_b=rw_gn_b.reshape(1, -1),
        head_ones=(head[:, None] == head[None, :]).astype(BF16),
        w_b_gla=w_b_gla.astype(BF16),
        w_b_rw=w_b_rw.astype(BF16),
        w_out=w_out.astype(BF16),
        peer_wq=peer_wq.astype(BF16),
        peer_keys=peer_keys.reshape(2 * PEER_HEADS, PEER_NKEYS, 128).astype(BF16),
        peer_u=peer_u.astype(BF16),
        peer_v=peer_v.astype(BF16),
        g_pre_mix=g_pre_mix.reshape(1, -1),
        g_post_mix=g_post_mix.reshape(1, -1),
        g_pre_ffn=g_pre_ffn.reshape(1, -1),
        g_post_ffn=g_post_ffn.reshape(1, -1),
    )


def _block(x, mod, s_gla0, s_rw0, shift0, wts, valid):
    B, T, _ = x.shape
    grp = _Group(B, T, valid)
    W = RW_HEADS * RW_HEAD
    x2d = x.reshape(grp.M, D_MODEL)
    modop = grp.mod_operand(mod)
    p2 = _projection(grp, x2d, modop, wts["g_pre_mix"], wts["w_proj"])
    p3 = p2.reshape(B, T, PROJ_PAD_W)
    og, s_gla = _gla(grp, p3, wts["wg_pad"], wts["bg"], s_gla0)
    sh_rkv = shift0[:, None, :3 * W]
    sh_lora = shift0[:, None, 3 * W:]
    r, k, v, an, bb, lw, gate = _rwkv_prep(grp, p3, sh_rkv, sh_lora, wts)
    orw, s_rw = _rwkv(grp, (r, k, v, an, bb, lw), wts["rw_gn_w"], wts["rw_gn_b"], wts["rw_r_k"], s_rw0)
    x1, h2, qp = _merge(grp, og.reshape(grp.M, -1), p2, orw.reshape(grp.M, W), gate.reshape(grp.M, W),
                        x2d, modop, wts)
    sel = _peer_select(grp, qp, wts["peer_keys"])
    out = _peer(grp, h2, sel, x1, modop, wts["g_post_ffn"], wts["peer_u"], wts["peer_v"])
    last = p3[:, valid - 1, :]
    shift_new = jnp.concatenate([last[:, :3 * W], last[:, COL_LORA:COL_LORA + 256]], axis=-1)
    return out.reshape(B, T, D_MODEL), s_gla, s_rw, shift_new


def kernel(x_prompt, x_sample, c_prompt, c_sample, state_gla, state_rwkv, state_shift, w_ada, b_ada, g_pre_mix, g_post_mix, g_pre_ffn, g_post_ffn, w_in, w_gla_gate, b_gla_gate, g_gla_norm, mu_shift, rw_w0, rw_w2, rw_a0, rw_a2, rw_g2, rw_k_k, rw_k_a, rw_r_k, rw_gn_w, rw_gn_b, w_b_gla, w_b_rw, w_out, peer_wq, peer_keys, peer_u, peer_v):
    depth = w_in.shape[0]
    bp, bs = x_prompt.shape[0], x_sample.shape[0]
    ts = x_sample.shape[1]
    ts_pad = -(-ts // 8) * 8
    xp = x_prompt
    xs = jnp.pad(x_sample, ((0, 0), (0, ts_pad - ts), (0, 0)))
    outs = [[] for _ in range(6)]
    for l in range(depth):
        wts = _prepare_weights(
            w_in[l], w_gla_gate[l], b_gla_gate[l], g_gla_norm[l], mu_shift[l], rw_w0[l], rw_w2[l],
            rw_a0[l], rw_a2[l], rw_g2[l], rw_k_k[l], rw_k_a[l], rw_r_k[l], rw_gn_w[l], rw_gn_b[l],
            w_b_gla[l], w_b_rw[l], w_out[l], peer_wq[l], peer_keys[l], peer_u[l], peer_v[l],
            g_pre_mix[l], g_post_mix[l], g_pre_ffn[l], g_post_ffn[l])
        mod = _modulation(jnp.concatenate([c_prompt, c_sample], axis=0), w_ada[l], b_ada[l])
        zg = jnp.zeros((bp, GLA_HEADS, GLA_DK, GLA_DV), F32)
        zr = jnp.zeros((bp, RW_HEADS, RW_HEAD, RW_HEAD), F32)
        zs = jnp.zeros((bp, RW_SHIFT_W), F32)
        xp, g1, r1, s1 = _block(xp, mod[:bp], zg, zr, zs, wts, xp.shape[1])
        xs, g2, r2, s2 = _block(xs, mod[bp:], state_gla[l], state_rwkv[l], state_shift[l], wts, ts)
        for acc, val in zip(outs, (g1, r1, s1, g2, r2, s2)):
            acc.append(val)
    stacked = [jnp.stack(o) for o in outs]
    return (xp, xs[:, :ts], *stacked)
```

```python
import functools

import jax
import jax.numpy as jnp
from jax import lax
from jax.experimental import pallas as pl
from jax.experimental.pallas import tpu as pltpu

F32 = jnp.float32
BF16 = jnp.bfloat16

D_MODEL = 1024
GLA_HEADS = 4
GLA_DK = 128
GLA_DV = 256
GLA_GATE_RANK = 16
GLA_GATE_TAU = 16.0
GLA_NORM_EPS = 1e-5
RW_HEAD = 64
RW_HEADS = 16
RW_PAIRS = RW_HEADS // 2
RW_GN_EPS = 64e-5
RW_SHIFT_W = 3328
PEER_HEADS = 8
PEER_NKEYS = 128
PEER_TOPK = 16
NORM_EPS = 1e-6
LANES = 128

COL_R, COL_K, COL_V = 0, 1024, 2048
COL_MG = 3072
COL_GG = 5120
COL_GV = 6144
COL_GQ = 7168
COL_GK = 7680
COL_LORA = 8192
COL_GA = 8448
PROJ_PAD_W = 8704
PROJ_TILE_N = 2176

CHUNK = 64
TOKEN_TILE = 512
MERGE_TILE = 256
PEER_EXPERT_TILE = 1024
VMEM_LIMIT = 56 * 1024 * 1024

NEG_INF = float("-inf")


def _mm(a, b):
    return jnp.dot(a.astype(BF16), b.astype(BF16), preferred_element_type=F32)


def _mm_nt(a, b):
    return lax.dot_general(a.astype(BF16), b.astype(BF16), (((1,), (1,)), ((), ())),
                           preferred_element_type=F32)


def _mm_tn(a, b):
    return lax.dot_general(a.astype(BF16), b.astype(BF16), (((0,), (0,)), ((), ())),
                           preferred_element_type=F32)


def _split(x):
    hi = x.astype(BF16)
    lo = (x - hi.astype(F32)).astype(BF16)
    return hi, lo


def _mm_exact_lhs(a, x):
    hi, lo = _split(x)
    a = a.astype(BF16)
    return (jnp.dot(a, hi, preferred_element_type=F32)
            + jnp.dot(a, lo, preferred_element_type=F32))


def _mm_exact_rhs(x, a):
    hi, lo = _split(x)
    a = a.astype(BF16)
    return (jnp.dot(hi, a, preferred_element_type=F32)
            + jnp.dot(lo, a, preferred_element_type=F32))


def _softplus(x):
    return jnp.maximum(x, 0.0) + jnp.log1p(jnp.exp(-jnp.abs(x)))


def _rms(x, g):
    return x * lax.rsqrt(jnp.mean(x * x, axis=-1, keepdims=True) + NORM_EPS) * g


def _params(sem):
    return pltpu.CompilerParams(dimension_semantics=sem, vmem_limit_bytes=VMEM_LIMIT)


def _mod_kernel(c_ref, w_ref, b_ref, o_ref):
    c = c_ref[...]
    o_ref[...] = _mm(c * jax.nn.sigmoid(c), w_ref[...]) + b_ref[...]


def _modulation(c, w_ada, b_ada):
    rows = c.shape[0]
    n = w_ada.shape[1]
    tn = 1536
    return pl.pallas_call(
        _mod_kernel,
        grid=(n // tn,),
        in_specs=[pl.BlockSpec((rows, D_MODEL), lambda j: (0, 0)),
                  pl.BlockSpec((D_MODEL, tn), lambda j: (0, j)),
                  pl.BlockSpec((1, tn), lambda j: (0, j))],
        out_specs=pl.BlockSpec((rows, tn), lambda j: (0, j)),
        out_shape=jax.ShapeDtypeStruct((rows, n), F32),
        compiler_params=_params(("arbitrary",)),
        name="adaln_mod",
    )(c, w_ada, b_ada.reshape(1, n))


class _Group:
    def __init__(self, batch, seq, valid):
        self.B, self.T, self.Tv = batch, seq, valid
        self.M = batch * seq
        self.tm = min(TOKEN_TILE, self.M)
        self.per_batch_mod = seq % self.tm == 0

    def mod_operand(self, mod):
        if self.per_batch_mod:
            return mod.reshape(self.B, 1, 6 * D_MODEL)
        return jnp.repeat(mod, self.T, axis=0)

    def mod_spec(self, col, tm=None):
        tm, T = tm or self.tm, self.T
        if self.per_batch_mod:
            return pl.BlockSpec((None, 1, D_MODEL), lambda i, *_: ((i * tm) // T, 0, col))
        return pl.BlockSpec((tm, D_MODEL), lambda i, *_: (i, col))


def _proj_kernel(x_ref, shift_ref, scale_ref, g_ref, w_ref, p_ref, h_scr):
    @pl.when(pl.program_id(1) == 0)
    def _():
        h = _rms(x_ref[...], g_ref[...]) * (1.0 + scale_ref[...]) + shift_ref[...]
        h_scr[...] = h.astype(BF16)

    p_ref[...] = jnp.dot(h_scr[...], w_ref[...], preferred_element_type=F32)


def _projection(grp, x2d, modop, g_pre, w_proj):
    tm = grp.tm
    return pl.pallas_call(
        _proj_kernel,
        grid=(grp.M // tm, PROJ_PAD_W // PROJ_TILE_N),
        in_specs=[pl.BlockSpec((tm, D_MODEL), lambda i, j: (i, 0)),
                  grp.mod_spec(0), grp.mod_spec(1),
                  pl.BlockSpec((1, D_MODEL), lambda i, j: (0, 0)),
                  pl.BlockSpec((D_MODEL, PROJ_TILE_N), lambda i, j: (0, j))],
        out_specs=pl.BlockSpec((tm, PROJ_TILE_N), lambda i, j: (i, j)),
        out_shape=jax.ShapeDtypeStruct((grp.M, PROJ_PAD_W), F32),
        scratch_shapes=[pltpu.VMEM((tm, D_MODEL), BF16)],
        compiler_params=_params(("arbitrary", "arbitrary")),
        name="in_proj",
    )(x2d, modop, modop, g_pre, w_proj)


def _gla_kernel(q_ref, k_ref, v_ref, ga_ref, wg_ref, bg_ref, s0_ref, og_ref, sout_ref, st_scr,
                *, chunk, n_chunks, tile, valid, seqs):
    t = pl.program_id(1)
    last = pl.num_programs(1) - 1
    row = lax.broadcasted_iota(jnp.int32, (chunk, chunk), 0)
    col = lax.broadcasted_iota(jnp.int32, (chunk, chunk), 1)
    causal = row >= col
    tril = jnp.where(causal, 1.0, 0.0).astype(BF16)

    def per_seq(b, carry):
        @pl.when(t == 0)
        def _():
            for h in range(GLA_HEADS):
                st_scr[b, h] = s0_ref[b, h].T

        for c in range(n_chunks):
            sl = pl.ds(c * chunk, chunk)
            z = _mm(ga_ref[b, sl, :], wg_ref[...]) + bg_ref[...]
            la = -_softplus(-z) / GLA_GATE_TAU
            k = k_ref[b, sl, :]
            if valid is not None:
                pos = t * tile + c * chunk + lax.broadcasted_iota(jnp.int32, (chunk, 1), 0)
                ok = pos < valid
                la = jnp.where(ok, la, 0.0)
                k = jnp.where(ok, k, 0.0)
            cum = _mm_exact_lhs(tril, la)
            end = cum[chunk - 1:chunk, :]
            qe = q_ref[b, sl, :] * (GLA_DK ** -0.5) * jnp.exp(cum)
            ke = k * jnp.exp(-cum)
            kd = k * jnp.exp(end - cum)
            decay = jnp.exp(end)
            for h in range(GLA_HEADS):
                ks = slice(h * GLA_DK, (h + 1) * GLA_DK)
                vs = slice(h * GLA_DV, (h + 1) * GLA_DV)
                v = v_ref[b, sl, vs]
                if valid is not None:
                    v = jnp.where(ok, v, 0.0)
                a = jnp.where(causal, _mm_nt(qe[:, ks], ke[:, ks]), 0.0)
                st = st_scr[b, h]
                o = _mm_nt(qe[:, ks], st) + _mm(a, v)
                st_scr[b, h] = st * decay[:, ks] + _mm_tn(v, kd[:, ks])
                og_ref[b, sl, vs] = o * lax.rsqrt(jnp.mean(o * o, axis=-1, keepdims=True) + GLA_NORM_EPS)

        @pl.when(t == last)
        def _():
            for h in range(GLA_HEADS):
                sout_ref[b, h] = st_scr[b, h].T

        return carry

    if seqs == 1:
        per_seq(0, 0)
    else:
        lax.fori_loop(0, seqs, per_seq, 0)


def _seqs_per_step(batch, seq):
    n = max(1, min(8, 64 // seq))
    while batch % n:
        n //= 2
    return n


def _gla(grp, p3, wg_pad, bg, s0):
    B, T = grp.B, grp.T
    tile = min(T, 4 * CHUNK)
    chunk = min(CHUNK, tile)
    nb = _seqs_per_step(B, T)
    kern = functools.partial(_gla_kernel, chunk=chunk, n_chunks=tile // chunk, tile=tile,
                             valid=None if grp.Tv == T else grp.Tv, seqs=nb)
    kw, vw = GLA_HEADS * GLA_DK, GLA_HEADS * GLA_DV
    st_spec = pl.BlockSpec((nb, GLA_HEADS, GLA_DK, GLA_DV), lambda i, t: (i, 0, 0, 0))
    return pl.pallas_call(
        kern,
        grid=(B // nb, T // tile),
        in_specs=[pl.BlockSpec((nb, tile, kw), lambda i, t: (i, t, COL_GQ // kw)),
                  pl.BlockSpec((nb, tile, kw), lambda i, t: (i, t, COL_GK // kw)),
                  pl.BlockSpec((nb, tile, vw), lambda i, t: (i, t, COL_GV // vw)),
                  pl.BlockSpec((nb, tile, LANES), lambda i, t: (i, t, COL_GA // LANES)),
                  pl.BlockSpec((LANES, kw), lambda i, t: (0, 0)),
                  pl.BlockSpec((1, kw), lambda i, t: (0, 0)),
                  st_spec],
        out_specs=[pl.BlockSpec((nb, tile, vw), lambda i, t: (i, t, 0)), st_spec],
        out_shape=[jax.ShapeDtypeStruct((B, T, vw), F32),
                   jax.ShapeDtypeStruct((B, GLA_HEADS, GLA_DK, GLA_DV), F32)],
        scratch_shapes=[pltpu.VMEM((nb, GLA_HEADS, GLA_DV, GLA_DK), F32)],
        compiler_params=_params(("arbitrary", "arbitrary")),
        name="gla_scan",
    )(p3, p3, p3, p3, wg_pad, bg, s0)


def _rwprep_kernel(r_ref, k_ref, v_ref, l_ref, pr_ref, pk_ref, pv_ref, pl_ref,
                   s0_ref, s0l_ref, mu_ref, mul_ref, w0_ref, w2_ref, a0_ref, a2_ref, g2_ref,
                   kk_ref, ka_ref, bd_ref,
                   ro_ref, ko_ref, vo_ref, an_ref, bb_ref, lw_ref, gate_ref,
                   *, tile, valid):
    t = pl.program_id(1)
    first = lax.broadcasted_iota(jnp.int32, (tile, 1), 0) == 0

    def shifted(cur_ref, prev_ref, s0, mu):
        cur = cur_ref[...]
        carry = jnp.where(t == 0, s0, prev_ref[7:8, :])
        prev = jnp.where(first, carry, pltpu.roll(cur, 1, 0))
        return cur + (prev - cur) * mu

    r = shifted(r_ref, pr_ref, s0_ref[:, 0:1024], mu_ref[:, 0:1024])
    k = shifted(k_ref, pk_ref, s0_ref[:, 1024:2048], mu_ref[:, 1024:2048])
    v = shifted(v_ref, pv_ref, s0_ref[:, 2048:3072], mu_ref[:, 2048:3072])
    lo = shifted(l_ref, pl_ref, s0l_ref[...], mul_ref[...])

    wpre = w0_ref[...] + _mm(jnp.tanh(lo), w2_ref[...])
    wlog = -_softplus(-wpre) - 0.5
    lw = -jnp.exp(wlog)
    a = jax.nn.sigmoid(a0_ref[...] + _mm(lo, a2_ref[...]))
    gate = _mm(jax.nn.sigmoid(lo), g2_ref[...])

    kk = k * kk_ref[...]
    ss = _mm_exact_rhs(kk * kk, bd_ref[...])
    kk = kk / jnp.maximum(jnp.sqrt(ss), 1e-12)
    kmod = k * (1.0 + (a - 1.0) * ka_ref[...])
    an = -kk
    bb = kk * a
    if valid is not None:
        ok = (t * tile + lax.broadcasted_iota(jnp.int32, (tile, 1), 0)) < valid
        lw = jnp.where(ok, lw, 0.0)
        an = jnp.where(ok, an, 0.0)
        bb = jnp.where(ok, bb, 0.0)
        kmod = jnp.where(ok, kmod, 0.0)
        v = jnp.where(ok, v, 0.0)
    ro_ref[...] = r
    ko_ref[...] = kmod
    vo_ref[...] = v
    an_ref[...] = an
    bb_ref[...] = bb
    lw_ref[...] = lw
    gate_ref[...] = gate


def _rwkv_prep(grp, p3, sh_rkv, sh_lora, wts):
    B, T = grp.B, grp.T
    tile = min(T, 256)
    tb = tile // 8
    W = RW_HEADS * RW_HEAD

    def cur(width, blk):
        return pl.BlockSpec((None, tile, width), lambda b, t: (b, t, blk))

    def prev(width, blk):
        return pl.BlockSpec((None, 8, width), lambda b, t: (b, jnp.maximum(t * tb - 1, 0), blk))

    def full(shape):
        return pl.BlockSpec(shape, lambda b, t: (0,) * len(shape))

    out_spec = pl.BlockSpec((None, tile, W), lambda b, t: (b, t, 0))
    kern = functools.partial(_rwprep_kernel, tile=tile, valid=None if grp.Tv == T else grp.Tv)
    lora_blk = COL_LORA // 256
    return pl.pallas_call(
        kern,
        grid=(B, T // tile),
        in_specs=[cur(W, 0), cur(W, 1), cur(W, 2), cur(256, lora_blk),
                  prev(W, 0), prev(W, 1), prev(W, 2), prev(256, lora_blk),
                  pl.BlockSpec((None, 1, 3 * W), lambda b, t: (b, 0, 0)),
                  pl.BlockSpec((None, 1, 256), lambda b, t: (b, 0, 0)),
                  full((1, 3 * W)), full((1, 256)),
                  full((1, W)), full((256, W)), full((1, W)), full((256, W)), full((256, W)),
                  full((1, W)), full((1, W)), full((W, W))],
        out_specs=[out_spec] * 7,
        out_shape=[jax.ShapeDtypeStruct((B, T, W), F32)] * 7,
        compiler_params=_params(("arbitrary", "arbitrary")),
        name="rwkv_prep",
    )(p3, p3, p3, p3, p3, p3, p3, p3, sh_rkv, sh_lora,
      wts["mu_rkv"], wts["mu_lora"], wts["rw_w0"], wts["w2p"], wts["rw_a0"], wts["a2p"], wts["g2p"],
      wts["rw_k_k"], wts["rw_k_a"], wts["head_ones"])


def _rwkv_pairs_chunk(at, rt, bt, kt, v, states, *, chunk, same_head, strict, incl):
    n2 = 2 * chunk
    wide = n2 % LANES == 0
    pairs = range(len(states))

    def blocks(x):
        return jnp.where(same_head, jnp.concatenate([x, x], axis=0), 0.0).astype(BF16)

    a2 = [blocks(x) for x in at]
    r2 = [blocks(x) for x in rt]
    b2 = [blocks(x) for x in bt]
    k2 = [blocks(x) for x in kt]
    v2 = [blocks(x) for x in v]
    ar = [jnp.concatenate([a2[p], r2[p]], axis=0) for p in pairs]
    bk = [jnp.concatenate([b2[p], k2[p]], axis=0) for p in pairs]
    if wide:
        g = [_mm_nt(ar[p], bk[p]) for p in pairs]
        l_ab = [jnp.where(strict, g[p][:n2, :n2], 0.0) for p in pairs]
        l_ak = [jnp.where(strict, g[p][:n2, n2:], 0.0) for p in pairs]
        m_rb = [jnp.where(incl, g[p][n2:, :n2], 0.0) for p in pairs]
        m_rk = [jnp.where(incl, g[p][n2:, n2:], 0.0) for p in pairs]
    else:
        l_ab = [jnp.where(strict, _mm_nt(a2[p], b2[p]), 0.0) for p in pairs]
        l_ak = [jnp.where(strict, _mm_nt(a2[p], k2[p]), 0.0) for p in pairs]
        m_rb = [jnp.where(incl, _mm_nt(r2[p], b2[p]), 0.0) for p in pairs]
        m_rk = [jnp.where(incl, _mm_nt(r2[p], k2[p]), 0.0) for p in pairs]
    from_state = [_mm_nt(ar[p], states[p]) for p in pairs]
    x = [from_state[p][:n2] + _mm(l_ak[p], v2[p]) for p in pairs]
    pw = l_ab
    for _ in range(chunk.bit_length() - 2):
        if wide:
            y = [_mm(pw[p], jnp.concatenate([x[p], pw[p]], axis=1)) for p in pairs]
            x = [x[p] + y[p][:, :LANES] for p in pairs]
            pw = [y[p][:, LANES:] for p in pairs]
        else:
            x = [x[p] + _mm(pw[p], x[p]) for p in pairs]
            pw = [_mm(pw[p], pw[p]) for p in pairs]
    u = [x[p] + _mm(pw[p], x[p]) for p in pairs]
    uv = [jnp.concatenate([u[p].astype(BF16), v2[p]], axis=0) for p in pairs]
    if wide:
        o2 = [from_state[p][n2:] + _mm(jnp.concatenate([m_rb[p], m_rk[p]], axis=1), uv[p]) for p in pairs]
    else:
        o2 = [from_state[p][n2:] + _mm(m_rb[p], u[p]) + _mm(m_rk[p], v2[p]) for p in pairs]
    s_new = [states[p] + _mm_tn(uv[p], bk[p]) for p in pairs]
    return o2, s_new


def _rwkv_kernel(r_ref, k_ref, v_ref, an_ref, bb_ref, lw_ref, gw_ref, gb_ref, rk_ref, s0_ref,
                 o_ref, sout_ref, s_scr, *, chunk, n_chunks):
    t = pl.program_id(1)
    zeros = jnp.zeros((RW_HEAD, RW_HEAD), F32)

    @pl.when(t == 0)
    def _():
        for p in range(RW_PAIRS):
            top = jnp.concatenate([s0_ref[2 * p], zeros], axis=1)
            bot = jnp.concatenate([zeros, s0_ref[2 * p + 1]], axis=1)
            s_scr[p] = jnp.concatenate([top, bot], axis=0)

    n2 = 2 * chunk
    row = lax.broadcasted_iota(jnp.int32, (n2, n2), 0)
    col = lax.broadcasted_iota(jnp.int32, (n2, n2), 1)
    def second(idx, size):
        return jnp.where(idx >= size, 1, 0)

    same = second(row, chunk) == second(col, chunk)
    strict = same & (row > col)
    incl = same & (row >= col)
    same_head = (second(lax.broadcasted_iota(jnp.int32, (n2, LANES), 0), chunk)
                 == second(lax.broadcasted_iota(jnp.int32, (n2, LANES), 1), RW_HEAD))
    trow = lax.broadcasted_iota(jnp.int32, (chunk, chunk), 0)
    tcol = lax.broadcasted_iota(jnp.int32, (chunk, chunk), 1)
    tril = jnp.where(trow >= tcol, 1.0, 0.0).astype(BF16)
    first_head = lax.broadcasted_iota(jnp.int32, (chunk, LANES), 1) < RW_HEAD

    def head_sum(x):
        lo = jnp.sum(jnp.where(first_head, x, 0.0), axis=-1, keepdims=True)
        hi = jnp.sum(jnp.where(first_head, 0.0, x), axis=-1, keepdims=True)
        return jnp.where(first_head, lo, hi)

    for c in range(n_chunks):
        sl = pl.ds(c * chunk, chunk)
        lw = lw_ref[sl, :]
        cum = _mm_exact_lhs(tril, lw)
        w_incl = jnp.exp(cum)
        w_inv = jnp.exp(-cum)
        r = r_ref[sl, :]
        k = k_ref[sl, :]
        v = v_ref[sl, :]
        at = an_ref[sl, :] * jnp.exp(cum - lw)
        rt = r * w_incl
        bt = bb_ref[sl, :] * w_inv
        kt = k * w_inv
        lanes = [slice(p * LANES, (p + 1) * LANES) for p in range(RW_PAIRS)]
        o2, s_new = _rwkv_pairs_chunk(
            [at[:, ls] for ls in lanes], [rt[:, ls] for ls in lanes], [bt[:, ls] for ls in lanes],
            [kt[:, ls] for ls in lanes], [v[:, ls] for ls in lanes], [s_scr[p] for p in range(RW_PAIRS)],
            chunk=chunk, same_head=same_head, strict=strict, incl=incl)
        for p, ls in enumerate(lanes):
            s_scr[p] = s_new[p] * w_incl[chunk - 1:chunk, ls]
            o = o2[p][:chunk] + o2[p][chunk:]
            mu = head_sum(o) * (1.0 / RW_HEAD)
            oc = o - mu
            var = head_sum(oc * oc) * (1.0 / RW_HEAD)
            on = oc * lax.rsqrt(var + RW_GN_EPS) * gw_ref[:, ls] + gb_ref[:, ls]
            bonus = head_sum(r[:, ls] * k[:, ls] * rk_ref[:, ls]) * v[:, ls]
            o_ref[sl, ls] = on + bonus

    @pl.when(t == pl.num_programs(1) - 1)
    def _():
        for p in range(RW_PAIRS):
            s = s_scr[p]
            sout_ref[2 * p] = s[:RW_HEAD, :RW_HEAD]
            sout_ref[2 * p + 1] = s[RW_HEAD:, RW_HEAD:]


def _rwkv(grp, seqs, gn_w, gn_b, r_k, s0):
    B, T = grp.B, grp.T
    tile = min(T, CHUNK)
    chunk = tile
    W = RW_HEADS * RW_HEAD
    seq_spec = pl.BlockSpec((None, tile, W), lambda b, t: (b, t, 0))
    vec_spec = pl.BlockSpec((1, W), lambda b, t: (0, 0))
    st_spec = pl.BlockSpec((None, RW_HEADS, RW_HEAD, RW_HEAD), lambda b, t: (b, 0, 0, 0))
    kern = functools.partial(_rwkv_kernel, chunk=chunk, n_chunks=tile // chunk)
    return pl.pallas_call(
        kern,
        grid=(B, T // tile),
        in_specs=[seq_spec] * 6 + [vec_spec] * 3 + [st_spec],
        out_specs=[seq_spec, st_spec],
        out_shape=[jax.ShapeDtypeStruct((B, T, W), F32),
                   jax.ShapeDtypeStruct((B, RW_HEADS, RW_HEAD, RW_HEAD), F32)],
        scratch_shapes=[pltpu.VMEM((RW_PAIRS, 2 * RW_HEAD, 2 * RW_HEAD), F32)],
        compiler_params=_params(("arbitrary", "arbitrary")),
        name="rwkv_scan",
    )(*seqs, gn_w, gn_b, r_k, s0)


def _merge_kernel(og_ref, gg_ref, orw_ref, gate_ref, mga_ref, mgb_ref, x_ref,
                  gate1_ref, shift2_ref, scale2_ref,
                  ggn_ref, gpost_ref, gpre_ref, wbg_ref, wbr_ref, wout_ref, wq_ref,
                  x1_ref, h2_ref, qp_ref):
    gg = gg_ref[...]
    gla_in = og_ref[...] * ggn_ref[...] * (gg * jax.nn.sigmoid(gg))
    br_gla = _mm(gla_in, wbg_ref[...])
    br_rw = _mm(orw_ref[...] * gate_ref[...], wbr_ref[...])
    merged = jax.nn.sigmoid(mga_ref[...]) * br_gla + jax.nn.sigmoid(mgb_ref[...]) * br_rw
    y = _mm(merged, wout_ref[...])
    x1 = x_ref[...] + gate1_ref[...] * _rms(y, gpost_ref[...])
    x1_ref[...] = x1
    h2 = (_rms(x1, gpre_ref[...]) * (1.0 + scale2_ref[...]) + shift2_ref[...]).astype(BF16)
    h2_ref[...] = h2
    qp_ref[...] = jnp.dot(h2, wq_ref[...], preferred_element_type=F32)


def _merge(grp, og, p2, orw, gate_rw, x2d, modop, wts):
    tm, M = min(grp.tm, MERGE_TILE), grp.M
    W = D_MODEL

    def tok(blk):
        return pl.BlockSpec((tm, W), lambda i: (i, blk))

    def full(shape):
        return pl.BlockSpec(shape, lambda i: (0,) * len(shape))

    return pl.pallas_call(
        _merge_kernel,
        grid=(M // tm,),
        in_specs=[tok(0), tok(COL_GG // W), tok(0), tok(0), tok(COL_MG // W), tok(COL_MG // W + 1), tok(0),
                  grp.mod_spec(2, tm), grp.mod_spec(3, tm), grp.mod_spec(4, tm),
                  full((1, W)), full((1, W)), full((1, W)),
                  full((W, W)), full((W, W)), full((W, W)), full((W, 2 * W))],
        out_specs=[tok(0), tok(0), pl.BlockSpec((tm, 2 * W), lambda i: (i, 0))],
        out_shape=[jax.ShapeDtypeStruct((M, W), F32), jax.ShapeDtypeStruct((M, W), BF16),
                   jax.ShapeDtypeStruct((M, 2 * W), F32)],
        compiler_params=_params(("arbitrary",)),
        name="merge",
    )(og, p2, orw, gate_rw, p2, p2, x2d, modop, modop, modop,
      wts["g_gla_norm"], wts["g_post_mix"], wts["g_pre_ffn"],
      wts["w_b_gla"], wts["w_b_rw"], wts["w_out"], wts["peer_wq"])


def _top_values(s, n, with_rank=False):
    rows = lax.broadcasted_iota(jnp.int32, (n, s.shape[1]), 0)
    vals = jnp.zeros((n, s.shape[1]), F32)
    rank = jnp.full(s.shape, float(n), F32) if with_rank else None
    for i in range(n):
        m = jnp.max(s, axis=0, keepdims=True)
        vals = jnp.where(rows == i, m, vals)
        hit = s == m
        if with_rank:
            rank = jnp.where(hit, float(i), rank)
        s = jnp.where(hit, NEG_INF, s)
    return vals, rank


def _candidate_groups(v1, v2):
    n, tm = v1.shape
    split = 4
    r_all = lax.broadcasted_iota(jnp.int32, (n, tm), 0)
    r_half = lax.broadcasted_iota(jnp.int32, (n // 2, tm), 0)
    groups = [v1[0:1, :] + v2]
    for a in range(1, split):
        groups.append(jnp.where(r_half < n // (a + 1), v1[a:a + 1, :] + v2[:n // 2, :], NEG_INF))
    groups.append(jnp.where(r_all >= split, v1 + v2[0:1, :], NEG_INF))
    for b in range(1, n // (split + 1)):
        keep = (r_half >= split) & (r_half < n // (b + 1))
        groups.append(jnp.where(keep, v1[:n // 2, :] + v2[b:b + 1, :], NEG_INF))
    return groups


def _group_max(groups, n):
    full = [g for g in groups if g.shape[0] == n]
    best = full[0]
    for g in full[1:]:
        best = jnp.maximum(best, g)
    best = jnp.maximum(best[:n // 2, :], best[n // 2:, :])
    for g in groups:
        if g.shape[0] != n:
            best = jnp.maximum(best, g)
    return jnp.max(best, axis=0, keepdims=True)


def _select_kernel(qp_ref, keys_ref, r2_ref, e2_ref, al_ref, n1_ref):
    tm = qp_ref.shape[0]
    for h in range(PEER_HEADS):
        q1 = qp_ref[:, (2 * h) * 128:(2 * h + 1) * 128]
        q2 = qp_ref[:, (2 * h + 1) * 128:(2 * h + 2) * 128]
        s1 = _mm_nt(keys_ref[2 * h], q1)
        s2 = _mm_nt(keys_ref[2 * h + 1], q2)
        v1, _ = _top_values(s1, PEER_TOPK)
        v2, rank2 = _top_values(s2, PEER_TOPK, with_rank=True)
        groups = _candidate_groups(v1, v2)
        cmax = v1[0:1, :] + v2[0:1, :]
        z = jnp.zeros((1, tm), F32)
        m = cmax
        for _ in range(PEER_TOPK):
            m = _group_max(groups, PEER_TOPK)
            z = z + jnp.exp(m - cmax)
            groups = [jnp.where(g == m, NEG_INF, g) for g in groups]
        th = m - s1
        n1 = jnp.zeros_like(s1)
        for b in range(PEER_TOPK):
            n1 = n1 + jnp.where(v2[b:b + 1, :] >= th, 1.0, 0.0)
        r2_ref[h] = rank2.astype(BF16)
        e2_ref[h] = jnp.exp(s2 - v2[0:1, :]).astype(BF16)
        al_ref[h] = jnp.exp(s1 - v1[0:1, :]) / z
        n1_ref[h] = n1


def _peer_select(grp, qp, keys):
    tm, M = grp.tm, grp.M
    rows = pl.BlockSpec((PEER_HEADS, PEER_NKEYS, tm), lambda i: (0, 0, i))
    return pl.pallas_call(
        _select_kernel,
        grid=(M // tm,),
        in_specs=[pl.BlockSpec((tm, 2 * D_MODEL), lambda i: (i, 0)),
                  pl.BlockSpec((2 * PEER_HEADS, PEER_NKEYS, 128), lambda i: (0, 0, 0))],
        out_specs=[rows] * 4,
        out_shape=[jax.ShapeDtypeStruct((PEER_HEADS, PEER_NKEYS, M), BF16)] * 2
        + [jax.ShapeDtypeStruct((PEER_HEADS, PEER_NKEYS, M), F32)] * 2,
        compiler_params=_params(("arbitrary",)),
        name="peer_select",
    )(qp, keys)


def _peer_kernel(h2_ref, u_ref, v_ref, r2_ref, e2_ref, al_ref, n1_ref, x1_ref, gate2_ref, gpost_ref,
                 o_ref, acc_ref, y_scr):
    e = pl.program_id(1)
    tm = h2_ref.shape[0]

    @pl.when(e == 0)
    def _():
        acc_ref[...] = jnp.zeros_like(acc_ref)

    act = lax.dot_general(u_ref[...], h2_ref[...], (((1,), (1,)), ((), ())),
                          preferred_element_type=F32)
    n_sub = PEER_EXPERT_TILE // PEER_NKEYS
    for j in range(n_sub):
        rows = slice(j * PEER_NKEYS, (j + 1) * PEER_NKEYS)
        a = act[rows, :].astype(BF16)
        gelu = 0.5 * a * (1.0 + lax.erf(a * 0.7071067811865476))
        w = jnp.zeros((PEER_NKEYS, tm), BF16)
        for h in range(PEER_HEADS):
            al = al_ref[h, j:j + 1, :].astype(BF16)
            n1 = n1_ref[h, j:j + 1, :].astype(BF16)
            w = w + al * jnp.where(r2_ref[h] < n1, e2_ref[h], 0.0)
        y_scr[rows, :] = gelu * w
    acc_ref[...] += lax.dot_general(y_scr[...], v_ref[...], (((0,), (0,)), ((), ())),
                                    preferred_element_type=F32)

    @pl.when(e == pl.num_programs(1) - 1)
    def _():
        o_ref[...] = x1_ref[...] + gate2_ref[...] * _rms(acc_ref[...], gpost_ref[...])


def _peer(grp, h2, sel, x1, modop, g_post, pu, pv):
    tm, M = grp.tm, grp.M
    te = PEER_EXPERT_TILE
    n_sub = te // PEER_NKEYS
    r2, e2, al, n1 = sel
    packed = pl.BlockSpec((PEER_HEADS, PEER_NKEYS, tm), lambda i, e: (0, 0, i))
    rows = pl.BlockSpec((PEER_HEADS, n_sub, tm), lambda i, e: (0, e, i))
    tok = pl.BlockSpec((tm, D_MODEL), lambda i, e: (i, 0))
    return pl.pallas_call(
        _peer_kernel,
        grid=(M // tm, pu.shape[0] // te),
        in_specs=[tok,
                  pl.BlockSpec((te, D_MODEL), lambda i, e: (e, 0)),
                  pl.BlockSpec((te, D_MODEL), lambda i, e: (e, 0)),
                  packed, packed, rows, rows, tok, grp.mod_spec(5),
                  pl.BlockSpec((1, D_MODEL), lambda i, e: (0, 0))],
        out_specs=tok,
        out_shape=jax.ShapeDtypeStruct((M, D_MODEL), F32),
        scratch_shapes=[pltpu.VMEM((tm, D_MODEL), F32), pltpu.VMEM((te, tm), BF16)],
        compiler_params=_params(("arbitrary", "arbitrary")),
        name="peer_dense",
    )(h2, pu, pv, r2, e2, al, n1, x1, modop, g_post)


def _prepare_weights(w_in, w_gla_gate, b_gla_gate, g_gla_norm, mu_shift, rw_w0, rw_w2, rw_a0, rw_a2,
                     rw_g2, rw_k_k, rw_k_a, rw_r_k, rw_gn_w, rw_gn_b, w_b_gla, w_b_rw, w_out,
                     peer_wq, peer_keys, peer_u, peer_v, g_pre_mix, g_post_mix, g_pre_ffn, g_post_ffn):
    W = RW_HEADS * RW_HEAD
    rw = w_in[:, 3088:6416]
    w_proj = jnp.concatenate(
        [rw[:, :3 * W], w_in[:, 6416:8464], w_in[:, 2048:3072], w_in[:, 1024:2048], w_in[:, 0:1024],
         rw[:, 3 * W:], w_in[:, 3072:3088], jnp.zeros((D_MODEL, PROJ_PAD_W - 8464), F32)],
        axis=1).astype(BF16)
    head = jnp.arange(W) // RW_HEAD
    zeros64 = jnp.zeros((64, W), F32)
    return dict(
        w_proj=w_proj,
        wg_pad=jnp.concatenate([w_gla_gate, jnp.zeros((128 - GLA_GATE_RANK, 512), F32)], axis=0),
        bg=b_gla_gate.reshape(1, -1),
        g_gla_norm=g_gla_norm.reshape(1, -1),
        mu_rkv=mu_shift[:3 * W].reshape(1, -1),
        mu_lora=mu_shift[3 * W:].reshape(1, -1),
        rw_w0=rw_w0.reshape(1, -1),
        w2p=jnp.concatenate([rw_w2, jnp.zeros((192, W), F32)], axis=0),
        rw_a0=rw_a0.reshape(1, -1),
        a2p=jnp.concatenate([zeros64, rw_a2, jnp.zeros((128, W), F32)], axis=0),
        g2p=jnp.concatenate([jnp.zeros((128, W), F32), rw_g2], axis=0),
        rw_k_k=rw_k_k.reshape(1, -1),
        rw_k_a=rw_k_a.reshape(1, -1),
        rw_r_k=rw_r_k.reshape(1, -1),
        rw_gn_w=rw_gn_w.reshape(1, -1),
        rw_gn_b=rw_gn_b.reshape(1, -1),
        head_ones=(head[:, None] == head[None, :]).astype(BF16),
        w_b_gla=w_b_gla.astype(BF16),
        w_b_rw=w_b_rw.astype(BF16),
        w_out=w_out.astype(BF16),
        peer_wq=peer_wq.astype(BF16),
        peer_keys=peer_keys.reshape(2 * PEER_HEADS, PEER_NKEYS, 128).astype(BF16),
        peer_u=peer_u.astype(BF16),
        peer_v=peer_v.astype(BF16),
        g_pre_mix=g_pre_mix.reshape(1, -1),
        g_post_mix=g_post_mix.reshape(1, -1),
        g_pre_ffn=g_pre_ffn.reshape(1, -1),
        g_post_ffn=g_post_ffn.reshape(1, -1),
    )


def _block(x, mod, s_gla0, s_rw0, shift0, wts, valid):
    B, T, _ = x.shape
    grp = _Group(B, T, valid)
    W = RW_HEADS * RW_HEAD
    x2d = x.reshape(grp.M, D_MODEL)
    modop = grp.mod_operand(mod)
    p2 = _projection(grp, x2d, modop, wts["g_pre_mix"], wts["w_proj"])
    p3 = p2.reshape(B, T, PROJ_PAD_W)
    og, s_gla = _gla(grp, p3, wts["wg_pad"], wts["bg"], s_gla0)
    sh_rkv = shift0[:, None, :3 * W]
    sh_lora = shift0[:, None, 3 * W:]
    r, k, v, an, bb, lw, gate = _rwkv_prep(grp, p3, sh_rkv, sh_lora, wts)
    orw, s_rw = _rwkv(grp, (r, k, v, an, bb, lw), wts["rw_gn_w"], wts["rw_gn_b"], wts["rw_r_k"], s_rw0)
    x1, h2, qp = _merge(grp, og.reshape(grp.M, -1), p2, orw.reshape(grp.M, W), gate.reshape(grp.M, W),
                        x2d, modop, wts)
    sel = _peer_select(grp, qp, wts["peer_keys"])
    out = _peer(grp, h2, sel, x1, modop, wts["g_post_ffn"], wts["peer_u"], wts["peer_v"])
    last = p3[:, valid - 1, :]
    shift_new = jnp.concatenate([last[:, :3 * W], last[:, COL_LORA:COL_LORA + 256]], axis=-1)
    return out.reshape(B, T, D_MODEL), s_gla, s_rw, shift_new


def kernel(x_prompt, x_sample, c_prompt, c_sample, state_gla, state_rwkv, state_shift, w_ada, b_ada, g_pre_mix, g_post_mix, g_pre_ffn, g_post_ffn, w_in, w_gla_gate, b_gla_gate, g_gla_norm, mu_shift, rw_w0, rw_w2, rw_a0, rw_a2, rw_g2, rw_k_k, rw_k_a, rw_r_k, rw_gn_w, rw_gn_b, w_b_gla, w_b_rw, w_out, peer_wq, peer_keys, peer_u, peer_v):
    depth = w_in.shape[0]
    bp, bs = x_prompt.shape[0], x_sample.shape[0]
    ts = x_sample.shape[1]
    ts_pad = -(-ts // 8) * 8
    xp = x_prompt
    xs = jnp.pad(x_sample, ((0, 0), (0, ts_pad - ts), (0, 0)))
    outs = [[] for _ in range(6)]
    for l in range(depth):
        wts = _prepare_weights(
            w_in[l], w_gla_gate[l], b_gla_gate[l], g_gla_norm[l], mu_shift[l], rw_w0[l], rw_w2[l],
            rw_a0[l], rw_a2[l], rw_g2[l], rw_k_k[l], rw_k_a[l], rw_r_k[l], rw_gn_w[l], rw_gn_b[l],
            w_b_gla[l], w_b_rw[l], w_out[l], peer_wq[l], peer_keys[l], peer_u[l], peer_v[l],
            g_pre_mix[l], g_post_mix[l], g_pre_ffn[l], g_post_ffn[l])
        mod = _modulation(jnp.concatenate([c_prompt, c_sample], axis=0), w_ada[l], b_ada[l])
        zg = jnp.zeros((bp, GLA_HEADS, GLA_DK, GLA_DV), F32)
        zr = jnp.zeros((bp, RW_HEADS, RW_HEAD, RW_HEAD), F32)
        zs = jnp.zeros((bp, RW_SHIFT_W), F32)
        xp, g1, r1, s1 = _block(xp, mod[:bp], zg, zr, zs, wts, xp.shape[1])
        xs, g2, r2, s2 = _block(xs, mod[bp:], state_gla[l], state_rwkv[l], state_shift[l], wts, ts)
        for acc, val in zip(outs, (g1, r1, s1, g2, r2, s2)):
            acc.append(val)
    stacked = [jnp.stack(o) for o in outs]
    return (xp, xs[:, :ts], *stacked)
```

```python
import functools

import jax
import jax.numpy as jnp
from jax import lax
from jax.experimental import pallas as pl
from jax.experimental.pallas import tpu as pltpu

F32 = jnp.float32
BF16 = jnp.bfloat16

D_MODEL = 1024
GLA_HEADS = 4
GLA_DK = 128
GLA_DV = 256
GLA_GATE_RANK = 16
GLA_GATE_TAU = 16.0
GLA_NORM_EPS = 1e-5
RW_HEAD = 64
RW_HEADS = 16
RW_PAIRS = RW_HEADS // 2
RW_GN_EPS = 64e-5
RW_SHIFT_W = 3328
PEER_HEADS = 8
PEER_NKEYS = 128
PEER_TOPK = 16
NORM_EPS = 1e-6
LANES = 128

COL_R, COL_K, COL_V = 0, 1024, 2048
COL_MG = 3072
COL_GG = 5120
COL_GV = 6144
COL_GQ = 7168
COL_GK = 7680
COL_LORA = 8192
COL_GA = 8448
PROJ_PAD_W = 8704
PROJ_TILE_N = 2176

CHUNK = 64
TOKEN_TILE = 512
MERGE_TILE = 256
PEER_EXPERT_TILE = 512
PEER_SUBTILES = 4
VMEM_LIMIT = 56 * 1024 * 1024

NEG_INF = float("-inf")


def _mm(a, b):
    return jnp.dot(a.astype(BF16), b.astype(BF16), preferred_element_type=F32)


def _mm_nt(a, b):
    return lax.dot_general(a.astype(BF16), b.astype(BF16), (((1,), (1,)), ((), ())),
                           preferred_element_type=F32)


def _mm_tn(a, b):
    return lax.dot_general(a.astype(BF16), b.astype(BF16), (((0,), (0,)), ((), ())),
                           preferred_element_type=F32)


def _split(x):
    hi = x.astype(BF16)
    lo = (x - hi.astype(F32)).astype(BF16)
    return hi, lo


def _mm_exact_lhs(a, x):
    hi, lo = _split(x)
    a = a.astype(BF16)
    return (jnp.dot(a, hi, preferred_element_type=F32)
            + jnp.dot(a, lo, preferred_element_type=F32))


def _mm_exact_rhs(x, a):
    hi, lo = _split(x)
    a = a.astype(BF16)
    return (jnp.dot(hi, a, preferred_element_type=F32)
            + jnp.dot(lo, a, preferred_element_type=F32))


def _softplus(x):
    return jnp.maximum(x, 0.0) + jnp.log1p(jnp.exp(-jnp.abs(x)))


def _rms(x, g):
    return x * lax.rsqrt(jnp.mean(x * x, axis=-1, keepdims=True) + NORM_EPS) * g


def _params(sem, flags=None):
    return pltpu.CompilerParams(dimension_semantics=sem, vmem_limit_bytes=VMEM_LIMIT, flags=flags)


def _mod_kernel(c_ref, w_ref, b_ref, o_ref):
    c = c_ref[...]
    o_ref[...] = _mm(c * jax.nn.sigmoid(c), w_ref[...]) + b_ref[...]


def _modulation(c, w_ada, b_ada):
    rows = c.shape[0]
    n = w_ada.shape[1]
    tn = 1536
    return pl.pallas_call(
        _mod_kernel,
        grid=(n // tn,),
        in_specs=[pl.BlockSpec((rows, D_MODEL), lambda j: (0, 0)),
                  pl.BlockSpec((D_MODEL, tn), lambda j: (0, j)),
                  pl.BlockSpec((1, tn), lambda j: (0, j))],
        out_specs=pl.BlockSpec((rows, tn), lambda j: (0, j)),
        out_shape=jax.ShapeDtypeStruct((rows, n), F32),
        compiler_params=_params(("arbitrary",)),
        name="adaln_mod",
    )(c, w_ada, b_ada.reshape(1, n))


class _Group:
    def __init__(self, batch, seq, valid):
        self.B, self.T, self.Tv = batch, seq, valid
        self.M = batch * seq
        self.tm = min(TOKEN_TILE, self.M)
        self.per_batch_mod = seq % self.tm == 0

    def mod_operand(self, mod):
        if self.per_batch_mod:
            return mod.reshape(self.B, 1, 6 * D_MODEL)
        return jnp.repeat(mod, self.T, axis=0)

    def mod_spec(self, col, tm=None, tile_index=None):
        tm, T = tm or self.tm, self.T
        tile = tile_index or (lambda i, *_: i)
        if self.per_batch_mod:
            return pl.BlockSpec((None, 1, D_MODEL), lambda *g: ((tile(*g) * tm) // T, 0, col))
        return pl.BlockSpec((tm, D_MODEL), lambda *g: (tile(*g), col))


def _proj_kernel(x_ref, shift_ref, scale_ref, g_ref, w_ref, p_ref, h_scr):
    @pl.when(pl.program_id(1) == 0)
    def _():
        h = _rms(x_ref[...], g_ref[...]) * (1.0 + scale_ref[...]) + shift_ref[...]
        h_scr[...] = h.astype(BF16)

    p_ref[...] = jnp.dot(h_scr[...], w_ref[...], preferred_element_type=F32)


def _projection(grp, x2d, modop, g_pre, w_proj):
    tm = grp.tm
    return pl.pallas_call(
        _proj_kernel,
        grid=(grp.M // tm, PROJ_PAD_W // PROJ_TILE_N),
        in_specs=[pl.BlockSpec((tm, D_MODEL), lambda i, j: (i, 0)),
                  grp.mod_spec(0), grp.mod_spec(1),
                  pl.BlockSpec((1, D_MODEL), lambda i, j: (0, 0)),
                  pl.BlockSpec((D_MODEL, PROJ_TILE_N), lambda i, j: (0, j))],
        out_specs=pl.BlockSpec((tm, PROJ_TILE_N), lambda i, j: (i, j)),
        out_shape=jax.ShapeDtypeStruct((grp.M, PROJ_PAD_W), F32),
        scratch_shapes=[pltpu.VMEM((tm, D_MODEL), BF16)],
        compiler_params=_params(("arbitrary", "arbitrary")),
        name="in_proj",
    )(x2d, modop, modop, g_pre, w_proj)


def _gla_kernel(q_ref, k_ref, v_ref, ga_ref, wg_ref, bg_ref, s0_ref, og_ref, sout_ref, st_scr,
                *, chunk, n_chunks, tile, valid, seqs):
    t = pl.program_id(1)
    last = pl.num_programs(1) - 1
    row = lax.broadcasted_iota(jnp.int32, (chunk, chunk), 0)
    col = lax.broadcasted_iota(jnp.int32, (chunk, chunk), 1)
    causal = row >= col
    tril = jnp.where(causal, 1.0, 0.0).astype(BF16)

    def per_seq(b, carry):
        @pl.when(t == 0)
        def _():
            for h in range(GLA_HEADS):
                st_scr[b, h] = s0_ref[b, h].T

        for c in range(n_chunks):
            sl = pl.ds(c * chunk, chunk)
            z = _mm(ga_ref[b, sl, :], wg_ref[...]) + bg_ref[...]
            la = -_softplus(-z) / GLA_GATE_TAU
            k = k_ref[b, sl, :]
            if valid is not None:
                pos = t * tile + c * chunk + lax.broadcasted_iota(jnp.int32, (chunk, 1), 0)
                ok = pos < valid
                la = jnp.where(ok, la, 0.0)
                k = jnp.where(ok, k, 0.0)
            cum = _mm_exact_lhs(tril, la)
            end = cum[chunk - 1:chunk, :]
            qe = q_ref[b, sl, :] * (GLA_DK ** -0.5) * jnp.exp(cum)
            ke = k * jnp.exp(-cum)
            kd = k * jnp.exp(end - cum)
            decay = jnp.exp(end)
            for h in range(GLA_HEADS):
                ks = slice(h * GLA_DK, (h + 1) * GLA_DK)
                vs = slice(h * GLA_DV, (h + 1) * GLA_DV)
                v = v_ref[b, sl, vs]
                if valid is not None:
                    v = jnp.where(ok, v, 0.0)
                a = jnp.where(causal, _mm_nt(qe[:, ks], ke[:, ks]), 0.0)
                st = st_scr[b, h]
                o = _mm_nt(qe[:, ks], st) + _mm(a, v)
                st_scr[b, h] = st * decay[:, ks] + _mm_tn(v, kd[:, ks])
                og_ref[b, sl, vs] = o * lax.rsqrt(jnp.mean(o * o, axis=-1, keepdims=True) + GLA_NORM_EPS)

        @pl.when(t == last)
        def _():
            for h in range(GLA_HEADS):
                sout_ref[b, h] = st_scr[b, h].T

        return carry

    if seqs == 1:
        per_seq(0, 0)
    else:
        lax.fori_loop(0, seqs, per_seq, 0)


def _seqs_per_step(batch, seq):
    n = max(1, min(8, 64 // seq))
    while batch % n:
        n //= 2
    return n


def _gla(grp, p3, wg_pad, bg, s0):
    B, T = grp.B, grp.T
    tile = min(T, 4 * CHUNK)
    chunk = min(CHUNK, tile)
    nb = _seqs_per_step(B, T)
    kern = functools.partial(_gla_kernel, chunk=chunk, n_chunks=tile // chunk, tile=tile,
                             valid=None if grp.Tv == T else grp.Tv, seqs=nb)
    kw, vw = GLA_HEADS * GLA_DK, GLA_HEADS * GLA_DV
    st_spec = pl.BlockSpec((nb, GLA_HEADS, GLA_DK, GLA_DV), lambda i, t: (i, 0, 0, 0))
    return pl.pallas_call(
        kern,
        grid=(B // nb, T // tile),
        in_specs=[pl.BlockSpec((nb, tile, kw), lambda i, t: (i, t, COL_GQ // kw)),
                  pl.BlockSpec((nb, tile, kw), lambda i, t: (i, t, COL_GK // kw)),
                  pl.BlockSpec((nb, tile, vw), lambda i, t: (i, t, COL_GV // vw)),
                  pl.BlockSpec((nb, tile, LANES), lambda i, t: (i, t, COL_GA // LANES)),
                  pl.BlockSpec((LANES, kw), lambda i, t: (0, 0)),
                  pl.BlockSpec((1, kw), lambda i, t: (0, 0)),
                  st_spec],
        out_specs=[pl.BlockSpec((nb, tile, vw), lambda i, t: (i, t, 0)), st_spec],
        out_shape=[jax.ShapeDtypeStruct((B, T, vw), F32),
                   jax.ShapeDtypeStruct((B, GLA_HEADS, GLA_DK, GLA_DV), F32)],
        scratch_shapes=[pltpu.VMEM((nb, GLA_HEADS, GLA_DV, GLA_DK), F32)],
        compiler_params=_params(("arbitrary", "arbitrary")),
        name="gla_scan",
    )(p3, p3, p3, p3, wg_pad, bg, s0)


def _rwprep_kernel(r_ref, k_ref, v_ref, l_ref, pr_ref, pk_ref, pv_ref, pl_ref,
                   s0_ref, s0l_ref, mu_ref, mul_ref, w0_ref, w2_ref, a0_ref, a2_ref, g2_ref,
                   kk_ref, ka_ref, bd_ref,
                   ro_ref, ko_ref, vo_ref, an_ref, bb_ref, lw_ref, gate_ref,
                   *, tile, valid):
    t = pl.program_id(1)
    first = lax.broadcasted_iota(jnp.int32, (tile, 1), 0) == 0

    def shifted(cur_ref, prev_ref, s0, mu):
        cur = cur_ref[...]
        carry = jnp.where(t == 0, s0, prev_ref[7:8, :])
        prev = jnp.where(first, carry, pltpu.roll(cur, 1, 0))
        return cur + (prev - cur) * mu

    r = shifted(r_ref, pr_ref, s0_ref[:, 0:1024], mu_ref[:, 0:1024])
    k = shifted(k_ref, pk_ref, s0_ref[:, 1024:2048], mu_ref[:, 1024:2048])
    v = shifted(v_ref, pv_ref, s0_ref[:, 2048:3072], mu_ref[:, 2048:3072])
    lo = shifted(l_ref, pl_ref, s0l_ref[...], mul_ref[...])

    wpre = w0_ref[...] + _mm(jnp.tanh(lo), w2_ref[...])
    wlog = -_softplus(-wpre) - 0.5
    lw = -jnp.exp(wlog)
    a = jax.nn.sigmoid(a0_ref[...] + _mm(lo, a2_ref[...]))
    gate = _mm(jax.nn.sigmoid(lo), g2_ref[...])

    kk = k * kk_ref[...]
    ss = _mm_exact_rhs(kk * kk, bd_ref[...])
    kk = kk / jnp.maximum(jnp.sqrt(ss), 1e-12)
    kmod = k * (1.0 + (a - 1.0) * ka_ref[...])
    an = -kk
    bb = kk * a
    if valid is not None:
        ok = (t * tile + lax.broadcasted_iota(jnp.int32, (tile, 1), 0)) < valid
        lw = jnp.where(ok, lw, 0.0)
        an = jnp.where(ok, an, 0.0)
        bb = jnp.where(ok, bb, 0.0)
        kmod = jnp.where(ok, kmod, 0.0)
        v = jnp.where(ok, v, 0.0)
    ro_ref[...] = r
    ko_ref[...] = kmod
    vo_ref[...] = v
    an_ref[...] = an
    bb_ref[...] = bb
    lw_ref[...] = lw
    gate_ref[...] = gate


def _rwkv_prep(grp, p3, sh_rkv, sh_lora, wts):
    B, T = grp.B, grp.T
    tile = min(T, 256)
    tb = tile // 8
    W = RW_HEADS * RW_HEAD

    def cur(width, blk):
        return pl.BlockSpec((None, tile, width), lambda b, t: (b, t, blk))

    def prev(width, blk):
        return pl.BlockSpec((None, 8, width), lambda b, t: (b, jnp.maximum(t * tb - 1, 0), blk))

    def full(shape):
        return pl.BlockSpec(shape, lambda b, t: (0,) * len(shape))

    out_spec = pl.BlockSpec((None, tile, W), lambda b, t: (b, t, 0))
    kern = functools.partial(_rwprep_kernel, tile=tile, valid=None if grp.Tv == T else grp.Tv)
    lora_blk = COL_LORA // 256
    return pl.pallas_call(
        kern,
        grid=(B, T // tile),
        in_specs=[cur(W, 0), cur(W, 1), cur(W, 2), cur(256, lora_blk),
                  prev(W, 0), prev(W, 1), prev(W, 2), prev(256, lora_blk),
                  pl.BlockSpec((None, 1, 3 * W), lambda b, t: (b, 0, 0)),
                  pl.BlockSpec((None, 1, 256), lambda b, t: (b, 0, 0)),
                  full((1, 3 * W)), full((1, 256)),
                  full((1, W)), full((256, W)), full((1, W)), full((256, W)), full((256, W)),
                  full((1, W)), full((1, W)), full((W, W))],
        out_specs=[out_spec] * 7,
        out_shape=[jax.ShapeDtypeStruct((B, T, W), F32)] * 7,
        compiler_params=_params(("arbitrary", "arbitrary")),
        name="rwkv_prep",
    )(p3, p3, p3, p3, p3, p3, p3, p3, sh_rkv, sh_lora,
      wts["mu_rkv"], wts["mu_lora"], wts["rw_w0"], wts["w2p"], wts["rw_a0"], wts["a2p"], wts["g2p"],
      wts["rw_k_k"], wts["rw_k_a"], wts["head_ones"])


def _rwkv_pairs_chunk(at, rt, bt, kt, v, states, *, chunk, same_head, strict, incl):
    n2 = 2 * chunk
    wide = n2 % LANES == 0
    pairs = range(len(states))

    def blocks(x):
        return jnp.where(same_head, jnp.concatenate([x, x], axis=0), 0.0).astype(BF16)

    a2 = [blocks(x) for x in at]
    r2 = [blocks(x) for x in rt]
    b2 = [blocks(x) for x in bt]
    k2 = [blocks(x) for x in kt]
    v2 = [blocks(x) for x in v]
    ar = [jnp.concatenate([a2[p], r2[p]], axis=0) for p in pairs]
    bk = [jnp.concatenate([b2[p], k2[p]], axis=0) for p in pairs]
    if wide:
        g = [_mm_nt(ar[p], bk[p]) for p in pairs]
        l_ab = [jnp.where(strict, g[p][:n2, :n2], 0.0) for p in pairs]
        l_ak = [jnp.where(strict, g[p][:n2, n2:], 0.0) for p in pairs]
        m_rb = [jnp.where(incl, g[p][n2:, :n2], 0.0) for p in pairs]
        m_rk = [jnp.where(incl, g[p][n2:, n2:], 0.0) for p in pairs]
    else:
        l_ab = [jnp.where(strict, _mm_nt(a2[p], b2[p]), 0.0) for p in pairs]
        l_ak = [jnp.where(strict, _mm_nt(a2[p], k2[p]), 0.0) for p in pairs]
        m_rb = [jnp.where(incl, _mm_nt(r2[p], b2[p]), 0.0) for p in pairs]
        m_rk = [jnp.where(incl, _mm_nt(r2[p], k2[p]), 0.0) for p in pairs]
    from_state = [_mm_nt(ar[p], states[p]) for p in pairs]
    x = [from_state[p][:n2] + _mm(l_ak[p], v2[p]) for p in pairs]
    pw = l_ab
    for _ in range(chunk.bit_length() - 2):
        if wide:
            y = [_mm(pw[p], jnp.concatenate([x[p], pw[p]], axis=1)) for p in pairs]
            x = [x[p] + y[p][:, :LANES] for p in pairs]
            pw = [y[p][:, LANES:] for p in pairs]
        else:
            x = [x[p] + _mm(pw[p], x[p]) for p in pairs]
            pw = [_mm(pw[p], pw[p]) for p in pairs]
    u = [x[p] + _mm(pw[p], x[p]) for p in pairs]
    uv = [jnp.concatenate([u[p].astype(BF16), v2[p]], axis=0) for p in pairs]
    if wide:
        o2 = [from_state[p][n2:] + _mm(jnp.concatenate([m_rb[p], m_rk[p]], axis=1), uv[p]) for p in pairs]
    else:
        o2 = [from_state[p][n2:] + _mm(m_rb[p], u[p]) + _mm(m_rk[p], v2[p]) for p in pairs]
    s_new = [states[p] + _mm_tn(uv[p], bk[p]) for p in pairs]
    return o2, s_new


def _rwkv_kernel(r_ref, k_ref, v_ref, an_ref, bb_ref, lw_ref, gw_ref, gb_ref, rk_ref, s0_ref,
                 o_ref, sout_ref, s_scr, *, chunk, n_chunks):
    t = pl.program_id(1)
    zeros = jnp.zeros((RW_HEAD, RW_HEAD), F32)

    @pl.when(t == 0)
    def _():
        for p in range(RW_PAIRS):
            top = jnp.concatenate([s0_ref[2 * p], zeros], axis=1)
            bot = jnp.concatenate([zeros, s0_ref[2 * p + 1]], axis=1)
            s_scr[p] = jnp.concatenate([top, bot], axis=0)

    n2 = 2 * chunk
    row = lax.broadcasted_iota(jnp.int32, (n2, n2), 0)
    col = lax.broadcasted_iota(jnp.int32, (n2, n2), 1)
    def second(idx, size):
        return jnp.where(idx >= size, 1, 0)

    same = second(row, chunk) == second(col, chunk)
    strict = same & (row > col)
    incl = same & (row >= col)
    same_head = (second(lax.broadcasted_iota(jnp.int32, (n2, LANES), 0), chunk)
                 == second(lax.broadcasted_iota(jnp.int32, (n2, LANES), 1), RW_HEAD))
    trow = lax.broadcasted_iota(jnp.int32, (chunk, chunk), 0)
    tcol = lax.broadcasted_iota(jnp.int32, (chunk, chunk), 1)
    tril = jnp.where(trow >= tcol, 1.0, 0.0).astype(BF16)
    first_head = lax.broadcasted_iota(jnp.int32, (chunk, LANES), 1) < RW_HEAD

    def head_sum(x):
        lo = jnp.sum(jnp.where(first_head, x, 0.0), axis=-1, keepdims=True)
        hi = jnp.sum(jnp.where(first_head, 0.0, x), axis=-1, keepdims=True)
        return jnp.where(first_head, lo, hi)

    for c in range(n_chunks):
        sl = pl.ds(c * chunk, chunk)
        lw = lw_ref[sl, :]
        cum = _mm_exact_lhs(tril, lw)
        w_incl = jnp.exp(cum)
        w_inv = jnp.exp(-cum)
        r = r_ref[sl, :]
        k = k_ref[sl, :]
        v = v_ref[sl, :]
        at = an_ref[sl, :] * jnp.exp(cum - lw)
        rt = r * w_incl
        bt = bb_ref[sl, :] * w_inv
        kt = k * w_inv
        lanes = [slice(p * LANES, (p + 1) * LANES) for p in range(RW_PAIRS)]
        o2, s_new = _rwkv_pairs_chunk(
            [at[:, ls] for ls in lanes], [rt[:, ls] for ls in lanes], [bt[:, ls] for ls in lanes],
            [kt[:, ls] for ls in lanes], [v[:, ls] for ls in lanes], [s_scr[p] for p in range(RW_PAIRS)],
            chunk=chunk, same_head=same_head, strict=strict, incl=incl)
        for p, ls in enumerate(lanes):
            s_scr[p] = s_new[p] * w_incl[chunk - 1:chunk, ls]
            o = o2[p][:chunk] + o2[p][chunk:]
            mu = head_sum(o) * (1.0 / RW_HEAD)
            oc = o - mu
            var = head_sum(oc * oc) * (1.0 / RW_HEAD)
            on = oc * lax.rsqrt(var + RW_GN_EPS) * gw_ref[:, ls] + gb_ref[:, ls]
            bonus = head_sum(r[:, ls] * k[:, ls] * rk_ref[:, ls]) * v[:, ls]
            o_ref[sl, ls] = on + bonus

    @pl.when(t == pl.num_programs(1) - 1)
    def _():
        for p in range(RW_PAIRS):
            s = s_scr[p]
            sout_ref[2 * p] = s[:RW_HEAD, :RW_HEAD]
            sout_ref[2 * p + 1] = s[RW_HEAD:, RW_HEAD:]


def _rwkv(grp, seqs, gn_w, gn_b, r_k, s0):
    B, T = grp.B, grp.T
    tile = min(T, CHUNK)
    chunk = tile
    W = RW_HEADS * RW_HEAD
    seq_spec = pl.BlockSpec((None, tile, W), lambda b, t: (b, t, 0))
    vec_spec = pl.BlockSpec((1, W), lambda b, t: (0, 0))
    st_spec = pl.BlockSpec((None, RW_HEADS, RW_HEAD, RW_HEAD), lambda b, t: (b, 0, 0, 0))
    kern = functools.partial(_rwkv_kernel, chunk=chunk, n_chunks=tile // chunk)
    return pl.pallas_call(
        kern,
        grid=(B, T // tile),
        in_specs=[seq_spec] * 6 + [vec_spec] * 3 + [st_spec],
        out_specs=[seq_spec, st_spec],
        out_shape=[jax.ShapeDtypeStruct((B, T, W), F32),
                   jax.ShapeDtypeStruct((B, RW_HEADS, RW_HEAD, RW_HEAD), F32)],
        scratch_shapes=[pltpu.VMEM((RW_PAIRS, 2 * RW_HEAD, 2 * RW_HEAD), F32)],
        compiler_params=_params(("arbitrary", "arbitrary")),
        name="rwkv_scan",
    )(*seqs, gn_w, gn_b, r_k, s0)


def _merge_kernel(og_ref, gg_ref, orw_ref, gate_ref, mga_ref, mgb_ref, x_ref,
                  gate1_ref, shift2_ref, scale2_ref,
                  ggn_ref, gpost_ref, gpre_ref, wbg_ref, wbr_ref, wout_ref, wq_ref,
                  x1_ref, h2t_ref, qp_ref):
    gg = gg_ref[...]
    gla_in = og_ref[...] * ggn_ref[...] * (gg * jax.nn.sigmoid(gg))
    br_gla = _mm(gla_in, wbg_ref[...])
    br_rw = _mm(orw_ref[...] * gate_ref[...], wbr_ref[...])
    merged = jax.nn.sigmoid(mga_ref[...]) * br_gla + jax.nn.sigmoid(mgb_ref[...]) * br_rw
    y = _mm(merged, wout_ref[...])
    x1 = x_ref[...] + gate1_ref[...] * _rms(y, gpost_ref[...])
    x1_ref[...] = x1
    h2 = _rms(x1, gpre_ref[...]) * (1.0 + scale2_ref[...]) + shift2_ref[...]
    h2t_ref[...] = h2.T.astype(BF16)
    qp_ref[...] = jnp.dot(h2.astype(BF16), wq_ref[...], preferred_element_type=F32).astype(BF16)


def _merge(grp, og, p2, orw, gate_rw, x2d, modop, wts):
    tm, M = min(grp.tm, MERGE_TILE), grp.M
    W = D_MODEL

    def tok(blk):
        return pl.BlockSpec((tm, W), lambda i: (i, blk))

    def full(shape):
        return pl.BlockSpec(shape, lambda i: (0,) * len(shape))

    return pl.pallas_call(
        _merge_kernel,
        grid=(M // tm,),
        in_specs=[tok(0), tok(COL_GG // W), tok(0), tok(0), tok(COL_MG // W), tok(COL_MG // W + 1), tok(0),
                  grp.mod_spec(2, tm), grp.mod_spec(3, tm), grp.mod_spec(4, tm),
                  full((1, W)), full((1, W)), full((1, W)),
                  full((W, W)), full((W, W)), full((W, W)), full((W, 2 * W))],
        out_specs=[tok(0), pl.BlockSpec((W, tm), lambda i: (0, i)), pl.BlockSpec((tm, 2 * W), lambda i: (i, 0))],
        out_shape=[jax.ShapeDtypeStruct((M, W), F32), jax.ShapeDtypeStruct((W, M), BF16),
                   jax.ShapeDtypeStruct((M, 2 * W), BF16)],
        compiler_params=_params(("arbitrary",)),
        name="merge",
    )(og, p2, orw, gate_rw, p2, p2, x2d, modop, modop, modop,
      wts["g_gla_norm"], wts["g_post_mix"], wts["g_pre_ffn"],
      wts["w_b_gla"], wts["w_b_rw"], wts["w_out"], wts["peer_wq"])


def _pair_bits(x):
    bits = pltpu.bitcast(x.astype(BF16).astype(F32), jnp.uint32)
    return bits | (bits >> 16)


def _row_tile(row, rows):
    tile = pltpu.bitcast(jnp.broadcast_to(row, (8, row.shape[1])), BF16)
    return jnp.tile(tile, (rows // 16, 1))


def _top_values(s, n, with_rank=False):
    rows = lax.broadcasted_iota(jnp.int32, (n, s.shape[1]), 0)
    vals = jnp.zeros((n, s.shape[1]), F32)
    rank = jnp.full(s.shape, float(n), F32) if with_rank else None
    for i in range(n):
        m = jnp.max(s, axis=0, keepdims=True)
        vals = jnp.where(rows == i, m, vals)
        hit = s == m
        if with_rank:
            rank = jnp.where(hit, float(i), rank)
        s = jnp.where(hit, NEG_INF, s)
    return vals, rank


def _candidate_groups(v1, v2):
    n, tm = v1.shape
    split = 4
    r_all = lax.broadcasted_iota(jnp.int32, (n, tm), 0)
    r_half = lax.broadcasted_iota(jnp.int32, (n // 2, tm), 0)
    groups = [v1[0:1, :] + v2]
    for a in range(1, split):
        groups.append(jnp.where(r_half < n // (a + 1), v1[a:a + 1, :] + v2[:n // 2, :], NEG_INF))
    groups.append(jnp.where(r_all >= split, v1 + v2[0:1, :], NEG_INF))
    for b in range(1, n // (split + 1)):
        keep = (r_half >= split) & (r_half < n // (b + 1))
        groups.append(jnp.where(keep, v1[:n // 2, :] + v2[b:b + 1, :], NEG_INF))
    return groups


def _group_max(groups, n):
    full = [g for g in groups if g.shape[0] == n]
    best = full[0]
    for g in full[1:]:
        best = jnp.maximum(best, g)
    best = jnp.maximum(best[:n // 2, :], best[n // 2:, :])
    for g in groups:
        if g.shape[0] != n:
            best = jnp.maximum(best, g)
    return jnp.max(best, axis=0, keepdims=True)


def _select_kernel(qp_ref, keys_ref, r2_ref, e2_ref, al_ref, n1_ref):
    tm = qp_ref.shape[0]
    for h in range(PEER_HEADS):
        q1 = qp_ref[:, (2 * h) * 128:(2 * h + 1) * 128]
        q2 = qp_ref[:, (2 * h + 1) * 128:(2 * h + 2) * 128]
        s1 = _mm_nt(keys_ref[2 * h], q1)
        s2 = _mm_nt(keys_ref[2 * h + 1], q2)
        v1, _ = _top_values(s1, PEER_TOPK)
        v2, rank2 = _top_values(s2, PEER_TOPK, with_rank=True)
        groups = _candidate_groups(v1, v2)
        cmax = v1[0:1, :] + v2[0:1, :]
        z = jnp.zeros((1, tm), F32)
        m = cmax
        for _ in range(PEER_TOPK):
            m = _group_max(groups, PEER_TOPK)
            z = z + jnp.exp(m - cmax)
            groups = [jnp.where(g == m, NEG_INF, g) for g in groups]
        th = m - s1
        n1 = jnp.zeros_like(s1)
        for b in range(PEER_TOPK):
            n1 = n1 + jnp.where(v2[b:b + 1, :] >= th, 1.0, 0.0)
        r2_ref[h] = rank2.astype(BF16)
        e2_ref[h] = jnp.exp(s2 - v2[0:1, :]).astype(BF16)
        al_ref[h] = _pair_bits(jnp.exp(s1 - v1[0:1, :]) / z)
        n1_ref[h] = _pair_bits(n1)


def _peer_select(grp, qp, keys):
    tm, M = grp.tm, grp.M
    rows = pl.BlockSpec((PEER_HEADS, PEER_NKEYS, tm), lambda i: (0, 0, i))
    return pl.pallas_call(
        _select_kernel,
        grid=(M // tm,),
        in_specs=[pl.BlockSpec((tm, 2 * D_MODEL), lambda i: (i, 0)),
                  pl.BlockSpec((2 * PEER_HEADS, PEER_NKEYS, 128), lambda i: (0, 0, 0))],
        out_specs=[rows] * 4,
        out_shape=[jax.ShapeDtypeStruct((PEER_HEADS, PEER_NKEYS, M), BF16)] * 2
        + [jax.ShapeDtypeStruct((PEER_HEADS, PEER_NKEYS, M), jnp.uint32)] * 2,
        compiler_params=_params(("arbitrary",)),
        name="peer_select",
    )(qp, keys)


def _peer_weights(act_ref, y_ref, r2_ref, e2_ref, al_ref, n1_ref, row0):
    tm = act_ref.shape[1]
    half = PEER_NKEYS // 2
    for j in range(act_ref.shape[0] // PEER_NKEYS):
        i1 = slice(row0 + j, row0 + j + 1)
        w = [jnp.zeros((half, tm), BF16) for _ in range(2)]
        for h in range(PEER_HEADS):
            n1 = _row_tile(n1_ref[h, i1, :], half)
            al = _row_tile(al_ref[h, i1, :], half)
            for r in range(2):
                keys = slice(r * half, (r + 1) * half)
                w[r] = w[r] + jnp.where(r2_ref[h, keys, :] < n1, e2_ref[h, keys, :], 0.0) * al
        for r in range(2):
            rows = slice(j * PEER_NKEYS + r * half, j * PEER_NKEYS + (r + 1) * half)
            a = act_ref[rows, :].astype(BF16)
            y_ref[rows, :] = (0.5 * a * (1.0 + lax.erf(a * 0.7071067811865476))) * w[r]


def _peer_kernel(h2t_ref, u_ref, vt_ref, r2_ref, e2_ref, al_ref, n1_ref, x1_ref, gate2_ref, gpost_ref,
                 o_ref, acc_ref, act_0, act_1, y_0, y_1):
    e = pl.program_id(1)
    ts = act_0.shape[0]
    per = ts // PEER_NKEYS
    acts, ys = (act_0, act_1), (y_0, y_1)

    @pl.when(e == 0)
    def _():
        acc_ref[...] = jnp.zeros_like(acc_ref)

    def scores(k):
        acts[k % 2][...] = jnp.dot(u_ref[k], h2t_ref[...], preferred_element_type=F32)

    def mix(k):
        acc_ref[...] += jnp.dot(vt_ref[k], ys[k % 2][...], preferred_element_type=F32)

    scores(0)
    scores(1)
    for k in range(PEER_SUBTILES):
        _peer_weights(acts[k % 2], ys[k % 2], r2_ref, e2_ref, al_ref, n1_ref, k * per)
        if k + 2 < PEER_SUBTILES:
            scores(k + 2)
        if k >= 1:
            mix(k - 1)
    mix(PEER_SUBTILES - 1)

    @pl.when(e == pl.num_programs(1) - 1)
    def _():
        o_ref[...] = x1_ref[...] + gate2_ref[...] * _rms(acc_ref[...].T, gpost_ref[...])


def _peer(grp, h2t, sel, x1, modop, g_post, pu, pvt):
    tm, M = grp.tm, grp.M
    ts = PEER_EXPERT_TILE
    te = ts * PEER_SUBTILES
    r2, e2, al, n1 = sel
    packed = pl.BlockSpec((PEER_HEADS, PEER_NKEYS, tm), lambda i, e: (0, 0, i))
    rows = pl.BlockSpec((PEER_HEADS, te // PEER_NKEYS, tm), lambda i, e: (0, e, i))
    tok = pl.BlockSpec((tm, D_MODEL), lambda i, e: (i, 0))
    return pl.pallas_call(
        _peer_kernel,
        grid=(M // tm, pu.shape[0] // PEER_SUBTILES),
        in_specs=[pl.BlockSpec((D_MODEL, tm), lambda i, e: (0, i)),
                  pl.BlockSpec((PEER_SUBTILES, ts, D_MODEL), lambda i, e: (e, 0, 0)),
                  pl.BlockSpec((PEER_SUBTILES, D_MODEL, ts), lambda i, e: (e, 0, 0)),
                  packed, packed, rows, rows,
                  pl.BlockSpec((tm, D_MODEL), lambda i, e: (i, 0), pipeline_mode=pl.Buffered(1)),
                  grp.mod_spec(5),
                  pl.BlockSpec((1, D_MODEL), lambda i, e: (0, 0))],
        out_specs=tok,
        out_shape=jax.ShapeDtypeStruct((M, D_MODEL), F32),
        scratch_shapes=[pltpu.VMEM((D_MODEL, tm), F32),
                        pltpu.VMEM((ts, tm), F32), pltpu.VMEM((ts, tm), F32),
                        pltpu.VMEM((ts, tm), BF16), pltpu.VMEM((ts, tm), BF16)],
        compiler_params=_params(("arbitrary", "arbitrary")),
        name="peer_dense",
    )(h2t, pu, pvt, r2, e2, al, n1, x1, modop, g_post)


def _prepare_weights(w_in, w_gla_gate, b_gla_gate, g_gla_norm, mu_shift, rw_w0, rw_w2, rw_a0, rw_a2,
                     rw_g2, rw_k_k, rw_k_a, rw_r_k, rw_gn_w, rw_gn_b, w_b_gla, w_b_rw, w_out,
                     peer_wq, peer_keys, peer_u, peer_v, g_pre_mix, g_post_mix, g_pre_ffn, g_post_ffn):
    W = RW_HEADS * RW_HEAD
    rw = w_in[:, 3088:6416]
    w_proj = jnp.concatenate(
        [rw[:, :3 * W], w_in[:, 6416:8464], w_in[:, 2048:3072], w_in[:, 1024:2048], w_in[:, 0:1024],
         rw[:, 3 * W:], w_in[:, 3072:3088], jnp.zeros((D_MODEL, PROJ_PAD_W - 8464), F32)],
        axis=1).astype(BF16)
    head = jnp.arange(W) // RW_HEAD
    zeros64 = jnp.zeros((64, W), F32)
    return dict(
        w_proj=w_proj,
        wg_pad=jnp.concatenate([w_gla_gate, jnp.zeros((128 - GLA_GATE_RANK, 512), F32)], axis=0),
        bg=b_gla_gate.reshape(1, -1),
        g_gla_norm=g_gla_norm.reshape(1, -1),
        mu_rkv=mu_shift[:3 * W].reshape(1, -1),
        mu_lora=mu_shift[3 * W:].reshape(1, -1),
        rw_w0=rw_w0.reshape(1, -1),
        w2p=jnp.concatenate([rw_w2, jnp.zeros((192, W), F32)], axis=0),
        rw_a0=rw_a0.reshape(1, -1),
        a2p=jnp.concatenate([zeros64, rw_a2, jnp.zeros((128, W), F32)], axis=0),
        g2p=jnp.concatenate([jnp.zeros((128, W), F32), rw_g2], axis=0),
        rw_k_k=rw_k_k.reshape(1, -1),
        rw_k_a=rw_k_a.reshape(1, -1),
        rw_r_k=rw_r_k.reshape(1, -1),
        rw_gn_w=rw_gn_w.reshape(1, -1),
        rw_gn_b=rw_gn_b.reshape(1, -1),
        head_ones=(head[:, None] == head[None, :]).astype(BF16),
        w_b_gla=w_b_gla.astype(BF16),
        w_b_rw=w_b_rw.astype(BF16),
        w_out=w_out.astype(BF16),
        peer_wq=peer_wq.astype(BF16),
        peer_keys=peer_keys.reshape(2 * PEER_HEADS, PEER_NKEYS, 128).astype(BF16),
        peer_u=peer_u.astype(BF16).reshape(-1, PEER_EXPERT_TILE, D_MODEL),
        peer_vt=peer_v.reshape(-1, PEER_EXPERT_TILE, D_MODEL).transpose(0, 2, 1).astype(BF16),
        g_pre_mix=g_pre_mix.reshape(1, -1),
        g_post_mix=g_post_mix.reshape(1, -1),
        g_pre_ffn=g_pre_ffn.reshape(1, -1),
        g_post_ffn=g_post_ffn.reshape(1, -1),
    )


def _block(x, mod, s_gla0, s_rw0, shift0, wts, valid):
    B, T, _ = x.shape
    grp = _Group(B, T, valid)
    W = RW_HEADS * RW_HEAD
    x2d = x.reshape(grp.M, D_MODEL)
    modop = grp.mod_operand(mod)
    p2 = _projection(grp, x2d, modop, wts["g_pre_mix"], wts["w_proj"])
    p3 = p2.reshape(B, T, PROJ_PAD_W)
    og, s_gla = _gla(grp, p3, wts["wg_pad"], wts["bg"], s_gla0)
    sh_rkv = shift0[:, None, :3 * W]
    sh_lora = shift0[:, None, 3 * W:]
    r, k, v, an, bb, lw, gate = _rwkv_prep(grp, p3, sh_rkv, sh_lora, wts)
    orw, s_rw = _rwkv(grp, (r, k, v, an, bb, lw), wts["rw_gn_w"], wts["rw_gn_b"], wts["rw_r_k"], s_rw0)
    x1, h2t, qp = _merge(grp, og.reshape(grp.M, -1), p2, orw.reshape(grp.M, W), gate.reshape(grp.M, W),
                        x2d, modop, wts)
    sel = _peer_select(grp, qp, wts["peer_keys"])
    out = _peer(grp, h2t, sel, x1, modop, wts["g_post_ffn"], wts["peer_u"], wts["peer_vt"])
    last = p3[:, valid - 1, :]
    shift_new = jnp.concatenate([last[:, :3 * W], last[:, COL_LORA:COL_LORA + 256]], axis=-1)
    return out.reshape(B, T, D_MODEL), s_gla, s_rw, shift_new


def kernel(x_prompt, x_sample, c_prompt, c_sample, state_gla, state_rwkv, state_shift, w_ada, b_ada, g_pre_mix, g_post_mix, g_pre_ffn, g_post_ffn, w_in, w_gla_gate, b_gla_gate, g_gla_norm, mu_shift, rw_w0, rw_w2, rw_a0, rw_a2, rw_g2, rw_k_k, rw_k_a, rw_r_k, rw_gn_w, rw_gn_b, w_b_gla, w_b_rw, w_out, peer_wq, peer_keys, peer_u, peer_v):
    depth = w_in.shape[0]
    bp, bs = x_prompt.shape[0], x_sample.shape[0]
    ts = x_sample.shape[1]
    ts_pad = -(-ts // 8) * 8
    xp = x_prompt
    xs = jnp.pad(x_sample, ((0, 0), (0, ts_pad - ts), (0, 0)))
    outs = [[] for _ in range(6)]
    for l in range(depth):
        wts = _prepare_weights(
            w_in[l], w_gla_gate[l], b_gla_gate[l], g_gla_norm[l], mu_shift[l], rw_w0[l], rw_w2[l],
            rw_a0[l], rw_a2[l], rw_g2[l], rw_k_k[l], rw_k_a[l], rw_r_k[l], rw_gn_w[l], rw_gn_b[l],
            w_b_gla[l], w_b_rw[l], w_out[l], peer_wq[l], peer_keys[l], peer_u[l], peer_v[l],
            g_pre_mix[l], g_post_mix[l], g_pre_ffn[l], g_post_ffn[l])
        mod = _modulation(jnp.concatenate([c_prompt, c_sample], axis=0), w_ada[l], b_ada[l])
        zg = jnp.zeros((bp, GLA_HEADS, GLA_DK, GLA_DV), F32)
        zr = jnp.zeros((bp, RW_HEADS, RW_HEAD, RW_HEAD), F32)
        zs = jnp.zeros((bp, RW_SHIFT_W), F32)
        xp, g1, r1, s1 = _block(xp, mod[:bp], zg, zr, zs, wts, xp.shape[1])
        xs, g2, r2, s2 = _block(xs, mod[bp:], state_gla[l], state_rwkv[l], state_shift[l], wts, ts)
        for acc, val in zip(outs, (g1, r1, s1, g2, r2, s2)):
            acc.append(val)
    stacked = [jnp.stack(o) for o in outs]
    return (xp, xs[:, :ts], *stacked)
```

```python
import functools

import jax
import jax.numpy as jnp
from jax import lax
from jax.experimental import pallas as pl
from jax.experimental.pallas import tpu as pltpu

F32 = jnp.float32
BF16 = jnp.bfloat16

D_MODEL = 1024
GLA_HEADS = 4
GLA_DK = 128
GLA_DV = 256
GLA_GATE_RANK = 16
GLA_GATE_TAU = 16.0
GLA_NORM_EPS = 1e-5
RW_HEAD = 64
RW_HEADS = 16
RW_PAIRS = RW_HEADS // 2
RW_GN_EPS = 64e-5
RW_SHIFT_W = 3328
PEER_HEADS = 8
PEER_NKEYS = 128
PEER_TOPK = 16
NORM_EPS = 1e-6
LANES = 128

COL_R, COL_K, COL_V = 0, 1024, 2048
COL_MG = 3072
COL_GG = 5120
COL_GV = 6144
COL_GQ = 7168
COL_GK = 7680
COL_LORA = 8192
COL_GA = 8448
PROJ_PAD_W = 8704
PROJ_TILE_N = 2176

CHUNK = 64
TOKEN_TILE = 512
MERGE_TILE = 256
PEER_EXPERT_TILE = 512
PEER_SUBTILES = 4
VMEM_LIMIT = 56 * 1024 * 1024

NEG_INF = float("-inf")


def _mm(a, b):
    return jnp.dot(a.astype(BF16), b.astype(BF16), preferred_element_type=F32)


def _mm_nt(a, b):
    return lax.dot_general(a.astype(BF16), b.astype(BF16), (((1,), (1,)), ((), ())),
                           preferred_element_type=F32)


def _mm_tn(a, b):
    return lax.dot_general(a.astype(BF16), b.astype(BF16), (((0,), (0,)), ((), ())),
                           preferred_element_type=F32)


def _split(x):
    hi = x.astype(BF16)
    lo = (x - hi.astype(F32)).astype(BF16)
    return hi, lo


def _mm_exact_lhs(a, x):
    hi, lo = _split(x)
    a = a.astype(BF16)
    return (jnp.dot(a, hi, preferred_element_type=F32)
            + jnp.dot(a, lo, preferred_element_type=F32))


def _mm_exact_rhs(x, a):
    hi, lo = _split(x)
    a = a.astype(BF16)
    return (jnp.dot(hi, a, preferred_element_type=F32)
            + jnp.dot(lo, a, preferred_element_type=F32))


def _softplus(x):
    return jnp.maximum(x, 0.0) + jnp.log1p(jnp.exp(-jnp.abs(x)))


def _rms(x, g):
    return x * lax.rsqrt(jnp.mean(x * x, axis=-1, keepdims=True) + NORM_EPS) * g


def _params(sem, flags=None):
    return pltpu.CompilerParams(dimension_semantics=sem, vmem_limit_bytes=VMEM_LIMIT, flags=flags)


def _mod_kernel(c_ref, w_ref, b_ref, o_ref):
    c = c_ref[...]
    o_ref[...] = _mm(c * jax.nn.sigmoid(c), w_ref[...]) + b_ref[...]


def _modulation(c, w_ada, b_ada):
    rows = c.shape[0]
    n = w_ada.shape[1]
    tn = 1536
    return pl.pallas_call(
        _mod_kernel,
        grid=(n // tn,),
        in_specs=[pl.BlockSpec((rows, D_MODEL), lambda j: (0, 0)),
                  pl.BlockSpec((D_MODEL, tn), lambda j: (0, j)),
                  pl.BlockSpec((1, tn), lambda j: (0, j))],
        out_specs=pl.BlockSpec((rows, tn), lambda j: (0, j)),
        out_shape=jax.ShapeDtypeStruct((rows, n), F32),
        compiler_params=_params(("arbitrary",)),
        name="adaln_mod",
    )(c, w_ada, b_ada.reshape(1, n))


class _Group:
    def __init__(self, batch, seq, valid):
        self.B, self.T, self.Tv = batch, seq, valid
        self.M = batch * seq
        self.tm = min(TOKEN_TILE, self.M)
        self.per_batch_mod = seq % self.tm == 0

    def seq_operand(self, arr):
        if self.per_batch_mod:
            return arr.reshape(self.B, 1, arr.shape[-1])
        return jnp.repeat(arr, self.T, axis=0)

    def seq_spec(self, width, col, tm=None, tile_index=None):
        tm, T = tm or self.tm, self.T
        tile = tile_index or (lambda i, *_: i)
        if self.per_batch_mod:
            return pl.BlockSpec((None, 1, width), lambda *g: ((tile(*g) * tm) // T, 0, col))
        return pl.BlockSpec((tm, width), lambda *g: (tile(*g), col))

    def mod_spec(self, col, tm=None, tile_index=None):
        return self.seq_spec(D_MODEL, col, tm, tile_index)


def _proj_kernel(x_ref, shift_ref, scale_ref, g_ref, w_ref, p_ref, h_scr):
    @pl.when(pl.program_id(1) == 0)
    def _():
        h = _rms(x_ref[...], g_ref[...]) * (1.0 + scale_ref[...]) + shift_ref[...]
        h_scr[...] = h.astype(BF16)

    p_ref[...] = jnp.dot(h_scr[...], w_ref[...], preferred_element_type=F32).astype(p_ref.dtype)


def _projection(grp, x2d, modop, g_pre, w_proj, dtype):
    tm = grp.tm
    return pl.pallas_call(
        _proj_kernel,
        grid=(grp.M // tm, PROJ_PAD_W // PROJ_TILE_N),
        in_specs=[pl.BlockSpec((tm, D_MODEL), lambda i, j: (i, 0)),
                  grp.mod_spec(0), grp.mod_spec(1),
                  pl.BlockSpec((1, D_MODEL), lambda i, j: (0, 0)),
                  pl.BlockSpec((D_MODEL, PROJ_TILE_N), lambda i, j: (0, j))],
        out_specs=pl.BlockSpec((tm, PROJ_TILE_N), lambda i, j: (i, j)),
        out_shape=jax.ShapeDtypeStruct((grp.M, PROJ_PAD_W), dtype),
        scratch_shapes=[pltpu.VMEM((tm, D_MODEL), BF16)],
        compiler_params=_params(("arbitrary", "arbitrary")),
        name="in_proj",
    )(x2d, modop, modop, g_pre, w_proj)


def _gla_kernel(q_ref, k_ref, v_ref, ga_ref, wg_ref, bg_ref, s0_ref, og_ref, sout_ref, st_scr,
                *, chunk, n_chunks, tile, valid, seqs):
    t = pl.program_id(1)
    last = pl.num_programs(1) - 1
    row = lax.broadcasted_iota(jnp.int32, (chunk, chunk), 0)
    col = lax.broadcasted_iota(jnp.int32, (chunk, chunk), 1)
    causal = row >= col
    tril = jnp.where(causal, 1.0, 0.0).astype(BF16)

    def per_seq(b, carry):
        @pl.when(t == 0)
        def _():
            for h in range(GLA_HEADS):
                st_scr[b, h] = s0_ref[b, h].T

        for c in range(n_chunks):
            sl = pl.ds(c * chunk, chunk)
            z = _mm(ga_ref[b, sl, :], wg_ref[...]) + bg_ref[...]
            la = -_softplus(-z) / GLA_GATE_TAU
            k = k_ref[b, sl, :].astype(F32)
            if valid is not None:
                pos = t * tile + c * chunk + lax.broadcasted_iota(jnp.int32, (chunk, 1), 0)
                ok = pos < valid
                la = jnp.where(ok, la, 0.0)
                k = jnp.where(ok, k, 0.0)
            cum = _mm_exact_lhs(tril, la)
            end = cum[chunk - 1:chunk, :]
            qe = q_ref[b, sl, :].astype(F32) * (GLA_DK ** -0.5) * jnp.exp(cum)
            ke = k * jnp.exp(-cum)
            kd = k * jnp.exp(end - cum)
            decay = jnp.exp(end)
            for h in range(GLA_HEADS):
                ks = slice(h * GLA_DK, (h + 1) * GLA_DK)
                vs = slice(h * GLA_DV, (h + 1) * GLA_DV)
                v = v_ref[b, sl, vs]
                if valid is not None:
                    v = jnp.where(ok, v, 0.0)
                a = jnp.where(causal, _mm_nt(qe[:, ks], ke[:, ks]), 0.0)
                st = st_scr[b, h]
                o = _mm_nt(qe[:, ks], st) + _mm(a, v)
                st_scr[b, h] = st * decay[:, ks] + _mm_tn(v, kd[:, ks])
                og_ref[b, sl, vs] = o * lax.rsqrt(jnp.mean(o * o, axis=-1, keepdims=True) + GLA_NORM_EPS)

        @pl.when(t == last)
        def _():
            for h in range(GLA_HEADS):
                sout_ref[b, h] = st_scr[b, h].T

        return carry

    if seqs == 1:
        per_seq(0, 0)
    else:
        lax.fori_loop(0, seqs, per_seq, 0)


def _seqs_per_step(batch, seq):
    n = max(1, min(8, 64 // seq))
    while batch % n:
        n //= 2
    return n


def _gla(grp, p3, wg_pad, bg, s0):
    B, T = grp.B, grp.T
    tile = min(T, 4 * CHUNK)
    chunk = min(CHUNK, tile)
    nb = _seqs_per_step(B, T)
    kern = functools.partial(_gla_kernel, chunk=chunk, n_chunks=tile // chunk, tile=tile,
                             valid=None if grp.Tv == T else grp.Tv, seqs=nb)
    kw, vw = GLA_HEADS * GLA_DK, GLA_HEADS * GLA_DV
    st_spec = pl.BlockSpec((nb, GLA_HEADS, GLA_DK, GLA_DV), lambda i, t: (i, 0, 0, 0))
    return pl.pallas_call(
        kern,
        grid=(B // nb, T // tile),
        in_specs=[pl.BlockSpec((nb, tile, kw), lambda i, t: (i, t, COL_GQ // kw)),
                  pl.BlockSpec((nb, tile, kw), lambda i, t: (i, t, COL_GK // kw)),
                  pl.BlockSpec((nb, tile, vw), lambda i, t: (i, t, COL_GV // vw)),
                  pl.BlockSpec((nb, tile, LANES), lambda i, t: (i, t, COL_GA // LANES)),
                  pl.BlockSpec((LANES, kw), lambda i, t: (0, 0)),
                  pl.BlockSpec((1, kw), lambda i, t: (0, 0)),
                  st_spec],
        out_specs=[pl.BlockSpec((nb, tile, vw), lambda i, t: (i, t, 0)), st_spec],
        out_shape=[jax.ShapeDtypeStruct((B, T, vw), F32),
                   jax.ShapeDtypeStruct((B, GLA_HEADS, GLA_DK, GLA_DV), F32)],
        scratch_shapes=[pltpu.VMEM((nb, GLA_HEADS, GLA_DV, GLA_DK), F32)],
        compiler_params=_params(("arbitrary", "arbitrary")),
        name="gla_scan",
    )(p3, p3, p3, p3, wg_pad, bg, s0)


def _rwprep_kernel(r_ref, k_ref, v_ref, l_ref, pr_ref, pk_ref, pv_ref, pl_ref,
                   s0r_ref, s0k_ref, s0v_ref, s0l_ref, mu_ref, mul_ref, w0_ref, w2_ref, a0_ref, a2_ref, g2_ref,
                   kk_ref, ka_ref, bd_ref,
                   ro_ref, ko_ref, vo_ref, an_ref, bb_ref, lw_ref, gate_ref,
                   *, tile, seq, valid, prev_rows):
    i = pl.program_id(0)
    local = lax.broadcasted_iota(jnp.int32, (tile, 1), 0)
    pos = lax.rem(i * tile + local, seq)
    first = pos == 0
    top = local == 0

    def shifted(cur_ref, prev_ref, s0_ref, mu):
        cur = cur_ref[...].astype(F32)
        before = prev_ref[prev_rows - 1:prev_rows, :].astype(F32)
        prev = jnp.where(first, s0_ref[...], jnp.where(top, before, pltpu.roll(cur, 1, 0)))
        return cur + (prev - cur) * mu

    r = shifted(r_ref, pr_ref, s0r_ref, mu_ref[:, 0:1024])
    k = shifted(k_ref, pk_ref, s0k_ref, mu_ref[:, 1024:2048])
    v = shifted(v_ref, pv_ref, s0v_ref, mu_ref[:, 2048:3072])
    lo = shifted(l_ref, pl_ref, s0l_ref, mul_ref[...])

    wpre = w0_ref[...] + _mm(jnp.tanh(lo), w2_ref[...])
    wlog = -_softplus(-wpre) - 0.5
    lw = -jnp.exp(wlog)
    a = jax.nn.sigmoid(a0_ref[...] + _mm(lo, a2_ref[...]))
    gate = _mm(jax.nn.sigmoid(lo), g2_ref[...])

    kk = k * kk_ref[...]
    ss = _mm_exact_rhs(kk * kk, bd_ref[...])
    kk = kk / jnp.maximum(jnp.sqrt(ss), 1e-12)
    kmod = k * (1.0 + (a - 1.0) * ka_ref[...])
    an = -kk
    bb = kk * a
    if valid is not None:
        ok = pos < valid
        lw = jnp.where(ok, lw, 0.0)
        an = jnp.where(ok, an, 0.0)
        bb = jnp.where(ok, bb, 0.0)
        kmod = jnp.where(ok, kmod, 0.0)
        v = jnp.where(ok, v, 0.0)
    ro_ref[...] = r
    ko_ref[...] = kmod
    vo_ref[...] = v
    an_ref[...] = an
    bb_ref[...] = bb
    lw_ref[...] = lw
    gate_ref[...] = gate


def _rwkv_prep(grp, p2, shift0, wts):
    M, T = grp.M, grp.T
    tile = min(M, 256)
    prev_rows = min(16, tile)
    W = RW_HEADS * RW_HEAD
    s0 = grp.seq_operand(shift0)

    def cur(width, blk):
        return pl.BlockSpec((tile, width), lambda i: (i, blk))

    def prev(width, blk):
        return pl.BlockSpec((prev_rows, width), lambda i: (jnp.maximum(i * (tile // prev_rows) - 1, 0), blk))

    def full(shape):
        return pl.BlockSpec(shape, lambda i: (0,) * len(shape))

    out_spec = pl.BlockSpec((tile, W), lambda i: (i, 0))
    kern = functools.partial(_rwprep_kernel, tile=tile, seq=T, valid=None if grp.Tv == T else grp.Tv,
                             prev_rows=prev_rows)
    lora_blk = COL_LORA // 256
    return pl.pallas_call(
        kern,
        grid=(M // tile,),
        in_specs=[cur(W, 0), cur(W, 1), cur(W, 2), cur(256, lora_blk),
                  prev(W, 0), prev(W, 1), prev(W, 2), prev(256, lora_blk),
                  grp.seq_spec(W, 0, tile), grp.seq_spec(W, 1, tile), grp.seq_spec(W, 2, tile),
                  grp.seq_spec(256, 3 * W // 256, tile),
                  full((1, 3 * W)), full((1, 256)),
                  full((1, W)), full((256, W)), full((1, W)), full((256, W)), full((256, W)),
                  full((1, W)), full((1, W)), full((W, W))],
        out_specs=[out_spec] * 7,
        out_shape=[jax.ShapeDtypeStruct((M, W), F32)] * 7,
        compiler_params=_params(("arbitrary",)),
        name="rwkv_prep",
    )(p2, p2, p2, p2, p2, p2, p2, p2, s0, s0, s0, s0,
      wts["mu_rkv"], wts["mu_lora"], wts["rw_w0"], wts["w2p"], wts["rw_a0"], wts["a2p"], wts["g2p"],
      wts["rw_k_k"], wts["rw_k_a"], wts["head_ones"])


def _rwkv_pairs_chunk(at, rt, bt, kt, v, states, *, chunk, same_head, strict, incl):
    n2 = 2 * chunk
    wide = n2 % LANES == 0
    pairs = range(len(states))

    def blocks(x):
        return jnp.where(same_head, jnp.concatenate([x, x], axis=0), 0.0).astype(BF16)

    a2 = [blocks(x) for x in at]
    r2 = [blocks(x) for x in rt]
    b2 = [blocks(x) for x in bt]
    k2 = [blocks(x) for x in kt]
    v2 = [blocks(x) for x in v]
    ar = [jnp.concatenate([a2[p], r2[p]], axis=0) for p in pairs]
    bk = [jnp.concatenate([b2[p], k2[p]], axis=0) for p in pairs]
    if wide:
        g = [_mm_nt(ar[p], bk[p]) for p in pairs]
        l_ab = [jnp.where(strict, g[p][:n2, :n2], 0.0) for p in pairs]
        l_ak = [jnp.where(strict, g[p][:n2, n2:], 0.0) for p in pairs]
        m_rb = [jnp.where(incl, g[p][n2:, :n2], 0.0) for p in pairs]
        m_rk = [jnp.where(incl, g[p][n2:, n2:], 0.0) for p in pairs]
    else:
        l_ab = [jnp.where(strict, _mm_nt(a2[p], b2[p]), 0.0) for p in pairs]
        l_ak = [jnp.where(strict, _mm_nt(a2[p], k2[p]), 0.0) for p in pairs]
        m_rb = [jnp.where(incl, _mm_nt(r2[p], b2[p]), 0.0) for p in pairs]
        m_rk = [jnp.where(incl, _mm_nt(r2[p], k2[p]), 0.0) for p in pairs]
    from_state = [_mm_nt(ar[p], states[p]) for p in pairs]
    x = [from_state[p][:n2] + _mm(l_ak[p], v2[p]) for p in pairs]
    pw = l_ab
    for _ in range(chunk.bit_length() - 2):
        if wide:
            y = [_mm(pw[p], jnp.concatenate([x[p], pw[p]], axis=1)) for p in pairs]
            x = [x[p] + y[p][:, :LANES] for p in pairs]
            pw = [y[p][:, LANES:] for p in pairs]
        else:
            x = [x[p] + _mm(pw[p], x[p]) for p in pairs]
            pw = [_mm(pw[p], pw[p]) for p in pairs]
    u = [x[p] + _mm(pw[p], x[p]) for p in pairs]
    uv = [jnp.concatenate([u[p].astype(BF16), v2[p]], axis=0) for p in pairs]
    if wide:
        o2 = [from_state[p][n2:] + _mm(jnp.concatenate([m_rb[p], m_rk[p]], axis=1), uv[p]) for p in pairs]
    else:
        o2 = [from_state[p][n2:] + _mm(m_rb[p], u[p]) + _mm(m_rk[p], v2[p]) for p in pairs]
    s_new = [states[p] + _mm_tn(uv[p], bk[p]) for p in pairs]
    return o2, s_new


def _rwkv_kernel(r_ref, k_ref, v_ref, an_ref, bb_ref, lw_ref, gw_ref, gb_ref, rk_ref, s0_ref,
                 o_ref, sout_ref, s_scr, *, chunk, seqs):
    t = pl.program_id(1)
    zeros = jnp.zeros((RW_HEAD, RW_HEAD), F32)

    @pl.when(t == 0)
    def _():
        for q in range(seqs):
            for p in range(RW_PAIRS):
                top = jnp.concatenate([s0_ref[q, 2 * p], zeros], axis=1)
                bot = jnp.concatenate([zeros, s0_ref[q, 2 * p + 1]], axis=1)
                s_scr[q, p] = jnp.concatenate([top, bot], axis=0)

    n2 = 2 * chunk
    row = lax.broadcasted_iota(jnp.int32, (n2, n2), 0)
    col = lax.broadcasted_iota(jnp.int32, (n2, n2), 1)

    def second(idx, size):
        return jnp.where(idx >= size, 1, 0)

    same = second(row, chunk) == second(col, chunk)
    strict = same & (row > col)
    incl = same & (row >= col)
    same_head = (second(lax.broadcasted_iota(jnp.int32, (n2, LANES), 0), chunk)
                 == second(lax.broadcasted_iota(jnp.int32, (n2, LANES), 1), RW_HEAD))
    trow = lax.broadcasted_iota(jnp.int32, (chunk, chunk), 0)
    tcol = lax.broadcasted_iota(jnp.int32, (chunk, chunk), 1)
    tril = jnp.where(trow >= tcol, 1.0, 0.0).astype(BF16)
    first_head = lax.broadcasted_iota(jnp.int32, (chunk, LANES), 1) < RW_HEAD

    def head_sum(x):
        lo = jnp.sum(jnp.where(first_head, x, 0.0), axis=-1, keepdims=True)
        hi = jnp.sum(jnp.where(first_head, 0.0, x), axis=-1, keepdims=True)
        return jnp.where(first_head, lo, hi)

    lanes = [slice(p * LANES, (p + 1) * LANES) for p in range(RW_PAIRS)]
    problems = [(q, p) for q in range(seqs) for p in range(RW_PAIRS)]
    rs, ks, vs, ats, rts, bts, kts, ends = [], [], [], [], [], [], [], []
    for q in range(seqs):
        lw = lw_ref[q]
        cum = _mm_exact_lhs(tril, lw)
        w_incl = jnp.exp(cum)
        w_inv = jnp.exp(-cum)
        r, k = r_ref[q], k_ref[q]
        rs.append(r)
        ks.append(k)
        vs.append(v_ref[q])
        ats.append(an_ref[q] * jnp.exp(cum - lw))
        rts.append(r * w_incl)
        bts.append(bb_ref[q] * w_inv)
        kts.append(k * w_inv)
        ends.append(w_incl[chunk - 1:chunk, :])

    def slabs(xs):
        return [xs[q][:, lanes[p]] for q, p in problems]

    o2, s_new = _rwkv_pairs_chunk(
        slabs(ats), slabs(rts), slabs(bts), slabs(kts), slabs(vs), [s_scr[q, p] for q, p in problems],
        chunk=chunk, same_head=same_head, strict=strict, incl=incl)
    for n, (q, p) in enumerate(problems):
        ls = lanes[p]
        s_scr[q, p] = s_new[n] * ends[q][:, ls]
        o = o2[n][:chunk] + o2[n][chunk:]
        mu = head_sum(o) * (1.0 / RW_HEAD)
        oc = o - mu
        var = head_sum(oc * oc) * (1.0 / RW_HEAD)
        on = oc * lax.rsqrt(var + RW_GN_EPS) * gw_ref[:, ls] + gb_ref[:, ls]
        bonus = head_sum(rs[q][:, ls] * ks[q][:, ls] * rk_ref[:, ls]) * vs[q][:, ls]
        o_ref[q, :, ls] = on + bonus

    @pl.when(t == pl.num_programs(1) - 1)
    def _():
        for q in range(seqs):
            for p in range(RW_PAIRS):
                s = s_scr[q, p]
                sout_ref[q, 2 * p] = s[:RW_HEAD, :RW_HEAD]
                sout_ref[q, 2 * p + 1] = s[RW_HEAD:, RW_HEAD:]


def _rwkv(grp, seqs, gn_w, gn_b, r_k, s0):
    B, T = grp.B, grp.T
    chunk = min(T, CHUNK)
    nb = min(4, _seqs_per_step(B, T))
    W = RW_HEADS * RW_HEAD
    seq_spec = pl.BlockSpec((nb, chunk, W), lambda b, t: (b, t, 0))
    vec_spec = pl.BlockSpec((1, W), lambda b, t: (0, 0))
    st_spec = pl.BlockSpec((nb, RW_HEADS, RW_HEAD, RW_HEAD), lambda b, t: (b, 0, 0, 0))
    kern = functools.partial(_rwkv_kernel, chunk=chunk, seqs=nb)
    return pl.pallas_call(
        kern,
        grid=(B // nb, T // chunk),
        in_specs=[seq_spec] * 6 + [vec_spec] * 3 + [st_spec],
        out_specs=[seq_spec, st_spec],
        out_shape=[jax.ShapeDtypeStruct((B, T, W), F32),
                   jax.ShapeDtypeStruct((B, RW_HEADS, RW_HEAD, RW_HEAD), F32)],
        scratch_shapes=[pltpu.VMEM((nb, RW_PAIRS, 2 * RW_HEAD, 2 * RW_HEAD), F32)],
        compiler_params=_params(("arbitrary", "arbitrary")),
        name="rwkv_scan",
    )(*seqs, gn_w, gn_b, r_k, s0)


def _merge_kernel(og_ref, gg_ref, orw_ref, gate_ref, mga_ref, mgb_ref, x_ref,
                  gate1_ref, shift2_ref, scale2_ref,
                  ggn_ref, gpost_ref, gpre_ref, wbg_ref, wbr_ref, wout_ref, wq_ref,
                  x1_ref, h2t_ref, qp_ref):
    gg = gg_ref[...].astype(F32)
    gla_in = og_ref[...] * ggn_ref[...] * (gg * jax.nn.sigmoid(gg))
    br_gla = _mm(gla_in, wbg_ref[...])
    br_rw = _mm(orw_ref[...] * gate_ref[...], wbr_ref[...])
    merged = (jax.nn.sigmoid(mga_ref[...].astype(F32)) * br_gla
              + jax.nn.sigmoid(mgb_ref[...].astype(F32)) * br_rw)
    y = _mm(merged, wout_ref[...])
    x1 = x_ref[...] + gate1_ref[...] * _rms(y, gpost_ref[...])
    x1_ref[...] = x1
    h2 = _rms(x1, gpre_ref[...]) * (1.0 + scale2_ref[...]) + shift2_ref[...]
    h2t_ref[...] = h2.T.astype(BF16)
    qp_ref[...] = jnp.dot(h2.astype(BF16), wq_ref[...], preferred_element_type=F32).astype(BF16)


def _merge(grp, og, p2, orw, gate_rw, x2d, modop, wts):
    tm, M = min(grp.tm, MERGE_TILE), grp.M
    W = D_MODEL

    def tok(blk):
        return pl.BlockSpec((tm, W), lambda i: (i, blk))

    def full(shape):
        return pl.BlockSpec(shape, lambda i: (0,) * len(shape))

    return pl.pallas_call(
        _merge_kernel,
        grid=(M // tm,),
        in_specs=[tok(0), tok(COL_GG // W), tok(0), tok(0), tok(COL_MG // W), tok(COL_MG // W + 1), tok(0),
                  grp.mod_spec(2, tm), grp.mod_spec(3, tm), grp.mod_spec(4, tm),
                  full((1, W)), full((1, W)), full((1, W)),
                  full((W, W)), full((W, W)), full((W, W)), full((W, 2 * W))],
        out_specs=[tok(0), pl.BlockSpec((W, tm), lambda i: (0, i)), pl.BlockSpec((tm, 2 * W), lambda i: (i, 0))],
        out_shape=[jax.ShapeDtypeStruct((M, W), F32), jax.ShapeDtypeStruct((W, M), BF16),
                   jax.ShapeDtypeStruct((M, 2 * W), BF16)],
        compiler_params=_params(("arbitrary",)),
        name="merge",
    )(og, p2, orw, gate_rw, p2, p2, x2d, modop, modop, modop,
      wts["g_gla_norm"], wts["g_post_mix"], wts["g_pre_ffn"],
      wts["w_b_gla"], wts["w_b_rw"], wts["w_out"], wts["peer_wq"])


def _pair_bits(x):
    bits = pltpu.bitcast(x.astype(BF16).astype(F32), jnp.uint32)
    return bits | (bits >> 16)


def _row_tile(row, rows):
    tile = pltpu.bitcast(jnp.broadcast_to(row, (8, row.shape[1])), BF16)
    return jnp.tile(tile, (rows // 16, 1))


def _top_values(s, n, with_rank=False):
    rows = lax.broadcasted_iota(jnp.int32, (n, s.shape[1]), 0)
    vals = jnp.zeros((n, s.shape[1]), F32)
    rank = jnp.full(s.shape, float(n), F32) if with_rank else None
    for i in range(n):
        m = jnp.max(s, axis=0, keepdims=True)
        vals = jnp.where(rows == i, m, vals)
        hit = s == m
        if with_rank:
            rank = jnp.where(hit, float(i), rank)
        s = jnp.where(hit, NEG_INF, s)
    return vals, rank


def _candidate_groups(v1, v2):
    n, tm = v1.shape
    split = 4
    r_all = lax.broadcasted_iota(jnp.int32, (n, tm), 0)
    r_half = lax.broadcasted_iota(jnp.int32, (n // 2, tm), 0)
    groups = [v1[0:1, :] + v2]
    for a in range(1, split):
        groups.append(jnp.where(r_half < n // (a + 1), v1[a:a + 1, :] + v2[:n // 2, :], NEG_INF))
    groups.append(jnp.where(r_all >= split, v1 + v2[0:1, :], NEG_INF))
    for b in range(1, n // (split + 1)):
        keep = (r_half >= split) & (r_half < n // (b + 1))
        groups.append(jnp.where(keep, v1[:n // 2, :] + v2[b:b + 1, :], NEG_INF))
    return groups


def _group_max(groups, n):
    full = [g for g in groups if g.shape[0] == n]
    best = full[0]
    for g in full[1:]:
        best = jnp.maximum(best, g)
    best = jnp.maximum(best[:n // 2, :], best[n // 2:, :])
    for g in groups:
        if g.shape[0] != n:
            best = jnp.maximum(best, g)
    return jnp.max(best, axis=0, keepdims=True)


def _select_kernel(qp_ref, keys_ref, r2_ref, e2_ref, al_ref, n1_ref):
    tm = qp_ref.shape[0]
    for h in range(PEER_HEADS):
        q1 = qp_ref[:, (2 * h) * 128:(2 * h + 1) * 128]
        q2 = qp_ref[:, (2 * h + 1) * 128:(2 * h + 2) * 128]
        s1 = _mm_nt(keys_ref[2 * h], q1)
        s2 = _mm_nt(keys_ref[2 * h + 1], q2)
        v1, _ = _top_values(s1, PEER_TOPK)
        v2, rank2 = _top_values(s2, PEER_TOPK, with_rank=True)
        groups = _candidate_groups(v1, v2)
        cmax = v1[0:1, :] + v2[0:1, :]
        z = jnp.zeros((1, tm), F32)
        m = cmax
        for _ in range(PEER_TOPK):
            m = _group_max(groups, PEER_TOPK)
            z = z + jnp.exp(m - cmax)
            groups = [jnp.where(g == m, NEG_INF, g) for g in groups]
        th = m - s1
        n1 = jnp.zeros_like(s1)
        for b in range(PEER_TOPK):
            n1 = n1 + jnp.where(v2[b:b + 1, :] >= th, 1.0, 0.0)
        r2_ref[h] = rank2.astype(BF16)
        e2_ref[h] = jnp.exp(s2 - v2[0:1, :]).astype(BF16)
        al_ref[h] = _pair_bits(jnp.exp(s1 - v1[0:1, :]) / z)
        n1_ref[h] = _pair_bits(n1)


def _peer_select(grp, qp, keys):
    tm, M = grp.tm, grp.M
    rows = pl.BlockSpec((PEER_HEADS, PEER_NKEYS, tm), lambda i: (0, 0, i))
    return pl.pallas_call(
        _select_kernel,
        grid=(M // tm,),
        in_specs=[pl.BlockSpec((tm, 2 * D_MODEL), lambda i: (i, 0)),
                  pl.BlockSpec((2 * PEER_HEADS, PEER_NKEYS, 128), lambda i: (0, 0, 0))],
        out_specs=[rows] * 4,
        out_shape=[jax.ShapeDtypeStruct((PEER_HEADS, PEER_NKEYS, M), BF16)] * 2
        + [jax.ShapeDtypeStruct((PEER_HEADS, PEER_NKEYS, M), jnp.uint32)] * 2,
        compiler_params=_params(("arbitrary",)),
        name="peer_select",
    )(qp, keys)


def _peer_weights(act_ref, y_ref, r2_ref, e2_ref, al_ref, n1_ref, row0):
    tm = act_ref.shape[1]
    half = PEER_NKEYS // 2
    for j in range(act_ref.shape[0] // PEER_NKEYS):
        i1 = slice(row0 + j, row0 + j + 1)
        w = [jnp.zeros((half, tm), BF16) for _ in range(2)]
        for h in range(PEER_HEADS):
            n1 = _row_tile(n1_ref[h, i1, :], half)
            al = _row_tile(al_ref[h, i1, :], half)
            for r in range(2):
                keys = slice(r * half, (r + 1) * half)
                w[r] = w[r] + jnp.where(r2_ref[h, keys, :] < n1, e2_ref[h, keys, :], 0.0) * al
        for r in range(2):
            rows = slice(j * PEER_NKEYS + r * half, j * PEER_NKEYS + (r + 1) * half)
            a = act_ref[rows, :].astype(BF16)
            y_ref[rows, :] = (0.5 * a * (1.0 + lax.erf(a * 0.7071067811865476))) * w[r]


def _peer_kernel(h2t_ref, u_ref, vt_ref, r2_ref, e2_ref, al_ref, n1_ref, x1_ref, gate2_ref, gpost_ref,
                 o_ref, acc_ref, act_0, act_1, y_0, y_1):
    e = pl.program_id(1)
    ts = act_0.shape[0]
    per = ts // PEER_NKEYS
    acts, ys = (act_0, act_1), (y_0, y_1)

    @pl.when(e == 0)
    def _():
        acc_ref[...] = jnp.zeros_like(acc_ref)

    def scores(k):
        acts[k % 2][...] = jnp.dot(u_ref[k], h2t_ref[...], preferred_element_type=F32)

    def mix(k):
        acc_ref[...] += jnp.dot(vt_ref[k], ys[k % 2][...], preferred_element_type=F32)

    scores(0)
    scores(1)
    for k in range(PEER_SUBTILES):
        _peer_weights(acts[k % 2], ys[k % 2], r2_ref, e2_ref, al_ref, n1_ref, k * per)
        if k + 2 < PEER_SUBTILES:
            scores(k + 2)
        if k >= 1:
            mix(k - 1)
    mix(PEER_SUBTILES - 1)

    @pl.when(e == pl.num_programs(1) - 1)
    def _():
        o_ref[...] = x1_ref[...] + gate2_ref[...] * _rms(acc_ref[...].T, gpost_ref[...])


def _peer(grp, h2t, sel, x1, modop, g_post, pu, pvt):
    tm, M = grp.tm, grp.M
    ts = PEER_EXPERT_TILE
    te = ts * PEER_SUBTILES
    r2, e2, al, n1 = sel
    packed = pl.BlockSpec((PEER_HEADS, PEER_NKEYS, tm), lambda i, e: (0, 0, i))
    rows = pl.BlockSpec((PEER_HEADS, te // PEER_NKEYS, tm), lambda i, e: (0, e, i))
    tok = pl.BlockSpec((tm, D_MODEL), lambda i, e: (i, 0))
    return pl.pallas_call(
        _peer_kernel,
        grid=(M // tm, pu.shape[0] // PEER_SUBTILES),
        in_specs=[pl.BlockSpec((D_MODEL, tm), lambda i, e: (0, i)),
                  pl.BlockSpec((PEER_SUBTILES, ts, D_MODEL), lambda i, e: (e, 0, 0)),
                  pl.BlockSpec((PEER_SUBTILES, D_MODEL, ts), lambda i, e: (e, 0, 0)),
                  packed, packed, rows, rows,
                  pl.BlockSpec((tm, D_MODEL), lambda i, e: (i, 0), pipeline_mode=pl.Buffered(1)),
                  grp.mod_spec(5),
                  pl.BlockSpec((1, D_MODEL), lambda i, e: (0, 0))],
        out_specs=tok,
        out_shape=jax.ShapeDtypeStruct((M, D_MODEL), F32),
        scratch_shapes=[pltpu.VMEM((D_MODEL, tm), F32),
                        pltpu.VMEM((ts, tm), F32), pltpu.VMEM((ts, tm), F32),
                        pltpu.VMEM((ts, tm), BF16), pltpu.VMEM((ts, tm), BF16)],
        compiler_params=_params(("arbitrary", "arbitrary")),
        name="peer_dense",
    )(h2t, pu, pvt, r2, e2, al, n1, x1, modop, g_post)


def _prepare_weights(w_in, w_gla_gate, b_gla_gate, g_gla_norm, mu_shift, rw_w0, rw_w2, rw_a0, rw_a2,
                     rw_g2, rw_k_k, rw_k_a, rw_r_k, rw_gn_w, rw_gn_b, w_b_gla, w_b_rw, w_out,
                     peer_wq, peer_keys, peer_u, peer_v, g_pre_mix, g_post_mix, g_pre_ffn, g_post_ffn):
    W = RW_HEADS * RW_HEAD
    rw = w_in[:, 3088:6416]
    w_proj = jnp.concatenate(
        [rw[:, :3 * W], w_in[:, 6416:8464], w_in[:, 2048:3072], w_in[:, 1024:2048], w_in[:, 0:1024],
         rw[:, 3 * W:], w_in[:, 3072:3088], jnp.zeros((D_MODEL, PROJ_PAD_W - 8464), F32)],
        axis=1).astype(BF16)
    head = jnp.arange(W) // RW_HEAD
    zeros64 = jnp.zeros((64, W), F32)
    return dict(
        w_proj=w_proj,
        wg_pad=jnp.concatenate([w_gla_gate, jnp.zeros((128 - GLA_GATE_RANK, 512), F32)], axis=0),
        bg=b_gla_gate.reshape(1, -1),
        g_gla_norm=g_gla_norm.reshape(1, -1),
        mu_rkv=mu_shift[:3 * W].reshape(1, -1),
        mu_lora=mu_shift[3 * W:].reshape(1, -1),
        rw_w0=rw_w0.reshape(1, -1),
        w2p=jnp.concatenate([rw_w2, jnp.zeros((192, W), F32)], axis=0),
        rw_a0=rw_a0.reshape(1, -1),
        a2p=jnp.concatenate([zeros64, rw_a2, jnp.zeros((128, W), F32)], axis=0),
        g2p=jnp.concatenate([jnp.zeros((128, W), F32), rw_g2], axis=0),
        rw_k_k=rw_k_k.reshape(1, -1),
        rw_k_a=rw_k_a.reshape(1, -1),
        rw_r_k=rw_r_k.reshape(1, -1),
        rw_gn_w=rw_gn_w.reshape(1, -1),
        rw_gn_b=rw_gn_b.reshape(1, -1),
        head_ones=(head[:, None] == head[None, :]).astype(BF16),
        w_b_gla=w_b_gla.astype(BF16),
        w_b_rw=w_b_rw.astype(BF16),
        w_out=w_out.astype(BF16),
        peer_wq=peer_wq.astype(BF16),
        peer_keys=peer_keys.reshape(2 * PEER_HEADS, PEER_NKEYS, 128).astype(BF16),
        peer_u=peer_u.astype(BF16).reshape(-1, PEER_EXPERT_TILE, D_MODEL),
        peer_vt=peer_v.reshape(-1, PEER_EXPERT_TILE, D_MODEL).transpose(0, 2, 1).astype(BF16),
        g_pre_mix=g_pre_mix.reshape(1, -1),
        g_post_mix=g_post_mix.reshape(1, -1),
        g_pre_ffn=g_pre_ffn.reshape(1, -1),
        g_post_ffn=g_post_ffn.reshape(1, -1),
    )


def _block(x, mod, s_gla0, s_rw0, shift0, wts, valid):
    B, T, _ = x.shape
    grp = _Group(B, T, valid)
    W = RW_HEADS * RW_HEAD
    x2d = x.reshape(grp.M, D_MODEL)
    modop = grp.seq_operand(mod)
    p2 = _projection(grp, x2d, modop, wts["g_pre_mix"], wts["w_proj"], BF16)
    p3 = p2.reshape(B, T, PROJ_PAD_W)
    og, s_gla = _gla(grp, p3, wts["wg_pad"], wts["bg"], s_gla0)
    r, k, v, an, bb, lw, gate = _rwkv_prep(grp, p2, shift0, wts)
    seqs = tuple(z.reshape(B, T, W) for z in (r, k, v, an, bb, lw))
    orw, s_rw = _rwkv(grp, seqs, wts["rw_gn_w"], wts["rw_gn_b"], wts["rw_r_k"], s_rw0)
    x1, h2t, qp = _merge(grp, og.reshape(grp.M, -1), p2, orw.reshape(grp.M, W), gate,
                        x2d, modop, wts)
    sel = _peer_select(grp, qp, wts["peer_keys"])
    out = _peer(grp, h2t, sel, x1, modop, wts["g_post_ffn"], wts["peer_u"], wts["peer_vt"])
    ends = _Group(B, 1, 1)
    last = _projection(ends, x[:, valid - 1, :], ends.seq_operand(mod), wts["g_pre_mix"], wts["w_proj"], F32)
    shift_new = jnp.concatenate([last[:, :3 * W], last[:, COL_LORA:COL_LORA + 256]], axis=-1)
    return out.reshape(B, T, D_MODEL), s_gla, s_rw, shift_new


def kernel(x_prompt, x_sample, c_prompt, c_sample, state_gla, state_rwkv, state_shift, w_ada, b_ada, g_pre_mix, g_post_mix, g_pre_ffn, g_post_ffn, w_in, w_gla_gate, b_gla_gate, g_gla_norm, mu_shift, rw_w0, rw_w2, rw_a0, rw_a2, rw_g2, rw_k_k, rw_k_a, rw_r_k, rw_gn_w, rw_gn_b, w_b_gla, w_b_rw, w_out, peer_wq, peer_keys, peer_u, peer_v):
    depth = w_in.shape[0]
    bp, bs = x_prompt.shape[0], x_sample.shape[0]
    ts = x_sample.shape[1]
    ts_pad = -(-ts // 8) * 8
    xp = x_prompt
    xs = jnp.pad(x_sample, ((0, 0), (0, ts_pad - ts), (0, 0)))
    outs = [[] for _ in range(6)]
    for l in range(depth):
        wts = _prepare_weights(
            w_in[l], w_gla_gate[l], b_gla_gate[l], g_gla_norm[l], mu_shift[l], rw_w0[l], rw_w2[l],
            rw_a0[l], rw_a2[l], rw_g2[l], rw_k_k[l], rw_k_a[l], rw_r_k[l], rw_gn_w[l], rw_gn_b[l],
            w_b_gla[l], w_b_rw[l], w_out[l], peer_wq[l], peer_keys[l], peer_u[l], peer_v[l],
            g_pre_mix[l], g_post_mix[l], g_pre_ffn[l], g_post_ffn[l])
        mod = _modulation(jnp.concatenate([c_prompt, c_sample], axis=0), w_ada[l], b_ada[l])
        zg = jnp.zeros((bp, GLA_HEADS, GLA_DK, GLA_DV), F32)
        zr = jnp.zeros((bp, RW_HEADS, RW_HEAD, RW_HEAD), F32)
        zs = jnp.zeros((bp, RW_SHIFT_W), F32)
        xp, g1, r1, s1 = _block(xp, mod[:bp], zg, zr, zs, wts, xp.shape[1])
        xs, g2, r2, s2 = _block(xs, mod[bp:], state_gla[l], state_rwkv[l], state_shift[l], wts, ts)
        for acc, val in zip(outs, (g1, r1, s1, g2, r2, s2)):
            acc.append(val)
    stacked = [jnp.stack(o) for o in outs]
    return (xp, xs[:, :ts], *stacked)
```

```python
import functools

import jax
import jax.numpy as jnp
from jax import lax
from jax.experimental import pallas as pl
from jax.experimental.pallas import tpu as pltpu

F32 = jnp.float32
BF16 = jnp.bfloat16

D_MODEL = 1024
GLA_HEADS = 4
GLA_DK = 128
GLA_DV = 256
GLA_GATE_RANK = 16
GLA_GATE_TAU = 16.0
GLA_NORM_EPS = 1e-5
RW_HEAD = 64
RW_HEADS = 16
RW_PAIRS = RW_HEADS // 2
RW_GN_EPS = 64e-5
RW_SHIFT_W = 3328
PEER_HEADS = 8
PEER_NKEYS = 128
PEER_TOPK = 16
NORM_EPS = 1e-6
LANES = 128

COL_R, COL_K, COL_V = 0, 1024, 2048
COL_MG = 3072
COL_GG = 5120
COL_GV = 6144
COL_GQ = 7168
COL_GK = 7680
COL_LORA = 8192
COL_GA = 8448
PROJ_PAD_W = 8704
PROJ_TILE_N = 2176

CHUNK = 64
TOKEN_TILE = 512
MERGE_TILE = 256
PEER_EXPERT_TILE = 512
PEER_SUBTILES = 4
VMEM_LIMIT = 56 * 1024 * 1024

NEG_INF = float("-inf")


def _mm(a, b):
    return jnp.dot(a.astype(BF16), b.astype(BF16), preferred_element_type=F32)


def _mm_nt(a, b):
    return lax.dot_general(a.astype(BF16), b.astype(BF16), (((1,), (1,)), ((), ())),
                           preferred_element_type=F32)


def _mm_tn(a, b):
    return lax.dot_general(a.astype(BF16), b.astype(BF16), (((0,), (0,)), ((), ())),
                           preferred_element_type=F32)


def _split(x):
    hi = x.astype(BF16)
    lo = (x - hi.astype(F32)).astype(BF16)
    return hi, lo


def _mm_exact_lhs(a, x):
    hi, lo = _split(x)
    a = a.astype(BF16)
    return (jnp.dot(a, hi, preferred_element_type=F32)
            + jnp.dot(a, lo, preferred_element_type=F32))


def _mm_exact_rhs(x, a):
    hi, lo = _split(x)
    a = a.astype(BF16)
    return (jnp.dot(hi, a, preferred_element_type=F32)
            + jnp.dot(lo, a, preferred_element_type=F32))


def _softplus(x):
    return jnp.maximum(x, 0.0) + jnp.log1p(jnp.exp(-jnp.abs(x)))


def _rms(x, g):
    return x * lax.rsqrt(jnp.mean(x * x, axis=-1, keepdims=True) + NORM_EPS) * g


def _params(sem, flags=None):
    return pltpu.CompilerParams(dimension_semantics=sem, vmem_limit_bytes=VMEM_LIMIT, flags=flags)


def _mod_kernel(c_ref, w_ref, b_ref, o_ref):
    c = c_ref[...]
    o_ref[...] = _mm(c * jax.nn.sigmoid(c), w_ref[...]) + b_ref[...]


def _modulation(c, w_ada, b_ada):
    rows = c.shape[0]
    n = w_ada.shape[1]
    tn = 1536
    return pl.pallas_call(
        _mod_kernel,
        grid=(n // tn,),
        in_specs=[pl.BlockSpec((rows, D_MODEL), lambda j: (0, 0)),
                  pl.BlockSpec((D_MODEL, tn), lambda j: (0, j)),
                  pl.BlockSpec((1, tn), lambda j: (0, j))],
        out_specs=pl.BlockSpec((rows, tn), lambda j: (0, j)),
        out_shape=jax.ShapeDtypeStruct((rows, n), F32),
        compiler_params=_params(("arbitrary",)),
        name="adaln_mod",
    )(c, w_ada, b_ada.reshape(1, n))


class _Group:
    def __init__(self, batch, seq, valid):
        self.B, self.T, self.Tv = batch, seq, valid
        self.M = batch * seq
        self.tm = min(TOKEN_TILE, self.M)
        self.per_batch_mod = seq % self.tm == 0

    def seq_operand(self, arr):
        if self.per_batch_mod:
            return arr.reshape(self.B, 1, arr.shape[-1])
        return jnp.repeat(arr, self.T, axis=0)

    def seq_spec(self, width, col, tm=None, tile_index=None):
        tm, T = tm or self.tm, self.T
        tile = tile_index or (lambda i, *_: i)
        if self.per_batch_mod:
            return pl.BlockSpec((None, 1, width), lambda *g: ((tile(*g) * tm) // T, 0, col))
        return pl.BlockSpec((tm, width), lambda *g: (tile(*g), col))

    def mod_spec(self, col, tm=None, tile_index=None):
        return self.seq_spec(D_MODEL, col, tm, tile_index)


def _proj_kernel(x_ref, shift_ref, scale_ref, g_ref, w_ref, p_ref, h_scr):
    @pl.when(pl.program_id(1) == 0)
    def _():
        h = _rms(x_ref[...], g_ref[...]) * (1.0 + scale_ref[...]) + shift_ref[...]
        h_scr[...] = h.astype(BF16)

    p_ref[...] = jnp.dot(h_scr[...], w_ref[...], preferred_element_type=F32).astype(p_ref.dtype)


def _projection(grp, x2d, modop, g_pre, w_proj, dtype):
    tm = grp.tm
    return pl.pallas_call(
        _proj_kernel,
        grid=(grp.M // tm, PROJ_PAD_W // PROJ_TILE_N),
        in_specs=[pl.BlockSpec((tm, D_MODEL), lambda i, j: (i, 0)),
                  grp.mod_spec(0), grp.mod_spec(1),
                  pl.BlockSpec((1, D_MODEL), lambda i, j: (0, 0)),
                  pl.BlockSpec((D_MODEL, PROJ_TILE_N), lambda i, j: (0, j))],
        out_specs=pl.BlockSpec((tm, PROJ_TILE_N), lambda i, j: (i, j)),
        out_shape=jax.ShapeDtypeStruct((grp.M, PROJ_PAD_W), dtype),
        scratch_shapes=[pltpu.VMEM((tm, D_MODEL), BF16)],
        compiler_params=_params(("arbitrary", "arbitrary")),
        name="in_proj",
    )(x2d, modop, modop, g_pre, w_proj)


def _gla_kernel(q_ref, k_ref, v_ref, ga_ref, wg_ref, bg_ref, s0_ref, og_ref, sout_ref, st_scr,
                *, chunk, n_chunks, tile, valid, seqs):
    t = pl.program_id(1)
    last = pl.num_programs(1) - 1
    row = lax.broadcasted_iota(jnp.int32, (chunk, chunk), 0)
    col = lax.broadcasted_iota(jnp.int32, (chunk, chunk), 1)
    causal = row >= col
    tril = jnp.where(causal, 1.0, 0.0).astype(BF16)

    def per_seq(b, carry):
        @pl.when(t == 0)
        def _():
            for h in range(GLA_HEADS):
                st_scr[b, h] = s0_ref[b, h].T

        for c in range(n_chunks):
            sl = pl.ds(c * chunk, chunk)
            z = _mm(ga_ref[b, sl, :], wg_ref[...]) + bg_ref[...]
            la = -_softplus(-z) / GLA_GATE_TAU
            k = k_ref[b, sl, :].astype(F32)
            if valid is not None:
                pos = t * tile + c * chunk + lax.broadcasted_iota(jnp.int32, (chunk, 1), 0)
                ok = pos < valid
                la = jnp.where(ok, la, 0.0)
                k = jnp.where(ok, k, 0.0)
            cum = _mm_exact_lhs(tril, la)
            end = cum[chunk - 1:chunk, :]
            qe = q_ref[b, sl, :].astype(F32) * (GLA_DK ** -0.5) * jnp.exp(cum)
            ke = k * jnp.exp(-cum)
            kd = k * jnp.exp(end - cum)
            decay = jnp.exp(end)
            for h in range(GLA_HEADS):
                ks = slice(h * GLA_DK, (h + 1) * GLA_DK)
                vs = slice(h * GLA_DV, (h + 1) * GLA_DV)
                v = v_ref[b, sl, vs]
                if valid is not None:
                    v = jnp.where(ok, v, 0.0)
                a = jnp.where(causal, _mm_nt(qe[:, ks], ke[:, ks]), 0.0)
                st = st_scr[b, h]
                o = _mm_nt(qe[:, ks], st) + _mm(a, v)
                st_scr[b, h] = st * decay[:, ks] + _mm_tn(v, kd[:, ks])
                og = o * lax.rsqrt(jnp.mean(o * o, axis=-1, keepdims=True) + GLA_NORM_EPS)
                og_ref[b, sl, vs] = og.astype(og_ref.dtype)

        @pl.when(t == last)
        def _():
            for h in range(GLA_HEADS):
                sout_ref[b, h] = st_scr[b, h].T

        return carry

    if seqs == 1:
        per_seq(0, 0)
    else:
        lax.fori_loop(0, seqs, per_seq, 0)


def _seqs_per_step(batch, seq):
    n = max(1, min(8, 64 // seq))
    while batch % n:
        n //= 2
    return n


def _gla(grp, p3, wg_pad, bg, s0):
    B, T = grp.B, grp.T
    tile = min(T, 4 * CHUNK)
    chunk = min(CHUNK, tile)
    nb = _seqs_per_step(B, T)
    kern = functools.partial(_gla_kernel, chunk=chunk, n_chunks=tile // chunk, tile=tile,
                             valid=None if grp.Tv == T else grp.Tv, seqs=nb)
    kw, vw = GLA_HEADS * GLA_DK, GLA_HEADS * GLA_DV
    st_spec = pl.BlockSpec((nb, GLA_HEADS, GLA_DK, GLA_DV), lambda i, t: (i, 0, 0, 0))
    return pl.pallas_call(
        kern,
        grid=(B // nb, T // tile),
        in_specs=[pl.BlockSpec((nb, tile, kw), lambda i, t: (i, t, COL_GQ // kw)),
                  pl.BlockSpec((nb, tile, kw), lambda i, t: (i, t, COL_GK // kw)),
                  pl.BlockSpec((nb, tile, vw), lambda i, t: (i, t, COL_GV // vw)),
                  pl.BlockSpec((nb, tile, LANES), lambda i, t: (i, t, COL_GA // LANES)),
                  pl.BlockSpec((LANES, kw), lambda i, t: (0, 0)),
                  pl.BlockSpec((1, kw), lambda i, t: (0, 0)),
                  st_spec],
        out_specs=[pl.BlockSpec((nb, tile, vw), lambda i, t: (i, t, 0)), st_spec],
        out_shape=[jax.ShapeDtypeStruct((B, T, vw), BF16),
                   jax.ShapeDtypeStruct((B, GLA_HEADS, GLA_DK, GLA_DV), F32)],
        scratch_shapes=[pltpu.VMEM((nb, GLA_HEADS, GLA_DV, GLA_DK), F32)],
        compiler_params=_params(("arbitrary", "arbitrary")),
        name="gla_scan",
    )(p3, p3, p3, p3, wg_pad, bg, s0)


def _rwprep_kernel(r_ref, k_ref, v_ref, l_ref, pr_ref, pk_ref, pv_ref, pl_ref,
                   s0r_ref, s0k_ref, s0v_ref, s0l_ref, mu_ref, mul_ref, w0_ref, w2_ref, a0_ref, a2_ref, g2_ref,
                   kk_ref, ka_ref, bd_ref,
                   ro_ref, ko_ref, vo_ref, an_ref, bb_ref, lw_ref, gate_ref,
                   *, tile, seq, valid, prev_rows):
    i = pl.program_id(0)
    local = lax.broadcasted_iota(jnp.int32, (tile, 1), 0)
    pos = lax.rem(i * tile + local, seq)
    first = pos == 0
    top = local == 0

    def shifted(cur_ref, prev_ref, s0_ref, mu):
        cur = cur_ref[...].astype(F32)
        before = prev_ref[prev_rows - 1:prev_rows, :].astype(F32)
        prev = jnp.where(first, s0_ref[...], jnp.where(top, before, pltpu.roll(cur, 1, 0)))
        return cur + (prev - cur) * mu

    r = shifted(r_ref, pr_ref, s0r_ref, mu_ref[:, 0:1024])
    k = shifted(k_ref, pk_ref, s0k_ref, mu_ref[:, 1024:2048])
    v = shifted(v_ref, pv_ref, s0v_ref, mu_ref[:, 2048:3072])
    lo = shifted(l_ref, pl_ref, s0l_ref, mul_ref[...])

    wpre = w0_ref[...] + _mm(jnp.tanh(lo), w2_ref[...])
    wlog = -_softplus(-wpre) - 0.5
    lw = -jnp.exp(wlog)
    a = jax.nn.sigmoid(a0_ref[...] + _mm(lo, a2_ref[...]))
    gate = _mm(jax.nn.sigmoid(lo), g2_ref[...])

    kk = k * kk_ref[...]
    ss = _mm_exact_rhs(kk * kk, bd_ref[...])
    kk = kk / jnp.maximum(jnp.sqrt(ss), 1e-12)
    kmod = k * (1.0 + (a - 1.0) * ka_ref[...])
    an = -kk
    bb = kk * a
    if valid is not None:
        ok = pos < valid
        lw = jnp.where(ok, lw, 0.0)
        an = jnp.where(ok, an, 0.0)
        bb = jnp.where(ok, bb, 0.0)
        kmod = jnp.where(ok, kmod, 0.0)
        v = jnp.where(ok, v, 0.0)
    ro_ref[...] = r.astype(ro_ref.dtype)
    ko_ref[...] = kmod.astype(ko_ref.dtype)
    vo_ref[...] = v.astype(vo_ref.dtype)
    an_ref[...] = an.astype(an_ref.dtype)
    bb_ref[...] = bb.astype(bb_ref.dtype)
    lw_ref[...] = lw
    gate_ref[...] = gate.astype(gate_ref.dtype)


def _rwkv_prep(grp, p2, shift0, wts):
    M, T = grp.M, grp.T
    tile = min(M, 256)
    prev_rows = min(16, tile)
    W = RW_HEADS * RW_HEAD
    s0 = grp.seq_operand(shift0)

    def cur(width, blk):
        return pl.BlockSpec((tile, width), lambda i: (i, blk))

    def prev(width, blk):
        return pl.BlockSpec((prev_rows, width), lambda i: (jnp.maximum(i * (tile // prev_rows) - 1, 0), blk))

    def full(shape):
        return pl.BlockSpec(shape, lambda i: (0,) * len(shape))

    out_spec = pl.BlockSpec((tile, W), lambda i: (i, 0))
    kern = functools.partial(_rwprep_kernel, tile=tile, seq=T, valid=None if grp.Tv == T else grp.Tv,
                             prev_rows=prev_rows)
    lora_blk = COL_LORA // 256
    return pl.pallas_call(
        kern,
        grid=(M // tile,),
        in_specs=[cur(W, 0), cur(W, 1), cur(W, 2), cur(256, lora_blk),
                  prev(W, 0), prev(W, 1), prev(W, 2), prev(256, lora_blk),
                  grp.seq_spec(W, 0, tile), grp.seq_spec(W, 1, tile), grp.seq_spec(W, 2, tile),
                  grp.seq_spec(256, 3 * W // 256, tile),
                  full((1, 3 * W)), full((1, 256)),
                  full((1, W)), full((256, W)), full((1, W)), full((256, W)), full((256, W)),
                  full((1, W)), full((1, W)), full((W, W))],
        out_specs=[out_spec] * 7,
        out_shape=[jax.ShapeDtypeStruct((M, W), BF16)] * 5
        + [jax.ShapeDtypeStruct((M, W), F32), jax.ShapeDtypeStruct((M, W), BF16)],
        compiler_params=_params(("arbitrary",)),
        name="rwkv_prep",
    )(p2, p2, p2, p2, p2, p2, p2, p2, s0, s0, s0, s0,
      wts["mu_rkv"], wts["mu_lora"], wts["rw_w0"], wts["w2p"], wts["rw_a0"], wts["a2p"], wts["g2p"],
      wts["rw_k_k"], wts["rw_k_a"], wts["head_ones"])


def _rwkv_pairs_chunk(at, rt, bt, kt, v, states, *, chunk, same_head, strict, incl):
    n2 = 2 * chunk
    wide = n2 % LANES == 0
    pairs = range(len(states))

    def blocks(x):
        return jnp.where(same_head, jnp.concatenate([x, x], axis=0), 0.0).astype(BF16)

    a2 = [blocks(x) for x in at]
    r2 = [blocks(x) for x in rt]
    b2 = [blocks(x) for x in bt]
    k2 = [blocks(x) for x in kt]
    v2 = [blocks(x) for x in v]
    ar = [jnp.concatenate([a2[p], r2[p]], axis=0) for p in pairs]
    bk = [jnp.concatenate([b2[p], k2[p]], axis=0) for p in pairs]
    if wide:
        g = [_mm_nt(ar[p], bk[p]) for p in pairs]
        l_ab = [jnp.where(strict, g[p][:n2, :n2], 0.0) for p in pairs]
        l_ak = [jnp.where(strict, g[p][:n2, n2:], 0.0) for p in pairs]
        m_rb = [jnp.where(incl, g[p][n2:, :n2], 0.0) for p in pairs]
        m_rk = [jnp.where(incl, g[p][n2:, n2:], 0.0) for p in pairs]
    else:
        l_ab = [jnp.where(strict, _mm_nt(a2[p], b2[p]), 0.0) for p in pairs]
        l_ak = [jnp.where(strict, _mm_nt(a2[p], k2[p]), 0.0) for p in pairs]
        m_rb = [jnp.where(incl, _mm_nt(r2[p], b2[p]), 0.0) for p in pairs]
        m_rk = [jnp.where(incl, _mm_nt(r2[p], k2[p]), 0.0) for p in pairs]
    from_state = [_mm_nt(ar[p], states[p]) for p in pairs]
    x = [from_state[p][:n2] + _mm(l_ak[p], v2[p]) for p in pairs]
    pw = l_ab
    for _ in range(chunk.bit_length() - 2):
        if wide:
            y = [_mm(pw[p], jnp.concatenate([x[p], pw[p]], axis=1)) for p in pairs]
            x = [x[p] + y[p][:, :LANES] for p in pairs]
            pw = [y[p][:, LANES:] for p in pairs]
        else:
            x = [x[p] + _mm(pw[p], x[p]) for p in pairs]
            pw = [_mm(pw[p], pw[p]) for p in pairs]
    u = [x[p] + _mm(pw[p], x[p]) for p in pairs]
    uv = [jnp.concatenate([u[p].astype(BF16), v2[p]], axis=0) for p in pairs]
    if wide:
        o2 = [from_state[p][n2:] + _mm(jnp.concatenate([m_rb[p], m_rk[p]], axis=1), uv[p]) for p in pairs]
    else:
        o2 = [from_state[p][n2:] + _mm(m_rb[p], u[p]) + _mm(m_rk[p], v2[p]) for p in pairs]
    s_new = [states[p] + _mm_tn(uv[p], bk[p]) for p in pairs]
    return o2, s_new


def _rwkv_kernel(r_ref, k_ref, v_ref, an_ref, bb_ref, lw_ref, gw_ref, gb_ref, rk_ref, s0_ref,
                 o_ref, sout_ref, s_scr, *, chunk, seqs):
    t = pl.program_id(1)
    zeros = jnp.zeros((RW_HEAD, RW_HEAD), F32)

    @pl.when(t == 0)
    def _():
        for q in range(seqs):
            for p in range(RW_PAIRS):
                top = jnp.concatenate([s0_ref[q, 2 * p], zeros], axis=1)
                bot = jnp.concatenate([zeros, s0_ref[q, 2 * p + 1]], axis=1)
                s_scr[q, p] = jnp.concatenate([top, bot], axis=0)

    n2 = 2 * chunk
    row = lax.broadcasted_iota(jnp.int32, (n2, n2), 0)
    col = lax.broadcasted_iota(jnp.int32, (n2, n2), 1)

    def second(idx, size):
        return jnp.where(idx >= size, 1, 0)

    same = second(row, chunk) == second(col, chunk)
    strict = same & (row > col)
    incl = same & (row >= col)
    same_head = (second(lax.broadcasted_iota(jnp.int32, (n2, LANES), 0), chunk)
                 == second(lax.broadcasted_iota(jnp.int32, (n2, LANES), 1), RW_HEAD))
    trow = lax.broadcasted_iota(jnp.int32, (chunk, chunk), 0)
    tcol = lax.broadcasted_iota(jnp.int32, (chunk, chunk), 1)
    tril = jnp.where(trow >= tcol, 1.0, 0.0).astype(BF16)
    first_head = lax.broadcasted_iota(jnp.int32, (chunk, LANES), 1) < RW_HEAD

    def head_sum(x):
        lo = jnp.sum(jnp.where(first_head, x, 0.0), axis=-1, keepdims=True)
        hi = jnp.sum(jnp.where(first_head, 0.0, x), axis=-1, keepdims=True)
        return jnp.where(first_head, lo, hi)

    lanes = [slice(p * LANES, (p + 1) * LANES) for p in range(RW_PAIRS)]
    problems = [(q, p) for q in range(seqs) for p in range(RW_PAIRS)]
    rs, ks, vs, ats, rts, bts, kts, ends = [], [], [], [], [], [], [], []
    for q in range(seqs):
        lw = lw_ref[q]
        cum = _mm_exact_lhs(tril, lw)
        w_incl = jnp.exp(cum)
        w_inv = jnp.exp(-cum)
        r, k = r_ref[q].astype(F32), k_ref[q].astype(F32)
        rs.append(r)
        ks.append(k)
        vs.append(v_ref[q].astype(F32))
        ats.append(an_ref[q].astype(F32) * jnp.exp(cum - lw))
        rts.append(r * w_incl)
        bts.append(bb_ref[q].astype(F32) * w_inv)
        kts.append(k * w_inv)
        ends.append(w_incl[chunk - 1:chunk, :])

    def slabs(xs):
        return [xs[q][:, lanes[p]] for q, p in problems]

    o2, s_new = _rwkv_pairs_chunk(
        slabs(ats), slabs(rts), slabs(bts), slabs(kts), slabs(vs), [s_scr[q, p] for q, p in problems],
        chunk=chunk, same_head=same_head, strict=strict, incl=incl)
    for n, (q, p) in enumerate(problems):
        ls = lanes[p]
        s_scr[q, p] = s_new[n] * ends[q][:, ls]
        o = o2[n][:chunk] + o2[n][chunk:]
        mu = head_sum(o) * (1.0 / RW_HEAD)
        oc = o - mu
        var = head_sum(oc * oc) * (1.0 / RW_HEAD)
        on = oc * lax.rsqrt(var + RW_GN_EPS) * gw_ref[:, ls] + gb_ref[:, ls]
        bonus = head_sum(rs[q][:, ls] * ks[q][:, ls] * rk_ref[:, ls]) * vs[q][:, ls]
        o_ref[q, :, ls] = (on + bonus).astype(o_ref.dtype)

    @pl.when(t == pl.num_programs(1) - 1)
    def _():
        for q in range(seqs):
            for p in range(RW_PAIRS):
                s = s_scr[q, p]
                sout_ref[q, 2 * p] = s[:RW_HEAD, :RW_HEAD]
                sout_ref[q, 2 * p + 1] = s[RW_HEAD:, RW_HEAD:]


def _rwkv(grp, seqs, gn_w, gn_b, r_k, s0):
    B, T = grp.B, grp.T
    chunk = min(T, CHUNK)
    nb = min(4, _seqs_per_step(B, T))
    W = RW_HEADS * RW_HEAD
    seq_spec = pl.BlockSpec((nb, chunk, W), lambda b, t: (b, t, 0))
    vec_spec = pl.BlockSpec((1, W), lambda b, t: (0, 0))
    st_spec = pl.BlockSpec((nb, RW_HEADS, RW_HEAD, RW_HEAD), lambda b, t: (b, 0, 0, 0))
    kern = functools.partial(_rwkv_kernel, chunk=chunk, seqs=nb)
    return pl.pallas_call(
        kern,
        grid=(B // nb, T // chunk),
        in_specs=[seq_spec] * 6 + [vec_spec] * 3 + [st_spec],
        out_specs=[seq_spec, st_spec],
        out_shape=[jax.ShapeDtypeStruct((B, T, W), BF16),
                   jax.ShapeDtypeStruct((B, RW_HEADS, RW_HEAD, RW_HEAD), F32)],
        scratch_shapes=[pltpu.VMEM((nb, RW_PAIRS, 2 * RW_HEAD, 2 * RW_HEAD), F32)],
        compiler_params=_params(("arbitrary", "arbitrary")),
        name="rwkv_scan",
    )(*seqs, gn_w, gn_b, r_k, s0)


def _merge_kernel(og_ref, gg_ref, orw_ref, gate_ref, mga_ref, mgb_ref, x_ref,
                  gate1_ref, shift2_ref, scale2_ref,
                  ggn_ref, gpost_ref, gpre_ref, wbg_ref, wbr_ref, wout_ref, wq_ref,
                  x1_ref, h2t_ref, qp_ref):
    gg = gg_ref[...].astype(F32)
    gla_in = og_ref[...].astype(F32) * ggn_ref[...] * (gg * jax.nn.sigmoid(gg))
    br_gla = _mm(gla_in, wbg_ref[...])
    br_rw = _mm(orw_ref[...].astype(F32) * gate_ref[...].astype(F32), wbr_ref[...])
    merged = (jax.nn.sigmoid(mga_ref[...].astype(F32)) * br_gla
              + jax.nn.sigmoid(mgb_ref[...].astype(F32)) * br_rw)
    y = _mm(merged, wout_ref[...])
    x1 = x_ref[...] + gate1_ref[...] * _rms(y, gpost_ref[...])
    x1_ref[...] = x1
    h2 = _rms(x1, gpre_ref[...]) * (1.0 + scale2_ref[...]) + shift2_ref[...]
    h2t_ref[...] = h2.T.astype(BF16)
    qp_ref[...] = jnp.dot(h2.astype(BF16), wq_ref[...], preferred_element_type=F32).astype(BF16)


def _merge(grp, og, p2, orw, gate_rw, x2d, modop, wts):
    tm, M = min(grp.tm, MERGE_TILE), grp.M
    W = D_MODEL

    def tok(blk):
        return pl.BlockSpec((tm, W), lambda i: (i, blk))

    def full(shape):
        return pl.BlockSpec(shape, lambda i: (0,) * len(shape))

    return pl.pallas_call(
        _merge_kernel,
        grid=(M // tm,),
        in_specs=[tok(0), tok(COL_GG // W), tok(0), tok(0), tok(COL_MG // W), tok(COL_MG // W + 1), tok(0),
                  grp.mod_spec(2, tm), grp.mod_spec(3, tm), grp.mod_spec(4, tm),
                  full((1, W)), full((1, W)), full((1, W)),
                  full((W, W)), full((W, W)), full((W, W)), full((W, 2 * W))],
        out_specs=[tok(0), pl.BlockSpec((W, tm), lambda i: (0, i)), pl.BlockSpec((tm, 2 * W), lambda i: (i, 0))],
        out_shape=[jax.ShapeDtypeStruct((M, W), F32), jax.ShapeDtypeStruct((W, M), BF16),
                   jax.ShapeDtypeStruct((M, 2 * W), BF16)],
        compiler_params=_params(("arbitrary",)),
        name="merge",
    )(og, p2, orw, gate_rw, p2, p2, x2d, modop, modop, modop,
      wts["g_gla_norm"], wts["g_post_mix"], wts["g_pre_ffn"],
      wts["w_b_gla"], wts["w_b_rw"], wts["w_out"], wts["peer_wq"])


def _pair_bits(x):
    bits = pltpu.bitcast(x.astype(BF16).astype(F32), jnp.uint32)
    return bits | (bits >> 16)


def _row_tile(row, rows):
    tile = pltpu.bitcast(jnp.broadcast_to(row, (8, row.shape[1])), BF16)
    return jnp.tile(tile, (rows // 16, 1))


def _batcher_network(n):
    def merge(lo, hi, r):
        step = r * 2
        if step < hi - lo:
            yield from merge(lo, hi, step)
            yield from merge(lo + r, hi, step)
            yield from [(i, i + r) for i in range(lo + r, hi - r, step)]
        else:
            yield (lo, lo + r)

    def sort(lo, hi):
        if hi - lo >= 1:
            mid = lo + (hi - lo) // 2
            yield from sort(lo, mid)
            yield from sort(mid + 1, hi)
            yield from merge(lo, hi, 1)

    return list(sort(0, n - 1))


def _top_values(s, n):
    sub = 8
    depth = s.shape[0] // sub
    cols = [s[i * sub:(i + 1) * sub, :] for i in range(depth)]
    for i, j in _batcher_network(depth):
        cols[i], cols[j] = jnp.maximum(cols[i], cols[j]), jnp.minimum(cols[i], cols[j])
    rows = lax.broadcasted_iota(jnp.int32, (n, s.shape[1]), 0)
    vals = jnp.zeros((n, s.shape[1]), F32)
    for it in range(n):
        m = jnp.max(cols[0], axis=0, keepdims=True)
        vals = jnp.where(rows == it, m, vals)
        hit = cols[0] == m
        for i in range(min(depth - 1, n - 1 - it)):
            cols[i] = jnp.where(hit, cols[i + 1], cols[i])
    return vals


def _rank_among(s, vals):
    rank = jnp.zeros_like(s)
    for b in range(vals.shape[0]):
        rank = rank + jnp.where(vals[b:b + 1, :] > s, 1.0, 0.0)
    return rank


def _candidate_groups(v1, v2):
    n, tm = v1.shape
    split = 4
    r_all = lax.broadcasted_iota(jnp.int32, (n, tm), 0)
    r_half = lax.broadcasted_iota(jnp.int32, (n // 2, tm), 0)
    groups = [v1[0:1, :] + v2]
    for a in range(1, split):
        groups.append(jnp.where(r_half < n // (a + 1), v1[a:a + 1, :] + v2[:n // 2, :], NEG_INF))
    groups.append(jnp.where(r_all >= split, v1 + v2[0:1, :], NEG_INF))
    for b in range(1, n // (split + 1)):
        keep = (r_half >= split) & (r_half < n // (b + 1))
        groups.append(jnp.where(keep, v1[:n // 2, :] + v2[b:b + 1, :], NEG_INF))
    return groups


def _group_max(groups, n):
    full = [g for g in groups if g.shape[0] == n]
    best = full[0]
    for g in full[1:]:
        best = jnp.maximum(best, g)
    best = jnp.maximum(best[:n // 2, :], best[n // 2:, :])
    for g in groups:
        if g.shape[0] != n:
            best = jnp.maximum(best, g)
    return jnp.max(best, axis=0, keepdims=True)


def _select_kernel(qp_ref, keys_ref, r2_ref, e2_ref, al_ref, n1_ref):
    tm = qp_ref.shape[0]
    for h in range(PEER_HEADS):
        q1 = qp_ref[:, (2 * h) * 128:(2 * h + 1) * 128]
        q2 = qp_ref[:, (2 * h + 1) * 128:(2 * h + 2) * 128]
        s1 = _mm_nt(keys_ref[2 * h], q1)
        s2 = _mm_nt(keys_ref[2 * h + 1], q2)
        v1 = _top_values(s1, PEER_TOPK)
        v2 = _top_values(s2, PEER_TOPK)
        rank2 = _rank_among(s2, v2)
        groups = _candidate_groups(v1, v2)
        cmax = v1[0:1, :] + v2[0:1, :]
        z = jnp.zeros((1, tm), F32)
        m = cmax
        for _ in range(PEER_TOPK):
            m = _group_max(groups, PEER_TOPK)
            z = z + jnp.exp(m - cmax)
            groups = [jnp.where(g == m, NEG_INF, g) for g in groups]
        th = m - s1
        n1 = jnp.zeros_like(s1)
        for b in range(PEER_TOPK):
            n1 = n1 + jnp.where(v2[b:b + 1, :] >= th, 1.0, 0.0)
        r2_ref[h] = rank2.astype(BF16)
        e2_ref[h] = jnp.exp(s2 - v2[0:1, :]).astype(BF16)
        al_ref[h] = _pair_bits(jnp.exp(s1 - v1[0:1, :]) / z)
        n1_ref[h] = _pair_bits(n1)


def _peer_select(grp, qp, keys):
    tm, M = grp.tm, grp.M
    rows = pl.BlockSpec((PEER_HEADS, PEER_NKEYS, tm), lambda i: (0, 0, i))
    return pl.pallas_call(
        _select_kernel,
        grid=(M // tm,),
        in_specs=[pl.BlockSpec((tm, 2 * D_MODEL), lambda i: (i, 0)),
                  pl.BlockSpec((2 * PEER_HEADS, PEER_NKEYS, 128), lambda i: (0, 0, 0))],
        out_specs=[rows] * 4,
        out_shape=[jax.ShapeDtypeStruct((PEER_HEADS, PEER_NKEYS, M), BF16)] * 2
        + [jax.ShapeDtypeStruct((PEER_HEADS, PEER_NKEYS, M), jnp.uint32)] * 2,
        compiler_params=_params(("arbitrary",)),
        name="peer_select",
    )(qp, keys)


def _peer_weights(act_ref, y_ref, r2_ref, e2_ref, al_ref, n1_ref, row0):
    tm = act_ref.shape[1]
    half = PEER_NKEYS // 2
    for j in range(act_ref.shape[0] // PEER_NKEYS):
        i1 = slice(row0 + j, row0 + j + 1)
        w = [jnp.zeros((half, tm), BF16) for _ in range(2)]
        for h in range(PEER_HEADS):
            n1 = _row_tile(n1_ref[h, i1, :], half)
            al = _row_tile(al_ref[h, i1, :], half)
            for r in range(2):
                keys = slice(r * half, (r + 1) * half)
                w[r] = w[r] + jnp.where(r2_ref[h, keys, :] < n1, e2_ref[h, keys, :], 0.0) * al
        for r in range(2):
            rows = slice(j * PEER_NKEYS + r * half, j * PEER_NKEYS + (r + 1) * half)
            a = act_ref[rows, :].astype(BF16)
            y_ref[rows, :] = (0.5 * a * (1.0 + lax.erf(a * 0.7071067811865476))) * w[r]


def _peer_kernel(h2t_ref, u_ref, vt_ref, r2_ref, e2_ref, al_ref, n1_ref, x1_ref, gate2_ref, gpost_ref,
                 o_ref, acc_ref, act_0, act_1, y_0, y_1):
    e = pl.program_id(1)
    ts = act_0.shape[0]
    per = ts // PEER_NKEYS
    acts, ys = (act_0, act_1), (y_0, y_1)

    @pl.when(e == 0)
    def _():
        acc_ref[...] = jnp.zeros_like(acc_ref)

    def scores(k):
        acts[k % 2][...] = jnp.dot(u_ref[k], h2t_ref[...], preferred_element_type=F32)

    def mix(k):
        acc_ref[...] += jnp.dot(vt_ref[k], ys[k % 2][...], preferred_element_type=F32)

    scores(0)
    scores(1)
    for k in range(PEER_SUBTILES):
        _peer_weights(acts[k % 2], ys[k % 2], r2_ref, e2_ref, al_ref, n1_ref, k * per)
        if k + 2 < PEER_SUBTILES:
            scores(k + 2)
        if k >= 1:
            mix(k - 1)
    mix(PEER_SUBTILES - 1)

    @pl.when(e == pl.num_programs(1) - 1)
    def _():
        o_ref[...] = x1_ref[...] + gate2_ref[...] * _rms(acc_ref[...].T, gpost_ref[...])


def _peer(grp, h2t, sel, x1, modop, g_post, pu, pvt):
    tm, M = grp.tm, grp.M
    ts = PEER_EXPERT_TILE
    te = ts * PEER_SUBTILES
    r2, e2, al, n1 = sel
    packed = pl.BlockSpec((PEER_HEADS, PEER_NKEYS, tm), lambda i, e: (0, 0, i))
    rows = pl.BlockSpec((PEER_HEADS, te // PEER_NKEYS, tm), lambda i, e: (0, e, i))
    tok = pl.BlockSpec((tm, D_MODEL), lambda i, e: (i, 0))
    return pl.pallas_call(
        _peer_kernel,
        grid=(M // tm, pu.shape[0] // PEER_SUBTILES),
        in_specs=[pl.BlockSpec((D_MODEL, tm), lambda i, e: (0, i)),
                  pl.BlockSpec((PEER_SUBTILES, ts, D_MODEL), lambda i, e: (e, 0, 0)),
                  pl.BlockSpec((PEER_SUBTILES, D_MODEL, ts), lambda i, e: (e, 0, 0)),
                  packed, packed, rows, rows,
                  pl.BlockSpec((tm, D_MODEL), lambda i, e: (i, 0), pipeline_mode=pl.Buffered(1)),
                  grp.mod_spec(5),
                  pl.BlockSpec((1, D_MODEL), lambda i, e: (0, 0))],
        out_specs=tok,
        out_shape=jax.ShapeDtypeStruct((M, D_MODEL), F32),
        scratch_shapes=[pltpu.VMEM((D_MODEL, tm), F32),
                        pltpu.VMEM((ts, tm), F32), pltpu.VMEM((ts, tm), F32),
                        pltpu.VMEM((ts, tm), BF16), pltpu.VMEM((ts, tm), BF16)],
        compiler_params=_params(("arbitrary", "arbitrary")),
        name="peer_dense",
    )(h2t, pu, pvt, r2, e2, al, n1, x1, modop, g_post)


def _prepare_weights(w_in, w_gla_gate, b_gla_gate, g_gla_norm, mu_shift, rw_w0, rw_w2, rw_a0, rw_a2,
                     rw_g2, rw_k_k, rw_k_a, rw_r_k, rw_gn_w, rw_gn_b, w_b_gla, w_b_rw, w_out,
                     peer_wq, peer_keys, peer_u, peer_v, g_pre_mix, g_post_mix, g_pre_ffn, g_post_ffn):
    W = RW_HEADS * RW_HEAD
    rw = w_in[:, 3088:6416]
    w_proj = jnp.concatenate(
        [rw[:, :3 * W], w_in[:, 6416:8464], w_in[:, 2048:3072], w_in[:, 1024:2048], w_in[:, 0:1024],
         rw[:, 3 * W:], w_in[:, 3072:3088], jnp.zeros((D_MODEL, PROJ_PAD_W - 8464), F32)],
        axis=1).astype(BF16)
    head = jnp.arange(W) // RW_HEAD
    zeros64 = jnp.zeros((64, W), F32)
    return dict(
        w_proj=w_proj,
        wg_pad=jnp.concatenate([w_gla_gate, jnp.zeros((128 - GLA_GATE_RANK, 512), F32)], axis=0),
        bg=b_gla_gate.reshape(1, -1),
        g_gla_norm=g_gla_norm.reshape(1, -1),
        mu_rkv=mu_shift[:3 * W].reshape(1, -1),
        mu_lora=mu_shift[3 * W:].reshape(1, -1),
        rw_w0=rw_w0.reshape(1, -1),
        w2p=jnp.concatenate([rw_w2, jnp.zeros((192, W), F32)], axis=0),
        rw_a0=rw_a0.reshape(1, -1),
        a2p=jnp.concatenate([zeros64, rw_a2, jnp.zeros((128, W), F32)], axis=0),
        g2p=jnp.concatenate([jnp.zeros((128, W), F32), rw_g2], axis=0),
        rw_k_k=rw_k_k.reshape(1, -1),
        rw_k_a=rw_k_a.reshape(1, -1),
        rw_r_k=rw_r_k.reshape(1, -1),
        rw_gn_w=rw_gn_w.reshape(1, -1),
        rw_gn_b=rw_gn_b.reshape(1, -1),
        head_ones=(head[:, None] == head[None, :]).astype(BF16),
        w_b_gla=w_b_gla.astype(BF16),
        w_b_rw=w_b_rw.astype(BF16),
        w_out=w_out.astype(BF16),
        peer_wq=peer_wq.astype(BF16),
        peer_keys=peer_keys.reshape(2 * PEER_HEADS, PEER_NKEYS, 128).astype(BF16),
        peer_u=peer_u.astype(BF16).reshape(-1, PEER_EXPERT_TILE, D_MODEL),
        peer_vt=peer_v.reshape(-1, PEER_EXPERT_TILE, D_MODEL).transpose(0, 2, 1).astype(BF16),
        g_pre_mix=g_pre_mix.reshape(1, -1),
        g_post_mix=g_post_mix.reshape(1, -1),
        g_pre_ffn=g_pre_ffn.reshape(1, -1),
        g_post_ffn=g_post_ffn.reshape(1, -1),
    )


def _block(x, mod, s_gla0, s_rw0, shift0, wts, valid):
    B, T, _ = x.shape
    grp = _Group(B, T, valid)
    W = RW_HEADS * RW_HEAD
    x2d = x.reshape(grp.M, D_MODEL)
    modop = grp.seq_operand(mod)
    p2 = _projection(grp, x2d, modop, wts["g_pre_mix"], wts["w_proj"], BF16)
    p3 = p2.reshape(B, T, PROJ_PAD_W)
    og, s_gla = _gla(grp, p3, wts["wg_pad"], wts["bg"], s_gla0)
    r, k, v, an, bb, lw, gate = _rwkv_prep(grp, p2, shift0, wts)
    seqs = tuple(z.reshape(B, T, W) for z in (r, k, v, an, bb, lw))
    orw, s_rw = _rwkv(grp, seqs, wts["rw_gn_w"], wts["rw_gn_b"], wts["rw_r_k"], s_rw0)
    x1, h2t, qp = _merge(grp, og.reshape(grp.M, -1), p2, orw.reshape(grp.M, W), gate,
                        x2d, modop, wts)
    sel = _peer_select(grp, qp, wts["peer_keys"])
    out = _peer(grp, h2t, sel, x1, modop, wts["g_post_ffn"], wts["peer_u"], wts["peer_vt"])
    ends = _Group(B, 1, 1)
    last = _projection(ends, x[:, valid - 1, :], ends.seq_operand(mod), wts["g_pre_mix"], wts["w_proj"], F32)
    shift_new = jnp.concatenate([last[:, :3 * W], last[:, COL_LORA:COL_LORA + 256]], axis=-1)
    return out.reshape(B, T, D_MODEL), s_gla, s_rw, shift_new


def kernel(x_prompt, x_sample, c_prompt, c_sample, state_gla, state_rwkv, state_shift, w_ada, b_ada, g_pre_mix, g_post_mix, g_pre_ffn, g_post_ffn, w_in, w_gla_gate, b_gla_gate, g_gla_norm, mu_shift, rw_w0, rw_w2, rw_a0, rw_a2, rw_g2, rw_k_k, rw_k_a, rw_r_k, rw_gn_w, rw_gn_b, w_b_gla, w_b_rw, w_out, peer_wq, peer_keys, peer_u, peer_v):
    depth = w_in.shape[0]
    bp, bs = x_prompt.shape[0], x_sample.shape[0]
    ts = x_sample.shape[1]
    ts_pad = -(-ts // 8) * 8
    xp = x_prompt
    xs = jnp.pad(x_sample, ((0, 0), (0, ts_pad - ts), (0, 0)))
    outs = [[] for _ in range(6)]
    for l in range(depth):
        wts = _prepare_weights(
            w_in[l], w_gla_gate[l], b_gla_gate[l], g_gla_norm[l], mu_shift[l], rw_w0[l], rw_w2[l],
            rw_a0[l], rw_a2[l], rw_g2[l], rw_k_k[l], rw_k_a[l], rw_r_k[l], rw_gn_w[l], rw_gn_b[l],
            w_b_gla[l], w_b_rw[l], w_out[l], peer_wq[l], peer_keys[l], peer_u[l], peer_v[l],
            g_pre_mix[l], g_post_mix[l], g_pre_ffn[l], g_post_ffn[l])
        mod = _modulation(jnp.concatenate([c_prompt, c_sample], axis=0), w_ada[l], b_ada[l])
        zg = jnp.zeros((bp, GLA_HEADS, GLA_DK, GLA_DV), F32)
        zr = jnp.zeros((bp, RW_HEADS, RW_HEAD, RW_HEAD), F32)
        zs = jnp.zeros((bp, RW_SHIFT_W), F32)
        xp, g1, r1, s1 = _block(xp, mod[:bp], zg, zr, zs, wts, xp.shape[1])
        xs, g2, r2, s2 = _block(xs, mod[bp:], state_gla[l], state_rwkv[l], state_shift[l], wts, ts)
        for acc, val in zip(outs, (g1, r1, s1, g2, r2, s2)):
            acc.append(val)
    stacked = [jnp.stack(o) for o in outs]
    return (xp, xs[:, :ts], *stacked)
```

```python
import functools

import jax
import jax.numpy as jnp
from jax import lax
from jax.experimental import pallas as pl
from jax.experimental.pallas import tpu as pltpu

F32 = jnp.float32
BF16 = jnp.bfloat16

D_MODEL = 1024
GLA_HEADS = 4
GLA_DK = 128
GLA_DV = 256
GLA_GATE_RANK = 16
GLA_GATE_TAU = 16.0
GLA_NORM_EPS = 1e-5
RW_HEAD = 64
RW_HEADS = 16
RW_PAIRS = RW_HEADS // 2
RW_GN_EPS = 64e-5
RW_SHIFT_W = 3328
PEER_HEADS = 8
PEER_NKEYS = 128
PEER_TOPK = 16
NORM_EPS = 1e-6
LANES = 128

COL_R, COL_K, COL_V = 0, 1024, 2048
COL_MG = 3072
COL_GG = 5120
COL_GV = 6144
COL_GQ = 7168
COL_GK = 7680
COL_LORA = 8192
COL_GA = 8448
PROJ_PAD_W = 8704
PROJ_TILE_N = 2176

CHUNK = 64
TOKEN_TILE = 512
MERGE_TILE = 256
PEER_EXPERT_TILE = 512
PEER_SUBTILES = 4
VMEM_LIMIT = 56 * 1024 * 1024

NEG_INF = float("-inf")


def _mm(a, b):
    return jnp.dot(a.astype(BF16), b.astype(BF16), preferred_element_type=F32)


def _mm_nt(a, b):
    return lax.dot_general(a.astype(BF16), b.astype(BF16), (((1,), (1,)), ((), ())),
                           preferred_element_type=F32)


def _mm_tn(a, b):
    return lax.dot_general(a.astype(BF16), b.astype(BF16), (((0,), (0,)), ((), ())),
                           preferred_element_type=F32)


def _split(x):
    hi = x.astype(BF16)
    lo = (x - hi.astype(F32)).astype(BF16)
    return hi, lo


def _mm_exact_lhs(a, x):
    hi, lo = _split(x)
    a = a.astype(BF16)
    return (jnp.dot(a, hi, preferred_element_type=F32)
            + jnp.dot(a, lo, preferred_element_type=F32))


def _mm_exact_rhs(x, a):
    hi, lo = _split(x)
    a = a.astype(BF16)
    return (jnp.dot(hi, a, preferred_element_type=F32)
            + jnp.dot(lo, a, preferred_element_type=F32))


def _softplus(x):
    return jnp.maximum(x, 0.0) + jnp.log1p(jnp.exp(-jnp.abs(x)))


def _rms(x, g):
    return x * lax.rsqrt(jnp.mean(x * x, axis=-1, keepdims=True) + NORM_EPS) * g


def _params(sem, flags=None):
    return pltpu.CompilerParams(dimension_semantics=sem, vmem_limit_bytes=VMEM_LIMIT, flags=flags)


def _mod_kernel(c_ref, w_ref, b_ref, o_ref):
    c = c_ref[...]
    o_ref[...] = _mm(c * jax.nn.sigmoid(c), w_ref[...]) + b_ref[...]


def _modulation(c, w_ada, b_ada):
    rows = c.shape[0]
    n = w_ada.shape[1]
    tn = 1536
    return pl.pallas_call(
        _mod_kernel,
        grid=(n // tn,),
        in_specs=[pl.BlockSpec((rows, D_MODEL), lambda j: (0, 0)),
                  pl.BlockSpec((D_MODEL, tn), lambda j: (0, j)),
                  pl.BlockSpec((1, tn), lambda j: (0, j))],
        out_specs=pl.BlockSpec((rows, tn), lambda j: (0, j)),
        out_shape=jax.ShapeDtypeStruct((rows, n), F32),
        compiler_params=_params(("arbitrary",)),
        name="adaln_mod",
    )(c, w_ada, b_ada.reshape(1, n))


class _Group:
    def __init__(self, batch, seq, valid):
        self.B, self.T, self.Tv = batch, seq, valid
        self.M = batch * seq
        self.tm = min(TOKEN_TILE, self.M)
        self.per_batch_mod = seq % self.tm == 0

    def seq_operand(self, arr):
        if self.per_batch_mod:
            return arr.reshape(self.B, 1, arr.shape[-1])
        return jnp.repeat(arr, self.T, axis=0)

    def seq_spec(self, width, col, tm=None, tile_index=None):
        tm, T = tm or self.tm, self.T
        tile = tile_index or (lambda i, *_: i)
        if self.per_batch_mod:
            return pl.BlockSpec((None, 1, width), lambda *g: ((tile(*g) * tm) // T, 0, col))
        return pl.BlockSpec((tm, width), lambda *g: (tile(*g), col))

    def mod_spec(self, col, tm=None, tile_index=None):
        return self.seq_spec(D_MODEL, col, tm, tile_index)


def _proj_kernel(x_ref, shift_ref, scale_ref, g_ref, w_ref, p_ref, h_scr):
    @pl.when(pl.program_id(1) == 0)
    def _():
        h = _rms(x_ref[...], g_ref[...]) * (1.0 + scale_ref[...]) + shift_ref[...]
        h_scr[...] = h.astype(BF16)

    p_ref[...] = jnp.dot(h_scr[...], w_ref[...], preferred_element_type=F32).astype(p_ref.dtype)


def _projection(grp, x2d, modop, g_pre, w_proj, dtype):
    tm = grp.tm
    return pl.pallas_call(
        _proj_kernel,
        grid=(grp.M // tm, PROJ_PAD_W // PROJ_TILE_N),
        in_specs=[pl.BlockSpec((tm, D_MODEL), lambda i, j: (i, 0)),
                  grp.mod_spec(0), grp.mod_spec(1),
                  pl.BlockSpec((1, D_MODEL), lambda i, j: (0, 0)),
                  pl.BlockSpec((D_MODEL, PROJ_TILE_N), lambda i, j: (0, j))],
        out_specs=pl.BlockSpec((tm, PROJ_TILE_N), lambda i, j: (i, j)),
        out_shape=jax.ShapeDtypeStruct((grp.M, PROJ_PAD_W), dtype),
        scratch_shapes=[pltpu.VMEM((tm, D_MODEL), BF16)],
        compiler_params=_params(("arbitrary", "arbitrary")),
        name="in_proj",
    )(x2d, modop, modop, g_pre, w_proj)


def _gla_kernel(q_ref, k_ref, v_ref, ga_ref, wg_ref, bg_ref, s0_ref, og_ref, sout_ref, st_scr,
                *, chunk, n_chunks, tile, valid, seqs):
    t = pl.program_id(1)
    last = pl.num_programs(1) - 1
    row = lax.broadcasted_iota(jnp.int32, (tile, tile), 0)
    col = lax.broadcasted_iota(jnp.int32, (tile, tile), 1)
    shift = chunk.bit_length() - 1
    causal = (lax.shift_right_logical(row, shift) == lax.shift_right_logical(col, shift)) & (row >= col)
    tril = jnp.where(causal, 1.0, 0.0).astype(BF16)

    def per_seq(b, carry):
        @pl.when(t == 0)
        def _():
            for h in range(GLA_HEADS):
                st_scr[b, h] = s0_ref[b, h].T

        heads, chunks = range(GLA_HEADS), range(n_chunks)
        ks = [slice(h * GLA_DK, (h + 1) * GLA_DK) for h in heads]
        vs = [slice(h * GLA_DV, (h + 1) * GLA_DV) for h in heads]
        rows = [slice(c * chunk, (c + 1) * chunk) for c in chunks]
        ends = [slice((c + 1) * chunk - 1, (c + 1) * chunk) for c in chunks]

        z = _mm(ga_ref[b], wg_ref[...]) + bg_ref[...]
        la = -_softplus(-z) / GLA_GATE_TAU
        k = k_ref[b].astype(F32)
        v = [v_ref[b, :, vs[h]] for h in heads]
        if valid is not None:
            ok = (t * tile + lax.broadcasted_iota(jnp.int32, (tile, 1), 0)) < valid
            la = jnp.where(ok, la, 0.0)
            k = jnp.where(ok, k, 0.0)
            v = [jnp.where(ok, x, 0.0) for x in v]
        cum = _mm_exact_lhs(tril, la)
        end = jnp.concatenate([jnp.broadcast_to(cum[ends[c], :], (chunk, cum.shape[1])) for c in chunks], axis=0)
        qe = q_ref[b].astype(F32) * (GLA_DK ** -0.5) * jnp.exp(cum)
        ke = k * jnp.exp(-cum)
        kd = k * jnp.exp(end - cum)
        decay = [jnp.exp(cum[ends[c], :]) for c in chunks]
        a = [jnp.where(causal, _mm_nt(qe[:, ks[h]], ke[:, ks[h]]), 0.0) for h in heads]
        inside = [_mm(a[h], v[h]) for h in heads]
        update = [[_mm_tn(v[h][rows[c], :], kd[rows[c], ks[h]]) for h in heads] for c in chunks]

        st = [[st_scr[b, h] for h in heads]]
        for c in chunks:
            st.append([st[c][h] * decay[c][:, ks[h]] + update[c][h] for h in heads])
        outer = [[_mm_nt(qe[rows[c], ks[h]], st[c][h]) for h in heads] for c in chunks]
        for h in heads:
            st_scr[b, h] = st[n_chunks][h]
            for c in chunks:
                o = outer[c][h] + inside[h][rows[c], :]
                og = o * lax.rsqrt(jnp.mean(o * o, axis=-1, keepdims=True) + GLA_NORM_EPS)
                og_ref[b, rows[c], vs[h]] = og.astype(og_ref.dtype)

        @pl.when(t == last)
        def _():
            for h in range(GLA_HEADS):
                sout_ref[b, h] = st_scr[b, h].T

        return carry

    if seqs == 1:
        per_seq(0, 0)
    else:
        lax.fori_loop(0, seqs, per_seq, 0)


def _seqs_per_step(batch, seq):
    n = max(1, min(8, 64 // seq))
    while batch % n:
        n //= 2
    return n


def _gla(grp, p3, wg_pad, bg, s0):
    B, T = grp.B, grp.T
    tile = min(T, 4 * CHUNK)
    chunk = min(CHUNK, tile)
    nb = _seqs_per_step(B, T)
    kern = functools.partial(_gla_kernel, chunk=chunk, n_chunks=tile // chunk, tile=tile,
                             valid=None if grp.Tv == T else grp.Tv, seqs=nb)
    kw, vw = GLA_HEADS * GLA_DK, GLA_HEADS * GLA_DV
    st_spec = pl.BlockSpec((nb, GLA_HEADS, GLA_DK, GLA_DV), lambda i, t: (i, 0, 0, 0))
    return pl.pallas_call(
        kern,
        grid=(B // nb, T // tile),
        in_specs=[pl.BlockSpec((nb, tile, kw), lambda i, t: (i, t, COL_GQ // kw)),
                  pl.BlockSpec((nb, tile, kw), lambda i, t: (i, t, COL_GK // kw)),
                  pl.BlockSpec((nb, tile, vw), lambda i, t: (i, t, COL_GV // vw)),
                  pl.BlockSpec((nb, tile, LANES), lambda i, t: (i, t, COL_GA // LANES)),
                  pl.BlockSpec((LANES, kw), lambda i, t: (0, 0)),
                  pl.BlockSpec((1, kw), lambda i, t: (0, 0)),
                  st_spec],
        out_specs=[pl.BlockSpec((nb, tile, vw), lambda i, t: (i, t, 0)), st_spec],
        out_shape=[jax.ShapeDtypeStruct((B, T, vw), BF16),
                   jax.ShapeDtypeStruct((B, GLA_HEADS, GLA_DK, GLA_DV), F32)],
        scratch_shapes=[pltpu.VMEM((nb, GLA_HEADS, GLA_DV, GLA_DK), F32)],
        compiler_params=_params(("arbitrary", "arbitrary")),
        name="gla_scan",
    )(p3, p3, p3, p3, wg_pad, bg, s0)


def _rwprep_kernel(r_ref, k_ref, v_ref, l_ref, pr_ref, pk_ref, pv_ref, pl_ref,
                   s0r_ref, s0k_ref, s0v_ref, s0l_ref, mu_ref, mul_ref, w0_ref, w2_ref, a0_ref, a2_ref, g2_ref,
                   kk_ref, ka_ref, bd_ref,
                   ro_ref, ko_ref, vo_ref, an_ref, bb_ref, lw_ref, gate_ref,
                   *, tile, seq, valid, prev_rows):
    i = pl.program_id(0)
    local = lax.broadcasted_iota(jnp.int32, (tile, 1), 0)
    pos = lax.rem(i * tile + local, seq)
    first = pos == 0
    top = local == 0

    def shifted(cur_ref, prev_ref, s0_ref, mu):
        cur = cur_ref[...].astype(F32)
        before = prev_ref[prev_rows - 1:prev_rows, :].astype(F32)
        prev = jnp.where(first, s0_ref[...], jnp.where(top, before, pltpu.roll(cur, 1, 0)))
        return cur + (prev - cur) * mu

    r = shifted(r_ref, pr_ref, s0r_ref, mu_ref[:, 0:1024])
    k = shifted(k_ref, pk_ref, s0k_ref, mu_ref[:, 1024:2048])
    v = shifted(v_ref, pv_ref, s0v_ref, mu_ref[:, 2048:3072])
    lo = shifted(l_ref, pl_ref, s0l_ref, mul_ref[...])

    wpre = w0_ref[...] + _mm(jnp.tanh(lo), w2_ref[...])
    wlog = -_softplus(-wpre) - 0.5
    lw = -jnp.exp(wlog)
    a = jax.nn.sigmoid(a0_ref[...] + _mm(lo, a2_ref[...]))
    gate = _mm(jax.nn.sigmoid(lo), g2_ref[...])

    kk = k * kk_ref[...]
    ss = _mm(kk * kk, bd_ref[...])
    kk = kk / jnp.maximum(jnp.sqrt(ss), 1e-12)
    kmod = k * (1.0 + (a - 1.0) * ka_ref[...])
    an = -kk
    bb = kk * a
    if valid is not None:
        ok = pos < valid
        lw = jnp.where(ok, lw, 0.0)
        an = jnp.where(ok, an, 0.0)
        bb = jnp.where(ok, bb, 0.0)
        kmod = jnp.where(ok, kmod, 0.0)
        v = jnp.where(ok, v, 0.0)
    ro_ref[...] = r.astype(ro_ref.dtype)
    ko_ref[...] = kmod.astype(ko_ref.dtype)
    vo_ref[...] = v.astype(vo_ref.dtype)
    an_ref[...] = an.astype(an_ref.dtype)
    bb_ref[...] = bb.astype(bb_ref.dtype)
    lw_ref[...] = lw
    gate_ref[...] = gate.astype(gate_ref.dtype)


def _rwkv_prep(grp, p2, shift0, wts):
    M, T = grp.M, grp.T
    tile = min(M, 256)
    prev_rows = min(16, tile)
    W = RW_HEADS * RW_HEAD
    s0 = grp.seq_operand(shift0)

    def cur(width, blk):
        return pl.BlockSpec((tile, width), lambda i: (i, blk))

    def prev(width, blk):
        return pl.BlockSpec((prev_rows, width), lambda i: (jnp.maximum(i * (tile // prev_rows) - 1, 0), blk))

    def full(shape):
        return pl.BlockSpec(shape, lambda i: (0,) * len(shape))

    out_spec = pl.BlockSpec((tile, W), lambda i: (i, 0))
    kern = functools.partial(_rwprep_kernel, tile=tile, seq=T, valid=None if grp.Tv == T else grp.Tv,
                             prev_rows=prev_rows)
    lora_blk = COL_LORA // 256
    return pl.pallas_call(
        kern,
        grid=(M // tile,),
        in_specs=[cur(W, 0), cur(W, 1), cur(W, 2), cur(256, lora_blk),
                  prev(W, 0), prev(W, 1), prev(W, 2), prev(256, lora_blk),
                  grp.seq_spec(W, 0, tile), grp.seq_spec(W, 1, tile), grp.seq_spec(W, 2, tile),
                  grp.seq_spec(256, 3 * W // 256, tile),
                  full((1, 3 * W)), full((1, 256)),
                  full((1, W)), full((256, W)), full((1, W)), full((256, W)), full((256, W)),
                  full((1, W)), full((1, W)), full((W, W))],
        out_specs=[out_spec] * 7,
        out_shape=[jax.ShapeDtypeStruct((M, W), BF16)] * 5
        + [jax.ShapeDtypeStruct((M, W), F32), jax.ShapeDtypeStruct((M, W), BF16)],
        compiler_params=_params(("arbitrary",)),
        name="rwkv_prep",
    )(p2, p2, p2, p2, p2, p2, p2, p2, s0, s0, s0, s0,
      wts["mu_rkv"], wts["mu_lora"], wts["rw_w0"], wts["w2p"], wts["rw_a0"], wts["a2p"], wts["g2p"],
      wts["rw_k_k"], wts["rw_k_a"], wts["head_ones"])


def _rwkv_pairs_chunk(at, rt, bt, kt, v, states, *, chunk, same_head, strict, incl):
    n2 = 2 * chunk
    wide = n2 % LANES == 0
    pairs = range(len(states))

    def blocks(x):
        return jnp.where(same_head, jnp.concatenate([x, x], axis=0), 0.0).astype(BF16)

    a2 = [blocks(x) for x in at]
    r2 = [blocks(x) for x in rt]
    b2 = [blocks(x) for x in bt]
    k2 = [blocks(x) for x in kt]
    v2 = [blocks(x) for x in v]
    ar = [jnp.concatenate([a2[p], r2[p]], axis=0) for p in pairs]
    bk = [jnp.concatenate([b2[p], k2[p]], axis=0) for p in pairs]
    if wide:
        g = [_mm_nt(ar[p], bk[p]) for p in pairs]
        l_ab = [jnp.where(strict, g[p][:n2, :n2], 0.0) for p in pairs]
        l_ak = [jnp.where(strict, g[p][:n2, n2:], 0.0) for p in pairs]
        m_rb = [jnp.where(incl, g[p][n2:, :n2], 0.0) for p in pairs]
        m_rk = [jnp.where(incl, g[p][n2:, n2:], 0.0) for p in pairs]
    else:
        l_ab = [jnp.where(strict, _mm_nt(a2[p], b2[p]), 0.0) for p in pairs]
        l_ak = [jnp.where(strict, _mm_nt(a2[p], k2[p]), 0.0) for p in pairs]
        m_rb = [jnp.where(incl, _mm_nt(r2[p], b2[p]), 0.0) for p in pairs]
        m_rk = [jnp.where(incl, _mm_nt(r2[p], k2[p]), 0.0) for p in pairs]
    from_state = [_mm_nt(ar[p], states[p]) for p in pairs]
    x = [from_state[p][:n2] + _mm(l_ak[p], v2[p]) for p in pairs]
    pw = l_ab
    for _ in range(chunk.bit_length() - 2):
        if wide:
            y = [_mm(pw[p], jnp.concatenate([x[p], pw[p]], axis=1)) for p in pairs]
            x = [x[p] + y[p][:, :LANES] for p in pairs]
            pw = [y[p][:, LANES:] for p in pairs]
        else:
            x = [x[p] + _mm(pw[p], x[p]) for p in pairs]
            pw = [_mm(pw[p], pw[p]) for p in pairs]
    u = [x[p] + _mm(pw[p], x[p]) for p in pairs]
    uv = [jnp.concatenate([u[p].astype(BF16), v2[p]], axis=0) for p in pairs]
    if wide:
        o2 = [from_state[p][n2:] + _mm(jnp.concatenate([m_rb[p], m_rk[p]], axis=1), uv[p]) for p in pairs]
    else:
        o2 = [from_state[p][n2:] + _mm(m_rb[p], u[p]) + _mm(m_rk[p], v2[p]) for p in pairs]
    s_new = [states[p] + _mm_tn(uv[p], bk[p]) for p in pairs]
    return o2, s_new


def _rwkv_kernel(r_ref, k_ref, v_ref, an_ref, bb_ref, lw_ref, gw_ref, gb_ref, rk_ref, s0_ref,
                 o_ref, sout_ref, s_scr, *, chunk, seqs):
    t = pl.program_id(1)
    zeros = jnp.zeros((RW_HEAD, RW_HEAD), F32)

    @pl.when(t == 0)
    def _():
        for q in range(seqs):
            for p in range(RW_PAIRS):
                top = jnp.concatenate([s0_ref[q, 2 * p], zeros], axis=1)
                bot = jnp.concatenate([zeros, s0_ref[q, 2 * p + 1]], axis=1)
                s_scr[q, p] = jnp.concatenate([top, bot], axis=0)

    n2 = 2 * chunk
    row = lax.broadcasted_iota(jnp.int32, (n2, n2), 0)
    col = lax.broadcasted_iota(jnp.int32, (n2, n2), 1)

    def second(idx, size):
        return jnp.where(idx >= size, 1, 0)

    same = second(row, chunk) == second(col, chunk)
    strict = same & (row > col)
    incl = same & (row >= col)
    same_head = (second(lax.broadcasted_iota(jnp.int32, (n2, LANES), 0), chunk)
                 == second(lax.broadcasted_iota(jnp.int32, (n2, LANES), 1), RW_HEAD))
    trow = lax.broadcasted_iota(jnp.int32, (chunk, chunk), 0)
    tcol = lax.broadcasted_iota(jnp.int32, (chunk, chunk), 1)
    tril = jnp.where(trow >= tcol, 1.0, 0.0).astype(BF16)
    first_head = lax.broadcasted_iota(jnp.int32, (chunk, LANES), 1) < RW_HEAD

    def head_sum(x):
        lo = jnp.sum(jnp.where(first_head, x, 0.0), axis=-1, keepdims=True)
        hi = jnp.sum(jnp.where(first_head, 0.0, x), axis=-1, keepdims=True)
        return jnp.where(first_head, lo, hi)

    lanes = [slice(p * LANES, (p + 1) * LANES) for p in range(RW_PAIRS)]
    problems = [(q, p) for q in range(seqs) for p in range(RW_PAIRS)]
    rs, ks, vs, ats, rts, bts, kts, ends = [], [], [], [], [], [], [], []
    for q in range(seqs):
        lw = lw_ref[q]
        cum = _mm_exact_lhs(tril, lw)
        w_incl = jnp.exp(cum)
        w_inv = jnp.exp(-cum)
        r, k = r_ref[q].astype(F32), k_ref[q].astype(F32)
        rs.append(r)
        ks.append(k)
        vs.append(v_ref[q].astype(F32))
        ats.append(an_ref[q].astype(F32) * jnp.exp(cum - lw))
        rts.append(r * w_incl)
        bts.append(bb_ref[q].astype(F32) * w_inv)
        kts.append(k * w_inv)
        ends.append(w_incl[chunk - 1:chunk, :])

    def slabs(xs):
        return [xs[q][:, lanes[p]] for q, p in problems]

    o2, s_new = _rwkv_pairs_chunk(
        slabs(ats), slabs(rts), slabs(bts), slabs(kts), slabs(vs), [s_scr[q, p] for q, p in problems],
        chunk=chunk, same_head=same_head, strict=strict, incl=incl)
    for n, (q, p) in enumerate(problems):
        ls = lanes[p]
        s_scr[q, p] = s_new[n] * ends[q][:, ls]
        o = o2[n][:chunk] + o2[n][chunk:]
        mu = head_sum(o) * (1.0 / RW_HEAD)
        oc = o - mu
        var = head_sum(oc * oc) * (1.0 / RW_HEAD)
        on = oc * lax.rsqrt(var + RW_GN_EPS) * gw_ref[:, ls] + gb_ref[:, ls]
        bonus = head_sum(rs[q][:, ls] * ks[q][:, ls] * rk_ref[:, ls]) * vs[q][:, ls]
        o_ref[q, :, ls] = (on + bonus).astype(o_ref.dtype)

    @pl.when(t == pl.num_programs(1) - 1)
    def _():
        for q in range(seqs):
            for p in range(RW_PAIRS):
                s = s_scr[q, p]
                sout_ref[q, 2 * p] = s[:RW_HEAD, :RW_HEAD]
                sout_ref[q, 2 * p + 1] = s[RW_HEAD:, RW_HEAD:]


def _rwkv(grp, seqs, gn_w, gn_b, r_k, s0):
    B, T = grp.B, grp.T
    chunk = min(T, CHUNK)
    nb = min(4, max(_seqs_per_step(B, T), 2 - B % 2))
    W = RW_HEADS * RW_HEAD
    seq_spec = pl.BlockSpec((nb, chunk, W), lambda b, t: (b, t, 0))
    vec_spec = pl.BlockSpec((1, W), lambda b, t: (0, 0))
    st_spec = pl.BlockSpec((nb, RW_HEADS, RW_HEAD, RW_HEAD), lambda b, t: (b, 0, 0, 0))
    kern = functools.partial(_rwkv_kernel, chunk=chunk, seqs=nb)
    return pl.pallas_call(
        kern,
        grid=(B // nb, T // chunk),
        in_specs=[seq_spec] * 6 + [vec_spec] * 3 + [st_spec],
        out_specs=[seq_spec, st_spec],
        out_shape=[jax.ShapeDtypeStruct((B, T, W), BF16),
                   jax.ShapeDtypeStruct((B, RW_HEADS, RW_HEAD, RW_HEAD), F32)],
        scratch_shapes=[pltpu.VMEM((nb, RW_PAIRS, 2 * RW_HEAD, 2 * RW_HEAD), F32)],
        compiler_params=_params(("arbitrary", "arbitrary")),
        name="rwkv_scan",
    )(*seqs, gn_w, gn_b, r_k, s0)


def _merge_kernel(og_ref, gg_ref, orw_ref, gate_ref, mga_ref, mgb_ref, x_ref,
                  gate1_ref, shift2_ref, scale2_ref,
                  ggn_ref, gpost_ref, gpre_ref, wbg_ref, wbr_ref, wout_ref, wq_ref,
                  x1_ref, h2t_ref, qp_ref):
    gg = gg_ref[...].astype(F32)
    gla_in = og_ref[...].astype(F32) * ggn_ref[...] * (gg * jax.nn.sigmoid(gg))
    br_gla = _mm(gla_in, wbg_ref[...])
    br_rw = _mm(orw_ref[...].astype(F32) * gate_ref[...].astype(F32), wbr_ref[...])
    merged = (jax.nn.sigmoid(mga_ref[...].astype(F32)) * br_gla
              + jax.nn.sigmoid(mgb_ref[...].astype(F32)) * br_rw)
    y = _mm(merged, wout_ref[...])
    x1 = x_ref[...] + gate1_ref[...] * _rms(y, gpost_ref[...])
    x1_ref[...] = x1
    h2 = _rms(x1, gpre_ref[...]) * (1.0 + scale2_ref[...]) + shift2_ref[...]
    h2t_ref[...] = h2.T.astype(BF16)
    qp_ref[...] = jnp.dot(h2.astype(BF16), wq_ref[...], preferred_element_type=F32).astype(BF16)


def _merge(grp, og, p2, orw, gate_rw, x2d, modop, wts):
    tm, M = min(grp.tm, MERGE_TILE), grp.M
    W = D_MODEL

    def tok(blk):
        return pl.BlockSpec((tm, W), lambda i: (i, blk))

    def full(shape):
        return pl.BlockSpec(shape, lambda i: (0,) * len(shape))

    return pl.pallas_call(
        _merge_kernel,
        grid=(M // tm,),
        in_specs=[tok(0), tok(COL_GG // W), tok(0), tok(0), tok(COL_MG // W), tok(COL_MG // W + 1), tok(0),
                  grp.mod_spec(2, tm), grp.mod_spec(3, tm), grp.mod_spec(4, tm),
                  full((1, W)), full((1, W)), full((1, W)),
                  full((W, W)), full((W, W)), full((W, W)), full((W, 2 * W))],
        out_specs=[tok(0), pl.BlockSpec((W, tm), lambda i: (0, i)), pl.BlockSpec((tm, 2 * W), lambda i: (i, 0))],
        out_shape=[jax.ShapeDtypeStruct((M, W), F32), jax.ShapeDtypeStruct((W, M), BF16),
                   jax.ShapeDtypeStruct((M, 2 * W), BF16)],
        compiler_params=_params(("arbitrary",)),
        name="merge",
    )(og, p2, orw, gate_rw, p2, p2, x2d, modop, modop, modop,
      wts["g_gla_norm"], wts["g_post_mix"], wts["g_pre_ffn"],
      wts["w_b_gla"], wts["w_b_rw"], wts["w_out"], wts["peer_wq"])


def _pair_bits(x):
    bits = pltpu.bitcast(x.astype(BF16).astype(F32), jnp.uint32)
    return bits | (bits >> 16)


def _row_tile(row, rows):
    tile = pltpu.bitcast(jnp.broadcast_to(row, (8, row.shape[1])), BF16)
    return jnp.tile(tile, (rows // 16, 1))


def _batcher_network(n):
    def merge(lo, hi, r):
        step = r * 2
        if step < hi - lo:
            yield from merge(lo, hi, step)
            yield from merge(lo + r, hi, step)
            yield from [(i, i + r) for i in range(lo + r, hi - r, step)]
        else:
            yield (lo, lo + r)

    def sort(lo, hi):
        if hi - lo >= 1:
            mid = lo + (hi - lo) // 2
            yield from sort(lo, mid)
            yield from sort(mid + 1, hi)
            yield from merge(lo, hi, 1)

    return list(sort(0, n - 1))


def _top_values(s, n):
    sub = 8
    depth = s.shape[0] // sub
    cols = [s[i * sub:(i + 1) * sub, :] for i in range(depth)]
    for i, j in _batcher_network(depth):
        cols[i], cols[j] = jnp.maximum(cols[i], cols[j]), jnp.minimum(cols[i], cols[j])
    rows = lax.broadcasted_iota(jnp.int32, (n, s.shape[1]), 0)
    vals = jnp.zeros((n, s.shape[1]), F32)
    for it in range(n):
        m = jnp.max(cols[0], axis=0, keepdims=True)
        vals = jnp.where(rows == it, m, vals)
        hit = cols[0] == m
        for i in range(min(depth - 1, n - 1 - it)):
            cols[i] = jnp.where(hit, cols[i + 1], cols[i])
    return vals


def _rank_among(s, vals):
    rank = jnp.zeros_like(s)
    for b in range(vals.shape[0]):
        rank = rank + jnp.where(vals[b:b + 1, :] > s, 1.0, 0.0)
    return rank


def _candidate_groups(v1, v2):
    n, tm = v1.shape
    split = 4
    r_all = lax.broadcasted_iota(jnp.int32, (n, tm), 0)
    r_half = lax.broadcasted_iota(jnp.int32, (n // 2, tm), 0)
    groups = [v1[0:1, :] + v2]
    for a in range(1, split):
        groups.append(jnp.where(r_half < n // (a + 1), v1[a:a + 1, :] + v2[:n // 2, :], NEG_INF))
    groups.append(jnp.where(r_all >= split, v1 + v2[0:1, :], NEG_INF))
    for b in range(1, n // (split + 1)):
        keep = (r_half >= split) & (r_half < n // (b + 1))
        groups.append(jnp.where(keep, v1[:n // 2, :] + v2[b:b + 1, :], NEG_INF))
    return groups


def _group_max(groups, n):
    full = [g for g in groups if g.shape[0] == n]
    best = full[0]
    for g in full[1:]:
        best = jnp.maximum(best, g)
    best = jnp.maximum(best[:n // 2, :], best[n // 2:, :])
    for g in groups:
        if g.shape[0] != n:
            best = jnp.maximum(best, g)
    return jnp.max(best, axis=0, keepdims=True)


def _select_kernel(qp_ref, keys_ref, r2_ref, e2_ref, al_ref, n1_ref):
    tm = qp_ref.shape[0]
    for h in range(PEER_HEADS):
        q1 = qp_ref[:, (2 * h) * 128:(2 * h + 1) * 128]
        q2 = qp_ref[:, (2 * h + 1) * 128:(2 * h + 2) * 128]
        s1 = _mm_nt(keys_ref[2 * h], q1)
        s2 = _mm_nt(keys_ref[2 * h + 1], q2)
        v1 = _top_values(s1, PEER_TOPK)
        v2 = _top_values(s2, PEER_TOPK)
        rank2 = _rank_among(s2, v2)
        groups = _candidate_groups(v1, v2)
        cmax = v1[0:1, :] + v2[0:1, :]
        z = jnp.zeros((1, tm), F32)
        m = cmax
        for _ in range(PEER_TOPK):
            m = _group_max(groups, PEER_TOPK)
            z = z + jnp.exp(m - cmax)
            groups = [jnp.where(g == m, NEG_INF, g) for g in groups]
        th = m - s1
        n1 = jnp.zeros_like(s1)
        for b in range(PEER_TOPK):
            n1 = n1 + jnp.where(v2[b:b + 1, :] >= th, 1.0, 0.0)
        r2_ref[h] = rank2.astype(BF16)
        e2_ref[h] = jnp.exp(s2 - v2[0:1, :]).astype(BF16)
        al_ref[h] = _pair_bits(jnp.exp(s1 - v1[0:1, :]) / z)
        n1_ref[h] = _pair_bits(n1)


def _peer_select(grp, qp, keys):
    tm, M = grp.tm, grp.M
    rows = pl.BlockSpec((PEER_HEADS, PEER_NKEYS, tm), lambda i: (0, 0, i))
    return pl.pallas_call(
        _select_kernel,
        grid=(M // tm,),
        in_specs=[pl.BlockSpec((tm, 2 * D_MODEL), lambda i: (i, 0)),
                  pl.BlockSpec((2 * PEER_HEADS, PEER_NKEYS, 128), lambda i: (0, 0, 0))],
        out_specs=[rows] * 4,
        out_shape=[jax.ShapeDtypeStruct((PEER_HEADS, PEER_NKEYS, M), BF16)] * 2
        + [jax.ShapeDtypeStruct((PEER_HEADS, PEER_NKEYS, M), jnp.uint32)] * 2,
        compiler_params=_params(("arbitrary",)),
        name="peer_select",
    )(qp, keys)


def _peer_weights(act_ref, y_ref, r2_ref, e2_ref, al_ref, n1_ref, row0):
    tm = act_ref.shape[1]
    half = PEER_NKEYS // 2
    for j in range(act_ref.shape[0] // PEER_NKEYS):
        i1 = slice(row0 + j, row0 + j + 1)
        w = [jnp.zeros((half, tm), BF16) for _ in range(2)]
        for h in range(PEER_HEADS):
            n1 = _row_tile(n1_ref[h, i1, :], half)
            al = _row_tile(al_ref[h, i1, :], half)
            for r in range(2):
                keys = slice(r * half, (r + 1) * half)
                w[r] = w[r] + jnp.where(r2_ref[h, keys, :] < n1, e2_ref[h, keys, :], 0.0) * al
        for r in range(2):
            rows = slice(j * PEER_NKEYS + r * half, j * PEER_NKEYS + (r + 1) * half)
            a = act_ref[rows, :].astype(BF16)
            y_ref[rows, :] = (0.5 * a * (1.0 + lax.erf(a * 0.7071067811865476))) * w[r]


def _peer_kernel(h2t_ref, u_ref, vt_ref, r2_ref, e2_ref, al_ref, n1_ref, x1_ref, gate2_ref, gpost_ref,
                 o_ref, acc_ref, act_0, act_1, y_0, y_1):
    e = pl.program_id(1)
    ts = act_0.shape[0]
    per = ts // PEER_NKEYS
    acts, ys = (act_0, act_1), (y_0, y_1)

    @pl.when(e == 0)
    def _():
        acc_ref[...] = jnp.zeros_like(acc_ref)

    def scores(k):
        acts[k % 2][...] = jnp.dot(u_ref[k], h2t_ref[...], preferred_element_type=F32)

    def mix(k):
        acc_ref[...] += jnp.dot(vt_ref[k], ys[k % 2][...], preferred_element_type=F32)

    scores(0)
    scores(1)
    for k in range(PEER_SUBTILES):
        _peer_weights(acts[k % 2], ys[k % 2], r2_ref, e2_ref, al_ref, n1_ref, k * per)
        if k + 2 < PEER_SUBTILES:
            scores(k + 2)
        if k >= 1:
            mix(k - 1)
    mix(PEER_SUBTILES - 1)

    @pl.when(e == pl.num_programs(1) - 1)
    def _():
        o_ref[...] = x1_ref[...] + gate2_ref[...] * _rms(acc_ref[...].T, gpost_ref[...])


def _peer(grp, h2t, sel, x1, modop, g_post, pu, pvt):
    tm, M = grp.tm, grp.M
    ts = PEER_EXPERT_TILE
    te = ts * PEER_SUBTILES
    r2, e2, al, n1 = sel
    packed = pl.BlockSpec((PEER_HEADS, PEER_NKEYS, tm), lambda i, e: (0, 0, i))
    rows = pl.BlockSpec((PEER_HEADS, te // PEER_NKEYS, tm), lambda i, e: (0, e, i))
    tok = pl.BlockSpec((tm, D_MODEL), lambda i, e: (i, 0))
    return pl.pallas_call(
        _peer_kernel,
        grid=(M // tm, pu.shape[0] // PEER_SUBTILES),
        in_specs=[pl.BlockSpec((D_MODEL, tm), lambda i, e: (0, i)),
                  pl.BlockSpec((PEER_SUBTILES, ts, D_MODEL), lambda i, e: (e, 0, 0)),
                  pl.BlockSpec((PEER_SUBTILES, D_MODEL, ts), lambda i, e: (e, 0, 0)),
                  packed, packed, rows, rows,
                  pl.BlockSpec((tm, D_MODEL), lambda i, e: (i, 0), pipeline_mode=pl.Buffered(1)),
                  grp.mod_spec(5),
                  pl.BlockSpec((1, D_MODEL), lambda i, e: (0, 0))],
        out_specs=tok,
        out_shape=jax.ShapeDtypeStruct((M, D_MODEL), F32),
        scratch_shapes=[pltpu.VMEM((D_MODEL, tm), F32),
                        pltpu.VMEM((ts, tm), F32), pltpu.VMEM((ts, tm), F32),
                        pltpu.VMEM((ts, tm), BF16), pltpu.VMEM((ts, tm), BF16)],
        compiler_params=_params(("arbitrary", "arbitrary")),
        name="peer_dense",
    )(h2t, pu, pvt, r2, e2, al, n1, x1, modop, g_post)


def _prepare_weights(w_in, w_gla_gate, b_gla_gate, g_gla_norm, mu_shift, rw_w0, rw_w2, rw_a0, rw_a2,
                     rw_g2, rw_k_k, rw_k_a, rw_r_k, rw_gn_w, rw_gn_b, w_b_gla, w_b_rw, w_out,
                     peer_wq, peer_keys, peer_u, peer_v, g_pre_mix, g_post_mix, g_pre_ffn, g_post_ffn):
    W = RW_HEADS * RW_HEAD
    rw = w_in[:, 3088:6416]
    w_proj = jnp.concatenate(
        [rw[:, :3 * W], w_in[:, 6416:8464], w_in[:, 2048:3072], w_in[:, 1024:2048], w_in[:, 0:1024],
         rw[:, 3 * W:], w_in[:, 3072:3088], jnp.zeros((D_MODEL, PROJ_PAD_W - 8464), F32)],
        axis=1).astype(BF16)
    head = jnp.arange(W) // RW_HEAD
    zeros64 = jnp.zeros((64, W), F32)
    return dict(
        w_proj=w_proj,
        wg_pad=jnp.concatenate([w_gla_gate, jnp.zeros((128 - GLA_GATE_RANK, 512), F32)], axis=0),
        bg=b_gla_gate.reshape(1, -1),
        g_gla_norm=g_gla_norm.reshape(1, -1),
        mu_rkv=mu_shift[:3 * W].reshape(1, -1),
        mu_lora=mu_shift[3 * W:].reshape(1, -1),
        rw_w0=rw_w0.reshape(1, -1),
        w2p=jnp.concatenate([rw_w2, jnp.zeros((192, W), F32)], axis=0),
        rw_a0=rw_a0.reshape(1, -1),
        a2p=jnp.concatenate([zeros64, rw_a2, jnp.zeros((128, W), F32)], axis=0),
        g2p=jnp.concatenate([jnp.zeros((128, W), F32), rw_g2], axis=0),
        rw_k_k=rw_k_k.reshape(1, -1),
        rw_k_a=rw_k_a.reshape(1, -1),
        rw_r_k=rw_r_k.reshape(1, -1),
        rw_gn_w=rw_gn_w.reshape(1, -1),
        rw_gn_b=rw_gn_b.reshape(1, -1),
        head_ones=(head[:, None] == head[None, :]).astype(BF16),
        w_b_gla=w_b_gla.astype(BF16),
        w_b_rw=w_b_rw.astype(BF16),
        w_out=w_out.astype(BF16),
        peer_wq=peer_wq.astype(BF16),
        peer_keys=peer_keys.reshape(2 * PEER_HEADS, PEER_NKEYS, 128).astype(BF16),
        peer_u=peer_u.astype(BF16).reshape(-1, PEER_EXPERT_TILE, D_MODEL),
        peer_vt=peer_v.reshape(-1, PEER_EXPERT_TILE, D_MODEL).transpose(0, 2, 1).astype(BF16),
        g_pre_mix=g_pre_mix.reshape(1, -1),
        g_post_mix=g_post_mix.reshape(1, -1),
        g_pre_ffn=g_pre_ffn.reshape(1, -1),
        g_post_ffn=g_post_ffn.reshape(1, -1),
    )


def _block(x, mod, s_gla0, s_rw0, shift0, wts, valid):
    B, T, _ = x.shape
    grp = _Group(B, T, valid)
    W = RW_HEADS * RW_HEAD
    x2d = x.reshape(grp.M, D_MODEL)
    modop = grp.seq_operand(mod)
    p2 = _projection(grp, x2d, modop, wts["g_pre_mix"], wts["w_proj"], BF16)
    p3 = p2.reshape(B, T, PROJ_PAD_W)
    og, s_gla = _gla(grp, p3, wts["wg_pad"], wts["bg"], s_gla0)
    r, k, v, an, bb, lw, gate = _rwkv_prep(grp, p2, shift0, wts)
    seqs = tuple(z.reshape(B, T, W) for z in (r, k, v, an, bb, lw))
    orw, s_rw = _rwkv(grp, seqs, wts["rw_gn_w"], wts["rw_gn_b"], wts["rw_r_k"], s_rw0)
    x1, h2t, qp = _merge(grp, og.reshape(grp.M, -1), p2, orw.reshape(grp.M, W), gate,
                        x2d, modop, wts)
    sel = _peer_select(grp, qp, wts["peer_keys"])
    out = _peer(grp, h2t, sel, x1, modop, wts["g_post_ffn"], wts["peer_u"], wts["peer_vt"])
    ends = _Group(B, 1, 1)
    last = _projection(ends, x[:, valid - 1, :], ends.seq_operand(mod), wts["g_pre_mix"], wts["w_proj"], F32)
    shift_new = jnp.concatenate([last[:, :3 * W], last[:, COL_LORA:COL_LORA + 256]], axis=-1)
    return out.reshape(B, T, D_MODEL), s_gla, s_rw, shift_new


def kernel(x_prompt, x_sample, c_prompt, c_sample, state_gla, state_rwkv, state_shift, w_ada, b_ada, g_pre_mix, g_post_mix, g_pre_ffn, g_post_ffn, w_in, w_gla_gate, b_gla_gate, g_gla_norm, mu_shift, rw_w0, rw_w2, rw_a0, rw_a2, rw_g2, rw_k_k, rw_k_a, rw_r_k, rw_gn_w, rw_gn_b, w_b_gla, w_b_rw, w_out, peer_wq, peer_keys, peer_u, peer_v):
    depth = w_in.shape[0]
    bp, bs = x_prompt.shape[0], x_sample.shape[0]
    ts = x_sample.shape[1]
    ts_pad = -(-ts // 8) * 8
    xp = x_prompt
    xs = jnp.pad(x_sample, ((0, 0), (0, ts_pad - ts), (0, 0)))
    outs = [[] for _ in range(6)]
    for l in range(depth):
        wts = _prepare_weights(
            w_in[l], w_gla_gate[l], b_gla_gate[l], g_gla_norm[l], mu_shift[l], rw_w0[l], rw_w2[l],
            rw_a0[l], rw_a2[l], rw_g2[l], rw_k_k[l], rw_k_a[l], rw_r_k[l], rw_gn_w[l], rw_gn_b[l],
            w_b_gla[l], w_b_rw[l], w_out[l], peer_wq[l], peer_keys[l], peer_u[l], peer_v[l],
            g_pre_mix[l], g_post_mix[l], g_pre_ffn[l], g_post_ffn[l])
        mod = _modulation(jnp.concatenate([c_prompt, c_sample], axis=0), w_ada[l], b_ada[l])
        zg = jnp.zeros((bp, GLA_HEADS, GLA_DK, GLA_DV), F32)
        zr = jnp.zeros((bp, RW_HEADS, RW_HEAD, RW_HEAD), F32)
        zs = jnp.zeros((bp, RW_SHIFT_W), F32)
        xp, g1, r1, s1 = _block(xp, mod[:bp], zg, zr, zs, wts, xp.shape[1])
        xs, g2, r2, s2 = _block(xs, mod[bp:], state_gla[l], state_rwkv[l], state_shift[l], wts, ts)
        for acc, val in zip(outs, (g1, r1, s1, g2, r2, s2)):
            acc.append(val)
    stacked = [jnp.stack(o) for o in outs]
    return (xp, xs[:, :ts], *stacked)
```

```python
import functools

import jax
import jax.numpy as jnp
from jax import lax
from jax.experimental import pallas as pl
from jax.experimental.pallas import tpu as pltpu

F32 = jnp.float32
BF16 = jnp.bfloat16

D_MODEL = 1024
GLA_HEADS = 4
GLA_DK = 128
GLA_DV = 256
GLA_GATE_RANK = 16
GLA_GATE_TAU = 16.0
GLA_NORM_EPS = 1e-5
RW_HEAD = 64
RW_HEADS = 16
RW_PAIRS = RW_HEADS // 2
RW_GN_EPS = 64e-5
RW_SHIFT_W = 3328
PEER_HEADS = 8
PEER_NKEYS = 128
PEER_TOPK = 16
NORM_EPS = 1e-6
LANES = 128

COL_R, COL_K, COL_V = 0, 1024, 2048
COL_MG = 3072
COL_GG = 5120
COL_GV = 6144
COL_GQ = 7168
COL_GK = 7680
COL_LORA = 8192
COL_GA = 8448
PROJ_PAD_W = 8704
PROJ_TILE_N = 2176
PROJ_TILE_M = 1024

CHUNK = 64
TOKEN_TILE = 512
PEER_EXPERT_TILE = 512
PEER_SUBTILES = 4
VMEM_LIMIT = 56 * 1024 * 1024

NEG_INF = float("-inf")


def _mm(a, b):
    return jnp.dot(a.astype(BF16), b.astype(BF16), preferred_element_type=F32)


def _mm_nt(a, b):
    return lax.dot_general(a.astype(BF16), b.astype(BF16), (((1,), (1,)), ((), ())),
                           preferred_element_type=F32)


def _mm_tn(a, b):
    return lax.dot_general(a.astype(BF16), b.astype(BF16), (((0,), (0,)), ((), ())),
                           preferred_element_type=F32)


def _split(x):
    hi = x.astype(BF16)
    lo = (x - hi.astype(F32)).astype(BF16)
    return hi, lo


def _mm_exact_lhs(a, x):
    hi, lo = _split(x)
    a = a.astype(BF16)
    return (jnp.dot(a, hi, preferred_element_type=F32)
            + jnp.dot(a, lo, preferred_element_type=F32))


def _mm_exact_rhs(x, a):
    hi, lo = _split(x)
    a = a.astype(BF16)
    return (jnp.dot(hi, a, preferred_element_type=F32)
            + jnp.dot(lo, a, preferred_element_type=F32))


def _softplus(x):
    return jnp.maximum(x, 0.0) + jnp.log1p(jnp.exp(-jnp.abs(x)))


def _rms(x, g):
    return x * lax.rsqrt(jnp.mean(x * x, axis=-1, keepdims=True) + NORM_EPS) * g


def _params(sem, flags=None):
    return pltpu.CompilerParams(dimension_semantics=sem, vmem_limit_bytes=VMEM_LIMIT, flags=flags)


def _mod_kernel(c_ref, w_ref, b_ref, o_ref):
    c = c_ref[...]
    o_ref[...] = _mm(c * jax.nn.sigmoid(c), w_ref[...]) + b_ref[...]


def _modulation(c, w_ada, b_ada):
    rows = c.shape[0]
    n = w_ada.shape[1]
    tn = 1536
    return pl.pallas_call(
        _mod_kernel,
        grid=(n // tn,),
        in_specs=[pl.BlockSpec((rows, D_MODEL), lambda j: (0, 0)),
                  pl.BlockSpec((D_MODEL, tn), lambda j: (0, j)),
                  pl.BlockSpec((1, tn), lambda j: (0, j))],
        out_specs=pl.BlockSpec((rows, tn), lambda j: (0, j)),
        out_shape=jax.ShapeDtypeStruct((rows, n), F32),
        compiler_params=_params(("arbitrary",)),
        name="adaln_mod",
    )(c, w_ada, b_ada.reshape(1, n))


class _Group:
    def __init__(self, batch, seq, valid):
        self.B, self.T, self.Tv = batch, seq, valid
        self.M = batch * seq
        self.tm = min(TOKEN_TILE, self.M)
        self.per_batch_mod = seq % self.tm == 0

    def seq_operand(self, arr):
        if self.per_batch_mod:
            return arr.reshape(self.B, 1, arr.shape[-1])
        return jnp.repeat(arr, self.T, axis=0)

    def seq_spec(self, width, col, tm=None, tile_index=None):
        tm, T = tm or self.tm, self.T
        tile = tile_index or (lambda i, *_: i)
        if self.per_batch_mod:
            return pl.BlockSpec((None, 1, width), lambda *g: ((tile(*g) * tm) // T, 0, col))
        return pl.BlockSpec((tm, width), lambda *g: (tile(*g), col))

    def mod_spec(self, col, tm=None, tile_index=None):
        return self.seq_spec(D_MODEL, col, tm, tile_index)


def _proj_kernel(x_ref, shift_ref, scale_ref, g_ref, w_ref, p_ref, h_scr):
    @pl.when(pl.program_id(1) == 0)
    def _():
        h = _rms(x_ref[...], g_ref[...]) * (1.0 + scale_ref[...]) + shift_ref[...]
        h_scr[...] = h.astype(BF16)

    p_ref[...] = jnp.dot(h_scr[...], w_ref[...], preferred_element_type=F32).astype(p_ref.dtype)


def _projection(grp, x2d, modop, g_pre, w_proj, dtype):
    tm = PROJ_TILE_M if grp.per_batch_mod and grp.T % PROJ_TILE_M == 0 else grp.tm
    return pl.pallas_call(
        _proj_kernel,
        grid=(grp.M // tm, PROJ_PAD_W // PROJ_TILE_N),
        in_specs=[pl.BlockSpec((tm, D_MODEL), lambda i, j: (i, 0)),
                  grp.mod_spec(0, tm), grp.mod_spec(1, tm),
                  pl.BlockSpec((1, D_MODEL), lambda i, j: (0, 0)),
                  pl.BlockSpec((D_MODEL, PROJ_TILE_N), lambda i, j: (0, j))],
        out_specs=pl.BlockSpec((tm, PROJ_TILE_N), lambda i, j: (i, j)),
        out_shape=jax.ShapeDtypeStruct((grp.M, PROJ_PAD_W), dtype),
        scratch_shapes=[pltpu.VMEM((tm, D_MODEL), BF16)],
        compiler_params=_params(("arbitrary", "arbitrary")),
        name="in_proj",
    )(x2d, modop, modop, g_pre, w_proj)


def _gla_kernel(q_ref, k_ref, v_ref, ga_ref, wg_ref, bg_ref, s0_ref, og_ref, sout_ref, st_scr,
                *, chunk, n_chunks, tile, valid, seqs):
    t = pl.program_id(1)
    last = pl.num_programs(1) - 1
    row = lax.broadcasted_iota(jnp.int32, (tile, tile), 0)
    col = lax.broadcasted_iota(jnp.int32, (tile, tile), 1)
    shift = chunk.bit_length() - 1
    causal = (lax.shift_right_logical(row, shift) == lax.shift_right_logical(col, shift)) & (row >= col)
    tril = jnp.where(causal, 1.0, 0.0).astype(BF16)

    def per_seq(b, carry):
        @pl.when(t == 0)
        def _():
            for h in range(GLA_HEADS):
                st_scr[b, h] = s0_ref[b, h].T

        heads, chunks = range(GLA_HEADS), range(n_chunks)
        ks = [slice(h * GLA_DK, (h + 1) * GLA_DK) for h in heads]
        vs = [slice(h * GLA_DV, (h + 1) * GLA_DV) for h in heads]
        rows = [slice(c * chunk, (c + 1) * chunk) for c in chunks]
        ends = [slice((c + 1) * chunk - 1, (c + 1) * chunk) for c in chunks]

        z = _mm(ga_ref[b], wg_ref[...]) + bg_ref[...]
        la = -_softplus(-z) / GLA_GATE_TAU
        k = k_ref[b].astype(F32)
        v = [v_ref[b, :, vs[h]] for h in heads]
        if valid is not None:
            ok = (t * tile + lax.broadcasted_iota(jnp.int32, (tile, 1), 0)) < valid
            la = jnp.where(ok, la, 0.0)
            k = jnp.where(ok, k, 0.0)
            v = [jnp.where(ok, x, 0.0) for x in v]
        cum = _mm_exact_lhs(tril, la)
        end = jnp.concatenate([jnp.broadcast_to(cum[ends[c], :], (chunk, cum.shape[1])) for c in chunks], axis=0)
        qe = q_ref[b].astype(F32) * (GLA_DK ** -0.5) * jnp.exp(cum)
        ke = k * jnp.exp(-cum)
        kd = k * jnp.exp(end - cum)
        decay = [jnp.exp(cum[ends[c], :]) for c in chunks]
        a = [jnp.where(causal, _mm_nt(qe[:, ks[h]], ke[:, ks[h]]), 0.0) for h in heads]
        inside = [_mm(a[h], v[h]) for h in heads]
        update = [[_mm_tn(v[h][rows[c], :], kd[rows[c], ks[h]]) for h in heads] for c in chunks]

        st = [[st_scr[b, h] for h in heads]]
        for c in chunks:
            st.append([st[c][h] * decay[c][:, ks[h]] + update[c][h] for h in heads])
        outer = [[_mm_nt(qe[rows[c], ks[h]], st[c][h]) for h in heads] for c in chunks]
        for h in heads:
            st_scr[b, h] = st[n_chunks][h]
            for c in chunks:
                o = outer[c][h] + inside[h][rows[c], :]
                og = o * lax.rsqrt(jnp.mean(o * o, axis=-1, keepdims=True) + GLA_NORM_EPS)
                og_ref[b, rows[c], vs[h]] = og.astype(og_ref.dtype)

        @pl.when(t == last)
        def _():
            for h in range(GLA_HEADS):
                sout_ref[b, h] = st_scr[b, h].T

        return carry

    if seqs == 1:
        per_seq(0, 0)
    else:
        lax.fori_loop(0, seqs, per_seq, 0)


def _seqs_per_step(batch, seq):
    n = max(1, min(8, 64 // seq))
    while batch % n:
        n //= 2
    return n


def _gla(grp, p3, wg_pad, bg, s0):
    B, T = grp.B, grp.T
    tile = min(T, 4 * CHUNK)
    chunk = min(CHUNK, tile)
    nb = _seqs_per_step(B, T)
    kern = functools.partial(_gla_kernel, chunk=chunk, n_chunks=tile // chunk, tile=tile,
                             valid=None if grp.Tv == T else grp.Tv, seqs=nb)
    kw, vw = GLA_HEADS * GLA_DK, GLA_HEADS * GLA_DV
    st_spec = pl.BlockSpec((nb, GLA_HEADS, GLA_DK, GLA_DV), lambda i, t: (i, 0, 0, 0))
    return pl.pallas_call(
        kern,
        grid=(B // nb, T // tile),
        in_specs=[pl.BlockSpec((nb, tile, kw), lambda i, t: (i, t, COL_GQ // kw)),
                  pl.BlockSpec((nb, tile, kw), lambda i, t: (i, t, COL_GK // kw)),
                  pl.BlockSpec((nb, tile, vw), lambda i, t: (i, t, COL_GV // vw)),
                  pl.BlockSpec((nb, tile, LANES), lambda i, t: (i, t, COL_GA // LANES)),
                  pl.BlockSpec((LANES, kw), lambda i, t: (0, 0)),
                  pl.BlockSpec((1, kw), lambda i, t: (0, 0)),
                  st_spec],
        out_specs=[pl.BlockSpec((nb, tile, vw), lambda i, t: (i, t, 0)), st_spec],
        out_shape=[jax.ShapeDtypeStruct((B, T, vw), BF16),
                   jax.ShapeDtypeStruct((B, GLA_HEADS, GLA_DK, GLA_DV), F32)],
        scratch_shapes=[pltpu.VMEM((nb, GLA_HEADS, GLA_DV, GLA_DK), F32)],
        compiler_params=_params(("arbitrary", "arbitrary")),
        name="gla_scan",
    )(p3, p3, p3, p3, wg_pad, bg, s0)


def _rwprep_kernel(r_ref, k_ref, v_ref, l_ref, pr_ref, pk_ref, pv_ref, pl_ref,
                   s0r_ref, s0k_ref, s0v_ref, s0l_ref, mu_ref, mul_ref, w0_ref, w2_ref, a0_ref, a2_ref, g2_ref,
                   kk_ref, ka_ref, bd_ref,
                   ro_ref, ko_ref, vo_ref, an_ref, bb_ref, lw_ref, gate_ref,
                   *, tile, seq, valid, prev_rows):
    i = pl.program_id(0)
    local = lax.broadcasted_iota(jnp.int32, (tile, 1), 0)
    pos = lax.rem(i * tile + local, seq)
    first = pos == 0
    top = local == 0

    def shifted(cur_ref, prev_ref, s0_ref, mu):
        cur = cur_ref[...].astype(F32)
        before = prev_ref[prev_rows - 1:prev_rows, :].astype(F32)
        prev = jnp.where(first, s0_ref[...], jnp.where(top, before, pltpu.roll(cur, 1, 0)))
        return cur + (prev - cur) * mu

    r = shifted(r_ref, pr_ref, s0r_ref, mu_ref[:, 0:1024])
    k = shifted(k_ref, pk_ref, s0k_ref, mu_ref[:, 1024:2048])
    v = shifted(v_ref, pv_ref, s0v_ref, mu_ref[:, 2048:3072])
    lo = shifted(l_ref, pl_ref, s0l_ref, mul_ref[...])

    wpre = w0_ref[...] + _mm(jnp.tanh(lo), w2_ref[...])
    wlog = -_softplus(-wpre) - 0.5
    lw = -jnp.exp(wlog)
    a = jax.nn.sigmoid(a0_ref[...] + _mm(lo, a2_ref[...]))
    gate = _mm(jax.nn.sigmoid(lo), g2_ref[...])

    kk = k * kk_ref[...]
    ss = _mm(kk * kk, bd_ref[...])
    kk = kk / jnp.maximum(jnp.sqrt(ss), 1e-12)
    kmod = k * (1.0 + (a - 1.0) * ka_ref[...])
    an = -kk
    bb = kk * a
    if valid is not None:
        ok = pos < valid
        lw = jnp.where(ok, lw, 0.0)
        an = jnp.where(ok, an, 0.0)
        bb = jnp.where(ok, bb, 0.0)
        kmod = jnp.where(ok, kmod, 0.0)
        v = jnp.where(ok, v, 0.0)
    ro_ref[...] = r.astype(ro_ref.dtype)
    ko_ref[...] = kmod.astype(ko_ref.dtype)
    vo_ref[...] = v.astype(vo_ref.dtype)
    an_ref[...] = an.astype(an_ref.dtype)
    bb_ref[...] = bb.astype(bb_ref.dtype)
    lw_ref[...] = lw
    gate_ref[...] = gate.astype(gate_ref.dtype)


def _rwkv_prep(grp, p2, shift0, wts):
    M, T = grp.M, grp.T
    tile = min(M, 512)
    prev_rows = min(16, tile)
    W = RW_HEADS * RW_HEAD
    s0 = grp.seq_operand(shift0)

    def cur(width, blk):
        return pl.BlockSpec((tile, width), lambda i: (i, blk))

    def prev(width, blk):
        return pl.BlockSpec((prev_rows, width), lambda i: (jnp.maximum(i * (tile // prev_rows) - 1, 0), blk))

    def full(shape):
        return pl.BlockSpec(shape, lambda i: (0,) * len(shape))

    out_spec = pl.BlockSpec((tile, W), lambda i: (i, 0))
    kern = functools.partial(_rwprep_kernel, tile=tile, seq=T, valid=None if grp.Tv == T else grp.Tv,
                             prev_rows=prev_rows)
    lora_blk = COL_LORA // 256
    return pl.pallas_call(
        kern,
        grid=(M // tile,),
        in_specs=[cur(W, 0), cur(W, 1), cur(W, 2), cur(256, lora_blk),
                  prev(W, 0), prev(W, 1), prev(W, 2), prev(256, lora_blk),
                  grp.seq_spec(W, 0, tile), grp.seq_spec(W, 1, tile), grp.seq_spec(W, 2, tile),
                  grp.seq_spec(256, 3 * W // 256, tile),
                  full((1, 3 * W)), full((1, 256)),
                  full((1, W)), full((256, W)), full((1, W)), full((256, W)), full((256, W)),
                  full((1, W)), full((1, W)), full((W, W))],
        out_specs=[out_spec] * 7,
        out_shape=[jax.ShapeDtypeStruct((M, W), BF16)] * 5
        + [jax.ShapeDtypeStruct((M, W), F32), jax.ShapeDtypeStruct((M, W), BF16)],
        compiler_params=_params(("arbitrary",)),
        name="rwkv_prep",
    )(p2, p2, p2, p2, p2, p2, p2, p2, s0, s0, s0, s0,
      wts["mu_rkv"], wts["mu_lora"], wts["rw_w0"], wts["w2p"], wts["rw_a0"], wts["a2p"], wts["g2p"],
      wts["rw_k_k"], wts["rw_k_a"], wts["head_ones"])


def _rwkv_pairs_chunk(at, rt, bt, kt, v, states, *, chunk, same_head, strict, incl):
    n2 = 2 * chunk
    wide = n2 % LANES == 0
    pairs = range(len(states))

    def blocks(x):
        return jnp.where(same_head, jnp.concatenate([x, x], axis=0), 0.0).astype(BF16)

    a2 = [blocks(x) for x in at]
    r2 = [blocks(x) for x in rt]
    b2 = [blocks(x) for x in bt]
    k2 = [blocks(x) for x in kt]
    v2 = [blocks(x) for x in v]
    ar = [jnp.concatenate([a2[p], r2[p]], axis=0) for p in pairs]
    bk = [jnp.concatenate([b2[p], k2[p]], axis=0) for p in pairs]
    if wide:
        g = [_mm_nt(ar[p], bk[p]) for p in pairs]
        l_ab = [jnp.where(strict, g[p][:n2, :n2], 0.0) for p in pairs]
        l_ak = [jnp.where(strict, g[p][:n2, n2:], 0.0) for p in pairs]
        m_rb = [jnp.where(incl, g[p][n2:, :n2], 0.0) for p in pairs]
        m_rk = [jnp.where(incl, g[p][n2:, n2:], 0.0) for p in pairs]
    else:
        l_ab = [jnp.where(strict, _mm_nt(a2[p], b2[p]), 0.0) for p in pairs]
        l_ak = [jnp.where(strict, _mm_nt(a2[p], k2[p]), 0.0) for p in pairs]
        m_rb = [jnp.where(incl, _mm_nt(r2[p], b2[p]), 0.0) for p in pairs]
        m_rk = [jnp.where(incl, _mm_nt(r2[p], k2[p]), 0.0) for p in pairs]
    from_state = [_mm_nt(ar[p], states[p]) for p in pairs]
    x = [from_state[p][:n2] + _mm(l_ak[p], v2[p]) for p in pairs]
    pw = l_ab
    for _ in range(chunk.bit_length() - 2):
        if wide:
            y = [_mm(pw[p], jnp.concatenate([x[p], pw[p]], axis=1)) for p in pairs]
            x = [x[p] + y[p][:, :LANES] for p in pairs]
            pw = [y[p][:, LANES:] for p in pairs]
        else:
            x = [x[p] + _mm(pw[p], x[p]) for p in pairs]
            pw = [_mm(pw[p], pw[p]) for p in pairs]
    u = [x[p] + _mm(pw[p], x[p]) for p in pairs]
    uv = [jnp.concatenate([u[p].astype(BF16), v2[p]], axis=0) for p in pairs]
    if wide:
        o2 = [from_state[p][n2:] + _mm(jnp.concatenate([m_rb[p], m_rk[p]], axis=1), uv[p]) for p in pairs]
    else:
        o2 = [from_state[p][n2:] + _mm(m_rb[p], u[p]) + _mm(m_rk[p], v2[p]) for p in pairs]
    s_new = [states[p] + _mm_tn(uv[p], bk[p]) for p in pairs]
    return o2, s_new


def _rwkv_kernel(r_ref, k_ref, v_ref, an_ref, bb_ref, lw_ref, gw_ref, gb_ref, rk_ref, s0_ref,
                 o_ref, sout_ref, s_scr, *, chunk, seqs):
    t = pl.program_id(1)
    zeros = jnp.zeros((RW_HEAD, RW_HEAD), F32)

    @pl.when(t == 0)
    def _():
        for q in range(seqs):
            for p in range(RW_PAIRS):
                top = jnp.concatenate([s0_ref[q, 2 * p], zeros], axis=1)
                bot = jnp.concatenate([zeros, s0_ref[q, 2 * p + 1]], axis=1)
                s_scr[q, p] = jnp.concatenate([top, bot], axis=0)

    n2 = 2 * chunk
    row = lax.broadcasted_iota(jnp.int32, (n2, n2), 0)
    col = lax.broadcasted_iota(jnp.int32, (n2, n2), 1)

    def second(idx, size):
        return jnp.where(idx >= size, 1, 0)

    same = second(row, chunk) == second(col, chunk)
    strict = same & (row > col)
    incl = same & (row >= col)
    same_head = (second(lax.broadcasted_iota(jnp.int32, (n2, LANES), 0), chunk)
                 == second(lax.broadcasted_iota(jnp.int32, (n2, LANES), 1), RW_HEAD))
    trow = lax.broadcasted_iota(jnp.int32, (chunk, chunk), 0)
    tcol = lax.broadcasted_iota(jnp.int32, (chunk, chunk), 1)
    tril = jnp.where(trow >= tcol, 1.0, 0.0).astype(BF16)
    first_head = lax.broadcasted_iota(jnp.int32, (chunk, LANES), 1) < RW_HEAD

    def head_sum(x):
        lo = jnp.sum(jnp.where(first_head, x, 0.0), axis=-1, keepdims=True)
        hi = jnp.sum(jnp.where(first_head, 0.0, x), axis=-1, keepdims=True)
        return jnp.where(first_head, lo, hi)

    lanes = [slice(p * LANES, (p + 1) * LANES) for p in range(RW_PAIRS)]
    problems = [(q, p) for q in range(seqs) for p in range(RW_PAIRS)]
    rs, ks, vs, ats, rts, bts, kts, ends = [], [], [], [], [], [], [], []
    for q in range(seqs):
        lw = lw_ref[q]
        cum = _mm_exact_lhs(tril, lw)
        w_incl = jnp.exp(cum)
        w_inv = jnp.exp(-cum)
        r, k = r_ref[q].astype(F32), k_ref[q].astype(F32)
        rs.append(r)
        ks.append(k)
        vs.append(v_ref[q].astype(F32))
        ats.append(an_ref[q].astype(F32) * jnp.exp(cum - lw))
        rts.append(r * w_incl)
        bts.append(bb_ref[q].astype(F32) * w_inv)
        kts.append(k * w_inv)
        ends.append(w_incl[chunk - 1:chunk, :])

    def slabs(xs):
        return [xs[q][:, lanes[p]] for q, p in problems]

    o2, s_new = _rwkv_pairs_chunk(
        slabs(ats), slabs(rts), slabs(bts), slabs(kts), slabs(vs), [s_scr[q, p] for q, p in problems],
        chunk=chunk, same_head=same_head, strict=strict, incl=incl)
    for n, (q, p) in enumerate(problems):
        ls = lanes[p]
        s_scr[q, p] = s_new[n] * ends[q][:, ls]
        o = o2[n][:chunk] + o2[n][chunk:]
        mu = head_sum(o) * (1.0 / RW_HEAD)
        oc = o - mu
        var = head_sum(oc * oc) * (1.0 / RW_HEAD)
        on = oc * lax.rsqrt(var + RW_GN_EPS) * gw_ref[:, ls] + gb_ref[:, ls]
        bonus = head_sum(rs[q][:, ls] * ks[q][:, ls] * rk_ref[:, ls]) * vs[q][:, ls]
        o_ref[q, :, ls] = (on + bonus).astype(o_ref.dtype)

    @pl.when(t == pl.num_programs(1) - 1)
    def _():
        for q in range(seqs):
            for p in range(RW_PAIRS):
                s = s_scr[q, p]
                sout_ref[q, 2 * p] = s[:RW_HEAD, :RW_HEAD]
                sout_ref[q, 2 * p + 1] = s[RW_HEAD:, RW_HEAD:]


def _rwkv(grp, seqs, gn_w, gn_b, r_k, s0):
    B, T = grp.B, grp.T
    chunk = min(T, CHUNK)
    nb = max(_seqs_per_step(B, T), 2 - B % 2)
    W = RW_HEADS * RW_HEAD
    seq_spec = pl.BlockSpec((nb, chunk, W), lambda b, t: (b, t, 0))
    vec_spec = pl.BlockSpec((1, W), lambda b, t: (0, 0))
    st_spec = pl.BlockSpec((nb, RW_HEADS, RW_HEAD, RW_HEAD), lambda b, t: (b, 0, 0, 0))
    kern = functools.partial(_rwkv_kernel, chunk=chunk, seqs=nb)
    return pl.pallas_call(
        kern,
        grid=(B // nb, T // chunk),
        in_specs=[seq_spec] * 6 + [vec_spec] * 3 + [st_spec],
        out_specs=[seq_spec, st_spec],
        out_shape=[jax.ShapeDtypeStruct((B, T, W), BF16),
                   jax.ShapeDtypeStruct((B, RW_HEADS, RW_HEAD, RW_HEAD), F32)],
        scratch_shapes=[pltpu.VMEM((nb, RW_PAIRS, 2 * RW_HEAD, 2 * RW_HEAD), F32)],
        compiler_params=_params(("arbitrary", "arbitrary")),
        name="rwkv_scan",
    )(*seqs, gn_w, gn_b, r_k, s0)


def _merge_kernel(og_ref, gg_ref, orw_ref, gate_ref, mga_ref, mgb_ref, x_ref,
                  gate1_ref, shift2_ref, scale2_ref,
                  ggn_ref, gpost_ref, gpre_ref, wbg_ref, wbr_ref, wout_ref, wq_ref,
                  x1_ref, h2t_ref, qp_ref):
    gg = gg_ref[...].astype(F32)
    gla_in = og_ref[...].astype(F32) * ggn_ref[...] * (gg * jax.nn.sigmoid(gg))
    br_gla = _mm(gla_in, wbg_ref[...])
    br_rw = _mm(orw_ref[...].astype(F32) * gate_ref[...].astype(F32), wbr_ref[...])
    merged = (jax.nn.sigmoid(mga_ref[...].astype(F32)) * br_gla
              + jax.nn.sigmoid(mgb_ref[...].astype(F32)) * br_rw)
    y = _mm(merged, wout_ref[...])
    x1 = x_ref[...] + gate1_ref[...] * _rms(y, gpost_ref[...])
    x1_ref[...] = x1
    h2 = _rms(x1, gpre_ref[...]) * (1.0 + scale2_ref[...]) + shift2_ref[...]
    h2t_ref[...] = h2.T.astype(BF16)
    qp_ref[...] = jnp.dot(h2.astype(BF16), wq_ref[...], preferred_element_type=F32).astype(BF16)


def _merge(grp, og, p2, orw, gate_rw, x2d, modop, wts):
    tm, M = grp.tm, grp.M
    W = D_MODEL

    def tok(blk):
        return pl.BlockSpec((tm, W), lambda i: (i, blk))

    def full(shape):
        return pl.BlockSpec(shape, lambda i: (0,) * len(shape), pipeline_mode=pl.Buffered(1))

    return pl.pallas_call(
        _merge_kernel,
        grid=(M // tm,),
        in_specs=[tok(0), tok(COL_GG // W), tok(0), tok(0), tok(COL_MG // W), tok(COL_MG // W + 1), tok(0),
                  grp.mod_spec(2, tm), grp.mod_spec(3, tm), grp.mod_spec(4, tm),
                  full((1, W)), full((1, W)), full((1, W)),
                  full((W, W)), full((W, W)), full((W, W)), full((W, 2 * W))],
        out_specs=[tok(0), pl.BlockSpec((W, tm), lambda i: (0, i)), pl.BlockSpec((tm, 2 * W), lambda i: (i, 0))],
        out_shape=[jax.ShapeDtypeStruct((M, W), F32), jax.ShapeDtypeStruct((W, M), BF16),
                   jax.ShapeDtypeStruct((M, 2 * W), BF16)],
        compiler_params=_params(("arbitrary",)),
        name="merge",
    )(og, p2, orw, gate_rw, p2, p2, x2d, modop, modop, modop,
      wts["g_gla_norm"], wts["g_post_mix"], wts["g_pre_ffn"],
      wts["w_b_gla"], wts["w_b_rw"], wts["w_out"], wts["peer_wq"])


def _pair_bits(x):
    bits = pltpu.bitcast(x.astype(BF16).astype(F32), jnp.uint32)
    return bits | (bits >> 16)


def _row_tile(row, rows):
    tile = pltpu.bitcast(jnp.broadcast_to(row, (8, row.shape[1])), BF16)
    return jnp.tile(tile, (rows // 16, 1))


def _batcher_network(n):
    def merge(lo, hi, r):
        step = r * 2
        if step < hi - lo:
            yield from merge(lo, hi, step)
            yield from merge(lo + r, hi, step)
            yield from [(i, i + r) for i in range(lo + r, hi - r, step)]
        else:
            yield (lo, lo + r)

    def sort(lo, hi):
        if hi - lo >= 1:
            mid = lo + (hi - lo) // 2
            yield from sort(lo, mid)
            yield from sort(mid + 1, hi)
            yield from merge(lo, hi, 1)

    return list(sort(0, n - 1))


def _top_values(s, n):
    sub = 8
    depth = s.shape[0] // sub
    cols = [s[i * sub:(i + 1) * sub, :] for i in range(depth)]
    for i, j in _batcher_network(depth):
        cols[i], cols[j] = jnp.maximum(cols[i], cols[j]), jnp.minimum(cols[i], cols[j])
    rows = lax.broadcasted_iota(jnp.int32, (n, s.shape[1]), 0)
    vals = jnp.zeros((n, s.shape[1]), F32)
    for it in range(n):
        m = jnp.max(cols[0], axis=0, keepdims=True)
        vals = jnp.where(rows == it, m, vals)
        hit = cols[0] == m
        for i in range(min(depth - 1, n - 1 - it)):
            cols[i] = jnp.where(hit, cols[i + 1], cols[i])
    return vals


def _rank_among(s, vals):
    rank = jnp.zeros_like(s)
    for b in range(vals.shape[0]):
        rank = rank + jnp.where(vals[b:b + 1, :] > s, 1.0, 0.0)
    return rank


def _candidate_groups(v1, v2):
    n, tm = v1.shape
    split = 4
    r_all = lax.broadcasted_iota(jnp.int32, (n, tm), 0)
    r_half = lax.broadcasted_iota(jnp.int32, (n // 2, tm), 0)
    groups = [v1[0:1, :] + v2]
    for a in range(1, split):
        groups.append(jnp.where(r_half < n // (a + 1), v1[a:a + 1, :] + v2[:n // 2, :], NEG_INF))
    groups.append(jnp.where(r_all >= split, v1 + v2[0:1, :], NEG_INF))
    for b in range(1, n // (split + 1)):
        keep = (r_half >= split) & (r_half < n // (b + 1))
        groups.append(jnp.where(keep, v1[:n // 2, :] + v2[b:b + 1, :], NEG_INF))
    return groups


def _group_max(groups, n):
    full = [g for g in groups if g.shape[0] == n]
    best = full[0]
    for g in full[1:]:
        best = jnp.maximum(best, g)
    best = jnp.maximum(best[:n // 2, :], best[n // 2:, :])
    for g in groups:
        if g.shape[0] != n:
            best = jnp.maximum(best, g)
    return jnp.max(best, axis=0, keepdims=True)


def _select_kernel(qp_ref, keys_ref, r2_ref, e2_ref, al_ref, n1_ref):
    tm = qp_ref.shape[0]
    for h in range(PEER_HEADS):
        q1 = qp_ref[:, (2 * h) * 128:(2 * h + 1) * 128]
        q2 = qp_ref[:, (2 * h + 1) * 128:(2 * h + 2) * 128]
        s1 = _mm_nt(keys_ref[2 * h], q1)
        s2 = _mm_nt(keys_ref[2 * h + 1], q2)
        v1 = _top_values(s1, PEER_TOPK)
        v2 = _top_values(s2, PEER_TOPK)
        rank2 = _rank_among(s2, v2)
        groups = _candidate_groups(v1, v2)
        cmax = v1[0:1, :] + v2[0:1, :]
        z = jnp.zeros((1, tm), F32)
        m = cmax
        for _ in range(PEER_TOPK):
            m = _group_max(groups, PEER_TOPK)
            z = z + jnp.exp(m - cmax)
            groups = [jnp.where(g == m, NEG_INF, g) for g in groups]
        th = m - s1
        n1 = jnp.zeros_like(s1)
        for b in range(PEER_TOPK):
            n1 = n1 + jnp.where(v2[b:b + 1, :] >= th, 1.0, 0.0)
        r2_ref[h] = rank2.astype(BF16)
        e2_ref[h] = jnp.exp(s2 - v2[0:1, :]).astype(BF16)
        al_ref[h] = _pair_bits(jnp.exp(s1 - v1[0:1, :]) / z)
        n1_ref[h] = _pair_bits(n1)


def _peer_select(grp, qp, keys):
    tm, M = grp.tm, grp.M
    rows = pl.BlockSpec((PEER_HEADS, PEER_NKEYS, tm), lambda i: (0, 0, i))
    return pl.pallas_call(
        _select_kernel,
        grid=(M // tm,),
        in_specs=[pl.BlockSpec((tm, 2 * D_MODEL), lambda i: (i, 0)),
                  pl.BlockSpec((2 * PEER_HEADS, PEER_NKEYS, 128), lambda i: (0, 0, 0))],
        out_specs=[rows] * 4,
        out_shape=[jax.ShapeDtypeStruct((PEER_HEADS, PEER_NKEYS, M), BF16)] * 2
        + [jax.ShapeDtypeStruct((PEER_HEADS, PEER_NKEYS, M), jnp.uint32)] * 2,
        compiler_params=_params(("arbitrary",)),
        name="peer_select",
    )(qp, keys)


def _peer_weights(act_ref, y_ref, r2_ref, e2_ref, al_ref, n1_ref, row0):
    tm = act_ref.shape[1]
    half = PEER_NKEYS // 2
    for j in range(act_ref.shape[0] // PEER_NKEYS):
        i1 = slice(row0 + j, row0 + j + 1)
        w = [jnp.zeros((half, tm), BF16) for _ in range(2)]
        for h in range(PEER_HEADS):
            n1 = _row_tile(n1_ref[h, i1, :], half)
            al = _row_tile(al_ref[h, i1, :], half)
            for r in range(2):
                keys = slice(r * half, (r + 1) * half)
                w[r] = w[r] + jnp.where(r2_ref[h, keys, :] < n1, e2_ref[h, keys, :], 0.0) * al
        for r in range(2):
            rows = slice(j * PEER_NKEYS + r * half, j * PEER_NKEYS + (r + 1) * half)
            a = act_ref[rows, :].astype(BF16)
            y_ref[rows, :] = (0.5 * a * (1.0 + lax.erf(a * 0.7071067811865476))) * w[r]


def _peer_kernel(h2t_ref, u_ref, vt_ref, r2_ref, e2_ref, al_ref, n1_ref, x1_ref, gate2_ref, gpost_ref,
                 o_ref, acc_ref, act_0, act_1, y_0, y_1):
    e = pl.program_id(1)
    ts = act_0.shape[0]
    per = ts // PEER_NKEYS
    acts, ys = (act_0, act_1), (y_0, y_1)

    @pl.when(e == 0)
    def _():
        acc_ref[...] = jnp.zeros_like(acc_ref)

    def scores(k):
        acts[k % 2][...] = jnp.dot(u_ref[k], h2t_ref[...], preferred_element_type=F32)

    def mix(k):
        acc_ref[...] += jnp.dot(vt_ref[k], ys[k % 2][...], preferred_element_type=F32)

    scores(0)
    scores(1)
    for k in range(PEER_SUBTILES):
        _peer_weights(acts[k % 2], ys[k % 2], r2_ref, e2_ref, al_ref, n1_ref, k * per)
        if k + 2 < PEER_SUBTILES:
            scores(k + 2)
        if k >= 1:
            mix(k - 1)
    mix(PEER_SUBTILES - 1)

    @pl.when(e == pl.num_programs(1) - 1)
    def _():
        o_ref[...] = x1_ref[...] + gate2_ref[...] * _rms(acc_ref[...].T, gpost_ref[...])


def _peer(grp, h2t, sel, x1, modop, g_post, pu, pvt):
    tm, M = grp.tm, grp.M
    ts = PEER_EXPERT_TILE
    te = ts * PEER_SUBTILES
    r2, e2, al, n1 = sel
    packed = pl.BlockSpec((PEER_HEADS, PEER_NKEYS, tm), lambda i, e: (0, 0, i))
    rows = pl.BlockSpec((PEER_HEADS, te // PEER_NKEYS, tm), lambda i, e: (0, e, i))
    tok = pl.BlockSpec((tm, D_MODEL), lambda i, e: (i, 0))
    return pl.pallas_call(
        _peer_kernel,
        grid=(M // tm, pu.shape[0] // PEER_SUBTILES),
        in_specs=[pl.BlockSpec((D_MODEL, tm), lambda i, e: (0, i)),
                  pl.BlockSpec((PEER_SUBTILES, ts, D_MODEL), lambda i, e: (e, 0, 0)),
                  pl.BlockSpec((PEER_SUBTILES, D_MODEL, ts), lambda i, e: (e, 0, 0)),
                  packed, packed, rows, rows,
                  pl.BlockSpec((tm, D_MODEL), lambda i, e: (i, 0), pipeline_mode=pl.Buffered(1)),
                  grp.mod_spec(5),
                  pl.BlockSpec((1, D_MODEL), lambda i, e: (0, 0))],
        out_specs=tok,
        out_shape=jax.ShapeDtypeStruct((M, D_MODEL), F32),
        scratch_shapes=[pltpu.VMEM((D_MODEL, tm), F32),
                        pltpu.VMEM((ts, tm), F32), pltpu.VMEM((ts, tm), F32),
                        pltpu.VMEM((ts, tm), BF16), pltpu.VMEM((ts, tm), BF16)],
        compiler_params=_params(("arbitrary", "arbitrary")),
        name="peer_dense",
    )(h2t, pu, pvt, r2, e2, al, n1, x1, modop, g_post)


def _prepare_weights(w_in, w_gla_gate, b_gla_gate, g_gla_norm, mu_shift, rw_w0, rw_w2, rw_a0, rw_a2,
                     rw_g2, rw_k_k, rw_k_a, rw_r_k, rw_gn_w, rw_gn_b, w_b_gla, w_b_rw, w_out,
                     peer_wq, peer_keys, peer_u, peer_v, g_pre_mix, g_post_mix, g_pre_ffn, g_post_ffn):
    W = RW_HEADS * RW_HEAD
    rw = w_in[:, 3088:6416]
    w_proj = jnp.concatenate(
        [rw[:, :3 * W], w_in[:, 6416:8464], w_in[:, 2048:3072], w_in[:, 1024:2048], w_in[:, 0:1024],
         rw[:, 3 * W:], w_in[:, 3072:3088], jnp.zeros((D_MODEL, PROJ_PAD_W - 8464), F32)],
        axis=1).astype(BF16)
    head = jnp.arange(W) // RW_HEAD
    zeros64 = jnp.zeros((64, W), F32)
    return dict(
        w_proj=w_proj,
        wg_pad=jnp.concatenate([w_gla_gate, jnp.zeros((128 - GLA_GATE_RANK, 512), F32)], axis=0),
        bg=b_gla_gate.reshape(1, -1),
        g_gla_norm=g_gla_norm.reshape(1, -1),
        mu_rkv=mu_shift[:3 * W].reshape(1, -1),
        mu_lora=mu_shift[3 * W:].reshape(1, -1),
        rw_w0=rw_w0.reshape(1, -1),
        w2p=jnp.concatenate([rw_w2, jnp.zeros((192, W), F32)], axis=0),
        rw_a0=rw_a0.reshape(1, -1),
        a2p=jnp.concatenate([zeros64, rw_a2, jnp.zeros((128, W), F32)], axis=0),
        g2p=jnp.concatenate([jnp.zeros((128, W), F32), rw_g2], axis=0),
        rw_k_k=rw_k_k.reshape(1, -1),
        rw_k_a=rw_k_a.reshape(1, -1),
        rw_r_k=rw_r_k.reshape(1, -1),
        rw_gn_w=rw_gn_w.reshape(1, -1),
        rw_gn_b=rw_gn_b.reshape(1, -1),
        head_ones=(head[:, None] == head[None, :]).astype(BF16),
        w_b_gla=w_b_gla.astype(BF16),
        w_b_rw=w_b_rw.astype(BF16),
        w_out=w_out.astype(BF16),
        peer_wq=peer_wq.astype(BF16),
        peer_keys=peer_keys.reshape(2 * PEER_HEADS, PEER_NKEYS, 128).astype(BF16),
        peer_u=peer_u.astype(BF16).reshape(-1, PEER_EXPERT_TILE, D_MODEL),
        peer_vt=peer_v.reshape(-1, PEER_EXPERT_TILE, D_MODEL).transpose(0, 2, 1).astype(BF16),
        g_pre_mix=g_pre_mix.reshape(1, -1),
        g_post_mix=g_post_mix.reshape(1, -1),
        g_pre_ffn=g_pre_ffn.reshape(1, -1),
        g_post_ffn=g_post_ffn.reshape(1, -1),
    )


def _block(x, mod, s_gla0, s_rw0, shift0, wts, valid):
    B, T, _ = x.shape
    grp = _Group(B, T, valid)
    W = RW_HEADS * RW_HEAD
    x2d = x.reshape(grp.M, D_MODEL)
    modop = grp.seq_operand(mod)
    p2 = _projection(grp, x2d, modop, wts["g_pre_mix"], wts["w_proj"], BF16)
    p3 = p2.reshape(B, T, PROJ_PAD_W)
    og, s_gla = _gla(grp, p3, wts["wg_pad"], wts["bg"], s_gla0)
    r, k, v, an, bb, lw, gate = _rwkv_prep(grp, p2, shift0, wts)
    seqs = tuple(z.reshape(B, T, W) for z in (r, k, v, an, bb, lw))
    orw, s_rw = _rwkv(grp, seqs, wts["rw_gn_w"], wts["rw_gn_b"], wts["rw_r_k"], s_rw0)
    x1, h2t, qp = _merge(grp, og.reshape(grp.M, -1), p2, orw.reshape(grp.M, W), gate,
                        x2d, modop, wts)
    sel = _peer_select(grp, qp, wts["peer_keys"])
    out = _peer(grp, h2t, sel, x1, modop, wts["g_post_ffn"], wts["peer_u"], wts["peer_vt"])
    ends = _Group(B, 1, 1)
    last = _projection(ends, x[:, valid - 1, :], ends.seq_operand(mod), wts["g_pre_mix"], wts["w_proj"], F32)
    shift_new = jnp.concatenate([last[:, :3 * W], last[:, COL_LORA:COL_LORA + 256]], axis=-1)
    return out.reshape(B, T, D_MODEL), s_gla, s_rw, shift_new


def kernel(x_prompt, x_sample, c_prompt, c_sample, state_gla, state_rwkv, state_shift, w_ada, b_ada, g_pre_mix, g_post_mix, g_pre_ffn, g_post_ffn, w_in, w_gla_gate, b_gla_gate, g_gla_norm, mu_shift, rw_w0, rw_w2, rw_a0, rw_a2, rw_g2, rw_k_k, rw_k_a, rw_r_k, rw_gn_w, rw_gn_b, w_b_gla, w_b_rw, w_out, peer_wq, peer_keys, peer_u, peer_v):
    depth = w_in.shape[0]
    bp, bs = x_prompt.shape[0], x_sample.shape[0]
    ts = x_sample.shape[1]
    ts_pad = -(-ts // 8) * 8
    xp = x_prompt
    xs = jnp.pad(x_sample, ((0, 0), (0, ts_pad - ts), (0, 0)))
    outs = [[] for _ in range(6)]
    for l in range(depth):
        wts = _prepare_weights(
            w_in[l], w_gla_gate[l], b_gla_gate[l], g_gla_norm[l], mu_shift[l], rw_w0[l], rw_w2[l],
            rw_a0[l], rw_a2[l], rw_g2[l], rw_k_k[l], rw_k_a[l], rw_r_k[l], rw_gn_w[l], rw_gn_b[l],
            w_b_gla[l], w_b_rw[l], w_out[l], peer_wq[l], peer_keys[l], peer_u[l], peer_v[l],
            g_pre_mix[l], g_post_mix[l], g_pre_ffn[l], g_post_ffn[l])
        mod = _modulation(jnp.concatenate([c_prompt, c_sample], axis=0), w_ada[l], b_ada[l])
        zg = jnp.zeros((bp, GLA_HEADS, GLA_DK, GLA_DV), F32)
        zr = jnp.zeros((bp, RW_HEADS, RW_HEAD, RW_HEAD), F32)
        zs = jnp.zeros((bp, RW_SHIFT_W), F32)
        xp, g1, r1, s1 = _block(xp, mod[:bp], zg, zr, zs, wts, xp.shape[1])
        xs, g2, r2, s2 = _block(xs, mod[bp:], state_gla[l], state_rwkv[l], state_shift[l], wts, ts)
        for acc, val in zip(outs, (g1, r1, s1, g2, r2, s2)):
            acc.append(val)
    stacked = [jnp.stack(o) for o in outs]
    return (xp, xs[:, :ts], *stacked)
```

```python
import functools

import jax
import jax.numpy as jnp
from jax import lax
from jax.experimental import pallas as pl
from jax.experimental.pallas import tpu as pltpu

F32 = jnp.float32
BF16 = jnp.bfloat16

D_MODEL = 1024
GLA_HEADS = 4
GLA_DK = 128
GLA_DV = 256
GLA_GATE_RANK = 16
GLA_GATE_TAU = 16.0
GLA_NORM_EPS = 1e-5
RW_HEAD = 64
RW_HEADS = 16
RW_PAIRS = RW_HEADS // 2
RW_GN_EPS = 64e-5
RW_SHIFT_W = 3328
PEER_HEADS = 8
PEER_NKEYS = 128
PEER_TOPK = 16
NORM_EPS = 1e-6
LANES = 128

COL_R, COL_K, COL_V = 0, 1024, 2048
COL_MG = 3072
COL_GG = 5120
COL_GV = 6144
COL_GQ = 7168
COL_GK = 7680
COL_LORA = 8192
COL_GA = 8448
PROJ_PAD_W = 8704
PROJ_TILE_N = 2176
PROJ_TILE_M = 1024

CHUNK = 64
TOKEN_TILE = 512
PEER_EXPERT_TILE = 512
PEER_SUBTILES = 4
VMEM_LIMIT = 56 * 1024 * 1024

NEG_INF = float("-inf")


def _mm(a, b):
    return jnp.dot(a.astype(BF16), b.astype(BF16), preferred_element_type=F32)


def _mm_nt(a, b):
    return lax.dot_general(a.astype(BF16), b.astype(BF16), (((1,), (1,)), ((), ())),
                           preferred_element_type=F32)


def _mm_tn(a, b):
    return lax.dot_general(a.astype(BF16), b.astype(BF16), (((0,), (0,)), ((), ())),
                           preferred_element_type=F32)


def _split(x):
    hi = x.astype(BF16)
    lo = (x - hi.astype(F32)).astype(BF16)
    return hi, lo


def _mm_exact_lhs(a, x):
    hi, lo = _split(x)
    a = a.astype(BF16)
    return (jnp.dot(a, hi, preferred_element_type=F32)
            + jnp.dot(a, lo, preferred_element_type=F32))


def _mm_exact_rhs(x, a):
    hi, lo = _split(x)
    a = a.astype(BF16)
    return (jnp.dot(hi, a, preferred_element_type=F32)
            + jnp.dot(lo, a, preferred_element_type=F32))


def _softplus(x):
    return jnp.maximum(x, 0.0) + jnp.log1p(jnp.exp(-jnp.abs(x)))


def _rms(x, g):
    return x * lax.rsqrt(jnp.mean(x * x, axis=-1, keepdims=True) + NORM_EPS) * g


def _params(sem, flags=None):
    return pltpu.CompilerParams(dimension_semantics=sem, vmem_limit_bytes=VMEM_LIMIT, flags=flags)


def _mod_kernel(c_ref, w_ref, b_ref, o_ref):
    c = c_ref[...]
    o_ref[...] = _mm(c * jax.nn.sigmoid(c), w_ref[...]) + b_ref[...]


def _modulation(c, w_ada, b_ada):
    rows = c.shape[0]
    n = w_ada.shape[1]
    tn = 1536
    return pl.pallas_call(
        _mod_kernel,
        grid=(n // tn,),
        in_specs=[pl.BlockSpec((rows, D_MODEL), lambda j: (0, 0)),
                  pl.BlockSpec((D_MODEL, tn), lambda j: (0, j)),
                  pl.BlockSpec((1, tn), lambda j: (0, j))],
        out_specs=pl.BlockSpec((rows, tn), lambda j: (0, j)),
        out_shape=jax.ShapeDtypeStruct((rows, n), F32),
        compiler_params=_params(("arbitrary",)),
        name="adaln_mod",
    )(c, w_ada, b_ada.reshape(1, n))


class _Group:
    def __init__(self, batch, seq, valid):
        self.B, self.T, self.Tv = batch, seq, valid
        self.M = batch * seq
        self.tm = min(TOKEN_TILE, self.M)
        self.per_batch_mod = seq % self.tm == 0

    def seq_operand(self, arr):
        if self.per_batch_mod:
            return arr.reshape(self.B, 1, arr.shape[-1])
        return jnp.repeat(arr, self.T, axis=0)

    def seq_spec(self, width, col, tm=None, tile_index=None):
        tm, T = tm or self.tm, self.T
        tile = tile_index or (lambda i, *_: i)
        if self.per_batch_mod:
            return pl.BlockSpec((None, 1, width), lambda *g: ((tile(*g) * tm) // T, 0, col))
        return pl.BlockSpec((tm, width), lambda *g: (tile(*g), col))

    def mod_spec(self, col, tm=None, tile_index=None):
        return self.seq_spec(D_MODEL, col, tm, tile_index)


def _proj_kernel(x_ref, shift_ref, scale_ref, g_ref, w_ref, p_ref, h_scr):
    @pl.when(pl.program_id(1) == 0)
    def _():
        h = _rms(x_ref[...], g_ref[...]) * (1.0 + scale_ref[...]) + shift_ref[...]
        h_scr[...] = h.astype(BF16)

    p_ref[...] = jnp.dot(h_scr[...], w_ref[...], preferred_element_type=F32).astype(p_ref.dtype)


def _projection(grp, x2d, modop, g_pre, w_proj, dtype):
    tm = PROJ_TILE_M if grp.per_batch_mod and grp.T % PROJ_TILE_M == 0 else grp.tm
    return pl.pallas_call(
        _proj_kernel,
        grid=(grp.M // tm, PROJ_PAD_W // PROJ_TILE_N),
        in_specs=[pl.BlockSpec((tm, D_MODEL), lambda i, j: (i, 0)),
                  grp.mod_spec(0, tm), grp.mod_spec(1, tm),
                  pl.BlockSpec((1, D_MODEL), lambda i, j: (0, 0)),
                  pl.BlockSpec((D_MODEL, PROJ_TILE_N), lambda i, j: (0, j))],
        out_specs=pl.BlockSpec((tm, PROJ_TILE_N), lambda i, j: (i, j)),
        out_shape=jax.ShapeDtypeStruct((grp.M, PROJ_PAD_W), dtype),
        scratch_shapes=[pltpu.VMEM((tm, D_MODEL), BF16)],
        compiler_params=_params(("arbitrary", "arbitrary")),
        name="in_proj",
    )(x2d, modop, modop, g_pre, w_proj)


def _gla_kernel(q_ref, k_ref, v_ref, ga_ref, wg_ref, bg_ref, s0_ref, og_ref, sout_ref, st_scr,
                *, chunk, n_chunks, tile, valid, seqs):
    t = pl.program_id(1)
    last = pl.num_programs(1) - 1
    row = lax.broadcasted_iota(jnp.int32, (tile, tile), 0)
    col = lax.broadcasted_iota(jnp.int32, (tile, tile), 1)
    shift = chunk.bit_length() - 1
    causal = (lax.shift_right_logical(row, shift) == lax.shift_right_logical(col, shift)) & (row >= col)
    tril = jnp.where(causal, 1.0, 0.0).astype(BF16)

    batch, heads, chunks = range(seqs), range(GLA_HEADS), range(n_chunks)
    pairs = [(b, h) for b in batch for h in heads]
    ks = [slice(h * GLA_DK, (h + 1) * GLA_DK) for h in heads]
    vs = [slice(h * GLA_DV, (h + 1) * GLA_DV) for h in heads]
    rows = [slice(c * chunk, (c + 1) * chunk) for c in chunks]
    ends = [slice((c + 1) * chunk - 1, (c + 1) * chunk) for c in chunks]

    @pl.when(t == 0)
    def _():
        for b, h in pairs:
            st_scr[b, h] = s0_ref[b, h].T

    z = [_mm(ga_ref[b], wg_ref[...]) + bg_ref[...] for b in batch]
    la = [-_softplus(-z[b]) / GLA_GATE_TAU for b in batch]
    k = [k_ref[b].astype(F32) for b in batch]
    v = {(b, h): v_ref[b, :, vs[h]] for b, h in pairs}
    if valid is not None:
        ok = (t * tile + lax.broadcasted_iota(jnp.int32, (tile, 1), 0)) < valid
        la = [jnp.where(ok, x, 0.0) for x in la]
        k = [jnp.where(ok, x, 0.0) for x in k]
        v = {key: jnp.where(ok, x, 0.0) for key, x in v.items()}
    cum = [_mm_exact_lhs(tril, la[b]) for b in batch]
    end = [jnp.concatenate([jnp.broadcast_to(cum[b][ends[c], :], (chunk, cum[b].shape[1])) for c in chunks], axis=0)
           for b in batch]
    qe = [q_ref[b].astype(F32) * (GLA_DK ** -0.5) * jnp.exp(cum[b]) for b in batch]
    ke = [k[b] * jnp.exp(-cum[b]) for b in batch]
    kd = [k[b] * jnp.exp(end[b] - cum[b]) for b in batch]
    decay = [[jnp.exp(cum[b][ends[c], :]) for c in chunks] for b in batch]
    a = {(b, h): jnp.where(causal, _mm_nt(qe[b][:, ks[h]], ke[b][:, ks[h]]), 0.0) for b, h in pairs}
    inside = {key: _mm(a[key], v[key]) for key in pairs}
    update = {(b, h, c): _mm_tn(v[b, h][rows[c], :], kd[b][rows[c], ks[h]]) for b, h in pairs for c in chunks}

    st = {(b, h, 0): st_scr[b, h] for b, h in pairs}
    for c in chunks:
        for b, h in pairs:
            st[b, h, c + 1] = st[b, h, c] * decay[b][c][:, ks[h]] + update[b, h, c]
    outer = {(b, h, c): _mm_nt(qe[b][rows[c], ks[h]], st[b, h, c]) for b, h in pairs for c in chunks}
    for b, h in pairs:
        st_scr[b, h] = st[b, h, n_chunks]
        for c in chunks:
            o = outer[b, h, c] + inside[b, h][rows[c], :]
            og = o * lax.rsqrt(jnp.mean(o * o, axis=-1, keepdims=True) + GLA_NORM_EPS)
            og_ref[b, rows[c], vs[h]] = og.astype(og_ref.dtype)

    @pl.when(t == last)
    def _():
        for b, h in pairs:
            sout_ref[b, h] = st_scr[b, h].T


def _seqs_per_step(batch, seq):
    n = max(1, min(8, 64 // seq))
    while batch % n:
        n //= 2
    return n


def _gla(grp, p3, wg_pad, bg, s0):
    B, T = grp.B, grp.T
    tile = min(T, 4 * CHUNK)
    chunk = min(CHUNK, tile)
    nb = _seqs_per_step(B, T)
    kern = functools.partial(_gla_kernel, chunk=chunk, n_chunks=tile // chunk, tile=tile,
                             valid=None if grp.Tv == T else grp.Tv, seqs=nb)
    kw, vw = GLA_HEADS * GLA_DK, GLA_HEADS * GLA_DV
    st_spec = pl.BlockSpec((nb, GLA_HEADS, GLA_DK, GLA_DV), lambda i, t: (i, 0, 0, 0))
    return pl.pallas_call(
        kern,
        grid=(B // nb, T // tile),
        in_specs=[pl.BlockSpec((nb, tile, kw), lambda i, t: (i, t, COL_GQ // kw)),
                  pl.BlockSpec((nb, tile, kw), lambda i, t: (i, t, COL_GK // kw)),
                  pl.BlockSpec((nb, tile, vw), lambda i, t: (i, t, COL_GV // vw)),
                  pl.BlockSpec((nb, tile, LANES), lambda i, t: (i, t, COL_GA // LANES)),
                  pl.BlockSpec((LANES, kw), lambda i, t: (0, 0)),
                  pl.BlockSpec((1, kw), lambda i, t: (0, 0)),
                  st_spec],
        out_specs=[pl.BlockSpec((nb, tile, vw), lambda i, t: (i, t, 0)), st_spec],
        out_shape=[jax.ShapeDtypeStruct((B, T, vw), BF16),
                   jax.ShapeDtypeStruct((B, GLA_HEADS, GLA_DK, GLA_DV), F32)],
        scratch_shapes=[pltpu.VMEM((nb, GLA_HEADS, GLA_DV, GLA_DK), F32)],
        compiler_params=_params(("arbitrary", "arbitrary")),
        name="gla_scan",
    )(p3, p3, p3, p3, wg_pad, bg, s0)


def _rwprep_kernel(r_ref, k_ref, v_ref, l_ref, pr_ref, pk_ref, pv_ref, pl_ref,
                   s0r_ref, s0k_ref, s0v_ref, s0l_ref, mu_ref, mul_ref, w0_ref, w2_ref, a0_ref, a2_ref, g2_ref,
                   kk_ref, ka_ref, bd_ref,
                   ro_ref, ko_ref, vo_ref, an_ref, bb_ref, lw_ref, gate_ref,
                   *, tile, seq, valid, prev_rows):
    i = pl.program_id(0)
    local = lax.broadcasted_iota(jnp.int32, (tile, 1), 0)
    pos = lax.rem(i * tile + local, seq)
    first = pos == 0
    top = local == 0

    def shifted(cur_ref, prev_ref, s0_ref, mu):
        cur = cur_ref[...].astype(F32)
        before = prev_ref[prev_rows - 1:prev_rows, :].astype(F32)
        prev = jnp.where(first, s0_ref[...], jnp.where(top, before, pltpu.roll(cur, 1, 0)))
        return cur + (prev - cur) * mu

    r = shifted(r_ref, pr_ref, s0r_ref, mu_ref[:, 0:1024])
    k = shifted(k_ref, pk_ref, s0k_ref, mu_ref[:, 1024:2048])
    v = shifted(v_ref, pv_ref, s0v_ref, mu_ref[:, 2048:3072])
    lo = shifted(l_ref, pl_ref, s0l_ref, mul_ref[...])

    wpre = w0_ref[...] + _mm(jnp.tanh(lo), w2_ref[...])
    wlog = -_softplus(-wpre) - 0.5
    lw = -jnp.exp(wlog)
    a = jax.nn.sigmoid(a0_ref[...] + _mm(lo, a2_ref[...]))
    gate = _mm(jax.nn.sigmoid(lo), g2_ref[...])

    kk = k * kk_ref[...]
    ss = _mm(kk * kk, bd_ref[...])
    kk = kk / jnp.maximum(jnp.sqrt(ss), 1e-12)
    kmod = k * (1.0 + (a - 1.0) * ka_ref[...])
    an = -kk
    bb = kk * a
    if valid is not None:
        ok = pos < valid
        lw = jnp.where(ok, lw, 0.0)
        an = jnp.where(ok, an, 0.0)
        bb = jnp.where(ok, bb, 0.0)
        kmod = jnp.where(ok, kmod, 0.0)
        v = jnp.where(ok, v, 0.0)
    ro_ref[...] = r.astype(ro_ref.dtype)
    ko_ref[...] = kmod.astype(ko_ref.dtype)
    vo_ref[...] = v.astype(vo_ref.dtype)
    an_ref[...] = an.astype(an_ref.dtype)
    bb_ref[...] = bb.astype(bb_ref.dtype)
    lw_ref[...] = lw
    gate_ref[...] = gate.astype(gate_ref.dtype)


def _rwkv_prep(grp, p2, shift0, wts):
    M, T = grp.M, grp.T
    tile = min(M, 512)
    prev_rows = min(16, tile)
    W = RW_HEADS * RW_HEAD
    s0 = grp.seq_operand(shift0)

    def cur(width, blk):
        return pl.BlockSpec((tile, width), lambda i: (i, blk))

    def prev(width, blk):
        return pl.BlockSpec((prev_rows, width), lambda i: (jnp.maximum(i * (tile // prev_rows) - 1, 0), blk))

    def full(shape):
        return pl.BlockSpec(shape, lambda i: (0,) * len(shape))

    out_spec = pl.BlockSpec((tile, W), lambda i: (i, 0))
    kern = functools.partial(_rwprep_kernel, tile=tile, seq=T, valid=None if grp.Tv == T else grp.Tv,
                             prev_rows=prev_rows)
    lora_blk = COL_LORA // 256
    return pl.pallas_call(
        kern,
        grid=(M // tile,),
        in_specs=[cur(W, 0), cur(W, 1), cur(W, 2), cur(256, lora_blk),
                  prev(W, 0), prev(W, 1), prev(W, 2), prev(256, lora_blk),
                  grp.seq_spec(W, 0, tile), grp.seq_spec(W, 1, tile), grp.seq_spec(W, 2, tile),
                  grp.seq_spec(256, 3 * W // 256, tile),
                  full((1, 3 * W)), full((1, 256)),
                  full((1, W)), full((256, W)), full((1, W)), full((256, W)), full((256, W)),
                  full((1, W)), full((1, W)), full((W, W))],
        out_specs=[out_spec] * 7,
        out_shape=[jax.ShapeDtypeStruct((M, W), BF16)] * 5
        + [jax.ShapeDtypeStruct((M, W), F32), jax.ShapeDtypeStruct((M, W), BF16)],
        compiler_params=_params(("arbitrary",)),
        name="rwkv_prep",
    )(p2, p2, p2, p2, p2, p2, p2, p2, s0, s0, s0, s0,
      wts["mu_rkv"], wts["mu_lora"], wts["rw_w0"], wts["w2p"], wts["rw_a0"], wts["a2p"], wts["g2p"],
      wts["rw_k_k"], wts["rw_k_a"], wts["head_ones"])


def _rwkv_pairs_chunk(at, rt, bt, kt, v, states, *, chunk, same_head, strict, incl):
    n2 = 2 * chunk
    wide = n2 % LANES == 0
    pairs = range(len(states))

    def blocks(x):
        return jnp.where(same_head, jnp.concatenate([x, x], axis=0), 0.0).astype(BF16)

    a2 = [blocks(x) for x in at]
    r2 = [blocks(x) for x in rt]
    b2 = [blocks(x) for x in bt]
    k2 = [blocks(x) for x in kt]
    v2 = [blocks(x) for x in v]
    ar = [jnp.concatenate([a2[p], r2[p]], axis=0) for p in pairs]
    bk = [jnp.concatenate([b2[p], k2[p]], axis=0) for p in pairs]
    if wide:
        g = [_mm_nt(ar[p], bk[p]) for p in pairs]
        l_ab = [jnp.where(strict, g[p][:n2, :n2], 0.0) for p in pairs]
        l_ak = [jnp.where(strict, g[p][:n2, n2:], 0.0) for p in pairs]
        m_rb = [jnp.where(incl, g[p][n2:, :n2], 0.0) for p in pairs]
        m_rk = [jnp.where(incl, g[p][n2:, n2:], 0.0) for p in pairs]
    else:
        l_ab = [jnp.where(strict, _mm_nt(a2[p], b2[p]), 0.0) for p in pairs]
        l_ak = [jnp.where(strict, _mm_nt(a2[p], k2[p]), 0.0) for p in pairs]
        m_rb = [jnp.where(incl, _mm_nt(r2[p], b2[p]), 0.0) for p in pairs]
        m_rk = [jnp.where(incl, _mm_nt(r2[p], k2[p]), 0.0) for p in pairs]
    from_state = [_mm_nt(ar[p], states[p]) for p in pairs]
    x = [from_state[p][:n2] + _mm(l_ak[p], v2[p]) for p in pairs]
    pw = l_ab
    for _ in range(chunk.bit_length() - 2):
        if wide:
            y = [_mm(pw[p], jnp.concatenate([x[p], pw[p]], axis=1)) for p in pairs]
            x = [x[p] + y[p][:, :LANES] for p in pairs]
            pw = [y[p][:, LANES:] for p in pairs]
        else:
            x = [x[p] + _mm(pw[p], x[p]) for p in pairs]
            pw = [_mm(pw[p], pw[p]) for p in pairs]
    u = [x[p] + _mm(pw[p], x[p]) for p in pairs]
    uv = [jnp.concatenate([u[p].astype(BF16), v2[p]], axis=0) for p in pairs]
    if wide:
        o2 = [from_state[p][n2:] + _mm(jnp.concatenate([m_rb[p], m_rk[p]], axis=1), uv[p]) for p in pairs]
    else:
        o2 = [from_state[p][n2:] + _mm(m_rb[p], u[p]) + _mm(m_rk[p], v2[p]) for p in pairs]
    s_new = [states[p] + _mm_tn(uv[p], bk[p]) for p in pairs]
    return o2, s_new


def _rwkv_kernel(r_ref, k_ref, v_ref, an_ref, bb_ref, lw_ref, gw_ref, gb_ref, rk_ref, s0_ref,
                 o_ref, sout_ref, s_scr, *, chunk, seqs):
    t = pl.program_id(1)
    zeros = jnp.zeros((RW_HEAD, RW_HEAD), F32)

    @pl.when(t == 0)
    def _():
        for q in range(seqs):
            for p in range(RW_PAIRS):
                top = jnp.concatenate([s0_ref[q, 2 * p], zeros], axis=1)
                bot = jnp.concatenate([zeros, s0_ref[q, 2 * p + 1]], axis=1)
                s_scr[q, p] = jnp.concatenate([top, bot], axis=0)

    n2 = 2 * chunk
    row = lax.broadcasted_iota(jnp.int32, (n2, n2), 0)
    col = lax.broadcasted_iota(jnp.int32, (n2, n2), 1)

    def second(idx, size):
        return jnp.where(idx >= size, 1, 0)

    same = second(row, chunk) == second(col, chunk)
    strict = same & (row > col)
    incl = same & (row >= col)
    same_head = (second(lax.broadcasted_iota(jnp.int32, (n2, LANES), 0), chunk)
                 == second(lax.broadcasted_iota(jnp.int32, (n2, LANES), 1), RW_HEAD))
    trow = lax.broadcasted_iota(jnp.int32, (chunk, chunk), 0)
    tcol = lax.broadcasted_iota(jnp.int32, (chunk, chunk), 1)
    tril = jnp.where(trow >= tcol, 1.0, 0.0).astype(BF16)
    first_head = lax.broadcasted_iota(jnp.int32, (chunk, LANES), 1) < RW_HEAD

    def head_sum(x):
        lo = jnp.sum(jnp.where(first_head, x, 0.0), axis=-1, keepdims=True)
        hi = jnp.sum(jnp.where(first_head, 0.0, x), axis=-1, keepdims=True)
        return jnp.where(first_head, lo, hi)

    lanes = [slice(p * LANES, (p + 1) * LANES) for p in range(RW_PAIRS)]
    problems = [(q, p) for q in range(seqs) for p in range(RW_PAIRS)]
    rs, ks, vs, ats, rts, bts, kts, ends = [], [], [], [], [], [], [], []
    for q in range(seqs):
        lw = lw_ref[q]
        cum = _mm_exact_lhs(tril, lw)
        w_incl = jnp.exp(cum)
        w_inv = jnp.exp(-cum)
        r, k = r_ref[q].astype(F32), k_ref[q].astype(F32)
        rs.append(r)
        ks.append(k)
        vs.append(v_ref[q].astype(F32))
        ats.append(an_ref[q].astype(F32) * jnp.exp(cum - lw))
        rts.append(r * w_incl)
        bts.append(bb_ref[q].astype(F32) * w_inv)
        kts.append(k * w_inv)
        ends.append(w_incl[chunk - 1:chunk, :])

    def slabs(xs):
        return [xs[q][:, lanes[p]] for q, p in problems]

    o2, s_new = _rwkv_pairs_chunk(
        slabs(ats), slabs(rts), slabs(bts), slabs(kts), slabs(vs), [s_scr[q, p] for q, p in problems],
        chunk=chunk, same_head=same_head, strict=strict, incl=incl)
    for n, (q, p) in enumerate(problems):
        ls = lanes[p]
        s_scr[q, p] = s_new[n] * ends[q][:, ls]
        o = o2[n][:chunk] + o2[n][chunk:]
        mu = head_sum(o) * (1.0 / RW_HEAD)
        oc = o - mu
        var = head_sum(oc * oc) * (1.0 / RW_HEAD)
        on = oc * lax.rsqrt(var + RW_GN_EPS) * gw_ref[:, ls] + gb_ref[:, ls]
        bonus = head_sum(rs[q][:, ls] * ks[q][:, ls] * rk_ref[:, ls]) * vs[q][:, ls]
        o_ref[q, :, ls] = (on + bonus).astype(o_ref.dtype)

    @pl.when(t == pl.num_programs(1) - 1)
    def _():
        for q in range(seqs):
            for p in range(RW_PAIRS):
                s = s_scr[q, p]
                sout_ref[q, 2 * p] = s[:RW_HEAD, :RW_HEAD]
                sout_ref[q, 2 * p + 1] = s[RW_HEAD:, RW_HEAD:]


def _rwkv(grp, seqs, gn_w, gn_b, r_k, s0):
    B, T = grp.B, grp.T
    chunk = min(T, CHUNK)
    nb = max(_seqs_per_step(B, T), 2 - B % 2)
    W = RW_HEADS * RW_HEAD
    seq_spec = pl.BlockSpec((nb, chunk, W), lambda b, t: (b, t, 0))
    vec_spec = pl.BlockSpec((1, W), lambda b, t: (0, 0))
    st_spec = pl.BlockSpec((nb, RW_HEADS, RW_HEAD, RW_HEAD), lambda b, t: (b, 0, 0, 0))
    kern = functools.partial(_rwkv_kernel, chunk=chunk, seqs=nb)
    return pl.pallas_call(
        kern,
        grid=(B // nb, T // chunk),
        in_specs=[seq_spec] * 6 + [vec_spec] * 3 + [st_spec],
        out_specs=[seq_spec, st_spec],
        out_shape=[jax.ShapeDtypeStruct((B, T, W), BF16),
                   jax.ShapeDtypeStruct((B, RW_HEADS, RW_HEAD, RW_HEAD), F32)],
        scratch_shapes=[pltpu.VMEM((nb, RW_PAIRS, 2 * RW_HEAD, 2 * RW_HEAD), F32)],
        compiler_params=_params(("arbitrary", "arbitrary")),
        name="rwkv_scan",
    )(*seqs, gn_w, gn_b, r_k, s0)


def _merge_kernel(og_ref, gg_ref, orw_ref, gate_ref, mga_ref, mgb_ref, x_ref,
                  gate1_ref, shift2_ref, scale2_ref,
                  ggn_ref, gpost_ref, gpre_ref, wbg_ref, wbr_ref, wout_ref, wq_ref,
                  x1_ref, h2t_ref, qp_ref):
    gg = gg_ref[...].astype(F32)
    gla_in = og_ref[...].astype(F32) * ggn_ref[...] * (gg * jax.nn.sigmoid(gg))
    br_gla = _mm(gla_in, wbg_ref[...])
    br_rw = _mm(orw_ref[...].astype(F32) * gate_ref[...].astype(F32), wbr_ref[...])
    merged = (jax.nn.sigmoid(mga_ref[...].astype(F32)) * br_gla
              + jax.nn.sigmoid(mgb_ref[...].astype(F32)) * br_rw)
    y = _mm(merged, wout_ref[...])
    x1 = x_ref[...] + gate1_ref[...] * _rms(y, gpost_ref[...])
    x1_ref[...] = x1
    h2 = _rms(x1, gpre_ref[...]) * (1.0 + scale2_ref[...]) + shift2_ref[...]
    h2t_ref[...] = h2.T.astype(BF16)
    qp_ref[...] = jnp.dot(h2.astype(BF16), wq_ref[...], preferred_element_type=F32).astype(BF16)


def _merge(grp, og, p2, orw, gate_rw, x2d, modop, wts):
    tm, M = grp.tm, grp.M
    W = D_MODEL

    def tok(blk):
        return pl.BlockSpec((tm, W), lambda i: (i, blk))

    def full(shape):
        return pl.BlockSpec(shape, lambda i: (0,) * len(shape), pipeline_mode=pl.Buffered(1))

    return pl.pallas_call(
        _merge_kernel,
        grid=(M // tm,),
        in_specs=[tok(0), tok(COL_GG // W), tok(0), tok(0), tok(COL_MG // W), tok(COL_MG // W + 1), tok(0),
                  grp.mod_spec(2, tm), grp.mod_spec(3, tm), grp.mod_spec(4, tm),
                  full((1, W)), full((1, W)), full((1, W)),
                  full((W, W)), full((W, W)), full((W, W)), full((W, 2 * W))],
        out_specs=[tok(0), pl.BlockSpec((W, tm), lambda i: (0, i)), pl.BlockSpec((tm, 2 * W), lambda i: (i, 0))],
        out_shape=[jax.ShapeDtypeStruct((M, W), F32), jax.ShapeDtypeStruct((W, M), BF16),
                   jax.ShapeDtypeStruct((M, 2 * W), BF16)],
        compiler_params=_params(("arbitrary",)),
        name="merge",
    )(og, p2, orw, gate_rw, p2, p2, x2d, modop, modop, modop,
      wts["g_gla_norm"], wts["g_post_mix"], wts["g_pre_ffn"],
      wts["w_b_gla"], wts["w_b_rw"], wts["w_out"], wts["peer_wq"])


def _pair_bits(x):
    bits = pltpu.bitcast(x.astype(BF16).astype(F32), jnp.uint32)
    return bits | (bits >> 16)


def _row_tile(row, rows):
    tile = pltpu.bitcast(jnp.broadcast_to(row, (8, row.shape[1])), BF16)
    return jnp.tile(tile, (rows // 16, 1))


def _batcher_network(n):
    def merge(lo, hi, r):
        step = r * 2
        if step < hi - lo:
            yield from merge(lo, hi, step)
            yield from merge(lo + r, hi, step)
            yield from [(i, i + r) for i in range(lo + r, hi - r, step)]
        else:
            yield (lo, lo + r)

    def sort(lo, hi):
        if hi - lo >= 1:
            mid = lo + (hi - lo) // 2
            yield from sort(lo, mid)
            yield from sort(mid + 1, hi)
            yield from merge(lo, hi, 1)

    return list(sort(0, n - 1))


def _top_values(s, n):
    sub = 8
    depth = s.shape[0] // sub
    cols = [s[i * sub:(i + 1) * sub, :] for i in range(depth)]
    for i, j in _batcher_network(depth):
        cols[i], cols[j] = jnp.maximum(cols[i], cols[j]), jnp.minimum(cols[i], cols[j])
    rows = lax.broadcasted_iota(jnp.int32, (n, s.shape[1]), 0)
    vals = jnp.zeros((n, s.shape[1]), F32)
    for it in range(n):
        m = jnp.max(cols[0], axis=0, keepdims=True)
        vals = jnp.where(rows == it, m, vals)
        hit = cols[0] == m
        for i in range(min(depth - 1, n - 1 - it)):
            cols[i] = jnp.where(hit, cols[i + 1], cols[i])
    return vals


def _count_prefix(vals, pred):
    def row(i):
        return vals[i:i + 1, :]

    c8 = pred(row(7))
    c4 = pred(jnp.where(c8, row(11), row(3)))
    c2 = pred(jnp.where(c8, jnp.where(c4, row(13), row(9)), jnp.where(c4, row(5), row(1))))
    upper = jnp.where(c4, jnp.where(c2, row(14), row(12)), jnp.where(c2, row(10), row(8)))
    lower = jnp.where(c4, jnp.where(c2, row(6), row(4)), jnp.where(c2, row(2), row(0)))
    c1 = pred(jnp.where(c8, upper, lower))
    c16 = pred(row(15))
    return (jnp.where(c8, 8.0, 0.0) + jnp.where(c4, 4.0, 0.0) + jnp.where(c2, 2.0, 0.0)
            + jnp.where(c1, 1.0, 0.0) + jnp.where(c16, 1.0, 0.0))


def _rank_among(s, vals):
    return _count_prefix(vals, lambda x: x > s)


def _candidate_groups(v1, v2):
    n, tm = v1.shape
    split = 4
    r_all = lax.broadcasted_iota(jnp.int32, (n, tm), 0)
    r_half = lax.broadcasted_iota(jnp.int32, (n // 2, tm), 0)
    groups = [v1[0:1, :] + v2]
    for a in range(1, split):
        groups.append(jnp.where(r_half < n // (a + 1), v1[a:a + 1, :] + v2[:n // 2, :], NEG_INF))
    groups.append(jnp.where(r_all >= split, v1 + v2[0:1, :], NEG_INF))
    for b in range(1, n // (split + 1)):
        keep = (r_half >= split) & (r_half < n // (b + 1))
        groups.append(jnp.where(keep, v1[:n // 2, :] + v2[b:b + 1, :], NEG_INF))
    return groups


def _group_max(groups, n):
    full = [g for g in groups if g.shape[0] == n]
    best = full[0]
    for g in full[1:]:
        best = jnp.maximum(best, g)
    best = jnp.maximum(best[:n // 2, :], best[n // 2:, :])
    for g in groups:
        if g.shape[0] != n:
            best = jnp.maximum(best, g)
    return jnp.max(best, axis=0, keepdims=True)


def _select_kernel(qp_ref, keys_ref, r2_ref, e2_ref, al_ref, n1_ref):
    tm = qp_ref.shape[0]
    for h in range(PEER_HEADS):
        q1 = qp_ref[:, (2 * h) * 128:(2 * h + 1) * 128]
        q2 = qp_ref[:, (2 * h + 1) * 128:(2 * h + 2) * 128]
        s1 = _mm_nt(keys_ref[2 * h], q1)
        s2 = _mm_nt(keys_ref[2 * h + 1], q2)
        v1 = _top_values(s1, PEER_TOPK)
        v2 = _top_values(s2, PEER_TOPK)
        rank2 = _rank_among(s2, v2)
        groups = _candidate_groups(v1, v2)
        cmax = v1[0:1, :] + v2[0:1, :]
        z = jnp.zeros((1, tm), F32)
        m = cmax
        for _ in range(PEER_TOPK):
            m = _group_max(groups, PEER_TOPK)
            z = z + jnp.exp(m - cmax)
            groups = [jnp.where(g == m, NEG_INF, g) for g in groups]
        th = m - s1
        n1 = _count_prefix(v2, lambda x: x >= th)
        r2_ref[h] = rank2.astype(BF16)
        e2_ref[h] = jnp.exp(s2 - v2[0:1, :]).astype(BF16)
        al_ref[h] = _pair_bits(jnp.exp(s1 - v1[0:1, :]) / z)
        n1_ref[h] = _pair_bits(n1)


def _peer_select(grp, qp, keys):
    tm, M = grp.tm, grp.M
    rows = pl.BlockSpec((PEER_HEADS, PEER_NKEYS, tm), lambda i: (0, 0, i))
    return pl.pallas_call(
        _select_kernel,
        grid=(M // tm,),
        in_specs=[pl.BlockSpec((tm, 2 * D_MODEL), lambda i: (i, 0)),
                  pl.BlockSpec((2 * PEER_HEADS, PEER_NKEYS, 128), lambda i: (0, 0, 0))],
        out_specs=[rows] * 4,
        out_shape=[jax.ShapeDtypeStruct((PEER_HEADS, PEER_NKEYS, M), BF16)] * 2
        + [jax.ShapeDtypeStruct((PEER_HEADS, PEER_NKEYS, M), jnp.uint32)] * 2,
        compiler_params=_params(("arbitrary",)),
        name="peer_select",
    )(qp, keys)


def _peer_weights(act_ref, y_ref, r2_ref, e2_ref, al_ref, n1_ref, row0):
    tm = act_ref.shape[1]
    half = PEER_NKEYS // 2
    for j in range(act_ref.shape[0] // PEER_NKEYS):
        i1 = slice(row0 + j, row0 + j + 1)
        w = [jnp.zeros((half, tm), BF16) for _ in range(2)]
        for h in range(PEER_HEADS):
            n1 = _row_tile(n1_ref[h, i1, :], half)
            al = _row_tile(al_ref[h, i1, :], half)
            for r in range(2):
                keys = slice(r * half, (r + 1) * half)
                w[r] = w[r] + jnp.where(r2_ref[h, keys, :] < n1, e2_ref[h, keys, :], 0.0) * al
        for r in range(2):
            rows = slice(j * PEER_NKEYS + r * half, j * PEER_NKEYS + (r + 1) * half)
            a = act_ref[rows, :].astype(BF16)
            y_ref[rows, :] = (0.5 * a * (1.0 + lax.erf(a * 0.7071067811865476))) * w[r]


def _peer_kernel(h2t_ref, u_ref, vt_ref, r2_ref, e2_ref, al_ref, n1_ref, x1_ref, gate2_ref, gpost_ref,
                 o_ref, acc_ref, act_0, act_1, y_0, y_1):
    e = pl.program_id(1)
    ts = act_0.shape[0]
    per = ts // PEER_NKEYS
    acts, ys = (act_0, act_1), (y_0, y_1)

    @pl.when(e == 0)
    def _():
        acc_ref[...] = jnp.zeros_like(acc_ref)

    def scores(k):
        acts[k % 2][...] = jnp.dot(u_ref[k], h2t_ref[...], preferred_element_type=F32)

    def mix(k):
        acc_ref[...] += jnp.dot(vt_ref[k], ys[k % 2][...], preferred_element_type=F32)

    scores(0)
    scores(1)
    for k in range(PEER_SUBTILES):
        _peer_weights(acts[k % 2], ys[k % 2], r2_ref, e2_ref, al_ref, n1_ref, k * per)
        if k + 2 < PEER_SUBTILES:
            scores(k + 2)
        if k >= 1:
            mix(k - 1)
    mix(PEER_SUBTILES - 1)

    @pl.when(e == pl.num_programs(1) - 1)
    def _():
        o_ref[...] = x1_ref[...] + gate2_ref[...] * _rms(acc_ref[...].T, gpost_ref[...])


def _peer(grp, h2t, sel, x1, modop, g_post, pu, pvt):
    tm, M = grp.tm, grp.M
    ts = PEER_EXPERT_TILE
    te = ts * PEER_SUBTILES
    r2, e2, al, n1 = sel
    packed = pl.BlockSpec((PEER_HEADS, PEER_NKEYS, tm), lambda i, e: (0, 0, i))
    rows = pl.BlockSpec((PEER_HEADS, te // PEER_NKEYS, tm), lambda i, e: (0, e, i))
    tok = pl.BlockSpec((tm, D_MODEL), lambda i, e: (i, 0))
    return pl.pallas_call(
        _peer_kernel,
        grid=(M // tm, pu.shape[0] // PEER_SUBTILES),
        in_specs=[pl.BlockSpec((D_MODEL, tm), lambda i, e: (0, i)),
                  pl.BlockSpec((PEER_SUBTILES, ts, D_MODEL), lambda i, e: (e, 0, 0)),
                  pl.BlockSpec((PEER_SUBTILES, D_MODEL, ts), lambda i, e: (e, 0, 0)),
                  packed, packed, rows, rows,
                  pl.BlockSpec((tm, D_MODEL), lambda i, e: (i, 0), pipeline_mode=pl.Buffered(1)),
                  grp.mod_spec(5),
                  pl.BlockSpec((1, D_MODEL), lambda i, e: (0, 0))],
        out_specs=tok,
        out_shape=jax.ShapeDtypeStruct((M, D_MODEL), F32),
        scratch_shapes=[pltpu.VMEM((D_MODEL, tm), F32),
                        pltpu.VMEM((ts, tm), F32), pltpu.VMEM((ts, tm), F32),
                        pltpu.VMEM((ts, tm), BF16), pltpu.VMEM((ts, tm), BF16)],
        compiler_params=_params(("arbitrary", "arbitrary")),
        name="peer_dense",
    )(h2t, pu, pvt, r2, e2, al, n1, x1, modop, g_post)


def _prepare_weights(w_in, w_gla_gate, b_gla_gate, g_gla_norm, mu_shift, rw_w0, rw_w2, rw_a0, rw_a2,
                     rw_g2, rw_k_k, rw_k_a, rw_r_k, rw_gn_w, rw_gn_b, w_b_gla, w_b_rw, w_out,
                     peer_wq, peer_keys, peer_u, peer_v, g_pre_mix, g_post_mix, g_pre_ffn, g_post_ffn):
    W = RW_HEADS * RW_HEAD
    rw = w_in[:, 3088:6416]
    w_proj = jnp.concatenate(
        [rw[:, :3 * W], w_in[:, 6416:8464], w_in[:, 2048:3072], w_in[:, 1024:2048], w_in[:, 0:1024],
         rw[:, 3 * W:], w_in[:, 3072:3088], jnp.zeros((D_MODEL, PROJ_PAD_W - 8464), F32)],
        axis=1).astype(BF16)
    head = jnp.arange(W) // RW_HEAD
    zeros64 = jnp.zeros((64, W), F32)
    return dict(
        w_proj=w_proj,
        wg_pad=jnp.concatenate([w_gla_gate, jnp.zeros((128 - GLA_GATE_RANK, 512), F32)], axis=0),
        bg=b_gla_gate.reshape(1, -1),
        g_gla_norm=g_gla_norm.reshape(1, -1),
        mu_rkv=mu_shift[:3 * W].reshape(1, -1),
        mu_lora=mu_shift[3 * W:].reshape(1, -1),
        rw_w0=rw_w0.reshape(1, -1),
        w2p=jnp.concatenate([rw_w2, jnp.zeros((192, W), F32)], axis=0),
        rw_a0=rw_a0.reshape(1, -1),
        a2p=jnp.concatenate([zeros64, rw_a2, jnp.zeros((128, W), F32)], axis=0),
        g2p=jnp.concatenate([jnp.zeros((128, W), F32), rw_g2], axis=0),
        rw_k_k=rw_k_k.reshape(1, -1),
        rw_k_a=rw_k_a.reshape(1, -1),
        rw_r_k=rw_r_k.reshape(1, -1),
        rw_gn_w=rw_gn_w.reshape(1, -1),
        rw_gn_b=rw_gn_b.reshape(1, -1),
        head_ones=(head[:, None] == head[None, :]).astype(BF16),
        w_b_gla=w_b_gla.astype(BF16),
        w_b_rw=w_b_rw.astype(BF16),
        w_out=w_out.astype(BF16),
        peer_wq=peer_wq.astype(BF16),
        peer_keys=peer_keys.reshape(2 * PEER_HEADS, PEER_NKEYS, 128).astype(BF16),
        peer_u=peer_u.astype(BF16).reshape(-1, PEER_EXPERT_TILE, D_MODEL),
        peer_vt=peer_v.reshape(-1, PEER_EXPERT_TILE, D_MODEL).transpose(0, 2, 1).astype(BF16),
        g_pre_mix=g_pre_mix.reshape(1, -1),
        g_post_mix=g_post_mix.reshape(1, -1),
        g_pre_ffn=g_pre_ffn.reshape(1, -1),
        g_post_ffn=g_post_ffn.reshape(1, -1),
    )


def _block(x, mod, s_gla0, s_rw0, shift0, wts):
    B, valid, _ = x.shape
    T = -(-valid // 8) * 8
    x = jnp.pad(x, ((0, 0), (0, T - valid), (0, 0)))
    grp = _Group(B, T, valid)
    W = RW_HEADS * RW_HEAD
    x2d = x.reshape(grp.M, D_MODEL)
    modop = grp.seq_operand(mod)
    p2 = _projection(grp, x2d, modop, wts["g_pre_mix"], wts["w_proj"], BF16)
    p3 = p2.reshape(B, T, PROJ_PAD_W)
    og, s_gla = _gla(grp, p3, wts["wg_pad"], wts["bg"], s_gla0)
    r, k, v, an, bb, lw, gate = _rwkv_prep(grp, p2, shift0, wts)
    seqs = tuple(z.reshape(B, T, W) for z in (r, k, v, an, bb, lw))
    orw, s_rw = _rwkv(grp, seqs, wts["rw_gn_w"], wts["rw_gn_b"], wts["rw_r_k"], s_rw0)
    x1, h2t, qp = _merge(grp, og.reshape(grp.M, -1), p2, orw.reshape(grp.M, W), gate,
                        x2d, modop, wts)
    if valid != T:
        grp = _Group(B, valid, valid)
        modop = grp.seq_operand(mod)
        x1 = x1.reshape(B, T, D_MODEL)[:, :valid].reshape(grp.M, D_MODEL)
        h2t = h2t.reshape(D_MODEL, B, T)[:, :, :valid].reshape(D_MODEL, grp.M)
        qp = qp.reshape(B, T, -1)[:, :valid].reshape(grp.M, -1)
    sel = _peer_select(grp, qp, wts["peer_keys"])
    out = _peer(grp, h2t, sel, x1, modop, wts["g_post_ffn"], wts["peer_u"], wts["peer_vt"])
    ends = _Group(B, 1, 1)
    last = _projection(ends, x[:, valid - 1, :], ends.seq_operand(mod), wts["g_pre_mix"], wts["w_proj"], F32)
    shift_new = jnp.concatenate([last[:, :3 * W], last[:, COL_LORA:COL_LORA + 256]], axis=-1)
    return out.reshape(B, grp.T, D_MODEL)[:, :valid], s_gla, s_rw, shift_new


def kernel(x_prompt, x_sample, c_prompt, c_sample, state_gla, state_rwkv, state_shift, w_ada, b_ada, g_pre_mix, g_post_mix, g_pre_ffn, g_post_ffn, w_in, w_gla_gate, b_gla_gate, g_gla_norm, mu_shift, rw_w0, rw_w2, rw_a0, rw_a2, rw_g2, rw_k_k, rw_k_a, rw_r_k, rw_gn_w, rw_gn_b, w_b_gla, w_b_rw, w_out, peer_wq, peer_keys, peer_u, peer_v):
    depth = w_in.shape[0]
    bp = x_prompt.shape[0]
    xp, xs = x_prompt, x_sample
    outs = [[] for _ in range(6)]
    for l in range(depth):
        wts = _prepare_weights(
            w_in[l], w_gla_gate[l], b_gla_gate[l], g_gla_norm[l], mu_shift[l], rw_w0[l], rw_w2[l],
            rw_a0[l], rw_a2[l], rw_g2[l], rw_k_k[l], rw_k_a[l], rw_r_k[l], rw_gn_w[l], rw_gn_b[l],
            w_b_gla[l], w_b_rw[l], w_out[l], peer_wq[l], peer_keys[l], peer_u[l], peer_v[l],
            g_pre_mix[l], g_post_mix[l], g_pre_ffn[l], g_post_ffn[l])
        mod = _modulation(jnp.concatenate([c_prompt, c_sample], axis=0), w_ada[l], b_ada[l])
        zg = jnp.zeros((bp, GLA_HEADS, GLA_DK, GLA_DV), F32)
        zr = jnp.zeros((bp, RW_HEADS, RW_HEAD, RW_HEAD), F32)
        zs = jnp.zeros((bp, RW_SHIFT_W), F32)
        xp, g1, r1, s1 = _block(xp, mod[:bp], zg, zr, zs, wts)
        xs, g2, r2, s2 = _block(xs, mod[bp:], state_gla[l], state_rwkv[l], state_shift[l], wts)
        for acc, val in zip(outs, (g1, r1, s1, g2, r2, s2)):
            acc.append(val)
    stacked = [jnp.stack(o) for o in outs]
    return (xp, xs, *stacked)
```

```python
import functools

import jax
import jax.numpy as jnp
from jax import lax
from jax.experimental import pallas as pl
from jax.experimental.pallas import tpu as pltpu

F32 = jnp.float32
BF16 = jnp.bfloat16

D_MODEL = 1024
GLA_HEADS = 4
GLA_DK = 128
GLA_DV = 256
GLA_GATE_RANK = 16
GLA_GATE_TAU = 16.0
GLA_NORM_EPS = 1e-5
RW_HEAD = 64
RW_HEADS = 16
RW_PAIRS = RW_HEADS // 2
RW_GN_EPS = 64e-5
RW_SHIFT_W = 3328
PEER_HEADS = 8
PEER_NKEYS = 128
PEER_TOPK = 16
NORM_EPS = 1e-6
LANES = 128

COL_R, COL_K, COL_V = 0, 1024, 2048
COL_MG = 3072
COL_GG = 5120
COL_GV = 6144
COL_GQ = 7168
COL_GK = 7680
COL_LORA = 8192
COL_GA = 8448
PROJ_PAD_W = 8704
PROJ_TILE_N = 4352
PROJ_TILE_M = 1024

CHUNK = 64
TOKEN_TILE = 512
PEER_EXPERT_TILE = 512
PEER_SUBTILES = 4
VMEM_LIMIT = 56 * 1024 * 1024

NEG_INF = float("-inf")


def _mm(a, b):
    return jnp.dot(a.astype(BF16), b.astype(BF16), preferred_element_type=F32)


def _mm_nt(a, b):
    return lax.dot_general(a.astype(BF16), b.astype(BF16), (((1,), (1,)), ((), ())),
                           preferred_element_type=F32)


def _mm_tn(a, b):
    return lax.dot_general(a.astype(BF16), b.astype(BF16), (((0,), (0,)), ((), ())),
                           preferred_element_type=F32)


def _split(x):
    hi = x.astype(BF16)
    lo = (x - hi.astype(F32)).astype(BF16)
    return hi, lo


def _mm_exact_lhs(a, x):
    hi, lo = _split(x)
    a = a.astype(BF16)
    return (jnp.dot(a, hi, preferred_element_type=F32)
            + jnp.dot(a, lo, preferred_element_type=F32))


def _mm_exact_rhs(x, a):
    hi, lo = _split(x)
    a = a.astype(BF16)
    return (jnp.dot(hi, a, preferred_element_type=F32)
            + jnp.dot(lo, a, preferred_element_type=F32))


def _softplus(x):
    return jnp.maximum(x, 0.0) + jnp.log1p(jnp.exp(-jnp.abs(x)))


def _rms(x, g):
    return x * lax.rsqrt(jnp.mean(x * x, axis=-1, keepdims=True) + NORM_EPS) * g


def _params(sem, flags=None):
    return pltpu.CompilerParams(dimension_semantics=sem, vmem_limit_bytes=VMEM_LIMIT, flags=flags)


def _mod_kernel(c_ref, w_ref, b_ref, o_ref):
    c = c_ref[...]
    o_ref[...] = _mm(c * jax.nn.sigmoid(c), w_ref[...]) + b_ref[...]


def _modulation(c, w_ada, b_ada):
    rows = c.shape[0]
    n = w_ada.shape[1]
    tn = 1536
    return pl.pallas_call(
        _mod_kernel,
        grid=(n // tn,),
        in_specs=[pl.BlockSpec((rows, D_MODEL), lambda j: (0, 0)),
                  pl.BlockSpec((D_MODEL, tn), lambda j: (0, j)),
                  pl.BlockSpec((1, tn), lambda j: (0, j))],
        out_specs=pl.BlockSpec((rows, tn), lambda j: (0, j)),
        out_shape=jax.ShapeDtypeStruct((rows, n), F32),
        compiler_params=_params(("arbitrary",)),
        name="adaln_mod",
    )(c, w_ada, b_ada.reshape(1, n))


class _Group:
    def __init__(self, batch, seq, valid):
        self.B, self.T, self.Tv = batch, seq, valid
        self.M = batch * seq
        self.tm = min(TOKEN_TILE, self.M)
        self.per_batch_mod = seq % self.tm == 0

    def seq_operand(self, arr):
        if self.per_batch_mod:
            return arr.reshape(self.B, 1, arr.shape[-1])
        return jnp.repeat(arr, self.T, axis=0)

    def seq_spec(self, width, col, tm=None, tile_index=None):
        tm, T = tm or self.tm, self.T
        tile = tile_index or (lambda i, *_: i)
        if self.per_batch_mod:
            return pl.BlockSpec((None, 1, width), lambda *g: ((tile(*g) * tm) // T, 0, col))
        return pl.BlockSpec((tm, width), lambda *g: (tile(*g), col))

    def mod_spec(self, col, tm=None, tile_index=None):
        return self.seq_spec(D_MODEL, col, tm, tile_index)


def _proj_kernel(x_ref, shift_ref, scale_ref, g_ref, w_ref, p_ref, h_scr):
    @pl.when(pl.program_id(1) == 0)
    def _():
        h = _rms(x_ref[...], g_ref[...]) * (1.0 + scale_ref[...]) + shift_ref[...]
        h_scr[...] = h.astype(BF16)

    p_ref[...] = jnp.dot(h_scr[...], w_ref[...], preferred_element_type=F32).astype(p_ref.dtype)


def _projection(grp, x2d, modop, g_pre, w_proj, dtype):
    tm = PROJ_TILE_M if grp.per_batch_mod and grp.T % PROJ_TILE_M == 0 else grp.tm
    return pl.pallas_call(
        _proj_kernel,
        grid=(grp.M // tm, PROJ_PAD_W // PROJ_TILE_N),
        in_specs=[pl.BlockSpec((tm, D_MODEL), lambda i, j: (i, 0)),
                  grp.mod_spec(0, tm), grp.mod_spec(1, tm),
                  pl.BlockSpec((1, D_MODEL), lambda i, j: (0, 0)),
                  pl.BlockSpec((D_MODEL, PROJ_TILE_N), lambda i, j: (0, j))],
        out_specs=pl.BlockSpec((tm, PROJ_TILE_N), lambda i, j: (i, j)),
        out_shape=jax.ShapeDtypeStruct((grp.M, PROJ_PAD_W), dtype),
        scratch_shapes=[pltpu.VMEM((tm, D_MODEL), BF16)],
        compiler_params=_params(("arbitrary", "arbitrary")),
        name="in_proj",
    )(x2d, modop, modop, g_pre, w_proj)


def _gla_kernel(q_ref, k_ref, v_ref, ga_ref, wg_ref, bg_ref, s0_ref, og_ref, sout_ref, st_scr,
                *, chunk, n_chunks, tile, valid, seqs):
    t = pl.program_id(1)
    last = pl.num_programs(1) - 1
    row = lax.broadcasted_iota(jnp.int32, (tile, tile), 0)
    col = lax.broadcasted_iota(jnp.int32, (tile, tile), 1)
    shift = chunk.bit_length() - 1
    causal = (lax.shift_right_logical(row, shift) == lax.shift_right_logical(col, shift)) & (row >= col)
    tril = jnp.where(causal, 1.0, 0.0).astype(BF16)

    batch, heads, chunks = range(seqs), range(GLA_HEADS), range(n_chunks)
    pairs = [(b, h) for b in batch for h in heads]
    ks = [slice(h * GLA_DK, (h + 1) * GLA_DK) for h in heads]
    vs = [slice(h * GLA_DV, (h + 1) * GLA_DV) for h in heads]
    rows = [slice(c * chunk, (c + 1) * chunk) for c in chunks]
    ends = [slice((c + 1) * chunk - 1, (c + 1) * chunk) for c in chunks]

    @pl.when(t == 0)
    def _():
        for b, h in pairs:
            st_scr[b, h] = s0_ref[b, h].T

    z = [_mm(ga_ref[b], wg_ref[...]) + bg_ref[...] for b in batch]
    la = [-_softplus(-z[b]) / GLA_GATE_TAU for b in batch]
    k = [k_ref[b].astype(F32) for b in batch]
    v = {(b, h): v_ref[b, :, vs[h]] for b, h in pairs}
    if valid is not None:
        ok = (t * tile + lax.broadcasted_iota(jnp.int32, (tile, 1), 0)) < valid
        la = [jnp.where(ok, x, 0.0) for x in la]
        k = [jnp.where(ok, x, 0.0) for x in k]
        v = {key: jnp.where(ok, x, 0.0) for key, x in v.items()}
    cum = [_mm_exact_lhs(tril, la[b]) for b in batch]
    end = [jnp.concatenate([jnp.broadcast_to(cum[b][ends[c], :], (chunk, cum[b].shape[1])) for c in chunks], axis=0)
           for b in batch]
    qe = [q_ref[b].astype(F32) * (GLA_DK ** -0.5) * jnp.exp(cum[b]) for b in batch]
    ke = [k[b] * jnp.exp(-cum[b]) for b in batch]
    kd = [k[b] * jnp.exp(end[b] - cum[b]) for b in batch]
    decay = [[jnp.exp(cum[b][ends[c], :]) for c in chunks] for b in batch]
    a = {(b, h): jnp.where(causal, _mm_nt(qe[b][:, ks[h]], ke[b][:, ks[h]]), 0.0) for b, h in pairs}
    inside = {key: _mm(a[key], v[key]) for key in pairs}
    update = {(b, h, c): _mm_tn(v[b, h][rows[c], :], kd[b][rows[c], ks[h]]) for b, h in pairs for c in chunks}

    st = {(b, h, 0): st_scr[b, h] for b, h in pairs}
    for c in chunks:
        for b, h in pairs:
            st[b, h, c + 1] = st[b, h, c] * decay[b][c][:, ks[h]] + update[b, h, c]
    outer = {(b, h, c): _mm_nt(qe[b][rows[c], ks[h]], st[b, h, c]) for b, h in pairs for c in chunks}
    for b, h in pairs:
        st_scr[b, h] = st[b, h, n_chunks]
        for c in chunks:
            o = outer[b, h, c] + inside[b, h][rows[c], :]
            og = o * lax.rsqrt(jnp.mean(o * o, axis=-1, keepdims=True) + GLA_NORM_EPS)
            og_ref[b, rows[c], vs[h]] = og.astype(og_ref.dtype)

    @pl.when(t == last)
    def _():
        for b, h in pairs:
            sout_ref[b, h] = st_scr[b, h].T


def _seqs_per_step(batch, seq):
    n = max(1, min(8, 64 // seq))
    while batch % n:
        n //= 2
    return n


def _gla(grp, p3, wg_pad, bg, s0):
    B, T = grp.B, grp.T
    tile = min(T, 4 * CHUNK)
    chunk = min(CHUNK, tile)
    nb = _seqs_per_step(B, T)
    kern = functools.partial(_gla_kernel, chunk=chunk, n_chunks=tile // chunk, tile=tile,
                             valid=None if grp.Tv == T else grp.Tv, seqs=nb)
    kw, vw = GLA_HEADS * GLA_DK, GLA_HEADS * GLA_DV
    st_spec = pl.BlockSpec((nb, GLA_HEADS, GLA_DK, GLA_DV), lambda i, t: (i, 0, 0, 0))
    return pl.pallas_call(
        kern,
        grid=(B // nb, T // tile),
        in_specs=[pl.BlockSpec((nb, tile, kw), lambda i, t: (i, t, COL_GQ // kw)),
                  pl.BlockSpec((nb, tile, kw), lambda i, t: (i, t, COL_GK // kw)),
                  pl.BlockSpec((nb, tile, vw), lambda i, t: (i, t, COL_GV // vw)),
                  pl.BlockSpec((nb, tile, LANES), lambda i, t: (i, t, COL_GA // LANES)),
                  pl.BlockSpec((LANES, kw), lambda i, t: (0, 0)),
                  pl.BlockSpec((1, kw), lambda i, t: (0, 0)),
                  st_spec],
        out_specs=[pl.BlockSpec((nb, tile, vw), lambda i, t: (i, t, 0)), st_spec],
        out_shape=[jax.ShapeDtypeStruct((B, T, vw), BF16),
                   jax.ShapeDtypeStruct((B, GLA_HEADS, GLA_DK, GLA_DV), F32)],
        scratch_shapes=[pltpu.VMEM((nb, GLA_HEADS, GLA_DV, GLA_DK), F32)],
        compiler_params=_params(("arbitrary", "arbitrary")),
        name="gla_scan",
    )(p3, p3, p3, p3, wg_pad, bg, s0)


def _rwprep_kernel(r_ref, k_ref, v_ref, l_ref, pr_ref, pk_ref, pv_ref, pl_ref,
                   s0r_ref, s0k_ref, s0v_ref, s0l_ref, mu_ref, mul_ref, w0_ref, w2_ref, a0_ref, a2_ref, g2_ref,
                   kk_ref, ka_ref, bd_ref,
                   ro_ref, ko_ref, vo_ref, an_ref, bb_ref, lw_ref, gate_ref,
                   *, tile, seq, valid, prev_rows):
    i = pl.program_id(0)
    local = lax.broadcasted_iota(jnp.int32, (tile, 1), 0)
    pos = lax.rem(i * tile + local, seq)
    first = pos == 0
    top = local == 0

    def shifted(cur_ref, prev_ref, s0_ref, mu):
        cur = cur_ref[...].astype(F32)
        before = prev_ref[prev_rows - 1:prev_rows, :].astype(F32)
        prev = jnp.where(first, s0_ref[...], jnp.where(top, before, pltpu.roll(cur, 1, 0)))
        return cur + (prev - cur) * mu

    r = shifted(r_ref, pr_ref, s0r_ref, mu_ref[:, 0:1024])
    k = shifted(k_ref, pk_ref, s0k_ref, mu_ref[:, 1024:2048])
    v = shifted(v_ref, pv_ref, s0v_ref, mu_ref[:, 2048:3072])
    lo = shifted(l_ref, pl_ref, s0l_ref, mul_ref[...])

    wpre = w0_ref[...] + _mm(jnp.tanh(lo), w2_ref[...])
    wlog = -_softplus(-wpre) - 0.5
    lw = -jnp.exp(wlog)
    a = jax.nn.sigmoid(a0_ref[...] + _mm(lo, a2_ref[...]))
    gate = _mm(jax.nn.sigmoid(lo), g2_ref[...])

    kk = k * kk_ref[...]
    ss = _mm(kk * kk, bd_ref[...])
    kk = kk / jnp.maximum(jnp.sqrt(ss), 1e-12)
    kmod = k * (1.0 + (a - 1.0) * ka_ref[...])
    an = -kk
    bb = kk * a
    if valid is not None:
        ok = pos < valid
        lw = jnp.where(ok, lw, 0.0)
        an = jnp.where(ok, an, 0.0)
        bb = jnp.where(ok, bb, 0.0)
        kmod = jnp.where(ok, kmod, 0.0)
        v = jnp.where(ok, v, 0.0)
    ro_ref[...] = r.astype(ro_ref.dtype)
    ko_ref[...] = kmod.astype(ko_ref.dtype)
    vo_ref[...] = v.astype(vo_ref.dtype)
    an_ref[...] = an.astype(an_ref.dtype)
    bb_ref[...] = bb.astype(bb_ref.dtype)
    lw_ref[...] = lw
    gate_ref[...] = gate.astype(gate_ref.dtype)


def _rwkv_prep(grp, p2, shift0, wts):
    M, T = grp.M, grp.T
    tile = min(M, 512)
    prev_rows = min(16, tile)
    W = RW_HEADS * RW_HEAD
    s0 = grp.seq_operand(shift0)

    def cur(width, blk):
        return pl.BlockSpec((tile, width), lambda i: (i, blk))

    def prev(width, blk):
        return pl.BlockSpec((prev_rows, width), lambda i: (jnp.maximum(i * (tile // prev_rows) - 1, 0), blk))

    def full(shape):
        return pl.BlockSpec(shape, lambda i: (0,) * len(shape))

    out_spec = pl.BlockSpec((tile, W), lambda i: (i, 0))
    kern = functools.partial(_rwprep_kernel, tile=tile, seq=T, valid=None if grp.Tv == T else grp.Tv,
                             prev_rows=prev_rows)
    lora_blk = COL_LORA // 256
    return pl.pallas_call(
        kern,
        grid=(M // tile,),
        in_specs=[cur(W, 0), cur(W, 1), cur(W, 2), cur(256, lora_blk),
                  prev(W, 0), prev(W, 1), prev(W, 2), prev(256, lora_blk),
                  grp.seq_spec(W, 0, tile), grp.seq_spec(W, 1, tile), grp.seq_spec(W, 2, tile),
                  grp.seq_spec(256, 3 * W // 256, tile),
                  full((1, 3 * W)), full((1, 256)),
                  full((1, W)), full((256, W)), full((1, W)), full((256, W)), full((256, W)),
                  full((1, W)), full((1, W)), full((W, W))],
        out_specs=[out_spec] * 7,
        out_shape=[jax.ShapeDtypeStruct((M, W), BF16)] * 5
        + [jax.ShapeDtypeStruct((M, W), F32), jax.ShapeDtypeStruct((M, W), BF16)],
        compiler_params=_params(("arbitrary",)),
        name="rwkv_prep",
    )(p2, p2, p2, p2, p2, p2, p2, p2, s0, s0, s0, s0,
      wts["mu_rkv"], wts["mu_lora"], wts["rw_w0"], wts["w2p"], wts["rw_a0"], wts["a2p"], wts["g2p"],
      wts["rw_k_k"], wts["rw_k_a"], wts["head_ones"])


def _rwkv_pairs_chunk(at, rt, bt, kt, v, states, *, chunk, same_head, strict, incl):
    n2 = 2 * chunk
    wide = n2 % LANES == 0
    pairs = range(len(states))

    def blocks(x):
        return jnp.where(same_head, jnp.concatenate([x, x], axis=0), 0.0).astype(BF16)

    a2 = [blocks(x) for x in at]
    r2 = [blocks(x) for x in rt]
    b2 = [blocks(x) for x in bt]
    k2 = [blocks(x) for x in kt]
    v2 = [blocks(x) for x in v]
    ar = [jnp.concatenate([a2[p], r2[p]], axis=0) for p in pairs]
    bk = [jnp.concatenate([b2[p], k2[p]], axis=0) for p in pairs]
    if wide:
        g = [_mm_nt(ar[p], bk[p]) for p in pairs]
        l_ab = [jnp.where(strict, g[p][:n2, :n2], 0.0) for p in pairs]
        l_ak = [jnp.where(strict, g[p][:n2, n2:], 0.0) for p in pairs]
        m_rb = [jnp.where(incl, g[p][n2:, :n2], 0.0) for p in pairs]
        m_rk = [jnp.where(incl, g[p][n2:, n2:], 0.0) for p in pairs]
    else:
        l_ab = [jnp.where(strict, _mm_nt(a2[p], b2[p]), 0.0) for p in pairs]
        l_ak = [jnp.where(strict, _mm_nt(a2[p], k2[p]), 0.0) for p in pairs]
        m_rb = [jnp.where(incl, _mm_nt(r2[p], b2[p]), 0.0) for p in pairs]
        m_rk = [jnp.where(incl, _mm_nt(r2[p], k2[p]), 0.0) for p in pairs]
    from_state = [_mm_nt(ar[p], states[p]) for p in pairs]
    x = [from_state[p][:n2] + _mm(l_ak[p], v2[p]) for p in pairs]
    pw = l_ab
    for _ in range(chunk.bit_length() - 2):
        if wide:
            y = [_mm(pw[p], jnp.concatenate([x[p], pw[p]], axis=1)) for p in pairs]
            x = [x[p] + y[p][:, :LANES] for p in pairs]
            pw = [y[p][:, LANES:] for p in pairs]
        else:
            x = [x[p] + _mm(pw[p], x[p]) for p in pairs]
            pw = [_mm(pw[p], pw[p]) for p in pairs]
    u = [x[p] + _mm(pw[p], x[p]) for p in pairs]
    uv = [jnp.concatenate([u[p].astype(BF16), v2[p]], axis=0) for p in pairs]
    if wide:
        o2 = [from_state[p][n2:] + _mm(jnp.concatenate([m_rb[p], m_rk[p]], axis=1), uv[p]) for p in pairs]
    else:
        o2 = [from_state[p][n2:] + _mm(m_rb[p], u[p]) + _mm(m_rk[p], v2[p]) for p in pairs]
    s_new = [states[p] + _mm_tn(uv[p], bk[p]) for p in pairs]
    return o2, s_new


def _rwkv_kernel(r_ref, k_ref, v_ref, an_ref, bb_ref, lw_ref, gw_ref, gb_ref, rk_ref, s0_ref,
                 o_ref, sout_ref, s_scr, *, chunk, seqs):
    t = pl.program_id(1)
    zeros = jnp.zeros((RW_HEAD, RW_HEAD), F32)

    @pl.when(t == 0)
    def _():
        for q in range(seqs):
            for p in range(RW_PAIRS):
                top = jnp.concatenate([s0_ref[q, 2 * p], zeros], axis=1)
                bot = jnp.concatenate([zeros, s0_ref[q, 2 * p + 1]], axis=1)
                s_scr[q, p] = jnp.concatenate([top, bot], axis=0)

    n2 = 2 * chunk
    row = lax.broadcasted_iota(jnp.int32, (n2, n2), 0)
    col = lax.broadcasted_iota(jnp.int32, (n2, n2), 1)

    def second(idx, size):
        return jnp.where(idx >= size, 1, 0)

    same = second(row, chunk) == second(col, chunk)
    strict = same & (row > col)
    incl = same & (row >= col)
    same_head = (second(lax.broadcasted_iota(jnp.int32, (n2, LANES), 0), chunk)
                 == second(lax.broadcasted_iota(jnp.int32, (n2, LANES), 1), RW_HEAD))
    trow = lax.broadcasted_iota(jnp.int32, (chunk, chunk), 0)
    tcol = lax.broadcasted_iota(jnp.int32, (chunk, chunk), 1)
    tril = jnp.where(trow >= tcol, 1.0, 0.0).astype(BF16)
    first_head = lax.broadcasted_iota(jnp.int32, (chunk, LANES), 1) < RW_HEAD

    def head_sum(x):
        lo = jnp.sum(jnp.where(first_head, x, 0.0), axis=-1, keepdims=True)
        hi = jnp.sum(jnp.where(first_head, 0.0, x), axis=-1, keepdims=True)
        return jnp.where(first_head, lo, hi)

    lanes = [slice(p * LANES, (p + 1) * LANES) for p in range(RW_PAIRS)]
    problems = [(q, p) for q in range(seqs) for p in range(RW_PAIRS)]
    rs, ks, vs, ats, rts, bts, kts, ends = [], [], [], [], [], [], [], []
    for q in range(seqs):
        lw = lw_ref[q]
        cum = _mm_exact_lhs(tril, lw)
        w_incl = jnp.exp(cum)
        w_inv = jnp.exp(-cum)
        r, k = r_ref[q].astype(F32), k_ref[q].astype(F32)
        rs.append(r)
        ks.append(k)
        vs.append(v_ref[q].astype(F32))
        ats.append(an_ref[q].astype(F32) * jnp.exp(cum - lw))
        rts.append(r * w_incl)
        bts.append(bb_ref[q].astype(F32) * w_inv)
        kts.append(k * w_inv)
        ends.append(w_incl[chunk - 1:chunk, :])

    def slabs(xs):
        return [xs[q][:, lanes[p]] for q, p in problems]

    o2, s_new = _rwkv_pairs_chunk(
        slabs(ats), slabs(rts), slabs(bts), slabs(kts), slabs(vs), [s_scr[q, p] for q, p in problems],
        chunk=chunk, same_head=same_head, strict=strict, incl=incl)
    for n, (q, p) in enumerate(problems):
        ls = lanes[p]
        s_scr[q, p] = s_new[n] * ends[q][:, ls]
        o = o2[n][:chunk] + o2[n][chunk:]
        mu = head_sum(o) * (1.0 / RW_HEAD)
        oc = o - mu
        var = head_sum(oc * oc) * (1.0 / RW_HEAD)
        on = oc * lax.rsqrt(var + RW_GN_EPS) * gw_ref[:, ls] + gb_ref[:, ls]
        bonus = head_sum(rs[q][:, ls] * ks[q][:, ls] * rk_ref[:, ls]) * vs[q][:, ls]
        o_ref[q, :, ls] = (on + bonus).astype(o_ref.dtype)

    @pl.when(t == pl.num_programs(1) - 1)
    def _():
        for q in range(seqs):
            for p in range(RW_PAIRS):
                s = s_scr[q, p]
                sout_ref[q, 2 * p] = s[:RW_HEAD, :RW_HEAD]
                sout_ref[q, 2 * p + 1] = s[RW_HEAD:, RW_HEAD:]


def _rwkv(grp, seqs, gn_w, gn_b, r_k, s0):
    B, T = grp.B, grp.T
    chunk = min(T, CHUNK)
    nb = max(_seqs_per_step(B, T), 4 if B % 4 == 0 else 2 - B % 2)
    W = RW_HEADS * RW_HEAD
    seq_spec = pl.BlockSpec((nb, chunk, W), lambda b, t: (b, t, 0))
    vec_spec = pl.BlockSpec((1, W), lambda b, t: (0, 0))
    st_spec = pl.BlockSpec((nb, RW_HEADS, RW_HEAD, RW_HEAD), lambda b, t: (b, 0, 0, 0))
    kern = functools.partial(_rwkv_kernel, chunk=chunk, seqs=nb)
    return pl.pallas_call(
        kern,
        grid=(B // nb, T // chunk),
        in_specs=[seq_spec] * 6 + [vec_spec] * 3 + [st_spec],
        out_specs=[seq_spec, st_spec],
        out_shape=[jax.ShapeDtypeStruct((B, T, W), BF16),
                   jax.ShapeDtypeStruct((B, RW_HEADS, RW_HEAD, RW_HEAD), F32)],
        scratch_shapes=[pltpu.VMEM((nb, RW_PAIRS, 2 * RW_HEAD, 2 * RW_HEAD), F32)],
        compiler_params=_params(("arbitrary", "arbitrary")),
        name="rwkv_scan",
    )(*seqs, gn_w, gn_b, r_k, s0)


def _merge_kernel(og_ref, gg_ref, orw_ref, gate_ref, mga_ref, mgb_ref, x_ref,
                  gate1_ref, shift2_ref, scale2_ref,
                  ggn_ref, gpost_ref, gpre_ref, wbg_ref, wbr_ref, wout_ref, wq_ref,
                  x1_ref, h2t_ref, qp_ref):
    gg = gg_ref[...].astype(F32)
    gla_in = og_ref[...].astype(F32) * ggn_ref[...] * (gg * jax.nn.sigmoid(gg))
    br_gla = _mm(gla_in, wbg_ref[...])
    br_rw = _mm(orw_ref[...].astype(F32) * gate_ref[...].astype(F32), wbr_ref[...])
    merged = (jax.nn.sigmoid(mga_ref[...].astype(F32)) * br_gla
              + jax.nn.sigmoid(mgb_ref[...].astype(F32)) * br_rw)
    y = _mm(merged, wout_ref[...])
    x1 = x_ref[...] + gate1_ref[...] * _rms(y, gpost_ref[...])
    x1_ref[...] = x1
    h2 = _rms(x1, gpre_ref[...]) * (1.0 + scale2_ref[...]) + shift2_ref[...]
    h2t_ref[...] = h2.T.astype(BF16)
    qp_ref[...] = jnp.dot(h2.astype(BF16), wq_ref[...], preferred_element_type=F32).astype(BF16)


def _merge(grp, og, p2, orw, gate_rw, x2d, modop, wts):
    tm, M = grp.tm, grp.M
    W = D_MODEL

    def tok(blk):
        return pl.BlockSpec((tm, W), lambda i: (i, blk))

    def full(shape):
        return pl.BlockSpec(shape, lambda i: (0,) * len(shape), pipeline_mode=pl.Buffered(1))

    return pl.pallas_call(
        _merge_kernel,
        grid=(M // tm,),
        in_specs=[tok(0), tok(COL_GG // W), tok(0), tok(0), tok(COL_MG // W), tok(COL_MG // W + 1), tok(0),
                  grp.mod_spec(2, tm), grp.mod_spec(3, tm), grp.mod_spec(4, tm),
                  full((1, W)), full((1, W)), full((1, W)),
                  full((W, W)), full((W, W)), full((W, W)), full((W, 2 * W))],
        out_specs=[tok(0), pl.BlockSpec((W, tm), lambda i: (0, i)), pl.BlockSpec((tm, 2 * W), lambda i: (i, 0))],
        out_shape=[jax.ShapeDtypeStruct((M, W), F32), jax.ShapeDtypeStruct((W, M), BF16),
                   jax.ShapeDtypeStruct((M, 2 * W), BF16)],
        compiler_params=_params(("arbitrary",)),
        name="merge",
    )(og, p2, orw, gate_rw, p2, p2, x2d, modop, modop, modop,
      wts["g_gla_norm"], wts["g_post_mix"], wts["g_pre_ffn"],
      wts["w_b_gla"], wts["w_b_rw"], wts["w_out"], wts["peer_wq"])


def _pair_bits(x):
    bits = pltpu.bitcast(x.astype(BF16).astype(F32), jnp.uint32)
    return bits | (bits >> 16)


def _row_tile(row, rows):
    tile = pltpu.bitcast(jnp.broadcast_to(row, (8, row.shape[1])), BF16)
    return jnp.tile(tile, (rows // 16, 1))


def _batcher_network(n):
    def merge(lo, hi, r):
        step = r * 2
        if step < hi - lo:
            yield from merge(lo, hi, step)
            yield from merge(lo + r, hi, step)
            yield from [(i, i + r) for i in range(lo + r, hi - r, step)]
        else:
            yield (lo, lo + r)

    def sort(lo, hi):
        if hi - lo >= 1:
            mid = lo + (hi - lo) // 2
            yield from sort(lo, mid)
            yield from sort(mid + 1, hi)
            yield from merge(lo, hi, 1)

    return list(sort(0, n - 1))


def _top_values(s, n):
    sub = 8
    depth = s.shape[0] // sub
    cols = [s[i * sub:(i + 1) * sub, :] for i in range(depth)]
    for i, j in _batcher_network(depth):
        cols[i], cols[j] = jnp.maximum(cols[i], cols[j]), jnp.minimum(cols[i], cols[j])
    rows = lax.broadcasted_iota(jnp.int32, (n, s.shape[1]), 0)
    vals = jnp.zeros((n, s.shape[1]), F32)
    for it in range(n):
        m = jnp.max(cols[0], axis=0, keepdims=True)
        vals = jnp.where(rows == it, m, vals)
        hit = cols[0] == m
        for i in range(min(depth - 1, n - 1 - it)):
            cols[i] = jnp.where(hit, cols[i + 1], cols[i])
    return vals


def _count_prefix(vals, pred):
    def row(i):
        return vals[i:i + 1, :]

    c8 = pred(row(7))
    c4 = pred(jnp.where(c8, row(11), row(3)))
    c2 = pred(jnp.where(c8, jnp.where(c4, row(13), row(9)), jnp.where(c4, row(5), row(1))))
    upper = jnp.where(c4, jnp.where(c2, row(14), row(12)), jnp.where(c2, row(10), row(8)))
    lower = jnp.where(c4, jnp.where(c2, row(6), row(4)), jnp.where(c2, row(2), row(0)))
    c1 = pred(jnp.where(c8, upper, lower))
    c16 = pred(row(15))
    return (jnp.where(c8, 8.0, 0.0) + jnp.where(c4, 4.0, 0.0) + jnp.where(c2, 2.0, 0.0)
            + jnp.where(c1, 1.0, 0.0) + jnp.where(c16, 1.0, 0.0))


def _rank_among(s, vals):
    return _count_prefix(vals, lambda x: x > s)


def _candidate_groups(v1, v2):
    n, tm = v1.shape
    split = 4
    r_all = lax.broadcasted_iota(jnp.int32, (n, tm), 0)
    r_half = lax.broadcasted_iota(jnp.int32, (n // 2, tm), 0)
    groups = [v1[0:1, :] + v2]
    for a in range(1, split):
        groups.append(jnp.where(r_half < n // (a + 1), v1[a:a + 1, :] + v2[:n // 2, :], NEG_INF))
    groups.append(jnp.where(r_all >= split, v1 + v2[0:1, :], NEG_INF))
    for b in range(1, n // (split + 1)):
        keep = (r_half >= split) & (r_half < n // (b + 1))
        groups.append(jnp.where(keep, v1[:n // 2, :] + v2[b:b + 1, :], NEG_INF))
    return groups


def _group_max(groups, n):
    full = [g for g in groups if g.shape[0] == n]
    best = full[0]
    for g in full[1:]:
        best = jnp.maximum(best, g)
    best = jnp.maximum(best[:n // 2, :], best[n // 2:, :])
    for g in groups:
        if g.shape[0] != n:
            best = jnp.maximum(best, g)
    return jnp.max(best, axis=0, keepdims=True)


def _select_kernel(qp_ref, keys_ref, r2_ref, e2_ref, al_ref, n1_ref):
    tm = qp_ref.shape[0]
    for h in range(PEER_HEADS):
        q1 = qp_ref[:, (2 * h) * 128:(2 * h + 1) * 128]
        q2 = qp_ref[:, (2 * h + 1) * 128:(2 * h + 2) * 128]
        s1 = _mm_nt(keys_ref[2 * h], q1)
        s2 = _mm_nt(keys_ref[2 * h + 1], q2)
        v1 = _top_values(s1, PEER_TOPK)
        v2 = _top_values(s2, PEER_TOPK)
        rank2 = _rank_among(s2, v2)
        groups = _candidate_groups(v1, v2)
        cmax = v1[0:1, :] + v2[0:1, :]
        z = jnp.zeros((1, tm), F32)
        m = cmax
        for _ in range(PEER_TOPK):
            m = _group_max(groups, PEER_TOPK)
            z = z + jnp.exp(m - cmax)
            groups = [jnp.where(g == m, NEG_INF, g) for g in groups]
        th = m - s1
        n1 = _count_prefix(v2, lambda x: x >= th)
        r2_ref[h] = rank2.astype(BF16)
        e2_ref[h] = jnp.exp(s2 - v2[0:1, :]).astype(BF16)
        al_ref[h] = _pair_bits(jnp.exp(s1 - v1[0:1, :]) / z)
        n1_ref[h] = _pair_bits(n1)


def _peer_select(grp, qp, keys):
    tm, M = grp.tm, grp.M
    rows = pl.BlockSpec((PEER_HEADS, PEER_NKEYS, tm), lambda i: (0, 0, i))
    return pl.pallas_call(
        _select_kernel,
        grid=(M // tm,),
        in_specs=[pl.BlockSpec((tm, 2 * D_MODEL), lambda i: (i, 0)),
                  pl.BlockSpec((2 * PEER_HEADS, PEER_NKEYS, 128), lambda i: (0, 0, 0))],
        out_specs=[rows] * 4,
        out_shape=[jax.ShapeDtypeStruct((PEER_HEADS, PEER_NKEYS, M), BF16)] * 2
        + [jax.ShapeDtypeStruct((PEER_HEADS, PEER_NKEYS, M), jnp.uint32)] * 2,
        compiler_params=_params(("arbitrary",)),
        name="peer_select",
    )(qp, keys)


def _peer_weights(act_ref, y_ref, r2_ref, e2_ref, al_ref, n1_ref, row0):
    tm = act_ref.shape[1]
    half = PEER_NKEYS // 2
    for j in range(act_ref.shape[0] // PEER_NKEYS):
        i1 = slice(row0 + j, row0 + j + 1)
        w = [jnp.zeros((half, tm), BF16) for _ in range(2)]
        for h in range(PEER_HEADS):
            n1 = _row_tile(n1_ref[h, i1, :], half)
            al = _row_tile(al_ref[h, i1, :], half)
            for r in range(2):
                keys = slice(r * half, (r + 1) * half)
                w[r] = w[r] + jnp.where(r2_ref[h, keys, :] < n1, e2_ref[h, keys, :], 0.0) * al
        for r in range(2):
            rows = slice(j * PEER_NKEYS + r * half, j * PEER_NKEYS + (r + 1) * half)
            a = act_ref[rows, :].astype(BF16)
            y_ref[rows, :] = (0.5 * a * (1.0 + lax.erf(a * 0.7071067811865476))) * w[r]


def _peer_kernel(h2t_ref, u_ref, vt_ref, r2_ref, e2_ref, al_ref, n1_ref, x1_ref, gate2_ref, gpost_ref,
                 o_ref, acc_ref, act_0, act_1, y_0, y_1):
    e = pl.program_id(1)
    ts = act_0.shape[0]
    per = ts // PEER_NKEYS
    acts, ys = (act_0, act_1), (y_0, y_1)

    @pl.when(e == 0)
    def _():
        acc_ref[...] = jnp.zeros_like(acc_ref)

    def scores(k):
        acts[k % 2][...] = jnp.dot(u_ref[k], h2t_ref[...], preferred_element_type=F32)

    def mix(k):
        acc_ref[...] += jnp.dot(vt_ref[k], ys[k % 2][...], preferred_element_type=F32)

    scores(0)
    scores(1)
    for k in range(PEER_SUBTILES):
        _peer_weights(acts[k % 2], ys[k % 2], r2_ref, e2_ref, al_ref, n1_ref, k * per)
        if k + 2 < PEER_SUBTILES:
            scores(k + 2)
        if k >= 1:
            mix(k - 1)
    mix(PEER_SUBTILES - 1)

    @pl.when(e == pl.num_programs(1) - 1)
    def _():
        o_ref[...] = x1_ref[...] + gate2_ref[...] * _rms(acc_ref[...].T, gpost_ref[...])


def _peer(grp, h2t, sel, x1, modop, g_post, pu, pvt):
    tm, M = grp.tm, grp.M
    ts = PEER_EXPERT_TILE
    te = ts * PEER_SUBTILES
    r2, e2, al, n1 = sel
    packed = pl.BlockSpec((PEER_HEADS, PEER_NKEYS, tm), lambda i, e: (0, 0, i))
    rows = pl.BlockSpec((PEER_HEADS, te // PEER_NKEYS, tm), lambda i, e: (0, e, i))
    tok = pl.BlockSpec((tm, D_MODEL), lambda i, e: (i, 0))
    return pl.pallas_call(
        _peer_kernel,
        grid=(M // tm, pu.shape[0] // PEER_SUBTILES),
        in_specs=[pl.BlockSpec((D_MODEL, tm), lambda i, e: (0, i)),
                  pl.BlockSpec((PEER_SUBTILES, ts, D_MODEL), lambda i, e: (e, 0, 0)),
                  pl.BlockSpec((PEER_SUBTILES, D_MODEL, ts), lambda i, e: (e, 0, 0)),
                  packed, packed, rows, rows,
                  pl.BlockSpec((tm, D_MODEL), lambda i, e: (i, 0), pipeline_mode=pl.Buffered(1)),
                  grp.mod_spec(5),
                  pl.BlockSpec((1, D_MODEL), lambda i, e: (0, 0))],
        out_specs=tok,
        out_shape=jax.ShapeDtypeStruct((M, D_MODEL), F32),
        scratch_shapes=[pltpu.VMEM((D_MODEL, tm), F32),
                        pltpu.VMEM((ts, tm), F32), pltpu.VMEM((ts, tm), F32),
                        pltpu.VMEM((ts, tm), BF16), pltpu.VMEM((ts, tm), BF16)],
        compiler_params=_params(("arbitrary", "arbitrary")),
        name="peer_dense",
    )(h2t, pu, pvt, r2, e2, al, n1, x1, modop, g_post)


def _prepare_weights(w_in, w_gla_gate, b_gla_gate, g_gla_norm, mu_shift, rw_w0, rw_w2, rw_a0, rw_a2,
                     rw_g2, rw_k_k, rw_k_a, rw_r_k, rw_gn_w, rw_gn_b, w_b_gla, w_b_rw, w_out,
                     peer_wq, peer_keys, peer_u, peer_v, g_pre_mix, g_post_mix, g_pre_ffn, g_post_ffn):
    W = RW_HEADS * RW_HEAD
    rw = w_in[:, 3088:6416]
    w_proj = jnp.concatenate(
        [rw[:, :3 * W], w_in[:, 6416:8464], w_in[:, 2048:3072], w_in[:, 1024:2048], w_in[:, 0:1024],
         rw[:, 3 * W:], w_in[:, 3072:3088], jnp.zeros((D_MODEL, PROJ_PAD_W - 8464), F32)],
        axis=1).astype(BF16)
    head = jnp.arange(W) // RW_HEAD
    zeros64 = jnp.zeros((64, W), F32)
    return dict(
        w_proj=w_proj,
        wg_pad=jnp.concatenate([w_gla_gate, jnp.zeros((128 - GLA_GATE_RANK, 512), F32)], axis=0),
        bg=b_gla_gate.reshape(1, -1),
        g_gla_norm=g_gla_norm.reshape(1, -1),
        mu_rkv=mu_shift[:3 * W].reshape(1, -1),
        mu_lora=mu_shift[3 * W:].reshape(1, -1),
        rw_w0=rw_w0.reshape(1, -1),
        w2p=jnp.concatenate([rw_w2, jnp.zeros((192, W), F32)], axis=0),
        rw_a0=rw_a0.reshape(1, -1),
        a2p=jnp.concatenate([zeros64, rw_a2, jnp.zeros((128, W), F32)], axis=0),
        g2p=jnp.concatenate([jnp.zeros((128, W), F32), rw_g2], axis=0),
        rw_k_k=rw_k_k.reshape(1, -1),
        rw_k_a=rw_k_a.reshape(1, -1),
        rw_r_k=rw_r_k.reshape(1, -1),
        rw_gn_w=rw_gn_w.reshape(1, -1),
        rw_gn_b=rw_gn_b.reshape(1, -1),
        head_ones=(head[:, None] == head[None, :]).astype(BF16),
        w_b_gla=w_b_gla.astype(BF16),
        w_b_rw=w_b_rw.astype(BF16),
        w_out=w_out.astype(BF16),
        peer_wq=peer_wq.astype(BF16),
        peer_keys=peer_keys.reshape(2 * PEER_HEADS, PEER_NKEYS, 128).astype(BF16),
        peer_u=peer_u.astype(BF16).reshape(-1, PEER_EXPERT_TILE, D_MODEL),
        peer_vt=peer_v.reshape(-1, PEER_EXPERT_TILE, D_MODEL).transpose(0, 2, 1).astype(BF16),
        g_pre_mix=g_pre_mix.reshape(1, -1),
        g_post_mix=g_post_mix.reshape(1, -1),
        g_pre_ffn=g_pre_ffn.reshape(1, -1),
        g_post_ffn=g_post_ffn.reshape(1, -1),
    )


def _block(x, mod, s_gla0, s_rw0, shift0, wts):
    B, valid, _ = x.shape
    T = -(-valid // 8) * 8
    x = jnp.pad(x, ((0, 0), (0, T - valid), (0, 0)))
    grp = _Group(B, T, valid)
    W = RW_HEADS * RW_HEAD
    x2d = x.reshape(grp.M, D_MODEL)
    modop = grp.seq_operand(mod)
    p2 = _projection(grp, x2d, modop, wts["g_pre_mix"], wts["w_proj"], BF16)
    p3 = p2.reshape(B, T, PROJ_PAD_W)
    og, s_gla = _gla(grp, p3, wts["wg_pad"], wts["bg"], s_gla0)
    r, k, v, an, bb, lw, gate = _rwkv_prep(grp, p2, shift0, wts)
    seqs = tuple(z.reshape(B, T, W) for z in (r, k, v, an, bb, lw))
    orw, s_rw = _rwkv(grp, seqs, wts["rw_gn_w"], wts["rw_gn_b"], wts["rw_r_k"], s_rw0)
    x1, h2t, qp = _merge(grp, og.reshape(grp.M, -1), p2, orw.reshape(grp.M, W), gate,
                        x2d, modop, wts)
    if valid != T:
        grp = _Group(B, valid, valid)
        modop = grp.seq_operand(mod)
        x1 = x1.reshape(B, T, D_MODEL)[:, :valid].reshape(grp.M, D_MODEL)
        h2t = h2t.reshape(D_MODEL, B, T)[:, :, :valid].reshape(D_MODEL, grp.M)
        qp = qp.reshape(B, T, -1)[:, :valid].reshape(grp.M, -1)
    sel = _peer_select(grp, qp, wts["peer_keys"])
    out = _peer(grp, h2t, sel, x1, modop, wts["g_post_ffn"], wts["peer_u"], wts["peer_vt"])
    ends = _Group(B, 1, 1)
    last = _projection(ends, x[:, valid - 1, :], ends.seq_operand(mod), wts["g_pre_mix"], wts["w_proj"], F32)
    shift_new = jnp.concatenate([last[:, :3 * W], last[:, COL_LORA:COL_LORA + 256]], axis=-1)
    return out.reshape(B, grp.T, D_MODEL)[:, :valid], s_gla, s_rw, shift_new


def kernel(x_prompt, x_sample, c_prompt, c_sample, state_gla, state_rwkv, state_shift, w_ada, b_ada, g_pre_mix, g_post_mix, g_pre_ffn, g_post_ffn, w_in, w_gla_gate, b_gla_gate, g_gla_norm, mu_shift, rw_w0, rw_w2, rw_a0, rw_a2, rw_g2, rw_k_k, rw_k_a, rw_r_k, rw_gn_w, rw_gn_b, w_b_gla, w_b_rw, w_out, peer_wq, peer_keys, peer_u, peer_v):
    depth = w_in.shape[0]
    bp = x_prompt.shape[0]
    xp, xs = x_prompt, x_sample
    outs = [[] for _ in range(6)]
    for l in range(depth):
        wts = _prepare_weights(
            w_in[l], w_gla_gate[l], b_gla_gate[l], g_gla_norm[l], mu_shift[l], rw_w0[l], rw_w2[l],
            rw_a0[l], rw_a2[l], rw_g2[l], rw_k_k[l], rw_k_a[l], rw_r_k[l], rw_gn_w[l], rw_gn_b[l],
            w_b_gla[l], w_b_rw[l], w_out[l], peer_wq[l], peer_keys[l], peer_u[l], peer_v[l],
            g_pre_mix[l], g_post_mix[l], g_pre_ffn[l], g_post_ffn[l])
        mod = _modulation(jnp.concatenate([c_prompt, c_sample], axis=0), w_ada[l], b_ada[l])
        zg = jnp.zeros((bp, GLA_HEADS, GLA_DK, GLA_DV), F32)
        zr = jnp.zeros((bp, RW_HEADS, RW_HEAD, RW_HEAD), F32)
        zs = jnp.zeros((bp, RW_SHIFT_W), F32)
        xp, g1, r1, s1 = _block(xp, mod[:bp], zg, zr, zs, wts)
        xs, g2, r2, s2 = _block(xs, mod[bp:], state_gla[l], state_rwkv[l], state_shift[l], wts)
        for acc, val in zip(outs, (g1, r1, s1, g2, r2, s2)):
            acc.append(val)
    stacked = [jnp.stack(o) for o in outs]
    return (xp, xs, *stacked)
```

```python
import functools

import jax
import jax.numpy as jnp
from jax import lax
from jax.experimental import pallas as pl
from jax.experimental.pallas import tpu as pltpu

F32 = jnp.float32
BF16 = jnp.bfloat16

D_MODEL = 1024
GLA_HEADS = 4
GLA_DK = 128
GLA_DV = 256
GLA_GATE_RANK = 16
GLA_GATE_TAU = 16.0
GLA_NORM_EPS = 1e-5
RW_HEAD = 64
RW_HEADS = 16
RW_PAIRS = RW_HEADS // 2
RW_GN_EPS = 64e-5
RW_SHIFT_W = 3328
PEER_HEADS = 8
PEER_NKEYS = 128
PEER_TOPK = 16
NORM_EPS = 1e-6
LANES = 128

COL_R, COL_K, COL_V = 0, 1024, 2048
COL_MG = 3072
COL_GG = 5120
COL_GV = 6144
COL_GQ = 7168
COL_GK = 7680
COL_LORA = 8192
COL_GA = 8448
PROJ_PAD_W = 8704
PROJ_TILE_N = 4352
PROJ_TILE_M = 1024

CHUNK = 64
TOKEN_TILE = 512
PEER_EXPERT_TILE = 512
PEER_SUBTILES = 4
VMEM_LIMIT = 56 * 1024 * 1024

NEG_INF = float("-inf")


def _mm(a, b):
    return jnp.dot(a.astype(BF16), b.astype(BF16), preferred_element_type=F32)


def _mm_nt(a, b):
    return lax.dot_general(a.astype(BF16), b.astype(BF16), (((1,), (1,)), ((), ())),
                           preferred_element_type=F32)


def _mm_tn(a, b):
    return lax.dot_general(a.astype(BF16), b.astype(BF16), (((0,), (0,)), ((), ())),
                           preferred_element_type=F32)


def _split(x):
    hi = x.astype(BF16)
    lo = (x - hi.astype(F32)).astype(BF16)
    return hi, lo


def _mm_exact_lhs(a, x):
    hi, lo = _split(x)
    a = a.astype(BF16)
    return (jnp.dot(a, hi, preferred_element_type=F32)
            + jnp.dot(a, lo, preferred_element_type=F32))


def _mm_exact_rhs(x, a):
    hi, lo = _split(x)
    a = a.astype(BF16)
    return (jnp.dot(hi, a, preferred_element_type=F32)
            + jnp.dot(lo, a, preferred_element_type=F32))


def _softplus(x):
    return jnp.maximum(x, 0.0) + jnp.log1p(jnp.exp(-jnp.abs(x)))


def _rms(x, g):
    return x * lax.rsqrt(jnp.mean(x * x, axis=-1, keepdims=True) + NORM_EPS) * g


def _params(sem, flags=None):
    return pltpu.CompilerParams(dimension_semantics=sem, vmem_limit_bytes=VMEM_LIMIT, flags=flags)


def _mod_kernel(c_ref, w_ref, b_ref, o_ref):
    c = c_ref[...]
    o_ref[...] = _mm(c * jax.nn.sigmoid(c), w_ref[...]) + b_ref[...]


def _modulation(c, w_ada, b_ada):
    rows = c.shape[0]
    n = w_ada.shape[1]
    tn = 1536
    return pl.pallas_call(
        _mod_kernel,
        grid=(n // tn,),
        in_specs=[pl.BlockSpec((rows, D_MODEL), lambda j: (0, 0)),
                  pl.BlockSpec((D_MODEL, tn), lambda j: (0, j)),
                  pl.BlockSpec((1, tn), lambda j: (0, j))],
        out_specs=pl.BlockSpec((rows, tn), lambda j: (0, j)),
        out_shape=jax.ShapeDtypeStruct((rows, n), F32),
        compiler_params=_params(("arbitrary",)),
        name="adaln_mod",
    )(c, w_ada, b_ada.reshape(1, n))


class _Group:
    def __init__(self, batch, seq, valid):
        self.B, self.T, self.Tv = batch, seq, valid
        self.M = batch * seq
        self.tm = min(TOKEN_TILE, self.M)
        self.per_batch_mod = seq % self.tm == 0

    def seq_operand(self, arr):
        if self.per_batch_mod:
            return arr.reshape(self.B, 1, arr.shape[-1])
        return jnp.repeat(arr, self.T, axis=0)

    def seq_spec(self, width, col, tm=None, tile_index=None):
        tm, T = tm or self.tm, self.T
        tile = tile_index or (lambda i, *_: i)
        if self.per_batch_mod:
            return pl.BlockSpec((None, 1, width), lambda *g: ((tile(*g) * tm) // T, 0, col))
        return pl.BlockSpec((tm, width), lambda *g: (tile(*g), col))

    def mod_spec(self, col, tm=None, tile_index=None):
        return self.seq_spec(D_MODEL, col, tm, tile_index)


def _proj_kernel(x_ref, shift_ref, scale_ref, g_ref, w_ref, p_ref, h_scr):
    @pl.when(pl.program_id(1) == 0)
    def _():
        h = _rms(x_ref[...], g_ref[...]) * (1.0 + scale_ref[...]) + shift_ref[...]
        h_scr[...] = h.astype(BF16)

    p_ref[...] = jnp.dot(h_scr[...], w_ref[...], preferred_element_type=F32).astype(p_ref.dtype)


def _projection(grp, x2d, modop, g_pre, w_proj, dtype):
    tm = PROJ_TILE_M if grp.per_batch_mod and grp.T % PROJ_TILE_M == 0 else grp.tm
    return pl.pallas_call(
        _proj_kernel,
        grid=(grp.M // tm, PROJ_PAD_W // PROJ_TILE_N),
        in_specs=[pl.BlockSpec((tm, D_MODEL), lambda i, j: (i, 0)),
                  grp.mod_spec(0, tm), grp.mod_spec(1, tm),
                  pl.BlockSpec((1, D_MODEL), lambda i, j: (0, 0)),
                  pl.BlockSpec((D_MODEL, PROJ_TILE_N), lambda i, j: (0, j))],
        out_specs=pl.BlockSpec((tm, PROJ_TILE_N), lambda i, j: (i, j)),
        out_shape=jax.ShapeDtypeStruct((grp.M, PROJ_PAD_W), dtype),
        scratch_shapes=[pltpu.VMEM((tm, D_MODEL), BF16)],
        compiler_params=_params(("arbitrary", "arbitrary")),
        name="in_proj",
    )(x2d, modop, modop, g_pre, w_proj)


def _gla_kernel(q_ref, k_ref, v_ref, ga_ref, wg_ref, bg_ref, s0_ref, og_ref, sout_ref, st_scr,
                *, chunk, n_chunks, tile, valid, seqs):
    t = pl.program_id(1)
    last = pl.num_programs(1) - 1
    row = lax.broadcasted_iota(jnp.int32, (tile, tile), 0)
    col = lax.broadcasted_iota(jnp.int32, (tile, tile), 1)
    shift = chunk.bit_length() - 1
    causal = (lax.shift_right_logical(row, shift) == lax.shift_right_logical(col, shift)) & (row >= col)
    tril = jnp.where(causal, 1.0, 0.0).astype(BF16)

    batch, heads, chunks = range(seqs), range(GLA_HEADS), range(n_chunks)
    pairs = [(b, h) for b in batch for h in heads]
    ks = [slice(h * GLA_DK, (h + 1) * GLA_DK) for h in heads]
    vs = [slice(h * GLA_DV, (h + 1) * GLA_DV) for h in heads]
    rows = [slice(c * chunk, (c + 1) * chunk) for c in chunks]
    ends = [slice((c + 1) * chunk - 1, (c + 1) * chunk) for c in chunks]

    @pl.when(t == 0)
    def _():
        for b, h in pairs:
            st_scr[b, h] = s0_ref[b, h].T

    z = [_mm(ga_ref[b], wg_ref[...]) + bg_ref[...] for b in batch]
    la = [-_softplus(-z[b]) / GLA_GATE_TAU for b in batch]
    k = [k_ref[b].astype(F32) for b in batch]
    v = {(b, h): v_ref[b, :, vs[h]] for b, h in pairs}
    if valid is not None:
        ok = (t * tile + lax.broadcasted_iota(jnp.int32, (tile, 1), 0)) < valid
        la = [jnp.where(ok, x, 0.0) for x in la]
        k = [jnp.where(ok, x, 0.0) for x in k]
        v = {key: jnp.where(ok, x, 0.0) for key, x in v.items()}
    cum = [_mm_exact_lhs(tril, la[b]) for b in batch]
    end = [jnp.concatenate([jnp.broadcast_to(cum[b][ends[c], :], (chunk, cum[b].shape[1])) for c in chunks], axis=0)
           for b in batch]
    qe = [q_ref[b].astype(F32) * (GLA_DK ** -0.5) * jnp.exp(cum[b]) for b in batch]
    ke = [k[b] * jnp.exp(-cum[b]) for b in batch]
    kd = [k[b] * jnp.exp(end[b] - cum[b]) for b in batch]
    decay = [[jnp.exp(cum[b][ends[c], :]) for c in chunks] for b in batch]
    a = {(b, h): jnp.where(causal, _mm_nt(qe[b][:, ks[h]], ke[b][:, ks[h]]), 0.0) for b, h in pairs}
    inside = {key: _mm(a[key], v[key]) for key in pairs}
    update = {(b, h, c): _mm_tn(v[b, h][rows[c], :], kd[b][rows[c], ks[h]]) for b, h in pairs for c in chunks}

    st = {(b, h, 0): st_scr[b, h] for b, h in pairs}
    for c in chunks:
        for b, h in pairs:
            st[b, h, c + 1] = st[b, h, c] * decay[b][c][:, ks[h]] + update[b, h, c]
    outer = {(b, h, c): _mm_nt(qe[b][rows[c], ks[h]], st[b, h, c]) for b, h in pairs for c in chunks}
    for b, h in pairs:
        st_scr[b, h] = st[b, h, n_chunks]
        for c in chunks:
            o = outer[b, h, c] + inside[b, h][rows[c], :]
            og = o * lax.rsqrt(jnp.mean(o * o, axis=-1, keepdims=True) + GLA_NORM_EPS)
            og_ref[b, rows[c], vs[h]] = og.astype(og_ref.dtype)

    @pl.when(t == last)
    def _():
        for b, h in pairs:
            sout_ref[b, h] = st_scr[b, h].T


def _seqs_per_step(batch, seq):
    n = max(1, min(8, 64 // seq))
    while batch % n:
        n //= 2
    return n


def _gla(grp, p3, wg_pad, bg, s0):
    B, T = grp.B, grp.T
    tile = min(T, 4 * CHUNK)
    chunk = min(CHUNK, tile)
    nb = _seqs_per_step(B, T)
    kern = functools.partial(_gla_kernel, chunk=chunk, n_chunks=tile // chunk, tile=tile,
                             valid=None if grp.Tv == T else grp.Tv, seqs=nb)
    kw, vw = GLA_HEADS * GLA_DK, GLA_HEADS * GLA_DV
    st_spec = pl.BlockSpec((nb, GLA_HEADS, GLA_DK, GLA_DV), lambda i, t: (i, 0, 0, 0))
    return pl.pallas_call(
        kern,
        grid=(B // nb, T // tile),
        in_specs=[pl.BlockSpec((nb, tile, kw), lambda i, t: (i, t, COL_GQ // kw)),
                  pl.BlockSpec((nb, tile, kw), lambda i, t: (i, t, COL_GK // kw)),
                  pl.BlockSpec((nb, tile, vw), lambda i, t: (i, t, COL_GV // vw)),
                  pl.BlockSpec((nb, tile, LANES), lambda i, t: (i, t, COL_GA // LANES)),
                  pl.BlockSpec((LANES, kw), lambda i, t: (0, 0)),
                  pl.BlockSpec((1, kw), lambda i, t: (0, 0)),
                  st_spec],
        out_specs=[pl.BlockSpec((nb, tile, vw), lambda i, t: (i, t, 0)), st_spec],
        out_shape=[jax.ShapeDtypeStruct((B, T, vw), BF16),
                   jax.ShapeDtypeStruct((B, GLA_HEADS, GLA_DK, GLA_DV), F32)],
        scratch_shapes=[pltpu.VMEM((nb, GLA_HEADS, GLA_DV, GLA_DK), F32)],
        compiler_params=_params(("arbitrary", "arbitrary")),
        name="gla_scan",
    )(p3, p3, p3, p3, wg_pad, bg, s0)


def _rwprep_kernel(r_ref, k_ref, v_ref, l_ref, pr_ref, pk_ref, pv_ref, pl_ref,
                   s0r_ref, s0k_ref, s0v_ref, s0l_ref, mu_ref, mul_ref, w0_ref, w2_ref, a0_ref, a2_ref, g2_ref,
                   kk_ref, ka_ref, bd_ref,
                   ro_ref, ko_ref, vo_ref, an_ref, bb_ref, lw_ref, gate_ref,
                   *, tile, seq, valid, prev_rows):
    i = pl.program_id(0)
    local = lax.broadcasted_iota(jnp.int32, (tile, 1), 0)
    pos = lax.rem(i * tile + local, seq)
    first = pos == 0
    top = local == 0

    def shifted(cur_ref, prev_ref, s0_ref, mu):
        cur = cur_ref[...].astype(F32)
        before = prev_ref[prev_rows - 1:prev_rows, :].astype(F32)
        prev = jnp.where(first, s0_ref[...], jnp.where(top, before, pltpu.roll(cur, 1, 0)))
        return cur + (prev - cur) * mu

    r = shifted(r_ref, pr_ref, s0r_ref, mu_ref[:, 0:1024])
    k = shifted(k_ref, pk_ref, s0k_ref, mu_ref[:, 1024:2048])
    v = shifted(v_ref, pv_ref, s0v_ref, mu_ref[:, 2048:3072])
    lo = shifted(l_ref, pl_ref, s0l_ref, mul_ref[...])

    wpre = w0_ref[...] + _mm(jnp.tanh(lo), w2_ref[...])
    wlog = -_softplus(-wpre) - 0.5
    lw = -jnp.exp(wlog)
    a = jax.nn.sigmoid(a0_ref[...] + _mm(lo, a2_ref[...]))
    gate = _mm(jax.nn.sigmoid(lo), g2_ref[...])

    kk = k * kk_ref[...]
    ss = _mm(kk * kk, bd_ref[...])
    kk = kk / jnp.maximum(jnp.sqrt(ss), 1e-12)
    kmod = k * (1.0 + (a - 1.0) * ka_ref[...])
    an = -kk
    bb = kk * a
    if valid is not None:
        ok = pos < valid
        lw = jnp.where(ok, lw, 0.0)
        an = jnp.where(ok, an, 0.0)
        bb = jnp.where(ok, bb, 0.0)
        kmod = jnp.where(ok, kmod, 0.0)
        v = jnp.where(ok, v, 0.0)
    ro_ref[...] = r.astype(ro_ref.dtype)
    ko_ref[...] = kmod.astype(ko_ref.dtype)
    vo_ref[...] = v.astype(vo_ref.dtype)
    an_ref[...] = an.astype(an_ref.dtype)
    bb_ref[...] = bb.astype(bb_ref.dtype)
    lw_ref[...] = lw
    gate_ref[...] = gate.astype(gate_ref.dtype)


def _rwkv_prep(grp, p2, shift0, wts):
    M, T = grp.M, grp.T
    tile = min(M, 512)
    prev_rows = min(16, tile)
    W = RW_HEADS * RW_HEAD
    s0 = grp.seq_operand(shift0)

    def cur(width, blk):
        return pl.BlockSpec((tile, width), lambda i: (i, blk))

    def prev(width, blk):
        return pl.BlockSpec((prev_rows, width), lambda i: (jnp.maximum(i * (tile // prev_rows) - 1, 0), blk))

    def full(shape):
        return pl.BlockSpec(shape, lambda i: (0,) * len(shape))

    out_spec = pl.BlockSpec((tile, W), lambda i: (i, 0))
    kern = functools.partial(_rwprep_kernel, tile=tile, seq=T, valid=None if grp.Tv == T else grp.Tv,
                             prev_rows=prev_rows)
    lora_blk = COL_LORA // 256
    return pl.pallas_call(
        kern,
        grid=(M // tile,),
        in_specs=[cur(W, 0), cur(W, 1), cur(W, 2), cur(256, lora_blk),
                  prev(W, 0), prev(W, 1), prev(W, 2), prev(256, lora_blk),
                  grp.seq_spec(W, 0, tile), grp.seq_spec(W, 1, tile), grp.seq_spec(W, 2, tile),
                  grp.seq_spec(256, 3 * W // 256, tile),
                  full((1, 3 * W)), full((1, 256)),
                  full((1, W)), full((256, W)), full((1, W)), full((256, W)), full((256, W)),
                  full((1, W)), full((1, W)), full((W, W))],
        out_specs=[out_spec] * 7,
        out_shape=[jax.ShapeDtypeStruct((M, W), BF16)] * 5
        + [jax.ShapeDtypeStruct((M, W), F32), jax.ShapeDtypeStruct((M, W), BF16)],
        compiler_params=_params(("arbitrary",)),
        name="rwkv_prep",
    )(p2, p2, p2, p2, p2, p2, p2, p2, s0, s0, s0, s0,
      wts["mu_rkv"], wts["mu_lora"], wts["rw_w0"], wts["w2p"], wts["rw_a0"], wts["a2p"], wts["g2p"],
      wts["rw_k_k"], wts["rw_k_a"], wts["head_ones"])


def _rwkv_pairs_chunk(at, rt, bt, kt, v, states, *, chunk, same_head, strict, incl):
    n2 = 2 * chunk
    wide = n2 % LANES == 0
    pairs = range(len(states))

    def blocks(x):
        return jnp.where(same_head, jnp.concatenate([x, x], axis=0), 0.0).astype(BF16)

    a2 = [blocks(x) for x in at]
    r2 = [blocks(x) for x in rt]
    b2 = [blocks(x) for x in bt]
    k2 = [blocks(x) for x in kt]
    v2 = [blocks(x) for x in v]
    ar = [jnp.concatenate([a2[p], r2[p]], axis=0) for p in pairs]
    bk = [jnp.concatenate([b2[p], k2[p]], axis=0) for p in pairs]
    if wide:
        g = [_mm_nt(ar[p], bk[p]) for p in pairs]
        l_ab = [jnp.where(strict, g[p][:n2, :n2], 0.0) for p in pairs]
        l_ak = [jnp.where(strict, g[p][:n2, n2:], 0.0) for p in pairs]
        m_rb = [jnp.where(incl, g[p][n2:, :n2], 0.0) for p in pairs]
        m_rk = [jnp.where(incl, g[p][n2:, n2:], 0.0) for p in pairs]
    else:
        l_ab = [jnp.where(strict, _mm_nt(a2[p], b2[p]), 0.0) for p in pairs]
        l_ak = [jnp.where(strict, _mm_nt(a2[p], k2[p]), 0.0) for p in pairs]
        m_rb = [jnp.where(incl, _mm_nt(r2[p], b2[p]), 0.0) for p in pairs]
        m_rk = [jnp.where(incl, _mm_nt(r2[p], k2[p]), 0.0) for p in pairs]
    from_state = [_mm_nt(ar[p], states[p]) for p in pairs]
    x = [from_state[p][:n2] + _mm(l_ak[p], v2[p]) for p in pairs]
    pw = l_ab
    for _ in range(chunk.bit_length() - 2):
        if wide:
            y = [_mm(pw[p], jnp.concatenate([x[p], pw[p]], axis=1)) for p in pairs]
            x = [x[p] + y[p][:, :LANES] for p in pairs]
            pw = [y[p][:, LANES:] for p in pairs]
        else:
            x = [x[p] + _mm(pw[p], x[p]) for p in pairs]
            pw = [_mm(pw[p], pw[p]) for p in pairs]
    u = [x[p] + _mm(pw[p], x[p]) for p in pairs]
    uv = [jnp.concatenate([u[p].astype(BF16), v2[p]], axis=0) for p in pairs]
    if wide:
        o2 = [from_state[p][n2:] + _mm(jnp.concatenate([m_rb[p], m_rk[p]], axis=1), uv[p]) for p in pairs]
    else:
        o2 = [from_state[p][n2:] + _mm(m_rb[p], u[p]) + _mm(m_rk[p], v2[p]) for p in pairs]
    s_new = [states[p] + _mm_tn(uv[p], bk[p]) for p in pairs]
    return o2, s_new


def _rwkv_kernel(r_ref, k_ref, v_ref, an_ref, bb_ref, lw_ref, gw_ref, gb_ref, rk_ref, s0_ref,
                 o_ref, sout_ref, s_scr, *, chunk, seqs):
    t = pl.program_id(1)
    zeros = jnp.zeros((RW_HEAD, RW_HEAD), F32)

    @pl.when(t == 0)
    def _():
        for q in range(seqs):
            for p in range(RW_PAIRS):
                top = jnp.concatenate([s0_ref[q, 2 * p], zeros], axis=1)
                bot = jnp.concatenate([zeros, s0_ref[q, 2 * p + 1]], axis=1)
                s_scr[q, p] = jnp.concatenate([top, bot], axis=0)

    n2 = 2 * chunk
    row = lax.broadcasted_iota(jnp.int32, (n2, n2), 0)
    col = lax.broadcasted_iota(jnp.int32, (n2, n2), 1)

    def second(idx, size):
        return jnp.where(idx >= size, 1, 0)

    same = second(row, chunk) == second(col, chunk)
    strict = same & (row > col)
    incl = same & (row >= col)
    same_head = (second(lax.broadcasted_iota(jnp.int32, (n2, LANES), 0), chunk)
                 == second(lax.broadcasted_iota(jnp.int32, (n2, LANES), 1), RW_HEAD))
    trow = lax.broadcasted_iota(jnp.int32, (chunk, chunk), 0)
    tcol = lax.broadcasted_iota(jnp.int32, (chunk, chunk), 1)
    tril = jnp.where(trow >= tcol, 1.0, 0.0).astype(BF16)
    first_head = lax.broadcasted_iota(jnp.int32, (chunk, LANES), 1) < RW_HEAD

    def head_sum(x):
        lo = jnp.sum(jnp.where(first_head, x, 0.0), axis=-1, keepdims=True)
        hi = jnp.sum(jnp.where(first_head, 0.0, x), axis=-1, keepdims=True)
        return jnp.where(first_head, lo, hi)

    lanes = [slice(p * LANES, (p + 1) * LANES) for p in range(RW_PAIRS)]
    problems = [(q, p) for q in range(seqs) for p in range(RW_PAIRS)]
    rs, ks, vs, ats, rts, bts, kts, ends = [], [], [], [], [], [], [], []
    for q in range(seqs):
        lw = lw_ref[q]
        cum = _mm_exact_lhs(tril, lw)
        w_incl = jnp.exp(cum)
        w_inv = jnp.exp(-cum)
        r, k = r_ref[q].astype(F32), k_ref[q].astype(F32)
        rs.append(r)
        ks.append(k)
        vs.append(v_ref[q].astype(F32))
        ats.append(an_ref[q].astype(F32) * jnp.exp(cum - lw))
        rts.append(r * w_incl)
        bts.append(bb_ref[q].astype(F32) * w_inv)
        kts.append(k * w_inv)
        ends.append(w_incl[chunk - 1:chunk, :])

    def slabs(xs):
        return [xs[q][:, lanes[p]] for q, p in problems]

    o2, s_new = _rwkv_pairs_chunk(
        slabs(ats), slabs(rts), slabs(bts), slabs(kts), slabs(vs), [s_scr[q, p] for q, p in problems],
        chunk=chunk, same_head=same_head, strict=strict, incl=incl)
    for n, (q, p) in enumerate(problems):
        ls = lanes[p]
        s_scr[q, p] = s_new[n] * ends[q][:, ls]
        o = o2[n][:chunk] + o2[n][chunk:]
        mu = head_sum(o) * (1.0 / RW_HEAD)
        oc = o - mu
        var = head_sum(oc * oc) * (1.0 / RW_HEAD)
        on = oc * lax.rsqrt(var + RW_GN_EPS) * gw_ref[:, ls] + gb_ref[:, ls]
        bonus = head_sum(rs[q][:, ls] * ks[q][:, ls] * rk_ref[:, ls]) * vs[q][:, ls]
        o_ref[q, :, ls] = (on + bonus).astype(o_ref.dtype)

    @pl.when(t == pl.num_programs(1) - 1)
    def _():
        for q in range(seqs):
            for p in range(RW_PAIRS):
                s = s_scr[q, p]
                sout_ref[q, 2 * p] = s[:RW_HEAD, :RW_HEAD]
                sout_ref[q, 2 * p + 1] = s[RW_HEAD:, RW_HEAD:]


def _rwkv(grp, seqs, gn_w, gn_b, r_k, s0):
    B, T = grp.B, grp.T
    chunk = min(T, CHUNK)
    nb = max(_seqs_per_step(B, T), 4 if B % 4 == 0 else 2 - B % 2)
    W = RW_HEADS * RW_HEAD
    seq_spec = pl.BlockSpec((nb, chunk, W), lambda b, t: (b, t, 0))
    vec_spec = pl.BlockSpec((1, W), lambda b, t: (0, 0))
    st_spec = pl.BlockSpec((nb, RW_HEADS, RW_HEAD, RW_HEAD), lambda b, t: (b, 0, 0, 0))
    kern = functools.partial(_rwkv_kernel, chunk=chunk, seqs=nb)
    return pl.pallas_call(
        kern,
        grid=(B // nb, T // chunk),
        in_specs=[seq_spec] * 6 + [vec_spec] * 3 + [st_spec],
        out_specs=[seq_spec, st_spec],
        out_shape=[jax.ShapeDtypeStruct((B, T, W), BF16),
                   jax.ShapeDtypeStruct((B, RW_HEADS, RW_HEAD, RW_HEAD), F32)],
        scratch_shapes=[pltpu.VMEM((nb, RW_PAIRS, 2 * RW_HEAD, 2 * RW_HEAD), F32)],
        compiler_params=_params(("arbitrary", "arbitrary")),
        name="rwkv_scan",
    )(*seqs, gn_w, gn_b, r_k, s0)


def _merge_kernel(og_ref, gg_ref, orw_ref, gate_ref, mga_ref, mgb_ref, x_ref,
                  gate1_ref, shift2_ref, scale2_ref,
                  ggn_ref, gpost_ref, gpre_ref, wbg_ref, wbr_ref, wout_ref, wq_ref,
                  x1_ref, h2t_ref, qp_ref):
    gg = gg_ref[...].astype(F32)
    gla_in = og_ref[...].astype(F32) * ggn_ref[...] * (gg * jax.nn.sigmoid(gg))
    br_gla = _mm(gla_in, wbg_ref[...])
    br_rw = _mm(orw_ref[...].astype(F32) * gate_ref[...].astype(F32), wbr_ref[...])
    merged = (jax.nn.sigmoid(mga_ref[...].astype(F32)) * br_gla
              + jax.nn.sigmoid(mgb_ref[...].astype(F32)) * br_rw)
    y = _mm(merged, wout_ref[...])
    x1 = x_ref[...] + gate1_ref[...] * _rms(y, gpost_ref[...])
    x1_ref[...] = x1
    h2 = _rms(x1, gpre_ref[...]) * (1.0 + scale2_ref[...]) + shift2_ref[...]
    h2t_ref[...] = h2.T.astype(BF16)
    qp_ref[...] = jnp.dot(h2.astype(BF16), wq_ref[...], preferred_element_type=F32).astype(BF16)


def _merge(grp, og, p2, orw, gate_rw, x2d, modop, wts):
    tm, M = grp.tm, grp.M
    W = D_MODEL

    def tok(blk):
        return pl.BlockSpec((tm, W), lambda i: (i, blk))

    def full(shape):
        return pl.BlockSpec(shape, lambda i: (0,) * len(shape), pipeline_mode=pl.Buffered(1))

    return pl.pallas_call(
        _merge_kernel,
        grid=(M // tm,),
        in_specs=[tok(0), tok(COL_GG // W), tok(0), tok(0), tok(COL_MG // W), tok(COL_MG // W + 1), tok(0),
                  grp.mod_spec(2, tm), grp.mod_spec(3, tm), grp.mod_spec(4, tm),
                  full((1, W)), full((1, W)), full((1, W)),
                  full((W, W)), full((W, W)), full((W, W)), full((W, 2 * W))],
        out_specs=[tok(0), pl.BlockSpec((W, tm), lambda i: (0, i)), pl.BlockSpec((tm, 2 * W), lambda i: (i, 0))],
        out_shape=[jax.ShapeDtypeStruct((M, W), F32), jax.ShapeDtypeStruct((W, M), BF16),
                   jax.ShapeDtypeStruct((M, 2 * W), BF16)],
        compiler_params=_params(("arbitrary",)),
        name="merge",
    )(og, p2, orw, gate_rw, p2, p2, x2d, modop, modop, modop,
      wts["g_gla_norm"], wts["g_post_mix"], wts["g_pre_ffn"],
      wts["w_b_gla"], wts["w_b_rw"], wts["w_out"], wts["peer_wq"])


def _pair_bits(x):
    bits = pltpu.bitcast(x.astype(BF16).astype(F32), jnp.uint32)
    return bits | (bits >> 16)


def _row_tile(row, rows):
    tile = pltpu.bitcast(jnp.broadcast_to(row, (8, row.shape[1])), BF16)
    return jnp.tile(tile, (rows // 16, 1))


def _batcher_network(n):
    def merge(lo, hi, r):
        step = r * 2
        if step < hi - lo:
            yield from merge(lo, hi, step)
            yield from merge(lo + r, hi, step)
            yield from [(i, i + r) for i in range(lo + r, hi - r, step)]
        else:
            yield (lo, lo + r)

    def sort(lo, hi):
        if hi - lo >= 1:
            mid = lo + (hi - lo) // 2
            yield from sort(lo, mid)
            yield from sort(mid + 1, hi)
            yield from merge(lo, hi, 1)

    return list(sort(0, n - 1))


def _top_values(s, n):
    sub = 8
    depth = s.shape[0] // sub
    cols = [s[i * sub:(i + 1) * sub, :] for i in range(depth)]
    for i, j in _batcher_network(depth):
        cols[i], cols[j] = jnp.maximum(cols[i], cols[j]), jnp.minimum(cols[i], cols[j])
    rows = lax.broadcasted_iota(jnp.int32, (n, s.shape[1]), 0)
    vals = jnp.zeros((n, s.shape[1]), F32)
    for it in range(n):
        m = jnp.max(cols[0], axis=0, keepdims=True)
        vals = jnp.where(rows == it, m, vals)
        hit = cols[0] == m
        for i in range(min(depth - 1, n - 1 - it)):
            cols[i] = jnp.where(hit, cols[i + 1], cols[i])
    return vals


def _count_prefix(vals, pred):
    def row(i):
        return vals[i:i + 1, :]

    c8 = pred(row(7))
    c4 = pred(jnp.where(c8, row(11), row(3)))
    c2 = pred(jnp.where(c8, jnp.where(c4, row(13), row(9)), jnp.where(c4, row(5), row(1))))
    upper = jnp.where(c4, jnp.where(c2, row(14), row(12)), jnp.where(c2, row(10), row(8)))
    lower = jnp.where(c4, jnp.where(c2, row(6), row(4)), jnp.where(c2, row(2), row(0)))
    c1 = pred(jnp.where(c8, upper, lower))
    c16 = pred(row(15))
    return (jnp.where(c8, 8.0, 0.0) + jnp.where(c4, 4.0, 0.0) + jnp.where(c2, 2.0, 0.0)
            + jnp.where(c1, 1.0, 0.0) + jnp.where(c16, 1.0, 0.0))


def _rank_among(s, vals):
    return _count_prefix(vals, lambda x: x > s)


def _candidate_groups(v1, v2):
    n, tm = v1.shape
    split = 4
    r_all = lax.broadcasted_iota(jnp.int32, (n, tm), 0)
    r_half = lax.broadcasted_iota(jnp.int32, (n // 2, tm), 0)
    groups = [v1[0:1, :] + v2]
    for a in range(1, split):
        groups.append(jnp.where(r_half < n // (a + 1), v1[a:a + 1, :] + v2[:n // 2, :], NEG_INF))
    groups.append(jnp.where(r_all >= split, v1 + v2[0:1, :], NEG_INF))
    for b in range(1, n // (split + 1)):
        keep = (r_half >= split) & (r_half < n // (b + 1))
        groups.append(jnp.where(keep, v1[:n // 2, :] + v2[b:b + 1, :], NEG_INF))
    return groups


def _group_max(groups, n):
    full = [g for g in groups if g.shape[0] == n]
    best = full[0]
    for g in full[1:]:
        best = jnp.maximum(best, g)
    best = jnp.maximum(best[:n // 2, :], best[n // 2:, :])
    for g in groups:
        if g.shape[0] != n:
            best = jnp.maximum(best, g)
    return jnp.max(best, axis=0, keepdims=True)


def _select_kernel(qp_ref, keys_ref, r2_ref, e2_ref, al_ref, n1_ref):
    tm = qp_ref.shape[0]
    for h in range(PEER_HEADS):
        q1 = qp_ref[:, (2 * h) * 128:(2 * h + 1) * 128]
        q2 = qp_ref[:, (2 * h + 1) * 128:(2 * h + 2) * 128]
        s1 = _mm_nt(keys_ref[2 * h], q1)
        s2 = _mm_nt(keys_ref[2 * h + 1], q2)
        v1 = _top_values(s1, PEER_TOPK)
        v2 = _top_values(s2, PEER_TOPK)
        rank2 = _rank_among(s2, v2)
        groups = _candidate_groups(v1, v2)
        cmax = v1[0:1, :] + v2[0:1, :]
        z = jnp.zeros((1, tm), F32)
        m = cmax
        for _ in range(PEER_TOPK):
            m = _group_max(groups, PEER_TOPK)
            z = z + jnp.exp(m - cmax)
            groups = [jnp.where(g == m, NEG_INF, g) for g in groups]
        th = m - s1
        n1 = _count_prefix(v2, lambda x: x >= th)
        r2_ref[h] = rank2.astype(BF16)
        e2_ref[h] = jnp.exp(s2 - v2[0:1, :]).astype(BF16)
        al_ref[h] = _pair_bits(jnp.exp(s1 - v1[0:1, :]) / z)
        n1_ref[h] = _pair_bits(n1)


def _peer_select(grp, qp, keys):
    tm, M = grp.tm, grp.M
    rows = pl.BlockSpec((PEER_HEADS, PEER_NKEYS, tm), lambda i: (0, 0, i))
    return pl.pallas_call(
        _select_kernel,
        grid=(M // tm,),
        in_specs=[pl.BlockSpec((tm, 2 * D_MODEL), lambda i: (i, 0)),
                  pl.BlockSpec((2 * PEER_HEADS, PEER_NKEYS, 128), lambda i: (0, 0, 0))],
        out_specs=[rows] * 4,
        out_shape=[jax.ShapeDtypeStruct((PEER_HEADS, PEER_NKEYS, M), BF16)] * 2
        + [jax.ShapeDtypeStruct((PEER_HEADS, PEER_NKEYS, M), jnp.uint32)] * 2,
        compiler_params=_params(("arbitrary",)),
        name="peer_select",
    )(qp, keys)


def _peer_weights(act_ref, y_ref, r2_ref, e2_ref, al_ref, n1_ref, row0):
    tm = act_ref.shape[1]
    half = PEER_NKEYS // 2
    for j in range(act_ref.shape[0] // PEER_NKEYS):
        i1 = slice(row0 + j, row0 + j + 1)
        w = [jnp.zeros((half, tm), BF16) for _ in range(2)]
        for h in range(PEER_HEADS):
            n1 = _row_tile(n1_ref[h, i1, :], half)
            al = _row_tile(al_ref[h, i1, :], half)
            for r in range(2):
                keys = slice(r * half, (r + 1) * half)
                w[r] = w[r] + jnp.where(r2_ref[h, keys, :] < n1, e2_ref[h, keys, :], 0.0) * al
        for r in range(2):
            rows = slice(j * PEER_NKEYS + r * half, j * PEER_NKEYS + (r + 1) * half)
            a = act_ref[rows, :].astype(BF16)
            y_ref[rows, :] = (0.5 * a * (1.0 + lax.erf(a * 0.7071067811865476))) * w[r]


def _peer_kernel(h2t_ref, u_ref, vt_ref, r2_ref, e2_ref, al_ref, n1_ref, x1_ref, gate2_ref, gpost_ref,
                 o_ref, acc_ref, act_0, act_1, y_0, y_1):
    e = pl.program_id(1)
    ts = act_0.shape[0]
    per = ts // PEER_NKEYS
    acts, ys = (act_0, act_1), (y_0, y_1)

    @pl.when(e == 0)
    def _():
        acc_ref[...] = jnp.zeros_like(acc_ref)

    def scores(k):
        acts[k % 2][...] = jnp.dot(u_ref[k], h2t_ref[...], preferred_element_type=F32)

    def mix(k):
        acc_ref[...] += jnp.dot(vt_ref[k], ys[k % 2][...], preferred_element_type=F32)

    scores(0)
    scores(1)
    for k in range(PEER_SUBTILES):
        _peer_weights(acts[k % 2], ys[k % 2], r2_ref, e2_ref, al_ref, n1_ref, k * per)
        if k + 2 < PEER_SUBTILES:
            scores(k + 2)
        if k >= 1:
            mix(k - 1)
    mix(PEER_SUBTILES - 1)

    @pl.when(e == pl.num_programs(1) - 1)
    def _():
        o_ref[...] = x1_ref[...] + gate2_ref[...] * _rms(acc_ref[...].T, gpost_ref[...])


def _peer(grp, h2t, sel, x1, modop, g_post, pu, pvt):
    tm, M = grp.tm, grp.M
    ts = PEER_EXPERT_TILE
    te = ts * PEER_SUBTILES
    r2, e2, al, n1 = sel
    packed = pl.BlockSpec((PEER_HEADS, PEER_NKEYS, tm), lambda i, e: (0, 0, i))
    rows = pl.BlockSpec((PEER_HEADS, te // PEER_NKEYS, tm), lambda i, e: (0, e, i))
    tok = pl.BlockSpec((tm, D_MODEL), lambda i, e: (i, 0))
    return pl.pallas_call(
        _peer_kernel,
        grid=(M // tm, pu.shape[0] // PEER_SUBTILES),
        in_specs=[pl.BlockSpec((D_MODEL, tm), lambda i, e: (0, i)),
                  pl.BlockSpec((PEER_SUBTILES, ts, D_MODEL), lambda i, e: (e, 0, 0)),
                  pl.BlockSpec((PEER_SUBTILES, D_MODEL, ts), lambda i, e: (e, 0, 0)),
                  packed, packed, rows, rows,
                  pl.BlockSpec((tm, D_MODEL), lambda i, e: (i, 0), pipeline_mode=pl.Buffered(1)),
                  grp.mod_spec(5),
                  pl.BlockSpec((1, D_MODEL), lambda i, e: (0, 0))],
        out_specs=tok,
        out_shape=jax.ShapeDtypeStruct((M, D_MODEL), F32),
        scratch_shapes=[pltpu.VMEM((D_MODEL, tm), F32),
                        pltpu.VMEM((ts, tm), F32), pltpu.VMEM((ts, tm), F32),
                        pltpu.VMEM((ts, tm), BF16), pltpu.VMEM((ts, tm), BF16)],
        compiler_params=_params(("arbitrary", "arbitrary")),
        name="peer_dense",
    )(h2t, pu, pvt, r2, e2, al, n1, x1, modop, g_post)


def _transpose_cast_kernel(x_ref, o_ref):
    o_ref[...] = x_ref[...].T.astype(o_ref.dtype)


def _transposed_tiles(x, rows):
    n, d = x.shape
    return pl.pallas_call(
        _transpose_cast_kernel,
        grid=(n // rows,),
        in_specs=[pl.BlockSpec((rows, d), lambda k: (k, 0))],
        out_specs=pl.BlockSpec((None, d, rows), lambda k: (k, 0, 0)),
        out_shape=jax.ShapeDtypeStruct((n // rows, d, rows), BF16),
        compiler_params=_params(("arbitrary",)),
        name="transpose_cast",
    )(x)


def _prepare_weights(w_in, w_gla_gate, b_gla_gate, g_gla_norm, mu_shift, rw_w0, rw_w2, rw_a0, rw_a2,
                     rw_g2, rw_k_k, rw_k_a, rw_r_k, rw_gn_w, rw_gn_b, w_b_gla, w_b_rw, w_out,
                     peer_wq, peer_keys, peer_u, peer_v, g_pre_mix, g_post_mix, g_pre_ffn, g_post_ffn):
    W = RW_HEADS * RW_HEAD
    rw = w_in[:, 3088:6416]
    w_proj = jnp.concatenate(
        [rw[:, :3 * W], w_in[:, 6416:8464], w_in[:, 2048:3072], w_in[:, 1024:2048], w_in[:, 0:1024],
         rw[:, 3 * W:], w_in[:, 3072:3088], jnp.zeros((D_MODEL, PROJ_PAD_W - 8464), F32)],
        axis=1).astype(BF16)
    head = jnp.arange(W) // RW_HEAD
    zeros64 = jnp.zeros((64, W), F32)
    return dict(
        w_proj=w_proj,
        wg_pad=jnp.concatenate([w_gla_gate, jnp.zeros((128 - GLA_GATE_RANK, 512), F32)], axis=0),
        bg=b_gla_gate.reshape(1, -1),
        g_gla_norm=g_gla_norm.reshape(1, -1),
        mu_rkv=mu_shift[:3 * W].reshape(1, -1),
        mu_lora=mu_shift[3 * W:].reshape(1, -1),
        rw_w0=rw_w0.reshape(1, -1),
        w2p=jnp.concatenate([rw_w2, jnp.zeros((192, W), F32)], axis=0),
        rw_a0=rw_a0.reshape(1, -1),
        a2p=jnp.concatenate([zeros64, rw_a2, jnp.zeros((128, W), F32)], axis=0),
        g2p=jnp.concatenate([jnp.zeros((128, W), F32), rw_g2], axis=0),
        rw_k_k=rw_k_k.reshape(1, -1),
        rw_k_a=rw_k_a.reshape(1, -1),
        rw_r_k=rw_r_k.reshape(1, -1),
        rw_gn_w=rw_gn_w.reshape(1, -1),
        rw_gn_b=rw_gn_b.reshape(1, -1),
        head_ones=(head[:, None] == head[None, :]).astype(BF16),
        w_b_gla=w_b_gla.astype(BF16),
        w_b_rw=w_b_rw.astype(BF16),
        w_out=w_out.astype(BF16),
        peer_wq=peer_wq.astype(BF16),
        peer_keys=peer_keys.reshape(2 * PEER_HEADS, PEER_NKEYS, 128).astype(BF16),
        peer_u=peer_u.astype(BF16).reshape(-1, PEER_EXPERT_TILE, D_MODEL),
        peer_vt=_transposed_tiles(peer_v, PEER_EXPERT_TILE),
        g_pre_mix=g_pre_mix.reshape(1, -1),
        g_post_mix=g_post_mix.reshape(1, -1),
        g_pre_ffn=g_pre_ffn.reshape(1, -1),
        g_post_ffn=g_post_ffn.reshape(1, -1),
    )


def _block(x, mod, s_gla0, s_rw0, shift0, wts):
    B, valid, _ = x.shape
    T = -(-valid // 8) * 8
    x = jnp.pad(x, ((0, 0), (0, T - valid), (0, 0)))
    grp = _Group(B, T, valid)
    W = RW_HEADS * RW_HEAD
    x2d = x.reshape(grp.M, D_MODEL)
    modop = grp.seq_operand(mod)
    p2 = _projection(grp, x2d, modop, wts["g_pre_mix"], wts["w_proj"], BF16)
    p3 = p2.reshape(B, T, PROJ_PAD_W)
    og, s_gla = _gla(grp, p3, wts["wg_pad"], wts["bg"], s_gla0)
    r, k, v, an, bb, lw, gate = _rwkv_prep(grp, p2, shift0, wts)
    seqs = tuple(z.reshape(B, T, W) for z in (r, k, v, an, bb, lw))
    orw, s_rw = _rwkv(grp, seqs, wts["rw_gn_w"], wts["rw_gn_b"], wts["rw_r_k"], s_rw0)
    x1, h2t, qp = _merge(grp, og.reshape(grp.M, -1), p2, orw.reshape(grp.M, W), gate,
                        x2d, modop, wts)
    if valid != T:
        grp = _Group(B, valid, valid)
        modop = grp.seq_operand(mod)
        x1 = x1.reshape(B, T, D_MODEL)[:, :valid].reshape(grp.M, D_MODEL)
        h2t = h2t.reshape(D_MODEL, B, T)[:, :, :valid].reshape(D_MODEL, grp.M)
        qp = qp.reshape(B, T, -1)[:, :valid].reshape(grp.M, -1)
    sel = _peer_select(grp, qp, wts["peer_keys"])
    out = _peer(grp, h2t, sel, x1, modop, wts["g_post_ffn"], wts["peer_u"], wts["peer_vt"])
    ends = _Group(B, 1, 1)
    last = _projection(ends, x[:, valid - 1, :], ends.seq_operand(mod), wts["g_pre_mix"], wts["w_proj"], F32)
    shift_new = jnp.concatenate([last[:, :3 * W], last[:, COL_LORA:COL_LORA + 256]], axis=-1)
    return out.reshape(B, grp.T, D_MODEL)[:, :valid], s_gla, s_rw, shift_new


def kernel(x_prompt, x_sample, c_prompt, c_sample, state_gla, state_rwkv, state_shift, w_ada, b_ada, g_pre_mix, g_post_mix, g_pre_ffn, g_post_ffn, w_in, w_gla_gate, b_gla_gate, g_gla_norm, mu_shift, rw_w0, rw_w2, rw_a0, rw_a2, rw_g2, rw_k_k, rw_k_a, rw_r_k, rw_gn_w, rw_gn_b, w_b_gla, w_b_rw, w_out, peer_wq, peer_keys, peer_u, peer_v):
    depth = w_in.shape[0]
    bp = x_prompt.shape[0]
    xp, xs = x_prompt, x_sample
    outs = [[] for _ in range(6)]
    for l in range(depth):
        wts = _prepare_weights(
            w_in[l], w_gla_gate[l], b_gla_gate[l], g_gla_norm[l], mu_shift[l], rw_w0[l], rw_w2[l],
            rw_a0[l], rw_a2[l], rw_g2[l], rw_k_k[l], rw_k_a[l], rw_r_k[l], rw_gn_w[l], rw_gn_b[l],
            w_b_gla[l], w_b_rw[l], w_out[l], peer_wq[l], peer_keys[l], peer_u[l], peer_v[l],
            g_pre_mix[l], g_post_mix[l], g_pre_ffn[l], g_post_ffn[l])
        mod = _modulation(jnp.concatenate([c_prompt, c_sample], axis=0), w_ada[l], b_ada[l])
        zg = jnp.zeros((bp, GLA_HEADS, GLA_DK, GLA_DV), F32)
        zr = jnp.zeros((bp, RW_HEADS, RW_HEAD, RW_HEAD), F32)
        zs = jnp.zeros((bp, RW_SHIFT_W), F32)
        xp, g1, r1, s1 = _block(xp, mod[:bp], zg, zr, zs, wts)
        xs, g2, r2, s2 = _block(xs, mod[bp:], state_gla[l], state_rwkv[l], state_shift[l], wts)
        for acc, val in zip(outs, (g1, r1, s1, g2, r2, s2)):
            acc.append(val)
    stacked = [jnp.stack(o) for o in outs]
    return (xp, xs, *stacked)
```

```python
import functools

import jax
import jax.numpy as jnp
from jax import lax
from jax.experimental import pallas as pl
from jax.experimental.pallas import tpu as pltpu

F32 = jnp.float32
BF16 = jnp.bfloat16

D_MODEL = 1024
GLA_HEADS = 4
GLA_DK = 128
GLA_DV = 256
GLA_GATE_RANK = 16
GLA_GATE_TAU = 16.0
GLA_NORM_EPS = 1e-5
RW_HEAD = 64
RW_HEADS = 16
RW_PAIRS = RW_HEADS // 2
RW_GN_EPS = 64e-5
RW_SHIFT_W = 3328
PEER_HEADS = 8
PEER_NKEYS = 128
PEER_TOPK = 16
NORM_EPS = 1e-6
LANES = 128

COL_R, COL_K, COL_V = 0, 1024, 2048
COL_MG = 3072
COL_GG = 5120
COL_GV = 6144
COL_GQ = 7168
COL_GK = 7680
COL_LORA = 8192
COL_GA = 8448
PROJ_PAD_W = 8704
PROJ_TILE_N = 4352
PROJ_TILE_M = 1024

CHUNK = 64
TOKEN_TILE = 512
PEER_EXPERT_TILE = 512
PEER_SUBTILES = 4
VMEM_LIMIT = 56 * 1024 * 1024

NEG_INF = float("-inf")


def _mm(a, b):
    return jnp.dot(a.astype(BF16), b.astype(BF16), preferred_element_type=F32)


def _mm_nt(a, b):
    return lax.dot_general(a.astype(BF16), b.astype(BF16), (((1,), (1,)), ((), ())),
                           preferred_element_type=F32)


def _mm_tn(a, b):
    return lax.dot_general(a.astype(BF16), b.astype(BF16), (((0,), (0,)), ((), ())),
                           preferred_element_type=F32)


def _split(x):
    hi = x.astype(BF16)
    lo = (x - hi.astype(F32)).astype(BF16)
    return hi, lo


def _mm_exact_lhs(a, x):
    hi, lo = _split(x)
    a = a.astype(BF16)
    return (jnp.dot(a, hi, preferred_element_type=F32)
            + jnp.dot(a, lo, preferred_element_type=F32))


def _mm_exact_rhs(x, a):
    hi, lo = _split(x)
    a = a.astype(BF16)
    return (jnp.dot(hi, a, preferred_element_type=F32)
            + jnp.dot(lo, a, preferred_element_type=F32))


def _softplus(x):
    return jnp.maximum(x, 0.0) + jnp.log1p(jnp.exp(-jnp.abs(x)))


def _rms(x, g):
    return x * lax.rsqrt(jnp.mean(x * x, axis=-1, keepdims=True) + NORM_EPS) * g


def _params(sem, flags=None):
    return pltpu.CompilerParams(dimension_semantics=sem, vmem_limit_bytes=VMEM_LIMIT, flags=flags)


def _mod_kernel(c_ref, w_ref, b_ref, o_ref):
    c = c_ref[...]
    o_ref[...] = _mm(c * jax.nn.sigmoid(c), w_ref[...]) + b_ref[...]


def _modulation(c, w_ada, b_ada):
    rows = c.shape[0]
    n = w_ada.shape[1]
    tn = 1536
    return pl.pallas_call(
        _mod_kernel,
        grid=(n // tn,),
        in_specs=[pl.BlockSpec((rows, D_MODEL), lambda j: (0, 0)),
                  pl.BlockSpec((D_MODEL, tn), lambda j: (0, j)),
                  pl.BlockSpec((1, tn), lambda j: (0, j))],
        out_specs=pl.BlockSpec((rows, tn), lambda j: (0, j)),
        out_shape=jax.ShapeDtypeStruct((rows, n), F32),
        compiler_params=_params(("arbitrary",)),
        name="adaln_mod",
    )(c, w_ada, b_ada.reshape(1, n))


class _Group:
    def __init__(self, batch, seq, valid):
        self.B, self.T, self.Tv = batch, seq, valid
        self.M = batch * seq
        self.tm = min(TOKEN_TILE, self.M)
        self.per_batch_mod = seq % self.tm == 0

    def seq_operand(self, arr):
        if self.per_batch_mod:
            return arr.reshape(self.B, 1, arr.shape[-1])
        return jnp.repeat(arr, self.T, axis=0)

    def seq_spec(self, width, col, tm=None, tile_index=None):
        tm, T = tm or self.tm, self.T
        tile = tile_index or (lambda i, *_: i)
        if self.per_batch_mod:
            return pl.BlockSpec((None, 1, width), lambda *g: ((tile(*g) * tm) // T, 0, col))
        return pl.BlockSpec((tm, width), lambda *g: (tile(*g), col))

    def mod_spec(self, col, tm=None, tile_index=None):
        return self.seq_spec(D_MODEL, col, tm, tile_index)


def _proj_kernel(x_ref, shift_ref, scale_ref, g_ref, w_ref, p_ref, h_scr):
    @pl.when(pl.program_id(1) == 0)
    def _():
        h = _rms(x_ref[...], g_ref[...]) * (1.0 + scale_ref[...]) + shift_ref[...]
        h_scr[...] = h.astype(BF16)

    p_ref[...] = jnp.dot(h_scr[...], w_ref[...], preferred_element_type=F32).astype(p_ref.dtype)


def _projection(grp, x2d, modop, g_pre, w_proj, dtype):
    tm = PROJ_TILE_M if grp.per_batch_mod and grp.T % PROJ_TILE_M == 0 else grp.tm
    return pl.pallas_call(
        _proj_kernel,
        grid=(grp.M // tm, PROJ_PAD_W // PROJ_TILE_N),
        in_specs=[pl.BlockSpec((tm, D_MODEL), lambda i, j: (i, 0)),
                  grp.mod_spec(0, tm), grp.mod_spec(1, tm),
                  pl.BlockSpec((1, D_MODEL), lambda i, j: (0, 0)),
                  pl.BlockSpec((D_MODEL, PROJ_TILE_N), lambda i, j: (0, j))],
        out_specs=pl.BlockSpec((tm, PROJ_TILE_N), lambda i, j: (i, j)),
        out_shape=jax.ShapeDtypeStruct((grp.M, PROJ_PAD_W), dtype),
        scratch_shapes=[pltpu.VMEM((tm, D_MODEL), BF16)],
        compiler_params=_params(("arbitrary", "arbitrary")),
        name="in_proj",
    )(x2d, modop, modop, g_pre, w_proj)


def _gla_kernel(q_ref, k_ref, v_ref, ga_ref, wg_ref, bg_ref, s0_ref, og_ref, sout_ref, st_scr,
                *, chunk, n_chunks, tile, valid, seqs):
    t = pl.program_id(1)
    last = pl.num_programs(1) - 1
    row = lax.broadcasted_iota(jnp.int32, (tile, tile), 0)
    col = lax.broadcasted_iota(jnp.int32, (tile, tile), 1)
    shift = chunk.bit_length() - 1
    causal = (lax.shift_right_logical(row, shift) == lax.shift_right_logical(col, shift)) & (row >= col)
    tril = jnp.where(causal, 1.0, 0.0).astype(BF16)

    batch, heads, chunks = range(seqs), range(GLA_HEADS), range(n_chunks)
    pairs = [(b, h) for b in batch for h in heads]
    ks = [slice(h * GLA_DK, (h + 1) * GLA_DK) for h in heads]
    vs = [slice(h * GLA_DV, (h + 1) * GLA_DV) for h in heads]
    rows = [slice(c * chunk, (c + 1) * chunk) for c in chunks]
    ends = [slice((c + 1) * chunk - 1, (c + 1) * chunk) for c in chunks]

    @pl.when(t == 0)
    def _():
        for b, h in pairs:
            st_scr[b, h] = s0_ref[b, h].T

    z = [_mm(ga_ref[b], wg_ref[...]) + bg_ref[...] for b in batch]
    la = [-_softplus(-z[b]) / GLA_GATE_TAU for b in batch]
    k = [k_ref[b].astype(F32) for b in batch]
    v = {(b, h): v_ref[b, :, vs[h]] for b, h in pairs}
    if valid is not None:
        ok = (t * tile + lax.broadcasted_iota(jnp.int32, (tile, 1), 0)) < valid
        la = [jnp.where(ok, x, 0.0) for x in la]
        k = [jnp.where(ok, x, 0.0) for x in k]
        v = {key: jnp.where(ok, x, 0.0) for key, x in v.items()}
    cum = [_mm_exact_lhs(tril, la[b]) for b in batch]
    end = [jnp.concatenate([jnp.broadcast_to(cum[b][ends[c], :], (chunk, cum[b].shape[1])) for c in chunks], axis=0)
           for b in batch]
    qe = [q_ref[b].astype(F32) * (GLA_DK ** -0.5) * jnp.exp(cum[b]) for b in batch]
    ke = [k[b] * jnp.exp(-cum[b]) for b in batch]
    kd = [k[b] * jnp.exp(end[b] - cum[b]) for b in batch]
    decay = [[jnp.exp(cum[b][ends[c], :]) for c in chunks] for b in batch]
    a = {(b, h): jnp.where(causal, _mm_nt(qe[b][:, ks[h]], ke[b][:, ks[h]]), 0.0) for b, h in pairs}
    inside = {key: _mm(a[key], v[key]) for key in pairs}
    update = {(b, h, c): _mm_tn(v[b, h][rows[c], :], kd[b][rows[c], ks[h]]) for b, h in pairs for c in chunks}

    st = {(b, h, 0): st_scr[b, h] for b, h in pairs}
    for c in chunks:
        for b, h in pairs:
            st[b, h, c + 1] = st[b, h, c] * decay[b][c][:, ks[h]] + update[b, h, c]
    outer = {(b, h, c): _mm_nt(qe[b][rows[c], ks[h]], st[b, h, c]) for b, h in pairs for c in chunks}
    for b, h in pairs:
        st_scr[b, h] = st[b, h, n_chunks]
        for c in chunks:
            o = outer[b, h, c] + inside[b, h][rows[c], :]
            og = o * lax.rsqrt(jnp.mean(o * o, axis=-1, keepdims=True) + GLA_NORM_EPS)
            og_ref[b, rows[c], vs[h]] = og.astype(og_ref.dtype)

    @pl.when(t == last)
    def _():
        for b, h in pairs:
            sout_ref[b, h] = st_scr[b, h].T


def _seqs_per_step(batch, seq):
    n = max(1, min(8, 64 // seq))
    while batch % n:
        n //= 2
    return n


def _gla(grp, p3, wg_pad, bg, s0):
    B, T = grp.B, grp.T
    tile = min(T, 4 * CHUNK)
    chunk = min(CHUNK, tile)
    nb = _seqs_per_step(B, T)
    kern = functools.partial(_gla_kernel, chunk=chunk, n_chunks=tile // chunk, tile=tile,
                             valid=None if grp.Tv == T else grp.Tv, seqs=nb)
    kw, vw = GLA_HEADS * GLA_DK, GLA_HEADS * GLA_DV
    st_spec = pl.BlockSpec((nb, GLA_HEADS, GLA_DK, GLA_DV), lambda i, t: (i, 0, 0, 0))
    return pl.pallas_call(
        kern,
        grid=(B // nb, T // tile),
        in_specs=[pl.BlockSpec((nb, tile, kw), lambda i, t: (i, t, COL_GQ // kw)),
                  pl.BlockSpec((nb, tile, kw), lambda i, t: (i, t, COL_GK // kw)),
                  pl.BlockSpec((nb, tile, vw), lambda i, t: (i, t, COL_GV // vw)),
                  pl.BlockSpec((nb, tile, LANES), lambda i, t: (i, t, COL_GA // LANES)),
                  pl.BlockSpec((LANES, kw), lambda i, t: (0, 0)),
                  pl.BlockSpec((1, kw), lambda i, t: (0, 0)),
                  st_spec],
        out_specs=[pl.BlockSpec((nb, tile, vw), lambda i, t: (i, t, 0)), st_spec],
        out_shape=[jax.ShapeDtypeStruct((B, T, vw), BF16),
                   jax.ShapeDtypeStruct((B, GLA_HEADS, GLA_DK, GLA_DV), F32)],
        scratch_shapes=[pltpu.VMEM((nb, GLA_HEADS, GLA_DV, GLA_DK), F32)],
        compiler_params=_params(("arbitrary", "arbitrary")),
        name="gla_scan",
    )(p3, p3, p3, p3, wg_pad, bg, s0)


def _rwprep_kernel(r_ref, k_ref, v_ref, l_ref, pr_ref, pk_ref, pv_ref, pl_ref,
                   s0r_ref, s0k_ref, s0v_ref, s0l_ref, mu_ref, mul_ref, w0_ref, w2_ref, a0_ref, a2_ref, g2_ref,
                   kk_ref, ka_ref, bd_ref,
                   ro_ref, ko_ref, vo_ref, an_ref, bb_ref, lw_ref, gate_ref,
                   *, tile, seq, valid, prev_rows):
    i = pl.program_id(0)
    local = lax.broadcasted_iota(jnp.int32, (tile, 1), 0)
    pos = lax.rem(i * tile + local, seq)
    one_sequence = seq % tile == 0

    def shifted(cur_ref, prev_ref, s0_ref, mu):
        cur = cur_ref[...].astype(F32)
        if one_sequence:
            before = prev_ref[prev_rows - 1:prev_rows, :].astype(F32)
            edge = jnp.where(lax.rem(i * tile, seq) == 0, s0_ref[...], before)
            prev = jnp.where(local == 0, edge, pltpu.roll(cur, 1, 0))
        else:
            prev = jnp.where(pos == 0, s0_ref[...], pltpu.roll(cur, 1, 0))
        return cur + (prev - cur) * mu

    r = shifted(r_ref, pr_ref, s0r_ref, mu_ref[:, 0:1024])
    k = shifted(k_ref, pk_ref, s0k_ref, mu_ref[:, 1024:2048])
    v = shifted(v_ref, pv_ref, s0v_ref, mu_ref[:, 2048:3072])
    lo = shifted(l_ref, pl_ref, s0l_ref, mul_ref[...])

    wpre = w0_ref[...] + _mm(jnp.tanh(lo), w2_ref[...])
    wlog = -_softplus(-wpre) - 0.5
    lw = -jnp.exp(wlog)
    a = jax.nn.sigmoid(a0_ref[...] + _mm(lo, a2_ref[...]))
    gate = _mm(jax.nn.sigmoid(lo), g2_ref[...])

    kk = k * kk_ref[...]
    ss = _mm(kk * kk, bd_ref[...])
    kk = kk / jnp.maximum(jnp.sqrt(ss), 1e-12)
    kmod = k * (1.0 + (a - 1.0) * ka_ref[...])
    an = -kk
    bb = kk * a
    if valid is not None:
        ok = pos < valid
        lw = jnp.where(ok, lw, 0.0)
        an = jnp.where(ok, an, 0.0)
        bb = jnp.where(ok, bb, 0.0)
        kmod = jnp.where(ok, kmod, 0.0)
        v = jnp.where(ok, v, 0.0)
    ro_ref[...] = r.astype(ro_ref.dtype)
    ko_ref[...] = kmod.astype(ko_ref.dtype)
    vo_ref[...] = v.astype(vo_ref.dtype)
    an_ref[...] = an.astype(an_ref.dtype)
    bb_ref[...] = bb.astype(bb_ref.dtype)
    lw_ref[...] = lw
    gate_ref[...] = gate.astype(gate_ref.dtype)


def _rwkv_prep(grp, p2, shift0, wts):
    M, T = grp.M, grp.T
    tile = min(M, 512)
    assert T % tile == 0 or tile % T == 0
    prev_rows = min(16, tile)
    W = RW_HEADS * RW_HEAD
    s0 = grp.seq_operand(shift0)

    def cur(width, blk):
        return pl.BlockSpec((tile, width), lambda i: (i, blk))

    def prev(width, blk):
        return pl.BlockSpec((prev_rows, width), lambda i: (jnp.maximum(i * (tile // prev_rows) - 1, 0), blk))

    def full(shape):
        return pl.BlockSpec(shape, lambda i: (0,) * len(shape))

    out_spec = pl.BlockSpec((tile, W), lambda i: (i, 0))
    kern = functools.partial(_rwprep_kernel, tile=tile, seq=T, valid=None if grp.Tv == T else grp.Tv,
                             prev_rows=prev_rows)
    lora_blk = COL_LORA // 256
    return pl.pallas_call(
        kern,
        grid=(M // tile,),
        in_specs=[cur(W, 0), cur(W, 1), cur(W, 2), cur(256, lora_blk),
                  prev(W, 0), prev(W, 1), prev(W, 2), prev(256, lora_blk),
                  grp.seq_spec(W, 0, tile), grp.seq_spec(W, 1, tile), grp.seq_spec(W, 2, tile),
                  grp.seq_spec(256, 3 * W // 256, tile),
                  full((1, 3 * W)), full((1, 256)),
                  full((1, W)), full((256, W)), full((1, W)), full((256, W)), full((256, W)),
                  full((1, W)), full((1, W)), full((W, W))],
        out_specs=[out_spec] * 7,
        out_shape=[jax.ShapeDtypeStruct((M, W), BF16)] * 5
        + [jax.ShapeDtypeStruct((M, W), F32), jax.ShapeDtypeStruct((M, W), BF16)],
        compiler_params=_params(("arbitrary",)),
        name="rwkv_prep",
    )(p2, p2, p2, p2, p2, p2, p2, p2, s0, s0, s0, s0,
      wts["mu_rkv"], wts["mu_lora"], wts["rw_w0"], wts["w2p"], wts["rw_a0"], wts["a2p"], wts["g2p"],
      wts["rw_k_k"], wts["rw_k_a"], wts["head_ones"])


def _rwkv_pairs_chunk(at, rt, bt, kt, v, states, *, chunk, same_head, strict, incl):
    n2 = 2 * chunk
    wide = n2 % LANES == 0
    pairs = range(len(states))

    def blocks(x):
        return jnp.where(same_head, jnp.concatenate([x, x], axis=0), 0.0).astype(BF16)

    a2 = [blocks(x) for x in at]
    r2 = [blocks(x) for x in rt]
    b2 = [blocks(x) for x in bt]
    k2 = [blocks(x) for x in kt]
    v2 = [blocks(x) for x in v]
    ar = [jnp.concatenate([a2[p], r2[p]], axis=0) for p in pairs]
    bk = [jnp.concatenate([b2[p], k2[p]], axis=0) for p in pairs]
    if wide:
        g = [_mm_nt(ar[p], bk[p]) for p in pairs]
        l_ab = [jnp.where(strict, g[p][:n2, :n2], 0.0) for p in pairs]
        l_ak = [jnp.where(strict, g[p][:n2, n2:], 0.0) for p in pairs]
        m_rb = [jnp.where(incl, g[p][n2:, :n2], 0.0) for p in pairs]
        m_rk = [jnp.where(incl, g[p][n2:, n2:], 0.0) for p in pairs]
    else:
        l_ab = [jnp.where(strict, _mm_nt(a2[p], b2[p]), 0.0) for p in pairs]
        l_ak = [jnp.where(strict, _mm_nt(a2[p], k2[p]), 0.0) for p in pairs]
        m_rb = [jnp.where(incl, _mm_nt(r2[p], b2[p]), 0.0) for p in pairs]
        m_rk = [jnp.where(incl, _mm_nt(r2[p], k2[p]), 0.0) for p in pairs]
    from_state = [_mm_nt(ar[p], states[p]) for p in pairs]
    x = [from_state[p][:n2] + _mm(l_ak[p], v2[p]) for p in pairs]
    pw = l_ab
    for _ in range(chunk.bit_length() - 2):
        if wide:
            y = [_mm(pw[p], jnp.concatenate([x[p], pw[p]], axis=1)) for p in pairs]
            x = [x[p] + y[p][:, :LANES] for p in pairs]
            pw = [y[p][:, LANES:] for p in pairs]
        else:
            x = [x[p] + _mm(pw[p], x[p]) for p in pairs]
            pw = [_mm(pw[p], pw[p]) for p in pairs]
    u = [x[p] + _mm(pw[p], x[p]) for p in pairs]
    uv = [jnp.concatenate([u[p].astype(BF16), v2[p]], axis=0) for p in pairs]
    if wide:
        o2 = [from_state[p][n2:] + _mm(jnp.concatenate([m_rb[p], m_rk[p]], axis=1), uv[p]) for p in pairs]
    else:
        o2 = [from_state[p][n2:] + _mm(m_rb[p], u[p]) + _mm(m_rk[p], v2[p]) for p in pairs]
    s_new = [states[p] + _mm_tn(uv[p], bk[p]) for p in pairs]
    return o2, s_new


def _rwkv_kernel(r_ref, k_ref, v_ref, an_ref, bb_ref, lw_ref, gw_ref, gb_ref, rk_ref, s0_ref,
                 o_ref, sout_ref, s_scr, *, chunk, seqs):
    t = pl.program_id(1)
    zeros = jnp.zeros((RW_HEAD, RW_HEAD), F32)

    @pl.when(t == 0)
    def _():
        for q in range(seqs):
            for p in range(RW_PAIRS):
                top = jnp.concatenate([s0_ref[q, 2 * p], zeros], axis=1)
                bot = jnp.concatenate([zeros, s0_ref[q, 2 * p + 1]], axis=1)
                s_scr[q, p] = jnp.concatenate([top, bot], axis=0)

    n2 = 2 * chunk
    row = lax.broadcasted_iota(jnp.int32, (n2, n2), 0)
    col = lax.broadcasted_iota(jnp.int32, (n2, n2), 1)

    def second(idx, size):
        return jnp.where(idx >= size, 1, 0)

    same = second(row, chunk) == second(col, chunk)
    strict = same & (row > col)
    incl = same & (row >= col)
    same_head = (second(lax.broadcasted_iota(jnp.int32, (n2, LANES), 0), chunk)
                 == second(lax.broadcasted_iota(jnp.int32, (n2, LANES), 1), RW_HEAD))
    trow = lax.broadcasted_iota(jnp.int32, (chunk, chunk), 0)
    tcol = lax.broadcasted_iota(jnp.int32, (chunk, chunk), 1)
    tril = jnp.where(trow >= tcol, 1.0, 0.0).astype(BF16)
    first_head = lax.broadcasted_iota(jnp.int32, (chunk, LANES), 1) < RW_HEAD

    def head_sum(x):
        lo = jnp.sum(jnp.where(first_head, x, 0.0), axis=-1, keepdims=True)
        hi = jnp.sum(jnp.where(first_head, 0.0, x), axis=-1, keepdims=True)
        return jnp.where(first_head, lo, hi)

    lanes = [slice(p * LANES, (p + 1) * LANES) for p in range(RW_PAIRS)]
    problems = [(q, p) for q in range(seqs) for p in range(RW_PAIRS)]
    rs, ks, vs, ats, rts, bts, kts, ends = [], [], [], [], [], [], [], []
    for q in range(seqs):
        lw = lw_ref[q]
        cum = _mm_exact_lhs(tril, lw)
        w_incl = jnp.exp(cum)
        w_inv = jnp.exp(-cum)
        r, k = r_ref[q].astype(F32), k_ref[q].astype(F32)
        rs.append(r)
        ks.append(k)
        vs.append(v_ref[q].astype(F32))
        ats.append(an_ref[q].astype(F32) * jnp.exp(cum - lw))
        rts.append(r * w_incl)
        bts.append(bb_ref[q].astype(F32) * w_inv)
        kts.append(k * w_inv)
        ends.append(w_incl[chunk - 1:chunk, :])

    def slabs(xs):
        return [xs[q][:, lanes[p]] for q, p in problems]

    o2, s_new = _rwkv_pairs_chunk(
        slabs(ats), slabs(rts), slabs(bts), slabs(kts), slabs(vs), [s_scr[q, p] for q, p in problems],
        chunk=chunk, same_head=same_head, strict=strict, incl=incl)
    for n, (q, p) in enumerate(problems):
        ls = lanes[p]
        s_scr[q, p] = s_new[n] * ends[q][:, ls]
        o = o2[n][:chunk] + o2[n][chunk:]
        mu = head_sum(o) * (1.0 / RW_HEAD)
        oc = o - mu
        var = head_sum(oc * oc) * (1.0 / RW_HEAD)
        on = oc * lax.rsqrt(var + RW_GN_EPS) * gw_ref[:, ls] + gb_ref[:, ls]
        bonus = head_sum(rs[q][:, ls] * ks[q][:, ls] * rk_ref[:, ls]) * vs[q][:, ls]
        o_ref[q, :, ls] = (on + bonus).astype(o_ref.dtype)

    @pl.when(t == pl.num_programs(1) - 1)
    def _():
        for q in range(seqs):
            for p in range(RW_PAIRS):
                s = s_scr[q, p]
                sout_ref[q, 2 * p] = s[:RW_HEAD, :RW_HEAD]
                sout_ref[q, 2 * p + 1] = s[RW_HEAD:, RW_HEAD:]


def _rwkv(grp, seqs, gn_w, gn_b, r_k, s0):
    B, T = grp.B, grp.T
    chunk = min(T, CHUNK)
    nb = max(_seqs_per_step(B, T), 4 if B % 4 == 0 else 2 - B % 2)
    W = RW_HEADS * RW_HEAD
    seq_spec = pl.BlockSpec((nb, chunk, W), lambda b, t: (b, t, 0))
    vec_spec = pl.BlockSpec((1, W), lambda b, t: (0, 0))
    st_spec = pl.BlockSpec((nb, RW_HEADS, RW_HEAD, RW_HEAD), lambda b, t: (b, 0, 0, 0))
    kern = functools.partial(_rwkv_kernel, chunk=chunk, seqs=nb)
    return pl.pallas_call(
        kern,
        grid=(B // nb, T // chunk),
        in_specs=[seq_spec] * 6 + [vec_spec] * 3 + [st_spec],
        out_specs=[seq_spec, st_spec],
        out_shape=[jax.ShapeDtypeStruct((B, T, W), BF16),
                   jax.ShapeDtypeStruct((B, RW_HEADS, RW_HEAD, RW_HEAD), F32)],
        scratch_shapes=[pltpu.VMEM((nb, RW_PAIRS, 2 * RW_HEAD, 2 * RW_HEAD), F32)],
        compiler_params=_params(("arbitrary", "arbitrary")),
        name="rwkv_scan",
    )(*seqs, gn_w, gn_b, r_k, s0)


def _merge_kernel(og_ref, gg_ref, orw_ref, gate_ref, mga_ref, mgb_ref, x_ref,
                  gate1_ref, shift2_ref, scale2_ref,
                  ggn_ref, gpost_ref, gpre_ref, wbg_ref, wbr_ref, wout_ref, wq_ref,
                  x1_ref, h2t_ref, qp_ref):
    gg = gg_ref[...].astype(F32)
    gla_in = og_ref[...].astype(F32) * ggn_ref[...] * (gg * jax.nn.sigmoid(gg))
    br_gla = _mm(gla_in, wbg_ref[...])
    br_rw = _mm(orw_ref[...].astype(F32) * gate_ref[...].astype(F32), wbr_ref[...])
    merged = (jax.nn.sigmoid(mga_ref[...].astype(F32)) * br_gla
              + jax.nn.sigmoid(mgb_ref[...].astype(F32)) * br_rw)
    y = _mm(merged, wout_ref[...])
    x1 = x_ref[...] + gate1_ref[...] * _rms(y, gpost_ref[...])
    x1_ref[...] = x1
    h2 = _rms(x1, gpre_ref[...]) * (1.0 + scale2_ref[...]) + shift2_ref[...]
    h2t_ref[...] = h2.T.astype(BF16)
    qp_ref[...] = jnp.dot(h2.astype(BF16), wq_ref[...], preferred_element_type=F32).astype(BF16)


def _merge(grp, og, p2, orw, gate_rw, x2d, modop, wts):
    tm, M = grp.tm, grp.M
    W = D_MODEL

    def tok(blk):
        return pl.BlockSpec((tm, W), lambda i: (i, blk))

    def full(shape):
        return pl.BlockSpec(shape, lambda i: (0,) * len(shape), pipeline_mode=pl.Buffered(1))

    return pl.pallas_call(
        _merge_kernel,
        grid=(M // tm,),
        in_specs=[tok(0), tok(COL_GG // W), tok(0), tok(0), tok(COL_MG // W), tok(COL_MG // W + 1), tok(0),
                  grp.mod_spec(2, tm), grp.mod_spec(3, tm), grp.mod_spec(4, tm),
                  full((1, W)), full((1, W)), full((1, W)),
                  full((W, W)), full((W, W)), full((W, W)), full((W, 2 * W))],
        out_specs=[tok(0), pl.BlockSpec((W, tm), lambda i: (0, i)), pl.BlockSpec((tm, 2 * W), lambda i: (i, 0))],
        out_shape=[jax.ShapeDtypeStruct((M, W), F32), jax.ShapeDtypeStruct((W, M), BF16),
                   jax.ShapeDtypeStruct((M, 2 * W), BF16)],
        compiler_params=_params(("arbitrary",)),
        name="merge",
    )(og, p2, orw, gate_rw, p2, p2, x2d, modop, modop, modop,
      wts["g_gla_norm"], wts["g_post_mix"], wts["g_pre_ffn"],
      wts["w_b_gla"], wts["w_b_rw"], wts["w_out"], wts["peer_wq"])


def _pair_bits(x):
    bits = pltpu.bitcast(x.astype(BF16).astype(F32), jnp.uint32)
    return bits | (bits >> 16)


def _row_tile(row, rows):
    tile = pltpu.bitcast(jnp.broadcast_to(row, (8, row.shape[1])), BF16)
    return jnp.tile(tile, (rows // 16, 1))


def _batcher_network(n):
    def merge(lo, hi, r):
        step = r * 2
        if step < hi - lo:
            yield from merge(lo, hi, step)
            yield from merge(lo + r, hi, step)
            yield from [(i, i + r) for i in range(lo + r, hi - r, step)]
        else:
            yield (lo, lo + r)

    def sort(lo, hi):
        if hi - lo >= 1:
            mid = lo + (hi - lo) // 2
            yield from sort(lo, mid)
            yield from sort(mid + 1, hi)
            yield from merge(lo, hi, 1)

    return list(sort(0, n - 1))


def _top_values(s, n):
    sub = 8
    depth = s.shape[0] // sub
    cols = [s[i * sub:(i + 1) * sub, :] for i in range(depth)]
    for i, j in _batcher_network(depth):
        cols[i], cols[j] = jnp.maximum(cols[i], cols[j]), jnp.minimum(cols[i], cols[j])
    rows = lax.broadcasted_iota(jnp.int32, (n, s.shape[1]), 0)
    vals = jnp.zeros((n, s.shape[1]), F32)
    for it in range(n):
        m = jnp.max(cols[0], axis=0, keepdims=True)
        vals = jnp.where(rows == it, m, vals)
        hit = cols[0] == m
        for i in range(min(depth - 1, n - 1 - it)):
            cols[i] = jnp.where(hit, cols[i + 1], cols[i])
    return vals


def _count_prefix(vals, pred):
    def row(i):
        return vals[i:i + 1, :]

    c8 = pred(row(7))
    c4 = pred(jnp.where(c8, row(11), row(3)))
    c2 = pred(jnp.where(c8, jnp.where(c4, row(13), row(9)), jnp.where(c4, row(5), row(1))))
    upper = jnp.where(c4, jnp.where(c2, row(14), row(12)), jnp.where(c2, row(10), row(8)))
    lower = jnp.where(c4, jnp.where(c2, row(6), row(4)), jnp.where(c2, row(2), row(0)))
    c1 = pred(jnp.where(c8, upper, lower))
    c16 = pred(row(15))
    return (jnp.where(c8, 8.0, 0.0) + jnp.where(c4, 4.0, 0.0) + jnp.where(c2, 2.0, 0.0)
            + jnp.where(c1, 1.0, 0.0) + jnp.where(c16, 1.0, 0.0))


def _rank_among(s, vals):
    return _count_prefix(vals, lambda x: x > s)


def _candidate_groups(v1, v2):
    n, tm = v1.shape
    split = 4
    r_all = lax.broadcasted_iota(jnp.int32, (n, tm), 0)
    r_half = lax.broadcasted_iota(jnp.int32, (n // 2, tm), 0)
    groups = [v1[0:1, :] + v2]
    for a in range(1, split):
        groups.append(jnp.where(r_half < n // (a + 1), v1[a:a + 1, :] + v2[:n // 2, :], NEG_INF))
    groups.append(jnp.where(r_all >= split, v1 + v2[0:1, :], NEG_INF))
    for b in range(1, n // (split + 1)):
        keep = (r_half >= split) & (r_half < n // (b + 1))
        groups.append(jnp.where(keep, v1[:n // 2, :] + v2[b:b + 1, :], NEG_INF))
    return groups


def _group_max(groups, n):
    full = [g for g in groups if g.shape[0] == n]
    best = full[0]
    for g in full[1:]:
        best = jnp.maximum(best, g)
    best = jnp.maximum(best[:n // 2, :], best[n // 2:, :])
    for g in groups:
        if g.shape[0] != n:
            best = jnp.maximum(best, g)
    return jnp.max(best, axis=0, keepdims=True)


def _select_kernel(qp_ref, keys_ref, r2_ref, e2_ref, al_ref, n1_ref):
    tm = qp_ref.shape[0]
    for h in range(PEER_HEADS):
        q1 = qp_ref[:, (2 * h) * 128:(2 * h + 1) * 128]
        q2 = qp_ref[:, (2 * h + 1) * 128:(2 * h + 2) * 128]
        s1 = _mm_nt(keys_ref[2 * h], q1)
        s2 = _mm_nt(keys_ref[2 * h + 1], q2)
        v1 = _top_values(s1, PEER_TOPK)
        v2 = _top_values(s2, PEER_TOPK)
        rank2 = _rank_among(s2, v2)
        groups = _candidate_groups(v1, v2)
        cmax = v1[0:1, :] + v2[0:1, :]
        z = jnp.zeros((1, tm), F32)
        m = cmax
        for _ in range(PEER_TOPK):
            m = _group_max(groups, PEER_TOPK)
            z = z + jnp.exp(m - cmax)
            groups = [jnp.where(g == m, NEG_INF, g) for g in groups]
        th = m - s1
        n1 = _count_prefix(v2, lambda x: x >= th)
        r2_ref[h] = rank2.astype(BF16)
        e2_ref[h] = jnp.exp(s2 - v2[0:1, :]).astype(BF16)
        al_ref[h] = _pair_bits(jnp.exp(s1 - v1[0:1, :]) / z)
        n1_ref[h] = _pair_bits(n1)


def _peer_select(grp, qp, keys):
    tm, M = grp.tm, grp.M
    rows = pl.BlockSpec((PEER_HEADS, PEER_NKEYS, tm), lambda i: (0, 0, i))
    return pl.pallas_call(
        _select_kernel,
        grid=(M // tm,),
        in_specs=[pl.BlockSpec((tm, 2 * D_MODEL), lambda i: (i, 0)),
                  pl.BlockSpec((2 * PEER_HEADS, PEER_NKEYS, 128), lambda i: (0, 0, 0))],
        out_specs=[rows] * 4,
        out_shape=[jax.ShapeDtypeStruct((PEER_HEADS, PEER_NKEYS, M), BF16)] * 2
        + [jax.ShapeDtypeStruct((PEER_HEADS, PEER_NKEYS, M), jnp.uint32)] * 2,
        compiler_params=_params(("arbitrary",)),
        name="peer_select",
    )(qp, keys)


def _peer_weights(act_ref, y_ref, r2_ref, e2_ref, al_ref, n1_ref, row0):
    tm = act_ref.shape[1]
    half = PEER_NKEYS // 2
    for j in range(act_ref.shape[0] // PEER_NKEYS):
        i1 = slice(row0 + j, row0 + j + 1)
        w = [jnp.zeros((half, tm), BF16) for _ in range(2)]
        for h in range(PEER_HEADS):
            n1 = _row_tile(n1_ref[h, i1, :], half)
            al = _row_tile(al_ref[h, i1, :], half)
            for r in range(2):
                keys = slice(r * half, (r + 1) * half)
                w[r] = w[r] + jnp.where(r2_ref[h, keys, :] < n1, e2_ref[h, keys, :], 0.0) * al
        for r in range(2):
            rows = slice(j * PEER_NKEYS + r * half, j * PEER_NKEYS + (r + 1) * half)
            a = act_ref[rows, :].astype(BF16)
            y_ref[rows, :] = (0.5 * a * (1.0 + lax.erf(a * 0.7071067811865476))) * w[r]


def _peer_kernel(h2t_ref, u_ref, vt_ref, r2_ref, e2_ref, al_ref, n1_ref, x1_ref, gate2_ref, gpost_ref,
                 o_ref, acc_ref, act_0, act_1, y_0, y_1):
    e = pl.program_id(1)
    ts = act_0.shape[0]
    per = ts // PEER_NKEYS
    acts, ys = (act_0, act_1), (y_0, y_1)

    @pl.when(e == 0)
    def _():
        acc_ref[...] = jnp.zeros_like(acc_ref)

    def scores(k):
        acts[k % 2][...] = jnp.dot(u_ref[k], h2t_ref[...], preferred_element_type=F32)

    def mix(k):
        acc_ref[...] += jnp.dot(vt_ref[k], ys[k % 2][...], preferred_element_type=F32)

    scores(0)
    scores(1)
    for k in range(PEER_SUBTILES):
        _peer_weights(acts[k % 2], ys[k % 2], r2_ref, e2_ref, al_ref, n1_ref, k * per)
        if k + 2 < PEER_SUBTILES:
            scores(k + 2)
        if k >= 1:
            mix(k - 1)
    mix(PEER_SUBTILES - 1)

    @pl.when(e == pl.num_programs(1) - 1)
    def _():
        o_ref[...] = x1_ref[...] + gate2_ref[...] * _rms(acc_ref[...].T, gpost_ref[...])


def _peer(grp, h2t, sel, x1, modop, g_post, pu, pvt):
    tm, M = grp.tm, grp.M
    ts = PEER_EXPERT_TILE
    te = ts * PEER_SUBTILES
    r2, e2, al, n1 = sel
    packed = pl.BlockSpec((PEER_HEADS, PEER_NKEYS, tm), lambda i, e: (0, 0, i))
    rows = pl.BlockSpec((PEER_HEADS, te // PEER_NKEYS, tm), lambda i, e: (0, e, i))
    tok = pl.BlockSpec((tm, D_MODEL), lambda i, e: (i, 0))
    return pl.pallas_call(
        _peer_kernel,
        grid=(M // tm, pu.shape[0] // PEER_SUBTILES),
        in_specs=[pl.BlockSpec((D_MODEL, tm), lambda i, e: (0, i)),
                  pl.BlockSpec((PEER_SUBTILES, ts, D_MODEL), lambda i, e: (e, 0, 0)),
                  pl.BlockSpec((PEER_SUBTILES, D_MODEL, ts), lambda i, e: (e, 0, 0)),
                  packed, packed, rows, rows,
                  pl.BlockSpec((tm, D_MODEL), lambda i, e: (i, 0), pipeline_mode=pl.Buffered(1)),
                  grp.mod_spec(5),
                  pl.BlockSpec((1, D_MODEL), lambda i, e: (0, 0))],
        out_specs=tok,
        out_shape=jax.ShapeDtypeStruct((M, D_MODEL), F32),
        scratch_shapes=[pltpu.VMEM((D_MODEL, tm), F32),
                        pltpu.VMEM((ts, tm), F32), pltpu.VMEM((ts, tm), F32),
                        pltpu.VMEM((ts, tm), BF16), pltpu.VMEM((ts, tm), BF16)],
        compiler_params=_params(("arbitrary", "arbitrary")),
        name="peer_dense",
    )(h2t, pu, pvt, r2, e2, al, n1, x1, modop, g_post)


def _transpose_cast_kernel(x_ref, o_ref):
    rows = o_ref.shape[2]
    for t in range(o_ref.shape[0]):
        o_ref[t] = x_ref[t * rows:(t + 1) * rows, :].T.astype(o_ref.dtype)


def _transposed_tiles(x, rows, per_step=2):
    n, d = x.shape
    return pl.pallas_call(
        _transpose_cast_kernel,
        grid=(n // (rows * per_step),),
        in_specs=[pl.BlockSpec((rows * per_step, d), lambda k: (k, 0))],
        out_specs=pl.BlockSpec((per_step, d, rows), lambda k: (k, 0, 0)),
        out_shape=jax.ShapeDtypeStruct((n // rows, d, rows), BF16),
        compiler_params=_params(("arbitrary",)),
        name="transpose_cast",
    )(x)


def _prepare_weights(w_in, w_gla_gate, b_gla_gate, g_gla_norm, mu_shift, rw_w0, rw_w2, rw_a0, rw_a2,
                     rw_g2, rw_k_k, rw_k_a, rw_r_k, rw_gn_w, rw_gn_b, w_b_gla, w_b_rw, w_out,
                     peer_wq, peer_keys, peer_u, peer_v, g_pre_mix, g_post_mix, g_pre_ffn, g_post_ffn):
    W = RW_HEADS * RW_HEAD
    rw = w_in[:, 3088:6416]
    w_proj = jnp.concatenate(
        [rw[:, :3 * W], w_in[:, 6416:8464], w_in[:, 2048:3072], w_in[:, 1024:2048], w_in[:, 0:1024],
         rw[:, 3 * W:], w_in[:, 3072:3088], jnp.zeros((D_MODEL, PROJ_PAD_W - 8464), F32)],
        axis=1).astype(BF16)
    head = jnp.arange(W) // RW_HEAD
    zeros64 = jnp.zeros((64, W), F32)
    return dict(
        w_proj=w_proj,
        wg_pad=jnp.concatenate([w_gla_gate, jnp.zeros((128 - GLA_GATE_RANK, 512), F32)], axis=0),
        bg=b_gla_gate.reshape(1, -1),
        g_gla_norm=g_gla_norm.reshape(1, -1),
        mu_rkv=mu_shift[:3 * W].reshape(1, -1),
        mu_lora=mu_shift[3 * W:].reshape(1, -1),
        rw_w0=rw_w0.reshape(1, -1),
        w2p=jnp.concatenate([rw_w2, jnp.zeros((192, W), F32)], axis=0),
        rw_a0=rw_a0.reshape(1, -1),
        a2p=jnp.concatenate([zeros64, rw_a2, jnp.zeros((128, W), F32)], axis=0),
        g2p=jnp.concatenate([jnp.zeros((128, W), F32), rw_g2], axis=0),
        rw_k_k=rw_k_k.reshape(1, -1),
        rw_k_a=rw_k_a.reshape(1, -1),
        rw_r_k=rw_r_k.reshape(1, -1),
        rw_gn_w=rw_gn_w.reshape(1, -1),
        rw_gn_b=rw_gn_b.reshape(1, -1),
        head_ones=(head[:, None] == head[None, :]).astype(BF16),
        w_b_gla=w_b_gla.astype(BF16),
        w_b_rw=w_b_rw.astype(BF16),
        w_out=w_out.astype(BF16),
        peer_wq=peer_wq.astype(BF16),
        peer_keys=peer_keys.reshape(2 * PEER_HEADS, PEER_NKEYS, 128).astype(BF16),
        peer_u=peer_u.astype(BF16).reshape(-1, PEER_EXPERT_TILE, D_MODEL),
        peer_vt=_transposed_tiles(peer_v, PEER_EXPERT_TILE),
        g_pre_mix=g_pre_mix.reshape(1, -1),
        g_post_mix=g_post_mix.reshape(1, -1),
        g_pre_ffn=g_pre_ffn.reshape(1, -1),
        g_post_ffn=g_post_ffn.reshape(1, -1),
    )


def _block(x, mod, s_gla0, s_rw0, shift0, wts):
    B, valid, _ = x.shape
    T = -(-valid // 8) * 8
    x = jnp.pad(x, ((0, 0), (0, T - valid), (0, 0)))
    grp = _Group(B, T, valid)
    W = RW_HEADS * RW_HEAD
    x2d = x.reshape(grp.M, D_MODEL)
    modop = grp.seq_operand(mod)
    p2 = _projection(grp, x2d, modop, wts["g_pre_mix"], wts["w_proj"], BF16)
    p3 = p2.reshape(B, T, PROJ_PAD_W)
    og, s_gla = _gla(grp, p3, wts["wg_pad"], wts["bg"], s_gla0)
    r, k, v, an, bb, lw, gate = _rwkv_prep(grp, p2, shift0, wts)
    seqs = tuple(z.reshape(B, T, W) for z in (r, k, v, an, bb, lw))
    orw, s_rw = _rwkv(grp, seqs, wts["rw_gn_w"], wts["rw_gn_b"], wts["rw_r_k"], s_rw0)
    x1, h2t, qp = _merge(grp, og.reshape(grp.M, -1), p2, orw.reshape(grp.M, W), gate,
                        x2d, modop, wts)
    if valid != T:
        grp = _Group(B, valid, valid)
        modop = grp.seq_operand(mod)
        x1 = x1.reshape(B, T, D_MODEL)[:, :valid].reshape(grp.M, D_MODEL)
        h2t = h2t.reshape(D_MODEL, B, T)[:, :, :valid].reshape(D_MODEL, grp.M)
        qp = qp.reshape(B, T, -1)[:, :valid].reshape(grp.M, -1)
    sel = _peer_select(grp, qp, wts["peer_keys"])
    out = _peer(grp, h2t, sel, x1, modop, wts["g_post_ffn"], wts["peer_u"], wts["peer_vt"])
    ends = _Group(B, 1, 1)
    last = _projection(ends, x[:, valid - 1, :], ends.seq_operand(mod), wts["g_pre_mix"], wts["w_proj"], F32)
    shift_new = jnp.concatenate([last[:, :3 * W], last[:, COL_LORA:COL_LORA + 256]], axis=-1)
    return out.reshape(B, grp.T, D_MODEL)[:, :valid], s_gla, s_rw, shift_new


def kernel(x_prompt, x_sample, c_prompt, c_sample, state_gla, state_rwkv, state_shift, w_ada, b_ada, g_pre_mix, g_post_mix, g_pre_ffn, g_post_ffn, w_in, w_gla_gate, b_gla_gate, g_gla_norm, mu_shift, rw_w0, rw_w2, rw_a0, rw_a2, rw_g2, rw_k_k, rw_k_a, rw_r_k, rw_gn_w, rw_gn_b, w_b_gla, w_b_rw, w_out, peer_wq, peer_keys, peer_u, peer_v):
    depth = w_in.shape[0]
    bp = x_prompt.shape[0]
    xp, xs = x_prompt, x_sample
    outs = [[] for _ in range(6)]
    for l in range(depth):
        wts = _prepare_weights(
            w_in[l], w_gla_gate[l], b_gla_gate[l], g_gla_norm[l], mu_shift[l], rw_w0[l], rw_w2[l],
            rw_a0[l], rw_a2[l], rw_g2[l], rw_k_k[l], rw_k_a[l], rw_r_k[l], rw_gn_w[l], rw_gn_b[l],
            w_b_gla[l], w_b_rw[l], w_out[l], peer_wq[l], peer_keys[l], peer_u[l], peer_v[l],
            g_pre_mix[l], g_post_mix[l], g_pre_ffn[l], g_post_ffn[l])
        mod = _modulation(jnp.concatenate([c_prompt, c_sample], axis=0), w_ada[l], b_ada[l])
        zg = jnp.zeros((bp, GLA_HEADS, GLA_DK, GLA_DV), F32)
        zr = jnp.zeros((bp, RW_HEADS, RW_HEAD, RW_HEAD), F32)
        zs = jnp.zeros((bp, RW_SHIFT_W), F32)
        xp, g1, r1, s1 = _block(xp, mod[:bp], zg, zr, zs, wts)
        xs, g2, r2, s2 = _block(xs, mod[bp:], state_gla[l], state_rwkv[l], state_shift[l], wts)
        for acc, val in zip(outs, (g1, r1, s1, g2, r2, s2)):
            acc.append(val)
    stacked = [jnp.stack(o) for o in outs]
    return (xp, xs, *stacked)
```

```python
import functools

import jax
import jax.numpy as jnp
from jax import lax
from jax.experimental import pallas as pl
from jax.experimental.pallas import tpu as pltpu

F32 = jnp.float32
BF16 = jnp.bfloat16

D_MODEL = 1024
GLA_HEADS = 4
GLA_DK = 128
GLA_DV = 256
GLA_GATE_RANK = 16
GLA_GATE_TAU = 16.0
GLA_NORM_EPS = 1e-5
RW_HEAD = 64
RW_HEADS = 16
RW_PAIRS = RW_HEADS // 2
RW_GN_EPS = 64e-5
RW_SHIFT_W = 3328
PEER_HEADS = 8
PEER_NKEYS = 128
PEER_TOPK = 16
NORM_EPS = 1e-6
LANES = 128

COL_R, COL_K, COL_V = 0, 1024, 2048
COL_MG = 3072
COL_GG = 5120
COL_GV = 6144
COL_GQ = 7168
COL_GK = 7680
COL_LORA = 8192
COL_GA = 8448
PROJ_PAD_W = 8704
PROJ_TILE_N = 4352
PROJ_TILE_M = 1024

CHUNK = 64
TOKEN_TILE = 512
PEER_EXPERT_TILE = 512
PEER_SUBTILES = 4
VMEM_LIMIT = 56 * 1024 * 1024

NEG_INF = float("-inf")


def _mm(a, b):
    return jnp.dot(a.astype(BF16), b.astype(BF16), preferred_element_type=F32)


def _mm_nt(a, b):
    return lax.dot_general(a.astype(BF16), b.astype(BF16), (((1,), (1,)), ((), ())),
                           preferred_element_type=F32)


def _mm_tn(a, b):
    return lax.dot_general(a.astype(BF16), b.astype(BF16), (((0,), (0,)), ((), ())),
                           preferred_element_type=F32)


def _split(x):
    hi = x.astype(BF16)
    lo = (x - hi.astype(F32)).astype(BF16)
    return hi, lo


def _mm_exact_lhs(a, x):
    hi, lo = _split(x)
    a = a.astype(BF16)
    return (jnp.dot(a, hi, preferred_element_type=F32)
            + jnp.dot(a, lo, preferred_element_type=F32))


def _mm_exact_rhs(x, a):
    hi, lo = _split(x)
    a = a.astype(BF16)
    return (jnp.dot(hi, a, preferred_element_type=F32)
            + jnp.dot(lo, a, preferred_element_type=F32))


def _softplus(x):
    return jnp.maximum(x, 0.0) + jnp.log1p(jnp.exp(-jnp.abs(x)))


def _rms(x, g):
    return x * lax.rsqrt(jnp.mean(x * x, axis=-1, keepdims=True) + NORM_EPS) * g


def _params(sem, flags=None):
    return pltpu.CompilerParams(dimension_semantics=sem, vmem_limit_bytes=VMEM_LIMIT, flags=flags)


def _mod_kernel(c_ref, w_ref, b_ref, o_ref):
    c = c_ref[...]
    o_ref[...] = _mm(c * jax.nn.sigmoid(c), w_ref[...]) + b_ref[...]


def _modulation(c, w_ada, b_ada):
    rows = c.shape[0]
    n = w_ada.shape[1]
    tn = 1536
    return pl.pallas_call(
        _mod_kernel,
        grid=(n // tn,),
        in_specs=[pl.BlockSpec((rows, D_MODEL), lambda j: (0, 0)),
                  pl.BlockSpec((D_MODEL, tn), lambda j: (0, j)),
                  pl.BlockSpec((1, tn), lambda j: (0, j))],
        out_specs=pl.BlockSpec((rows, tn), lambda j: (0, j)),
        out_shape=jax.ShapeDtypeStruct((rows, n), F32),
        compiler_params=_params(("arbitrary",)),
        name="adaln_mod",
    )(c, w_ada, b_ada.reshape(1, n))


class _Group:
    def __init__(self, batch, seq, valid):
        self.B, self.T, self.Tv = batch, seq, valid
        self.M = batch * seq
        self.tm = min(TOKEN_TILE, self.M)
        self.per_batch_mod = seq % self.tm == 0

    def seq_operand(self, arr):
        if self.per_batch_mod:
            return arr.reshape(self.B, 1, arr.shape[-1])
        return jnp.repeat(arr, self.T, axis=0)

    def seq_spec(self, width, col, tm=None, tile_index=None):
        tm, T = tm or self.tm, self.T
        tile = tile_index or (lambda i, *_: i)
        if self.per_batch_mod:
            return pl.BlockSpec((None, 1, width), lambda *g: ((tile(*g) * tm) // T, 0, col))
        return pl.BlockSpec((tm, width), lambda *g: (tile(*g), col))

    def mod_spec(self, col, tm=None, tile_index=None):
        return self.seq_spec(D_MODEL, col, tm, tile_index)


def _proj_kernel(x_ref, shift_ref, scale_ref, g_ref, w_ref, p_ref, h_scr):
    @pl.when(pl.program_id(1) == 0)
    def _():
        h = _rms(x_ref[...], g_ref[...]) * (1.0 + scale_ref[...]) + shift_ref[...]
        h_scr[...] = h.astype(BF16)

    p_ref[...] = jnp.dot(h_scr[...], w_ref[...], preferred_element_type=F32).astype(p_ref.dtype)


def _projection(grp, x2d, modop, g_pre, w_proj, dtype):
    tm = PROJ_TILE_M if grp.per_batch_mod and grp.T % PROJ_TILE_M == 0 else grp.tm
    return pl.pallas_call(
        _proj_kernel,
        grid=(grp.M // tm, PROJ_PAD_W // PROJ_TILE_N),
        in_specs=[pl.BlockSpec((tm, D_MODEL), lambda i, j: (i, 0)),
                  grp.mod_spec(0, tm), grp.mod_spec(1, tm),
                  pl.BlockSpec((1, D_MODEL), lambda i, j: (0, 0)),
                  pl.BlockSpec((D_MODEL, PROJ_TILE_N), lambda i, j: (0, j))],
        out_specs=pl.BlockSpec((tm, PROJ_TILE_N), lambda i, j: (i, j)),
        out_shape=jax.ShapeDtypeStruct((grp.M, PROJ_PAD_W), dtype),
        scratch_shapes=[pltpu.VMEM((tm, D_MODEL), BF16)],
        compiler_params=_params(("arbitrary", "arbitrary")),
        name="in_proj",
    )(x2d, modop, modop, g_pre, w_proj)


def _gla_kernel(q_ref, k_ref, v_ref, ga_ref, wg_ref, bg_ref, s0_ref, og_ref, sout_ref, st_scr,
                *, chunk, n_chunks, tile, valid, seqs):
    t = pl.program_id(1)
    last = pl.num_programs(1) - 1
    row = lax.broadcasted_iota(jnp.int32, (tile, tile), 0)
    col = lax.broadcasted_iota(jnp.int32, (tile, tile), 1)
    shift = chunk.bit_length() - 1
    causal = (lax.shift_right_logical(row, shift) == lax.shift_right_logical(col, shift)) & (row >= col)
    tril = jnp.where(causal, 1.0, 0.0).astype(BF16)

    batch, heads, chunks = range(seqs), range(GLA_HEADS), range(n_chunks)
    pairs = [(b, h) for b in batch for h in heads]
    ks = [slice(h * GLA_DK, (h + 1) * GLA_DK) for h in heads]
    vs = [slice(h * GLA_DV, (h + 1) * GLA_DV) for h in heads]
    rows = [slice(c * chunk, (c + 1) * chunk) for c in chunks]
    ends = [slice((c + 1) * chunk - 1, (c + 1) * chunk) for c in chunks]

    @pl.when(t == 0)
    def _():
        for b, h in pairs:
            st_scr[b, h] = s0_ref[b, h].T

    z = [_mm(ga_ref[b], wg_ref[...]) + bg_ref[...] for b in batch]
    la = [-_softplus(-z[b]) / GLA_GATE_TAU for b in batch]
    k = [k_ref[b].astype(F32) for b in batch]
    v = {(b, h): v_ref[b, :, vs[h]] for b, h in pairs}
    if valid is not None:
        ok = (t * tile + lax.broadcasted_iota(jnp.int32, (tile, 1), 0)) < valid
        la = [jnp.where(ok, x, 0.0) for x in la]
        k = [jnp.where(ok, x, 0.0) for x in k]
        v = {key: jnp.where(ok, x, 0.0) for key, x in v.items()}
    cum = [_mm_exact_lhs(tril, la[b]) for b in batch]
    end = [jnp.concatenate([jnp.broadcast_to(cum[b][ends[c], :], (chunk, cum[b].shape[1])) for c in chunks], axis=0)
           for b in batch]
    qe = [q_ref[b].astype(F32) * (GLA_DK ** -0.5) * jnp.exp(cum[b]) for b in batch]
    ke = [k[b] * jnp.exp(-cum[b]) for b in batch]
    kd = [k[b] * jnp.exp(end[b] - cum[b]) for b in batch]
    decay = [[jnp.exp(cum[b][ends[c], :]) for c in chunks] for b in batch]
    a = {(b, h): jnp.where(causal, _mm_nt(qe[b][:, ks[h]], ke[b][:, ks[h]]), 0.0) for b, h in pairs}
    inside = {key: _mm(a[key], v[key]) for key in pairs}
    update = {(b, h, c): _mm_tn(v[b, h][rows[c], :], kd[b][rows[c], ks[h]]) for b, h in pairs for c in chunks}

    st = {(b, h, 0): st_scr[b, h] for b, h in pairs}
    for c in chunks:
        for b, h in pairs:
            st[b, h, c + 1] = st[b, h, c] * decay[b][c][:, ks[h]] + update[b, h, c]
    outer = {(b, h, c): _mm_nt(qe[b][rows[c], ks[h]], st[b, h, c]) for b, h in pairs for c in chunks}
    for b, h in pairs:
        st_scr[b, h] = st[b, h, n_chunks]
        for c in chunks:
            o = outer[b, h, c] + inside[b, h][rows[c], :]
            og = o * lax.rsqrt(jnp.mean(o * o, axis=-1, keepdims=True) + GLA_NORM_EPS)
            og_ref[b, rows[c], vs[h]] = og.astype(og_ref.dtype)

    @pl.when(t == last)
    def _():
        for b, h in pairs:
            sout_ref[b, h] = st_scr[b, h].T


def _seqs_per_step(batch, seq):
    n = max(1, min(8, 64 // seq))
    while batch % n:
        n //= 2
    return n


def _gla(grp, p3, wg_pad, bg, s0):
    B, T = grp.B, grp.T
    tile = min(T, 4 * CHUNK)
    chunk = min(CHUNK, tile)
    nb = max(_seqs_per_step(B, T), 4 if B % 4 == 0 else 2 - B % 2)
    kern = functools.partial(_gla_kernel, chunk=chunk, n_chunks=tile // chunk, tile=tile,
                             valid=None if grp.Tv == T else grp.Tv, seqs=nb)
    kw, vw = GLA_HEADS * GLA_DK, GLA_HEADS * GLA_DV
    st_spec = pl.BlockSpec((nb, GLA_HEADS, GLA_DK, GLA_DV), lambda i, t: (i, 0, 0, 0))
    return pl.pallas_call(
        kern,
        grid=(B // nb, T // tile),
        in_specs=[pl.BlockSpec((nb, tile, kw), lambda i, t: (i, t, COL_GQ // kw)),
                  pl.BlockSpec((nb, tile, kw), lambda i, t: (i, t, COL_GK // kw)),
                  pl.BlockSpec((nb, tile, vw), lambda i, t: (i, t, COL_GV // vw)),
                  pl.BlockSpec((nb, tile, LANES), lambda i, t: (i, t, COL_GA // LANES)),
                  pl.BlockSpec((LANES, kw), lambda i, t: (0, 0)),
                  pl.BlockSpec((1, kw), lambda i, t: (0, 0)),
                  st_spec],
        out_specs=[pl.BlockSpec((nb, tile, vw), lambda i, t: (i, t, 0)), st_spec],
        out_shape=[jax.ShapeDtypeStruct((B, T, vw), BF16),
                   jax.ShapeDtypeStruct((B, GLA_HEADS, GLA_DK, GLA_DV), F32)],
        scratch_shapes=[pltpu.VMEM((nb, GLA_HEADS, GLA_DV, GLA_DK), F32)],
        compiler_params=_params(("arbitrary", "arbitrary")),
        name="gla_scan",
    )(p3, p3, p3, p3, wg_pad, bg, s0)


def _rwprep_kernel(r_ref, k_ref, v_ref, l_ref, pr_ref, pk_ref, pv_ref, pl_ref,
                   s0r_ref, s0k_ref, s0v_ref, s0l_ref, mu_ref, mul_ref, w0_ref, w2_ref, a0_ref, a2_ref, g2_ref,
                   kk_ref, ka_ref, bd_ref,
                   ro_ref, ko_ref, vo_ref, an_ref, bb_ref, lw_ref, gate_ref,
                   *, tile, seq, valid, prev_rows):
    i = pl.program_id(0)
    local = lax.broadcasted_iota(jnp.int32, (tile, 1), 0)
    pos = lax.rem(i * tile + local, seq)
    one_sequence = seq % tile == 0

    def shifted(cur_ref, prev_ref, s0_ref, mu):
        cur = cur_ref[...].astype(F32)
        if one_sequence:
            before = prev_ref[prev_rows - 1:prev_rows, :].astype(F32)
            edge = jnp.where(lax.rem(i * tile, seq) == 0, s0_ref[...], before)
            prev = jnp.where(local == 0, edge, pltpu.roll(cur, 1, 0))
        else:
            prev = jnp.where(pos == 0, s0_ref[...], pltpu.roll(cur, 1, 0))
        return cur + (prev - cur) * mu

    r = shifted(r_ref, pr_ref, s0r_ref, mu_ref[:, 0:1024])
    k = shifted(k_ref, pk_ref, s0k_ref, mu_ref[:, 1024:2048])
    v = shifted(v_ref, pv_ref, s0v_ref, mu_ref[:, 2048:3072])
    lo = shifted(l_ref, pl_ref, s0l_ref, mul_ref[...])

    wpre = w0_ref[...] + _mm(jnp.tanh(lo), w2_ref[...])
    wlog = -_softplus(-wpre) - 0.5
    lw = -jnp.exp(wlog)
    a = jax.nn.sigmoid(a0_ref[...] + _mm(lo, a2_ref[...]))
    gate = _mm(jax.nn.sigmoid(lo), g2_ref[...])

    kk = k * kk_ref[...]
    ss = _mm(kk * kk, bd_ref[...])
    kk = kk / jnp.maximum(jnp.sqrt(ss), 1e-12)
    kmod = k * (1.0 + (a - 1.0) * ka_ref[...])
    an = -kk
    bb = kk * a
    if valid is not None:
        ok = pos < valid
        lw = jnp.where(ok, lw, 0.0)
        an = jnp.where(ok, an, 0.0)
        bb = jnp.where(ok, bb, 0.0)
        kmod = jnp.where(ok, kmod, 0.0)
        v = jnp.where(ok, v, 0.0)
    ro_ref[...] = r.astype(ro_ref.dtype)
    ko_ref[...] = kmod.astype(ko_ref.dtype)
    vo_ref[...] = v.astype(vo_ref.dtype)
    an_ref[...] = an.astype(an_ref.dtype)
    bb_ref[...] = bb.astype(bb_ref.dtype)
    lw_ref[...] = lw
    gate_ref[...] = gate.astype(gate_ref.dtype)


def _rwkv_prep(grp, p2, shift0, wts):
    M, T = grp.M, grp.T
    tile = min(M, 512)
    assert T % tile == 0 or tile % T == 0
    prev_rows = min(16, tile)
    W = RW_HEADS * RW_HEAD
    s0 = grp.seq_operand(shift0)

    def cur(width, blk):
        return pl.BlockSpec((tile, width), lambda i: (i, blk))

    def prev(width, blk):
        return pl.BlockSpec((prev_rows, width), lambda i: (jnp.maximum(i * (tile // prev_rows) - 1, 0), blk))

    def full(shape):
        return pl.BlockSpec(shape, lambda i: (0,) * len(shape))

    out_spec = pl.BlockSpec((tile, W), lambda i: (i, 0))
    kern = functools.partial(_rwprep_kernel, tile=tile, seq=T, valid=None if grp.Tv == T else grp.Tv,
                             prev_rows=prev_rows)
    lora_blk = COL_LORA // 256
    return pl.pallas_call(
        kern,
        grid=(M // tile,),
        in_specs=[cur(W, 0), cur(W, 1), cur(W, 2), cur(256, lora_blk),
                  prev(W, 0), prev(W, 1), prev(W, 2), prev(256, lora_blk),
                  grp.seq_spec(W, 0, tile), grp.seq_spec(W, 1, tile), grp.seq_spec(W, 2, tile),
                  grp.seq_spec(256, 3 * W // 256, tile),
                  full((1, 3 * W)), full((1, 256)),
                  full((1, W)), full((256, W)), full((1, W)), full((256, W)), full((256, W)),
                  full((1, W)), full((1, W)), full((W, W))],
        out_specs=[out_spec] * 7,
        out_shape=[jax.ShapeDtypeStruct((M, W), BF16)] * 5
        + [jax.ShapeDtypeStruct((M, W), F32), jax.ShapeDtypeStruct((M, W), BF16)],
        compiler_params=_params(("arbitrary",)),
        name="rwkv_prep",
    )(p2, p2, p2, p2, p2, p2, p2, p2, s0, s0, s0, s0,
      wts["mu_rkv"], wts["mu_lora"], wts["rw_w0"], wts["w2p"], wts["rw_a0"], wts["a2p"], wts["g2p"],
      wts["rw_k_k"], wts["rw_k_a"], wts["head_ones"])


def _rwkv_pairs_chunk(at, rt, bt, kt, v, states, *, chunk, same_head, strict, incl):
    n2 = 2 * chunk
    wide = n2 % LANES == 0
    pairs = range(len(states))

    def blocks(x):
        return jnp.where(same_head, jnp.concatenate([x, x], axis=0), 0.0).astype(BF16)

    a2 = [blocks(x) for x in at]
    r2 = [blocks(x) for x in rt]
    b2 = [blocks(x) for x in bt]
    k2 = [blocks(x) for x in kt]
    v2 = [blocks(x) for x in v]
    ar = [jnp.concatenate([a2[p], r2[p]], axis=0) for p in pairs]
    bk = [jnp.concatenate([b2[p], k2[p]], axis=0) for p in pairs]
    if wide:
        g = [_mm_nt(ar[p], bk[p]) for p in pairs]
        l_ab = [jnp.where(strict, g[p][:n2, :n2], 0.0) for p in pairs]
        l_ak = [jnp.where(strict, g[p][:n2, n2:], 0.0) for p in pairs]
        m_rb = [jnp.where(incl, g[p][n2:, :n2], 0.0) for p in pairs]
        m_rk = [jnp.where(incl, g[p][n2:, n2:], 0.0) for p in pairs]
    else:
        l_ab = [jnp.where(strict, _mm_nt(a2[p], b2[p]), 0.0) for p in pairs]
        l_ak = [jnp.where(strict, _mm_nt(a2[p], k2[p]), 0.0) for p in pairs]
        m_rb = [jnp.where(incl, _mm_nt(r2[p], b2[p]), 0.0) for p in pairs]
        m_rk = [jnp.where(incl, _mm_nt(r2[p], k2[p]), 0.0) for p in pairs]
    from_state = [_mm_nt(ar[p], states[p]) for p in pairs]
    x = [from_state[p][:n2] + _mm(l_ak[p], v2[p]) for p in pairs]
    pw = l_ab
    for _ in range(chunk.bit_length() - 2):
        if wide:
            y = [_mm(pw[p], jnp.concatenate([x[p], pw[p]], axis=1)) for p in pairs]
            x = [x[p] + y[p][:, :LANES] for p in pairs]
            pw = [y[p][:, LANES:] for p in pairs]
        else:
            x = [x[p] + _mm(pw[p], x[p]) for p in pairs]
            pw = [_mm(pw[p], pw[p]) for p in pairs]
    u = [x[p] + _mm(pw[p], x[p]) for p in pairs]
    uv = [jnp.concatenate([u[p].astype(BF16), v2[p]], axis=0) for p in pairs]
    if wide:
        o2 = [from_state[p][n2:] + _mm(jnp.concatenate([m_rb[p], m_rk[p]], axis=1), uv[p]) for p in pairs]
    else:
        o2 = [from_state[p][n2:] + _mm(m_rb[p], u[p]) + _mm(m_rk[p], v2[p]) for p in pairs]
    s_new = [states[p] + _mm_tn(uv[p], bk[p]) for p in pairs]
    return o2, s_new


def _rwkv_kernel(r_ref, k_ref, v_ref, an_ref, bb_ref, lw_ref, gw_ref, gb_ref, rk_ref, s0_ref,
                 o_ref, sout_ref, s_scr, *, chunk, seqs):
    t = pl.program_id(1)
    zeros = jnp.zeros((RW_HEAD, RW_HEAD), F32)

    @pl.when(t == 0)
    def _():
        for q in range(seqs):
            for p in range(RW_PAIRS):
                top = jnp.concatenate([s0_ref[q, 2 * p], zeros], axis=1)
                bot = jnp.concatenate([zeros, s0_ref[q, 2 * p + 1]], axis=1)
                s_scr[q, p] = jnp.concatenate([top, bot], axis=0)

    n2 = 2 * chunk
    row = lax.broadcasted_iota(jnp.int32, (n2, n2), 0)
    col = lax.broadcasted_iota(jnp.int32, (n2, n2), 1)

    def second(idx, size):
        return jnp.where(idx >= size, 1, 0)

    same = second(row, chunk) == second(col, chunk)
    strict = same & (row > col)
    incl = same & (row >= col)
    same_head = (second(lax.broadcasted_iota(jnp.int32, (n2, LANES), 0), chunk)
                 == second(lax.broadcasted_iota(jnp.int32, (n2, LANES), 1), RW_HEAD))
    trow = lax.broadcasted_iota(jnp.int32, (chunk, chunk), 0)
    tcol = lax.broadcasted_iota(jnp.int32, (chunk, chunk), 1)
    tril = jnp.where(trow >= tcol, 1.0, 0.0).astype(BF16)
    first_head = lax.broadcasted_iota(jnp.int32, (chunk, LANES), 1) < RW_HEAD

    def head_sum(x):
        lo = jnp.sum(jnp.where(first_head, x, 0.0), axis=-1, keepdims=True)
        hi = jnp.sum(jnp.where(first_head, 0.0, x), axis=-1, keepdims=True)
        return jnp.where(first_head, lo, hi)

    lanes = [slice(p * LANES, (p + 1) * LANES) for p in range(RW_PAIRS)]
    problems = [(q, p) for q in range(seqs) for p in range(RW_PAIRS)]
    rs, ks, vs, ats, rts, bts, kts, ends = [], [], [], [], [], [], [], []
    for q in range(seqs):
        lw = lw_ref[q]
        cum = _mm_exact_lhs(tril, lw)
        w_incl = jnp.exp(cum)
        w_inv = jnp.exp(-cum)
        r, k = r_ref[q].astype(F32), k_ref[q].astype(F32)
        rs.append(r)
        ks.append(k)
        vs.append(v_ref[q].astype(F32))
        ats.append(an_ref[q].astype(F32) * jnp.exp(cum - lw))
        rts.append(r * w_incl)
        bts.append(bb_ref[q].astype(F32) * w_inv)
        kts.append(k * w_inv)
        ends.append(w_incl[chunk - 1:chunk, :])

    def slabs(xs):
        return [xs[q][:, lanes[p]] for q, p in problems]

    o2, s_new = _rwkv_pairs_chunk(
        slabs(ats), slabs(rts), slabs(bts), slabs(kts), slabs(vs), [s_scr[q, p] for q, p in problems],
        chunk=chunk, same_head=same_head, strict=strict, incl=incl)
    for n, (q, p) in enumerate(problems):
        ls = lanes[p]
        s_scr[q, p] = s_new[n] * ends[q][:, ls]
        o = o2[n][:chunk] + o2[n][chunk:]
        mu = head_sum(o) * (1.0 / RW_HEAD)
        oc = o - mu
        var = head_sum(oc * oc) * (1.0 / RW_HEAD)
        on = oc * lax.rsqrt(var + RW_GN_EPS) * gw_ref[:, ls] + gb_ref[:, ls]
        bonus = head_sum(rs[q][:, ls] * ks[q][:, ls] * rk_ref[:, ls]) * vs[q][:, ls]
        o_ref[q, :, ls] = (on + bonus).astype(o_ref.dtype)

    @pl.when(t == pl.num_programs(1) - 1)
    def _():
        for q in range(seqs):
            for p in range(RW_PAIRS):
                s = s_scr[q, p]
                sout_ref[q, 2 * p] = s[:RW_HEAD, :RW_HEAD]
                sout_ref[q, 2 * p + 1] = s[RW_HEAD:, RW_HEAD:]


def _rwkv(grp, seqs, gn_w, gn_b, r_k, s0):
    B, T = grp.B, grp.T
    chunk = min(T, CHUNK)
    nb = max(_seqs_per_step(B, T), 4 if B % 4 == 0 else 2 - B % 2)
    W = RW_HEADS * RW_HEAD
    seq_spec = pl.BlockSpec((nb, chunk, W), lambda b, t: (b, t, 0))
    vec_spec = pl.BlockSpec((1, W), lambda b, t: (0, 0))
    st_spec = pl.BlockSpec((nb, RW_HEADS, RW_HEAD, RW_HEAD), lambda b, t: (b, 0, 0, 0))
    kern = functools.partial(_rwkv_kernel, chunk=chunk, seqs=nb)
    return pl.pallas_call(
        kern,
        grid=(B // nb, T // chunk),
        in_specs=[seq_spec] * 6 + [vec_spec] * 3 + [st_spec],
        out_specs=[seq_spec, st_spec],
        out_shape=[jax.ShapeDtypeStruct((B, T, W), BF16),
                   jax.ShapeDtypeStruct((B, RW_HEADS, RW_HEAD, RW_HEAD), F32)],
        scratch_shapes=[pltpu.VMEM((nb, RW_PAIRS, 2 * RW_HEAD, 2 * RW_HEAD), F32)],
        compiler_params=_params(("arbitrary", "arbitrary")),
        name="rwkv_scan",
    )(*seqs, gn_w, gn_b, r_k, s0)


def _merge_kernel(og_ref, gg_ref, orw_ref, gate_ref, mga_ref, mgb_ref, x_ref,
                  gate1_ref, shift2_ref, scale2_ref,
                  ggn_ref, gpost_ref, gpre_ref, wbg_ref, wbr_ref, wout_ref, wq_ref,
                  x1_ref, h2t_ref, qp_ref):
    gg = gg_ref[...].astype(F32)
    gla_in = og_ref[...].astype(F32) * ggn_ref[...] * (gg * jax.nn.sigmoid(gg))
    br_gla = _mm(gla_in, wbg_ref[...])
    br_rw = _mm(orw_ref[...].astype(F32) * gate_ref[...].astype(F32), wbr_ref[...])
    merged = (jax.nn.sigmoid(mga_ref[...].astype(F32)) * br_gla
              + jax.nn.sigmoid(mgb_ref[...].astype(F32)) * br_rw)
    y = _mm(merged, wout_ref[...])
    x1 = x_ref[...] + gate1_ref[...] * _rms(y, gpost_ref[...])
    x1_ref[...] = x1
    h2 = _rms(x1, gpre_ref[...]) * (1.0 + scale2_ref[...]) + shift2_ref[...]
    h2t_ref[...] = h2.T.astype(BF16)
    qp_ref[...] = jnp.dot(h2.astype(BF16), wq_ref[...], preferred_element_type=F32).astype(BF16)


def _merge(grp, og, p2, orw, gate_rw, x2d, modop, wts):
    tm, M = grp.tm, grp.M
    W = D_MODEL

    def tok(blk):
        return pl.BlockSpec((tm, W), lambda i: (i, blk))

    def full(shape):
        return pl.BlockSpec(shape, lambda i: (0,) * len(shape), pipeline_mode=pl.Buffered(1))

    return pl.pallas_call(
        _merge_kernel,
        grid=(M // tm,),
        in_specs=[tok(0), tok(COL_GG // W), tok(0), tok(0), tok(COL_MG // W), tok(COL_MG // W + 1), tok(0),
                  grp.mod_spec(2, tm), grp.mod_spec(3, tm), grp.mod_spec(4, tm),
                  full((1, W)), full((1, W)), full((1, W)),
                  full((W, W)), full((W, W)), full((W, W)), full((W, 2 * W))],
        out_specs=[tok(0), pl.BlockSpec((W, tm), lambda i: (0, i)), pl.BlockSpec((tm, 2 * W), lambda i: (i, 0))],
        out_shape=[jax.ShapeDtypeStruct((M, W), F32), jax.ShapeDtypeStruct((W, M), BF16),
                   jax.ShapeDtypeStruct((M, 2 * W), BF16)],
        compiler_params=_params(("arbitrary",)),
        name="merge",
    )(og, p2, orw, gate_rw, p2, p2, x2d, modop, modop, modop,
      wts["g_gla_norm"], wts["g_post_mix"], wts["g_pre_ffn"],
      wts["w_b_gla"], wts["w_b_rw"], wts["w_out"], wts["peer_wq"])


def _pair_bits(x):
    bits = pltpu.bitcast(x.astype(BF16).astype(F32), jnp.uint32)
    return bits | (bits >> 16)


def _row_tile(row, rows):
    tile = pltpu.bitcast(jnp.broadcast_to(row, (8, row.shape[1])), BF16)
    return jnp.tile(tile, (rows // 16, 1))


def _batcher_network(n):
    def merge(lo, hi, r):
        step = r * 2
        if step < hi - lo:
            yield from merge(lo, hi, step)
            yield from merge(lo + r, hi, step)
            yield from [(i, i + r) for i in range(lo + r, hi - r, step)]
        else:
            yield (lo, lo + r)

    def sort(lo, hi):
        if hi - lo >= 1:
            mid = lo + (hi - lo) // 2
            yield from sort(lo, mid)
            yield from sort(mid + 1, hi)
            yield from merge(lo, hi, 1)

    return list(sort(0, n - 1))


def _top_values(s, n):
    sub = 8
    depth = s.shape[0] // sub
    cols = [s[i * sub:(i + 1) * sub, :] for i in range(depth)]
    for i, j in _batcher_network(depth):
        cols[i], cols[j] = jnp.maximum(cols[i], cols[j]), jnp.minimum(cols[i], cols[j])
    rows = lax.broadcasted_iota(jnp.int32, (n, s.shape[1]), 0)
    vals = jnp.zeros((n, s.shape[1]), F32)
    for it in range(n):
        m = jnp.max(cols[0], axis=0, keepdims=True)
        vals = jnp.where(rows == it, m, vals)
        hit = cols[0] == m
        for i in range(min(depth - 1, n - 1 - it)):
            cols[i] = jnp.where(hit, cols[i + 1], cols[i])
    return vals


def _count_prefix(vals, pred):
    def row(i):
        return vals[i:i + 1, :]

    c8 = pred(row(7))
    c4 = pred(jnp.where(c8, row(11), row(3)))
    c2 = pred(jnp.where(c8, jnp.where(c4, row(13), row(9)), jnp.where(c4, row(5), row(1))))
    upper = jnp.where(c4, jnp.where(c2, row(14), row(12)), jnp.where(c2, row(10), row(8)))
    lower = jnp.where(c4, jnp.where(c2, row(6), row(4)), jnp.where(c2, row(2), row(0)))
    c1 = pred(jnp.where(c8, upper, lower))
    c16 = pred(row(15))
    return (jnp.where(c8, 8.0, 0.0) + jnp.where(c4, 4.0, 0.0) + jnp.where(c2, 2.0, 0.0)
            + jnp.where(c1, 1.0, 0.0) + jnp.where(c16, 1.0, 0.0))


def _rank_among(s, vals):
    return _count_prefix(vals, lambda x: x > s)


def _candidate_groups(v1, v2):
    n, tm = v1.shape
    split = 4
    r_all = lax.broadcasted_iota(jnp.int32, (n, tm), 0)
    r_half = lax.broadcasted_iota(jnp.int32, (n // 2, tm), 0)
    groups = [v1[0:1, :] + v2]
    for a in range(1, split):
        groups.append(jnp.where(r_half < n // (a + 1), v1[a:a + 1, :] + v2[:n // 2, :], NEG_INF))
    groups.append(jnp.where(r_all >= split, v1 + v2[0:1, :], NEG_INF))
    for b in range(1, n // (split + 1)):
        keep = (r_half >= split) & (r_half < n // (b + 1))
        groups.append(jnp.where(keep, v1[:n // 2, :] + v2[b:b + 1, :], NEG_INF))
    return groups


def _group_max(groups, n):
    full = [g for g in groups if g.shape[0] == n]
    best = full[0]
    for g in full[1:]:
        best = jnp.maximum(best, g)
    best = jnp.maximum(best[:n // 2, :], best[n // 2:, :])
    for g in groups:
        if g.shape[0] != n:
            best = jnp.maximum(best, g)
    return jnp.max(best, axis=0, keepdims=True)


def _select_kernel(qp_ref, keys_ref, r2_ref, e2_ref, al_ref, n1_ref):
    tm = qp_ref.shape[0]
    for h in range(PEER_HEADS):
        q1 = qp_ref[:, (2 * h) * 128:(2 * h + 1) * 128]
        q2 = qp_ref[:, (2 * h + 1) * 128:(2 * h + 2) * 128]
        s1 = _mm_nt(keys_ref[2 * h], q1)
        s2 = _mm_nt(keys_ref[2 * h + 1], q2)
        v1 = _top_values(s1, PEER_TOPK)
        v2 = _top_values(s2, PEER_TOPK)
        rank2 = _rank_among(s2, v2)
        groups = _candidate_groups(v1, v2)
        cmax = v1[0:1, :] + v2[0:1, :]
        z = jnp.zeros((1, tm), F32)
        m = cmax
        for _ in range(PEER_TOPK):
            m = _group_max(groups, PEER_TOPK)
            z = z + jnp.exp(m - cmax)
            groups = [jnp.where(g == m, NEG_INF, g) for g in groups]
        th = m - s1
        n1 = _count_prefix(v2, lambda x: x >= th)
        r2_ref[h] = rank2.astype(BF16)
        e2_ref[h] = jnp.exp(s2 - v2[0:1, :]).astype(BF16)
        al_ref[h] = _pair_bits(jnp.exp(s1 - v1[0:1, :]) / z)
        n1_ref[h] = _pair_bits(n1)


def _peer_select(grp, qp, keys):
    tm, M = grp.tm, grp.M
    rows = pl.BlockSpec((PEER_HEADS, PEER_NKEYS, tm), lambda i: (0, 0, i))
    return pl.pallas_call(
        _select_kernel,
        grid=(M // tm,),
        in_specs=[pl.BlockSpec((tm, 2 * D_MODEL), lambda i: (i, 0)),
                  pl.BlockSpec((2 * PEER_HEADS, PEER_NKEYS, 128), lambda i: (0, 0, 0))],
        out_specs=[rows] * 4,
        out_shape=[jax.ShapeDtypeStruct((PEER_HEADS, PEER_NKEYS, M), BF16)] * 2
        + [jax.ShapeDtypeStruct((PEER_HEADS, PEER_NKEYS, M), jnp.uint32)] * 2,
        compiler_params=_params(("arbitrary",)),
        name="peer_select",
    )(qp, keys)


def _peer_weights(act_ref, y_ref, r2_ref, e2_ref, al_ref, n1_ref, row0):
    tm = act_ref.shape[1]
    half = PEER_NKEYS // 2
    for j in range(act_ref.shape[0] // PEER_NKEYS):
        i1 = slice(row0 + j, row0 + j + 1)
        w = [jnp.zeros((half, tm), BF16) for _ in range(2)]
        for h in range(PEER_HEADS):
            n1 = _row_tile(n1_ref[h, i1, :], half)
            al = _row_tile(al_ref[h, i1, :], half)
            for r in range(2):
                keys = slice(r * half, (r + 1) * half)
                w[r] = w[r] + jnp.where(r2_ref[h, keys, :] < n1, e2_ref[h, keys, :], 0.0) * al
        for r in range(2):
            rows = slice(j * PEER_NKEYS + r * half, j * PEER_NKEYS + (r + 1) * half)
            a = act_ref[rows, :].astype(BF16)
            y_ref[rows, :] = (0.5 * a * (1.0 + lax.erf(a * 0.7071067811865476))) * w[r]


def _peer_kernel(h2t_ref, u_ref, vt_ref, r2_ref, e2_ref, al_ref, n1_ref, x1_ref, gate2_ref, gpost_ref,
                 o_ref, acc_ref, act_0, act_1, y_0, y_1):
    e = pl.program_id(1)
    ts = act_0.shape[0]
    per = ts // PEER_NKEYS
    acts, ys = (act_0, act_1), (y_0, y_1)

    @pl.when(e == 0)
    def _():
        acc_ref[...] = jnp.zeros_like(acc_ref)

    def scores(k):
        acts[k % 2][...] = jnp.dot(u_ref[k], h2t_ref[...], preferred_element_type=F32)

    def mix(k):
        acc_ref[...] += jnp.dot(vt_ref[k], ys[k % 2][...], preferred_element_type=F32)

    scores(0)
    scores(1)
    for k in range(PEER_SUBTILES):
        _peer_weights(acts[k % 2], ys[k % 2], r2_ref, e2_ref, al_ref, n1_ref, k * per)
        if k + 2 < PEER_SUBTILES:
            scores(k + 2)
        if k >= 1:
            mix(k - 1)
    mix(PEER_SUBTILES - 1)

    @pl.when(e == pl.num_programs(1) - 1)
    def _():
        o_ref[...] = x1_ref[...] + gate2_ref[...] * _rms(acc_ref[...].T, gpost_ref[...])


def _peer(grp, h2t, sel, x1, modop, g_post, pu, pvt):
    tm, M = grp.tm, grp.M
    ts = PEER_EXPERT_TILE
    te = ts * PEER_SUBTILES
    r2, e2, al, n1 = sel
    packed = pl.BlockSpec((PEER_HEADS, PEER_NKEYS, tm), lambda i, e: (0, 0, i))
    rows = pl.BlockSpec((PEER_HEADS, te // PEER_NKEYS, tm), lambda i, e: (0, e, i))
    tok = pl.BlockSpec((tm, D_MODEL), lambda i, e: (i, 0))
    return pl.pallas_call(
        _peer_kernel,
        grid=(M // tm, pu.shape[0] // PEER_SUBTILES),
        in_specs=[pl.BlockSpec((D_MODEL, tm), lambda i, e: (0, i)),
                  pl.BlockSpec((PEER_SUBTILES, ts, D_MODEL), lambda i, e: (e, 0, 0)),
                  pl.BlockSpec((PEER_SUBTILES, D_MODEL, ts), lambda i, e: (e, 0, 0)),
                  packed, packed, rows, rows,
                  pl.BlockSpec((tm, D_MODEL), lambda i, e: (i, 0), pipeline_mode=pl.Buffered(1)),
                  grp.mod_spec(5),
                  pl.BlockSpec((1, D_MODEL), lambda i, e: (0, 0))],
        out_specs=tok,
        out_shape=jax.ShapeDtypeStruct((M, D_MODEL), F32),
        scratch_shapes=[pltpu.VMEM((D_MODEL, tm), F32),
                        pltpu.VMEM((ts, tm), F32), pltpu.VMEM((ts, tm), F32),
                        pltpu.VMEM((ts, tm), BF16), pltpu.VMEM((ts, tm), BF16)],
        compiler_params=_params(("arbitrary", "arbitrary")),
        name="peer_dense",
    )(h2t, pu, pvt, r2, e2, al, n1, x1, modop, g_post)


def _transpose_cast_kernel(x_ref, o_ref):
    rows = o_ref.shape[2]
    for t in range(o_ref.shape[0]):
        o_ref[t] = x_ref[t * rows:(t + 1) * rows, :].T.astype(o_ref.dtype)


def _transposed_tiles(x, rows, per_step=2):
    n, d = x.shape
    return pl.pallas_call(
        _transpose_cast_kernel,
        grid=(n // (rows * per_step),),
        in_specs=[pl.BlockSpec((rows * per_step, d), lambda k: (k, 0))],
        out_specs=pl.BlockSpec((per_step, d, rows), lambda k: (k, 0, 0)),
        out_shape=jax.ShapeDtypeStruct((n // rows, d, rows), BF16),
        compiler_params=_params(("arbitrary",)),
        name="transpose_cast",
    )(x)


def _prepare_weights(w_in, w_gla_gate, b_gla_gate, g_gla_norm, mu_shift, rw_w0, rw_w2, rw_a0, rw_a2,
                     rw_g2, rw_k_k, rw_k_a, rw_r_k, rw_gn_w, rw_gn_b, w_b_gla, w_b_rw, w_out,
                     peer_wq, peer_keys, peer_u, peer_v, g_pre_mix, g_post_mix, g_pre_ffn, g_post_ffn):
    W = RW_HEADS * RW_HEAD
    rw = w_in[:, 3088:6416]
    w_proj = jnp.concatenate(
        [rw[:, :3 * W], w_in[:, 6416:8464], w_in[:, 2048:3072], w_in[:, 1024:2048], w_in[:, 0:1024],
         rw[:, 3 * W:], w_in[:, 3072:3088], jnp.zeros((D_MODEL, PROJ_PAD_W - 8464), F32)],
        axis=1).astype(BF16)
    head = jnp.arange(W) // RW_HEAD
    zeros64 = jnp.zeros((64, W), F32)
    return dict(
        w_proj=w_proj,
        wg_pad=jnp.concatenate([w_gla_gate, jnp.zeros((128 - GLA_GATE_RANK, 512), F32)], axis=0),
        bg=b_gla_gate.reshape(1, -1),
        g_gla_norm=g_gla_norm.reshape(1, -1),
        mu_rkv=mu_shift[:3 * W].reshape(1, -1),
        mu_lora=mu_shift[3 * W:].reshape(1, -1),
        rw_w0=rw_w0.reshape(1, -1),
        w2p=jnp.concatenate([rw_w2, jnp.zeros((192, W), F32)], axis=0),
        rw_a0=rw_a0.reshape(1, -1),
        a2p=jnp.concatenate([zeros64, rw_a2, jnp.zeros((128, W), F32)], axis=0),
        g2p=jnp.concatenate([jnp.zeros((128, W), F32), rw_g2], axis=0),
        rw_k_k=rw_k_k.reshape(1, -1),
        rw_k_a=rw_k_a.reshape(1, -1),
        rw_r_k=rw_r_k.reshape(1, -1),
        rw_gn_w=rw_gn_w.reshape(1, -1),
        rw_gn_b=rw_gn_b.reshape(1, -1),
        head_ones=(head[:, None] == head[None, :]).astype(BF16),
        w_b_gla=w_b_gla.astype(BF16),
        w_b_rw=w_b_rw.astype(BF16),
        w_out=w_out.astype(BF16),
        peer_wq=peer_wq.astype(BF16),
        peer_keys=peer_keys.reshape(2 * PEER_HEADS, PEER_NKEYS, 128).astype(BF16),
        peer_u=peer_u.astype(BF16).reshape(-1, PEER_EXPERT_TILE, D_MODEL),
        peer_vt=_transposed_tiles(peer_v, PEER_EXPERT_TILE),
        g_pre_mix=g_pre_mix.reshape(1, -1),
        g_post_mix=g_post_mix.reshape(1, -1),
        g_pre_ffn=g_pre_ffn.reshape(1, -1),
        g_post_ffn=g_post_ffn.reshape(1, -1),
    )


def _block(x, mod, s_gla0, s_rw0, shift0, wts):
    B, valid, _ = x.shape
    T = -(-valid // 8) * 8
    x = jnp.pad(x, ((0, 0), (0, T - valid), (0, 0)))
    grp = _Group(B, T, valid)
    W = RW_HEADS * RW_HEAD
    x2d = x.reshape(grp.M, D_MODEL)
    modop = grp.seq_operand(mod)
    p2 = _projection(grp, x2d, modop, wts["g_pre_mix"], wts["w_proj"], BF16)
    p3 = p2.reshape(B, T, PROJ_PAD_W)
    og, s_gla = _gla(grp, p3, wts["wg_pad"], wts["bg"], s_gla0)
    r, k, v, an, bb, lw, gate = _rwkv_prep(grp, p2, shift0, wts)
    seqs = tuple(z.reshape(B, T, W) for z in (r, k, v, an, bb, lw))
    orw, s_rw = _rwkv(grp, seqs, wts["rw_gn_w"], wts["rw_gn_b"], wts["rw_r_k"], s_rw0)
    x1, h2t, qp = _merge(grp, og.reshape(grp.M, -1), p2, orw.reshape(grp.M, W), gate,
                        x2d, modop, wts)
    if valid != T:
        grp = _Group(B, valid, valid)
        modop = grp.seq_operand(mod)
        x1 = x1.reshape(B, T, D_MODEL)[:, :valid].reshape(grp.M, D_MODEL)
        h2t = h2t.reshape(D_MODEL, B, T)[:, :, :valid].reshape(D_MODEL, grp.M)
        qp = qp.reshape(B, T, -1)[:, :valid].reshape(grp.M, -1)
    sel = _peer_select(grp, qp, wts["peer_keys"])
    out = _peer(grp, h2t, sel, x1, modop, wts["g_post_ffn"], wts["peer_u"], wts["peer_vt"])
    ends = _Group(B, 1, 1)
    last = _projection(ends, x[:, valid - 1, :], ends.seq_operand(mod), wts["g_pre_mix"], wts["w_proj"], F32)
    shift_new = jnp.concatenate([last[:, :3 * W], last[:, COL_LORA:COL_LORA + 256]], axis=-1)
    return out.reshape(B, grp.T, D_MODEL)[:, :valid], s_gla, s_rw, shift_new


def kernel(x_prompt, x_sample, c_prompt, c_sample, state_gla, state_rwkv, state_shift, w_ada, b_ada, g_pre_mix, g_post_mix, g_pre_ffn, g_post_ffn, w_in, w_gla_gate, b_gla_gate, g_gla_norm, mu_shift, rw_w0, rw_w2, rw_a0, rw_a2, rw_g2, rw_k_k, rw_k_a, rw_r_k, rw_gn_w, rw_gn_b, w_b_gla, w_b_rw, w_out, peer_wq, peer_keys, peer_u, peer_v):
    depth = w_in.shape[0]
    bp = x_prompt.shape[0]
    xp, xs = x_prompt, x_sample
    outs = [[] for _ in range(6)]
    for l in range(depth):
        wts = _prepare_weights(
            w_in[l], w_gla_gate[l], b_gla_gate[l], g_gla_norm[l], mu_shift[l], rw_w0[l], rw_w2[l],
            rw_a0[l], rw_a2[l], rw_g2[l], rw_k_k[l], rw_k_a[l], rw_r_k[l], rw_gn_w[l], rw_gn_b[l],
            w_b_gla[l], w_b_rw[l], w_out[l], peer_wq[l], peer_keys[l], peer_u[l], peer_v[l],
            g_pre_mix[l], g_post_mix[l], g_pre_ffn[l], g_post_ffn[l])
        mod = _modulation(jnp.concatenate([c_prompt, c_sample], axis=0), w_ada[l], b_ada[l])
        zg = jnp.zeros((bp, GLA_HEADS, GLA_DK, GLA_DV), F32)
        zr = jnp.zeros((bp, RW_HEADS, RW_HEAD, RW_HEAD), F32)
        zs = jnp.zeros((bp, RW_SHIFT_W), F32)
        xp, g1, r1, s1 = _block(xp, mod[:bp], zg, zr, zs, wts)
        xs, g2, r2, s2 = _block(xs, mod[bp:], state_gla[l], state_rwkv[l], state_shift[l], wts)
        for acc, val in zip(outs, (g1, r1, s1, g2, r2, s2)):
            acc.append(val)
    stacked = [jnp.stack(o) for o in outs]
    return (xp, xs, *stacked)
```

```python
import functools

import jax
import jax.numpy as jnp
from jax import lax
from jax.experimental import pallas as pl
from jax.experimental.pallas import tpu as pltpu

F32 = jnp.float32
BF16 = jnp.bfloat16

D_MODEL = 1024
GLA_HEADS = 4
GLA_DK = 128
GLA_DV = 256
GLA_GATE_RANK = 16
GLA_GATE_TAU = 16.0
GLA_NORM_EPS = 1e-5
RW_HEAD = 64
RW_HEADS = 16
RW_PAIRS = RW_HEADS // 2
RW_GN_EPS = 64e-5
RW_SHIFT_W = 3328
PEER_HEADS = 8
PEER_NKEYS = 128
PEER_TOPK = 16
NORM_EPS = 1e-6
LANES = 128

COL_R, COL_K, COL_V = 0, 1024, 2048
COL_MG = 3072
COL_GG = 5120
COL_GV = 6144
COL_GQ = 7168
COL_GK = 7680
COL_LORA = 8192
COL_GA = 8448
PROJ_PAD_W = 8704
PROJ_TILE_N = 4352
PROJ_TILE_M = 1024

CHUNK = 64
TOKEN_TILE = 512
PEER_EXPERT_TILE = 512
PEER_SUBTILES = 4
VMEM_LIMIT = 56 * 1024 * 1024

NEG_INF = float("-inf")


def _mm(a, b):
    return jnp.dot(a.astype(BF16), b.astype(BF16), preferred_element_type=F32)


def _mm_nt(a, b):
    return lax.dot_general(a.astype(BF16), b.astype(BF16), (((1,), (1,)), ((), ())),
                           preferred_element_type=F32)


def _mm_tn(a, b):
    return lax.dot_general(a.astype(BF16), b.astype(BF16), (((0,), (0,)), ((), ())),
                           preferred_element_type=F32)


def _split(x):
    hi = x.astype(BF16)
    lo = (x - hi.astype(F32)).astype(BF16)
    return hi, lo


def _mm_exact_lhs(a, x):
    hi, lo = _split(x)
    a = a.astype(BF16)
    return (jnp.dot(a, hi, preferred_element_type=F32)
            + jnp.dot(a, lo, preferred_element_type=F32))


def _mm_exact_rhs(x, a):
    hi, lo = _split(x)
    a = a.astype(BF16)
    return (jnp.dot(hi, a, preferred_element_type=F32)
            + jnp.dot(lo, a, preferred_element_type=F32))


def _softplus(x):
    return jnp.maximum(x, 0.0) + jnp.log1p(jnp.exp(-jnp.abs(x)))


def _rms(x, g):
    return x * lax.rsqrt(jnp.mean(x * x, axis=-1, keepdims=True) + NORM_EPS) * g


def _params(sem, flags=None):
    return pltpu.CompilerParams(dimension_semantics=sem, vmem_limit_bytes=VMEM_LIMIT, flags=flags)


def _mod_kernel(c_ref, w_ref, b_ref, o_ref):
    c = c_ref[...]
    o_ref[...] = _mm(c * jax.nn.sigmoid(c), w_ref[...]) + b_ref[...]


def _modulation(c, w_ada, b_ada):
    rows = c.shape[0]
    n = w_ada.shape[1]
    tn = 1536
    return pl.pallas_call(
        _mod_kernel,
        grid=(n // tn,),
        in_specs=[pl.BlockSpec((rows, D_MODEL), lambda j: (0, 0)),
                  pl.BlockSpec((D_MODEL, tn), lambda j: (0, j)),
                  pl.BlockSpec((1, tn), lambda j: (0, j))],
        out_specs=pl.BlockSpec((rows, tn), lambda j: (0, j)),
        out_shape=jax.ShapeDtypeStruct((rows, n), F32),
        compiler_params=_params(("arbitrary",)),
        name="adaln_mod",
    )(c, w_ada, b_ada.reshape(1, n))


class _Group:
    def __init__(self, batch, seq, valid):
        self.B, self.T, self.Tv = batch, seq, valid
        self.M = batch * seq
        self.tm = min(TOKEN_TILE, self.M)
        self.per_batch_mod = seq % self.tm == 0

    def seq_operand(self, arr):
        if self.per_batch_mod:
            return arr.reshape(self.B, 1, arr.shape[-1])
        return jnp.repeat(arr, self.T, axis=0)

    def seq_spec(self, width, col, tm=None, tile_index=None):
        tm, T = tm or self.tm, self.T
        tile = tile_index or (lambda i, *_: i)
        if self.per_batch_mod:
            return pl.BlockSpec((None, 1, width), lambda *g: ((tile(*g) * tm) // T, 0, col))
        return pl.BlockSpec((tm, width), lambda *g: (tile(*g), col))

    def mod_spec(self, col, tm=None, tile_index=None):
        return self.seq_spec(D_MODEL, col, tm, tile_index)


def _proj_kernel(x_ref, shift_ref, scale_ref, g_ref, w_ref, p_ref, h_scr):
    @pl.when(pl.program_id(1) == 0)
    def _():
        h = _rms(x_ref[...], g_ref[...]) * (1.0 + scale_ref[...]) + shift_ref[...]
        h_scr[...] = h.astype(BF16)

    p_ref[...] = jnp.dot(h_scr[...], w_ref[...], preferred_element_type=F32).astype(p_ref.dtype)


def _projection(grp, x2d, modop, g_pre, w_proj, dtype):
    tm = PROJ_TILE_M if grp.per_batch_mod and grp.T % PROJ_TILE_M == 0 else grp.tm
    return pl.pallas_call(
        _proj_kernel,
        grid=(grp.M // tm, PROJ_PAD_W // PROJ_TILE_N),
        in_specs=[pl.BlockSpec((tm, D_MODEL), lambda i, j: (i, 0)),
                  grp.mod_spec(0, tm), grp.mod_spec(1, tm),
                  pl.BlockSpec((1, D_MODEL), lambda i, j: (0, 0)),
                  pl.BlockSpec((D_MODEL, PROJ_TILE_N), lambda i, j: (0, j))],
        out_specs=pl.BlockSpec((tm, PROJ_TILE_N), lambda i, j: (i, j)),
        out_shape=jax.ShapeDtypeStruct((grp.M, PROJ_PAD_W), dtype),
        scratch_shapes=[pltpu.VMEM((tm, D_MODEL), BF16)],
        compiler_params=_params(("arbitrary", "arbitrary")),
        name="in_proj",
    )(x2d, modop, modop, g_pre, w_proj)


def _gla_kernel(q_ref, k_ref, v_ref, ga_ref, wg_ref, bg_ref, s0_ref, og_ref, sout_ref, st_scr,
                *, chunk, n_chunks, tile, valid, seqs):
    t = pl.program_id(1)
    last = pl.num_programs(1) - 1
    row = lax.broadcasted_iota(jnp.int32, (tile, tile), 0)
    col = lax.broadcasted_iota(jnp.int32, (tile, tile), 1)
    shift = chunk.bit_length() - 1
    causal = (lax.shift_right_logical(row, shift) == lax.shift_right_logical(col, shift)) & (row >= col)
    tril = jnp.where(causal, 1.0, 0.0).astype(BF16)

    batch, heads, chunks = range(seqs), range(GLA_HEADS), range(n_chunks)
    pairs = [(b, h) for b in batch for h in heads]
    ks = [slice(h * GLA_DK, (h + 1) * GLA_DK) for h in heads]
    vs = [slice(h * GLA_DV, (h + 1) * GLA_DV) for h in heads]
    rows = [slice(c * chunk, (c + 1) * chunk) for c in chunks]
    ends = [slice((c + 1) * chunk - 1, (c + 1) * chunk) for c in chunks]

    @pl.when(t == 0)
    def _():
        for b, h in pairs:
            st_scr[b, h] = s0_ref[b, h].T

    z = [_mm(ga_ref[b], wg_ref[...]) + bg_ref[...] for b in batch]
    la = [-_softplus(-z[b]) / GLA_GATE_TAU for b in batch]
    k = [k_ref[b].astype(F32) for b in batch]
    v = {(b, h): v_ref[b, :, vs[h]] for b, h in pairs}
    if valid is not None:
        ok = (t * tile + lax.broadcasted_iota(jnp.int32, (tile, 1), 0)) < valid
        la = [jnp.where(ok, x, 0.0) for x in la]
        k = [jnp.where(ok, x, 0.0) for x in k]
        v = {key: jnp.where(ok, x, 0.0) for key, x in v.items()}
    cum = [_mm_exact_lhs(tril, la[b]) for b in batch]
    end = [jnp.concatenate([jnp.broadcast_to(cum[b][ends[c], :], (chunk, cum[b].shape[1])) for c in chunks], axis=0)
           for b in batch]
    qe = [q_ref[b].astype(F32) * (GLA_DK ** -0.5) * jnp.exp(cum[b]) for b in batch]
    ke = [k[b] * jnp.exp(-cum[b]) for b in batch]
    kd = [k[b] * jnp.exp(end[b] - cum[b]) for b in batch]
    decay = [[jnp.exp(cum[b][ends[c], :]) for c in chunks] for b in batch]
    a = {(b, h): jnp.where(causal, _mm_nt(qe[b][:, ks[h]], ke[b][:, ks[h]]), 0.0) for b, h in pairs}
    inside = {key: _mm(a[key], v[key]) for key in pairs}
    update = {(b, h, c): _mm_tn(v[b, h][rows[c], :], kd[b][rows[c], ks[h]]) for b, h in pairs for c in chunks}

    st = {(b, h, 0): st_scr[b, h] for b, h in pairs}
    for c in chunks:
        for b, h in pairs:
            st[b, h, c + 1] = st[b, h, c] * decay[b][c][:, ks[h]] + update[b, h, c]
    outer = {(b, h, c): _mm_nt(qe[b][rows[c], ks[h]], st[b, h, c]) for b, h in pairs for c in chunks}
    for b, h in pairs:
        st_scr[b, h] = st[b, h, n_chunks]
        for c in chunks:
            o = outer[b, h, c] + inside[b, h][rows[c], :]
            og = o * lax.rsqrt(jnp.mean(o * o, axis=-1, keepdims=True) + GLA_NORM_EPS)
            og_ref[b, rows[c], vs[h]] = og.astype(og_ref.dtype)

    @pl.when(t == last)
    def _():
        for b, h in pairs:
            sout_ref[b, h] = st_scr[b, h].T


def _seqs_per_step(batch, seq):
    n = max(1, min(8, 64 // seq))
    while batch % n:
        n //= 2
    return n


def _gla(grp, p3, wg_pad, bg, s0):
    B, T = grp.B, grp.T
    tile = min(T, 4 * CHUNK)
    chunk = min(CHUNK, tile)
    nb = max(_seqs_per_step(B, T), 4 if B % 4 == 0 else 2 - B % 2)
    kern = functools.partial(_gla_kernel, chunk=chunk, n_chunks=tile // chunk, tile=tile,
                             valid=None if grp.Tv == T else grp.Tv, seqs=nb)
    kw, vw = GLA_HEADS * GLA_DK, GLA_HEADS * GLA_DV
    st_spec = pl.BlockSpec((nb, GLA_HEADS, GLA_DK, GLA_DV), lambda i, t: (i, 0, 0, 0))
    return pl.pallas_call(
        kern,
        grid=(B // nb, T // tile),
        in_specs=[pl.BlockSpec((nb, tile, kw), lambda i, t: (i, t, COL_GQ // kw)),
                  pl.BlockSpec((nb, tile, kw), lambda i, t: (i, t, COL_GK // kw)),
                  pl.BlockSpec((nb, tile, vw), lambda i, t: (i, t, COL_GV // vw)),
                  pl.BlockSpec((nb, tile, LANES), lambda i, t: (i, t, COL_GA // LANES)),
                  pl.BlockSpec((LANES, kw), lambda i, t: (0, 0)),
                  pl.BlockSpec((1, kw), lambda i, t: (0, 0)),
                  st_spec],
        out_specs=[pl.BlockSpec((nb, tile, vw), lambda i, t: (i, t, 0)), st_spec],
        out_shape=[jax.ShapeDtypeStruct((B, T, vw), BF16),
                   jax.ShapeDtypeStruct((B, GLA_HEADS, GLA_DK, GLA_DV), F32)],
        scratch_shapes=[pltpu.VMEM((nb, GLA_HEADS, GLA_DV, GLA_DK), F32)],
        compiler_params=_params(("arbitrary", "arbitrary")),
        name="gla_scan",
    )(p3, p3, p3, p3, wg_pad, bg, s0)


def _rwkv_pairs_chunk(at, rt, bt, kt, v, states, *, chunk, same_head, strict, incl):
    n2 = 2 * chunk
    wide = n2 % LANES == 0
    pairs = range(len(states))

    def blocks(x):
        return jnp.where(same_head, jnp.concatenate([x, x], axis=0), 0.0).astype(BF16)

    a2 = [blocks(x) for x in at]
    r2 = [blocks(x) for x in rt]
    b2 = [blocks(x) for x in bt]
    k2 = [blocks(x) for x in kt]
    v2 = [blocks(x) for x in v]
    ar = [jnp.concatenate([a2[p], r2[p]], axis=0) for p in pairs]
    bk = [jnp.concatenate([b2[p], k2[p]], axis=0) for p in pairs]
    if wide:
        g = [_mm_nt(ar[p], bk[p]) for p in pairs]
        l_ab = [jnp.where(strict, g[p][:n2, :n2], 0.0) for p in pairs]
        l_ak = [jnp.where(strict, g[p][:n2, n2:], 0.0) for p in pairs]
        m_rb = [jnp.where(incl, g[p][n2:, :n2], 0.0) for p in pairs]
        m_rk = [jnp.where(incl, g[p][n2:, n2:], 0.0) for p in pairs]
    else:
        l_ab = [jnp.where(strict, _mm_nt(a2[p], b2[p]), 0.0) for p in pairs]
        l_ak = [jnp.where(strict, _mm_nt(a2[p], k2[p]), 0.0) for p in pairs]
        m_rb = [jnp.where(incl, _mm_nt(r2[p], b2[p]), 0.0) for p in pairs]
        m_rk = [jnp.where(incl, _mm_nt(r2[p], k2[p]), 0.0) for p in pairs]
    from_state = [_mm_nt(ar[p], states[p]) for p in pairs]
    x = [from_state[p][:n2] + _mm(l_ak[p], v2[p]) for p in pairs]
    pw = l_ab
    for _ in range(chunk.bit_length() - 2):
        if wide:
            y = [_mm(pw[p], jnp.concatenate([x[p], pw[p]], axis=1)) for p in pairs]
            x = [x[p] + y[p][:, :LANES] for p in pairs]
            pw = [y[p][:, LANES:] for p in pairs]
        else:
            x = [x[p] + _mm(pw[p], x[p]) for p in pairs]
            pw = [_mm(pw[p], pw[p]) for p in pairs]
    u = [x[p] + _mm(pw[p], x[p]) for p in pairs]
    uv = [jnp.concatenate([u[p].astype(BF16), v2[p]], axis=0) for p in pairs]
    if wide:
        o2 = [from_state[p][n2:] + _mm(jnp.concatenate([m_rb[p], m_rk[p]], axis=1), uv[p]) for p in pairs]
    else:
        o2 = [from_state[p][n2:] + _mm(m_rb[p], u[p]) + _mm(m_rk[p], v2[p]) for p in pairs]
    s_new = [states[p] + _mm_tn(uv[p], bk[p]) for p in pairs]
    return o2, s_new


def _rwkv_kernel(r_ref, k_ref, v_ref, l_ref, shift0_ref, mu_ref, mul_ref, w0_ref, w2_ref, a0_ref, a2_ref, g2_ref,
                 kk_ref, ka_ref, bd_ref, gw_ref, gb_ref, rk_ref, s0_ref,
                 o_ref, gate_ref, sout_ref, s_scr, last_scr, *, chunk, seqs, valid):
    t = pl.program_id(1)
    zeros = jnp.zeros((RW_HEAD, RW_HEAD), F32)
    W = RW_HEADS * RW_HEAD

    @pl.when(t == 0)
    def _():
        for q in range(seqs):
            last_scr[q] = shift0_ref[q]
            for p in range(RW_PAIRS):
                top = jnp.concatenate([s0_ref[q, 2 * p], zeros], axis=1)
                bot = jnp.concatenate([zeros, s0_ref[q, 2 * p + 1]], axis=1)
                s_scr[q, p] = jnp.concatenate([top, bot], axis=0)

    n2 = 2 * chunk
    row = lax.broadcasted_iota(jnp.int32, (n2, n2), 0)
    col = lax.broadcasted_iota(jnp.int32, (n2, n2), 1)

    def second(idx, size):
        return jnp.where(idx >= size, 1, 0)

    same = second(row, chunk) == second(col, chunk)
    strict = same & (row > col)
    incl = same & (row >= col)
    same_head = (second(lax.broadcasted_iota(jnp.int32, (n2, LANES), 0), chunk)
                 == second(lax.broadcasted_iota(jnp.int32, (n2, LANES), 1), RW_HEAD))
    trow = lax.broadcasted_iota(jnp.int32, (chunk, chunk), 0)
    tcol = lax.broadcasted_iota(jnp.int32, (chunk, chunk), 1)
    tril = jnp.where(trow >= tcol, 1.0, 0.0).astype(BF16)
    first_head = lax.broadcasted_iota(jnp.int32, (chunk, LANES), 1) < RW_HEAD

    def head_sum(x):
        lo = jnp.sum(jnp.where(first_head, x, 0.0), axis=-1, keepdims=True)
        hi = jnp.sum(jnp.where(first_head, 0.0, x), axis=-1, keepdims=True)
        return jnp.where(first_head, lo, hi)

    lanes = [slice(p * LANES, (p + 1) * LANES) for p in range(RW_PAIRS)]
    problems = [(q, p) for q in range(seqs) for p in range(RW_PAIRS)]
    batch = range(seqs)
    local = lax.broadcasted_iota(jnp.int32, (chunk, 1), 0)

    def shifted(ref, q, lo, hi, mu):
        cur = ref[q].astype(F32)
        prev = jnp.where(local == 0, last_scr[q, :, lo:hi], pltpu.roll(cur, 1, 0))
        last_scr[q, :, lo:hi] = cur[chunk - 1:chunk, :]
        return cur + (prev - cur) * mu

    rs = [shifted(r_ref, q, 0, W, mu_ref[:, 0:W]) for q in batch]
    kin = [shifted(k_ref, q, W, 2 * W, mu_ref[:, W:2 * W]) for q in batch]
    vs = [shifted(v_ref, q, 2 * W, 3 * W, mu_ref[:, 2 * W:3 * W]) for q in batch]
    lo = jnp.concatenate([shifted(l_ref, q, 3 * W, 3 * W + 256, mul_ref[...]) for q in batch], axis=0)
    wpre = w0_ref[...] + _mm(jnp.tanh(lo), w2_ref[...])
    lw_all = -jnp.exp(-_softplus(-wpre) - 0.5)
    a_all = jax.nn.sigmoid(a0_ref[...] + _mm(lo, a2_ref[...]))
    gate_all = _mm(jax.nn.sigmoid(lo), g2_ref[...])
    kk_all = jnp.concatenate(kin, axis=0) * kk_ref[...]
    kk_all = kk_all / jnp.maximum(jnp.sqrt(_mm(kk_all * kk_all, bd_ref[...])), 1e-12)
    if valid is not None:
        ok = (t * chunk + local) < valid

    ks, ats, rts, bts, kts, ends = [], [], [], [], [], []
    for q in batch:
        span = slice(q * chunk, (q + 1) * chunk)
        lw, a, kk = lw_all[span, :], a_all[span, :], kk_all[span, :]
        k = kin[q] * (1.0 + (a - 1.0) * ka_ref[...])
        an, bb = -kk, kk * a
        gate_ref[q] = gate_all[span, :].astype(gate_ref.dtype)
        if valid is not None:
            lw = jnp.where(ok, lw, 0.0)
            an = jnp.where(ok, an, 0.0)
            bb = jnp.where(ok, bb, 0.0)
            k = jnp.where(ok, k, 0.0)
            vs[q] = jnp.where(ok, vs[q], 0.0)
        cum = _mm_exact_lhs(tril, lw)
        w_incl = jnp.exp(cum)
        w_inv = jnp.exp(-cum)
        ks.append(k)
        ats.append(an * jnp.exp(cum - lw))
        rts.append(rs[q] * w_incl)
        bts.append(bb * w_inv)
        kts.append(k * w_inv)
        ends.append(w_incl[chunk - 1:chunk, :])

    def slabs(xs):
        return [xs[q][:, lanes[p]] for q, p in problems]

    o2, s_new = _rwkv_pairs_chunk(
        slabs(ats), slabs(rts), slabs(bts), slabs(kts), slabs(vs), [s_scr[q, p] for q, p in problems],
        chunk=chunk, same_head=same_head, strict=strict, incl=incl)
    for n, (q, p) in enumerate(problems):
        ls = lanes[p]
        s_scr[q, p] = s_new[n] * ends[q][:, ls]
        o = o2[n][:chunk] + o2[n][chunk:]
        mu = head_sum(o) * (1.0 / RW_HEAD)
        oc = o - mu
        var = head_sum(oc * oc) * (1.0 / RW_HEAD)
        on = oc * lax.rsqrt(var + RW_GN_EPS) * gw_ref[:, ls] + gb_ref[:, ls]
        bonus = head_sum(rs[q][:, ls] * ks[q][:, ls] * rk_ref[:, ls]) * vs[q][:, ls]
        o_ref[q, :, ls] = (on + bonus).astype(o_ref.dtype)

    @pl.when(t == pl.num_programs(1) - 1)
    def _():
        for q in range(seqs):
            for p in range(RW_PAIRS):
                s = s_scr[q, p]
                sout_ref[q, 2 * p] = s[:RW_HEAD, :RW_HEAD]
                sout_ref[q, 2 * p + 1] = s[RW_HEAD:, RW_HEAD:]


def _rwkv(grp, p3, shift0, wts, s0):
    B, T = grp.B, grp.T
    chunk = min(T, CHUNK)
    nb = max(_seqs_per_step(B, T), 4 if B % 4 == 0 else 2 - B % 2)
    W = RW_HEADS * RW_HEAD

    def cols(width, blk):
        return pl.BlockSpec((nb, chunk, width), lambda b, t: (b, t, blk))

    def full(shape):
        return pl.BlockSpec(shape, lambda b, t: (0,) * len(shape))

    seq_spec = pl.BlockSpec((nb, chunk, W), lambda b, t: (b, t, 0))
    st_spec = pl.BlockSpec((nb, RW_HEADS, RW_HEAD, RW_HEAD), lambda b, t: (b, 0, 0, 0))
    kern = functools.partial(_rwkv_kernel, chunk=chunk, seqs=nb, valid=None if grp.Tv == T else grp.Tv)
    return pl.pallas_call(
        kern,
        grid=(B // nb, T // chunk),
        in_specs=[cols(W, 0), cols(W, 1), cols(W, 2), cols(256, COL_LORA // 256),
                  pl.BlockSpec((nb, 1, RW_SHIFT_W), lambda b, t: (b, 0, 0)),
                  full((1, 3 * W)), full((1, 256)),
                  full((1, W)), full((256, W)), full((1, W)), full((256, W)), full((256, W)),
                  full((1, W)), full((1, W)), full((W, W)),
                  full((1, W)), full((1, W)), full((1, W)), st_spec],
        out_specs=[seq_spec, seq_spec, st_spec],
        out_shape=[jax.ShapeDtypeStruct((B, T, W), BF16), jax.ShapeDtypeStruct((B, T, W), BF16),
                   jax.ShapeDtypeStruct((B, RW_HEADS, RW_HEAD, RW_HEAD), F32)],
        scratch_shapes=[pltpu.VMEM((nb, RW_PAIRS, 2 * RW_HEAD, 2 * RW_HEAD), F32),
                        pltpu.VMEM((nb, 1, RW_SHIFT_W), F32)],
        compiler_params=_params(("arbitrary", "arbitrary")),
        name="rwkv_scan",
    )(p3, p3, p3, p3, shift0.reshape(B, 1, RW_SHIFT_W),
      wts["mu_rkv"], wts["mu_lora"], wts["rw_w0"], wts["w2p"], wts["rw_a0"], wts["a2p"], wts["g2p"],
      wts["rw_k_k"], wts["rw_k_a"], wts["head_ones"], wts["rw_gn_w"], wts["rw_gn_b"], wts["rw_r_k"], s0)


def _merge_kernel(og_ref, gg_ref, orw_ref, gate_ref, mga_ref, mgb_ref, x_ref,
                  gate1_ref, shift2_ref, scale2_ref,
                  ggn_ref, gpost_ref, gpre_ref, wbg_ref, wbr_ref, wout_ref, wq_ref,
                  x1_ref, h2t_ref, qp_ref):
    gg = gg_ref[...].astype(F32)
    gla_in = og_ref[...].astype(F32) * ggn_ref[...] * (gg * jax.nn.sigmoid(gg))
    br_gla = _mm(gla_in, wbg_ref[...])
    br_rw = _mm(orw_ref[...].astype(F32) * gate_ref[...].astype(F32), wbr_ref[...])
    merged = (jax.nn.sigmoid(mga_ref[...].astype(F32)) * br_gla
              + jax.nn.sigmoid(mgb_ref[...].astype(F32)) * br_rw)
    y = _mm(merged, wout_ref[...])
    x1 = x_ref[...] + gate1_ref[...] * _rms(y, gpost_ref[...])
    x1_ref[...] = x1
    h2 = _rms(x1, gpre_ref[...]) * (1.0 + scale2_ref[...]) + shift2_ref[...]
    h2t_ref[...] = h2.T.astype(BF16)
    qp_ref[...] = jnp.dot(h2.astype(BF16), wq_ref[...], preferred_element_type=F32).astype(BF16)


def _merge(grp, og, p2, orw, gate_rw, x2d, modop, wts):
    tm, M = grp.tm, grp.M
    W = D_MODEL

    def tok(blk):
        return pl.BlockSpec((tm, W), lambda i: (i, blk))

    def full(shape):
        return pl.BlockSpec(shape, lambda i: (0,) * len(shape), pipeline_mode=pl.Buffered(1))

    return pl.pallas_call(
        _merge_kernel,
        grid=(M // tm,),
        in_specs=[tok(0), tok(COL_GG // W), tok(0), tok(0), tok(COL_MG // W), tok(COL_MG // W + 1), tok(0),
                  grp.mod_spec(2, tm), grp.mod_spec(3, tm), grp.mod_spec(4, tm),
                  full((1, W)), full((1, W)), full((1, W)),
                  full((W, W)), full((W, W)), full((W, W)), full((W, 2 * W))],
        out_specs=[tok(0), pl.BlockSpec((W, tm), lambda i: (0, i)), pl.BlockSpec((tm, 2 * W), lambda i: (i, 0))],
        out_shape=[jax.ShapeDtypeStruct((M, W), F32), jax.ShapeDtypeStruct((W, M), BF16),
                   jax.ShapeDtypeStruct((M, 2 * W), BF16)],
        compiler_params=_params(("arbitrary",)),
        name="merge",
    )(og, p2, orw, gate_rw, p2, p2, x2d, modop, modop, modop,
      wts["g_gla_norm"], wts["g_post_mix"], wts["g_pre_ffn"],
      wts["w_b_gla"], wts["w_b_rw"], wts["w_out"], wts["peer_wq"])


def _pair_bits(x):
    bits = pltpu.bitcast(x.astype(BF16).astype(F32), jnp.uint32)
    return bits | (bits >> 16)


def _row_tile(row, rows):
    tile = pltpu.bitcast(jnp.broadcast_to(row, (8, row.shape[1])), BF16)
    return jnp.tile(tile, (rows // 16, 1))


def _batcher_network(n):
    def merge(lo, hi, r):
        step = r * 2
        if step < hi - lo:
            yield from merge(lo, hi, step)
            yield from merge(lo + r, hi, step)
            yield from [(i, i + r) for i in range(lo + r, hi - r, step)]
        else:
            yield (lo, lo + r)

    def sort(lo, hi):
        if hi - lo >= 1:
            mid = lo + (hi - lo) // 2
            yield from sort(lo, mid)
            yield from sort(mid + 1, hi)
            yield from merge(lo, hi, 1)

    return list(sort(0, n - 1))


def _top_values(s, n):
    sub = 8
    depth = s.shape[0] // sub
    cols = [s[i * sub:(i + 1) * sub, :] for i in range(depth)]
    for i, j in _batcher_network(depth):
        cols[i], cols[j] = jnp.maximum(cols[i], cols[j]), jnp.minimum(cols[i], cols[j])
    rows = lax.broadcasted_iota(jnp.int32, (n, s.shape[1]), 0)
    vals = jnp.zeros((n, s.shape[1]), F32)
    for it in range(n):
        m = jnp.max(cols[0], axis=0, keepdims=True)
        vals = jnp.where(rows == it, m, vals)
        hit = cols[0] == m
        for i in range(min(depth - 1, n - 1 - it)):
            cols[i] = jnp.where(hit, cols[i + 1], cols[i])
    return vals


def _count_prefix(vals, pred):
    def row(i):
        return vals[i:i + 1, :]

    c8 = pred(row(7))
    c4 = pred(jnp.where(c8, row(11), row(3)))
    c2 = pred(jnp.where(c8, jnp.where(c4, row(13), row(9)), jnp.where(c4, row(5), row(1))))
    upper = jnp.where(c4, jnp.where(c2, row(14), row(12)), jnp.where(c2, row(10), row(8)))
    lower = jnp.where(c4, jnp.where(c2, row(6), row(4)), jnp.where(c2, row(2), row(0)))
    c1 = pred(jnp.where(c8, upper, lower))
    c16 = pred(row(15))
    return (jnp.where(c8, 8.0, 0.0) + jnp.where(c4, 4.0, 0.0) + jnp.where(c2, 2.0, 0.0)
            + jnp.where(c1, 1.0, 0.0) + jnp.where(c16, 1.0, 0.0))


def _rank_among(s, vals):
    return _count_prefix(vals, lambda x: x > s)


def _candidate_groups(v1, v2):
    n, tm = v1.shape
    split = 4
    r_all = lax.broadcasted_iota(jnp.int32, (n, tm), 0)
    r_half = lax.broadcasted_iota(jnp.int32, (n // 2, tm), 0)
    groups = [v1[0:1, :] + v2]
    for a in range(1, split):
        groups.append(jnp.where(r_half < n // (a + 1), v1[a:a + 1, :] + v2[:n // 2, :], NEG_INF))
    groups.append(jnp.where(r_all >= split, v1 + v2[0:1, :], NEG_INF))
    for b in range(1, n // (split + 1)):
        keep = (r_half >= split) & (r_half < n // (b + 1))
        groups.append(jnp.where(keep, v1[:n // 2, :] + v2[b:b + 1, :], NEG_INF))
    return groups


def _group_max(groups, n):
    full = [g for g in groups if g.shape[0] == n]
    best = full[0]
    for g in full[1:]:
        best = jnp.maximum(best, g)
    best = jnp.maximum(best[:n // 2, :], best[n // 2:, :])
    for g in groups:
        if g.shape[0] != n:
            best = jnp.maximum(best, g)
    return jnp.max(best, axis=0, keepdims=True)


def _select_kernel(qp_ref, keys_ref, r2_ref, e2_ref, al_ref, n1_ref):
    tm = qp_ref.shape[0]
    for h in range(PEER_HEADS):
        q1 = qp_ref[:, (2 * h) * 128:(2 * h + 1) * 128]
        q2 = qp_ref[:, (2 * h + 1) * 128:(2 * h + 2) * 128]
        s1 = _mm_nt(keys_ref[2 * h], q1)
        s2 = _mm_nt(keys_ref[2 * h + 1], q2)
        v1 = _top_values(s1, PEER_TOPK)
        v2 = _top_values(s2, PEER_TOPK)
        rank2 = _rank_among(s2, v2)
        groups = _candidate_groups(v1, v2)
        cmax = v1[0:1, :] + v2[0:1, :]
        z = jnp.zeros((1, tm), F32)
        m = cmax
        for _ in range(PEER_TOPK):
            m = _group_max(groups, PEER_TOPK)
            z = z + jnp.exp(m - cmax)
            groups = [jnp.where(g == m, NEG_INF, g) for g in groups]
        th = m - s1
        n1 = _count_prefix(v2, lambda x: x >= th)
        r2_ref[h] = rank2.astype(BF16)
        e2_ref[h] = jnp.exp(s2 - v2[0:1, :]).astype(BF16)
        al_ref[h] = _pair_bits(jnp.exp(s1 - v1[0:1, :]) / z)
        n1_ref[h] = _pair_bits(n1)


def _peer_select(grp, qp, keys):
    tm, M = grp.tm, grp.M
    rows = pl.BlockSpec((PEER_HEADS, PEER_NKEYS, tm), lambda i: (0, 0, i))
    return pl.pallas_call(
        _select_kernel,
        grid=(M // tm,),
        in_specs=[pl.BlockSpec((tm, 2 * D_MODEL), lambda i: (i, 0)),
                  pl.BlockSpec((2 * PEER_HEADS, PEER_NKEYS, 128), lambda i: (0, 0, 0))],
        out_specs=[rows] * 4,
        out_shape=[jax.ShapeDtypeStruct((PEER_HEADS, PEER_NKEYS, M), BF16)] * 2
        + [jax.ShapeDtypeStruct((PEER_HEADS, PEER_NKEYS, M), jnp.uint32)] * 2,
        compiler_params=_params(("arbitrary",)),
        name="peer_select",
    )(qp, keys)


def _peer_weights(act_ref, y_ref, r2_ref, e2_ref, al_ref, n1_ref, row0):
    tm = act_ref.shape[1]
    half = PEER_NKEYS // 2
    for j in range(act_ref.shape[0] // PEER_NKEYS):
        i1 = slice(row0 + j, row0 + j + 1)
        w = [jnp.zeros((half, tm), BF16) for _ in range(2)]
        for h in range(PEER_HEADS):
            n1 = _row_tile(n1_ref[h, i1, :], half)
            al = _row_tile(al_ref[h, i1, :], half)
            for r in range(2):
                keys = slice(r * half, (r + 1) * half)
                w[r] = w[r] + jnp.where(r2_ref[h, keys, :] < n1, e2_ref[h, keys, :], 0.0) * al
        for r in range(2):
            rows = slice(j * PEER_NKEYS + r * half, j * PEER_NKEYS + (r + 1) * half)
            a = act_ref[rows, :].astype(BF16)
            y_ref[rows, :] = (0.5 * a * (1.0 + lax.erf(a * 0.7071067811865476))) * w[r]


def _peer_kernel(h2t_ref, u_ref, vt_ref, r2_ref, e2_ref, al_ref, n1_ref, x1_ref, gate2_ref, gpost_ref,
                 o_ref, acc_ref, act_0, act_1, y_0, y_1):
    e = pl.program_id(1)
    ts = act_0.shape[0]
    per = ts // PEER_NKEYS
    acts, ys = (act_0, act_1), (y_0, y_1)

    @pl.when(e == 0)
    def _():
        acc_ref[...] = jnp.zeros_like(acc_ref)

    def scores(k):
        acts[k % 2][...] = jnp.dot(u_ref[k], h2t_ref[...], preferred_element_type=F32)

    def mix(k):
        acc_ref[...] += jnp.dot(vt_ref[k], ys[k % 2][...], preferred_element_type=F32)

    scores(0)
    scores(1)
    for k in range(PEER_SUBTILES):
        _peer_weights(acts[k % 2], ys[k % 2], r2_ref, e2_ref, al_ref, n1_ref, k * per)
        if k + 2 < PEER_SUBTILES:
            scores(k + 2)
        if k >= 1:
            mix(k - 1)
    mix(PEER_SUBTILES - 1)

    @pl.when(e == pl.num_programs(1) - 1)
    def _():
        o_ref[...] = x1_ref[...] + gate2_ref[...] * _rms(acc_ref[...].T, gpost_ref[...])


def _peer(grp, h2t, sel, x1, modop, g_post, pu, pvt):
    tm, M = grp.tm, grp.M
    ts = PEER_EXPERT_TILE
    te = ts * PEER_SUBTILES
    r2, e2, al, n1 = sel
    packed = pl.BlockSpec((PEER_HEADS, PEER_NKEYS, tm), lambda i, e: (0, 0, i))
    rows = pl.BlockSpec((PEER_HEADS, te // PEER_NKEYS, tm), lambda i, e: (0, e, i))
    tok = pl.BlockSpec((tm, D_MODEL), lambda i, e: (i, 0))
    return pl.pallas_call(
        _peer_kernel,
        grid=(M // tm, pu.shape[0] // PEER_SUBTILES),
        in_specs=[pl.BlockSpec((D_MODEL, tm), lambda i, e: (0, i)),
                  pl.BlockSpec((PEER_SUBTILES, ts, D_MODEL), lambda i, e: (e, 0, 0)),
                  pl.BlockSpec((PEER_SUBTILES, D_MODEL, ts), lambda i, e: (e, 0, 0)),
                  packed, packed, rows, rows,
                  pl.BlockSpec((tm, D_MODEL), lambda i, e: (i, 0), pipeline_mode=pl.Buffered(1)),
                  grp.mod_spec(5),
                  pl.BlockSpec((1, D_MODEL), lambda i, e: (0, 0))],
        out_specs=tok,
        out_shape=jax.ShapeDtypeStruct((M, D_MODEL), F32),
        scratch_shapes=[pltpu.VMEM((D_MODEL, tm), F32),
                        pltpu.VMEM((ts, tm), F32), pltpu.VMEM((ts, tm), F32),
                        pltpu.VMEM((ts, tm), BF16), pltpu.VMEM((ts, tm), BF16)],
        compiler_params=_params(("arbitrary", "arbitrary")),
        name="peer_dense",
    )(h2t, pu, pvt, r2, e2, al, n1, x1, modop, g_post)


def _transpose_cast_kernel(x_ref, o_ref):
    rows = o_ref.shape[2]
    for t in range(o_ref.shape[0]):
        o_ref[t] = x_ref[t * rows:(t + 1) * rows, :].T.astype(o_ref.dtype)


def _transposed_tiles(x, rows, per_step=2):
    n, d = x.shape
    return pl.pallas_call(
        _transpose_cast_kernel,
        grid=(n // (rows * per_step),),
        in_specs=[pl.BlockSpec((rows * per_step, d), lambda k: (k, 0))],
        out_specs=pl.BlockSpec((per_step, d, rows), lambda k: (k, 0, 0)),
        out_shape=jax.ShapeDtypeStruct((n // rows, d, rows), BF16),
        compiler_params=_params(("arbitrary",)),
        name="transpose_cast",
    )(x)


def _prepare_weights(w_in, w_gla_gate, b_gla_gate, g_gla_norm, mu_shift, rw_w0, rw_w2, rw_a0, rw_a2,
                     rw_g2, rw_k_k, rw_k_a, rw_r_k, rw_gn_w, rw_gn_b, w_b_gla, w_b_rw, w_out,
                     peer_wq, peer_keys, peer_u, peer_v, g_pre_mix, g_post_mix, g_pre_ffn, g_post_ffn):
    W = RW_HEADS * RW_HEAD
    rw = w_in[:, 3088:6416]
    w_proj = jnp.concatenate(
        [rw[:, :3 * W], w_in[:, 6416:8464], w_in[:, 2048:3072], w_in[:, 1024:2048], w_in[:, 0:1024],
         rw[:, 3 * W:], w_in[:, 3072:3088], jnp.zeros((D_MODEL, PROJ_PAD_W - 8464), F32)],
        axis=1).astype(BF16)
    head = jnp.arange(W) // RW_HEAD
    zeros64 = jnp.zeros((64, W), F32)
    return dict(
        w_proj=w_proj,
        wg_pad=jnp.concatenate([w_gla_gate, jnp.zeros((128 - GLA_GATE_RANK, 512), F32)], axis=0),
        bg=b_gla_gate.reshape(1, -1),
        g_gla_norm=g_gla_norm.reshape(1, -1),
        mu_rkv=mu_shift[:3 * W].reshape(1, -1),
        mu_lora=mu_shift[3 * W:].reshape(1, -1),
        rw_w0=rw_w0.reshape(1, -1),
        w2p=jnp.concatenate([rw_w2, jnp.zeros((192, W), F32)], axis=0),
        rw_a0=rw_a0.reshape(1, -1),
        a2p=jnp.concatenate([zeros64, rw_a2, jnp.zeros((128, W), F32)], axis=0),
        g2p=jnp.concatenate([jnp.zeros((128, W), F32), rw_g2], axis=0),
        rw_k_k=rw_k_k.reshape(1, -1),
        rw_k_a=rw_k_a.reshape(1, -1),
        rw_r_k=rw_r_k.reshape(1, -1),
        rw_gn_w=rw_gn_w.reshape(1, -1),
        rw_gn_b=rw_gn_b.reshape(1, -1),
        head_ones=(head[:, None] == head[None, :]).astype(BF16),
        w_b_gla=w_b_gla.astype(BF16),
        w_b_rw=w_b_rw.astype(BF16),
        w_out=w_out.astype(BF16),
        peer_wq=peer_wq.astype(BF16),
        peer_keys=peer_keys.reshape(2 * PEER_HEADS, PEER_NKEYS, 128).astype(BF16),
        peer_u=peer_u.astype(BF16).reshape(-1, PEER_EXPERT_TILE, D_MODEL),
        peer_vt=_transposed_tiles(peer_v, PEER_EXPERT_TILE),
        g_pre_mix=g_pre_mix.reshape(1, -1),
        g_post_mix=g_post_mix.reshape(1, -1),
        g_pre_ffn=g_pre_ffn.reshape(1, -1),
        g_post_ffn=g_post_ffn.reshape(1, -1),
    )


def _block(x, mod, s_gla0, s_rw0, shift0, wts):
    B, valid, _ = x.shape
    T = -(-valid // 8) * 8
    x = jnp.pad(x, ((0, 0), (0, T - valid), (0, 0)))
    grp = _Group(B, T, valid)
    W = RW_HEADS * RW_HEAD
    x2d = x.reshape(grp.M, D_MODEL)
    modop = grp.seq_operand(mod)
    p2 = _projection(grp, x2d, modop, wts["g_pre_mix"], wts["w_proj"], BF16)
    p3 = p2.reshape(B, T, PROJ_PAD_W)
    og, s_gla = _gla(grp, p3, wts["wg_pad"], wts["bg"], s_gla0)
    orw, gate, s_rw = _rwkv(grp, p3, shift0, wts, s_rw0)
    x1, h2t, qp = _merge(grp, og.reshape(grp.M, -1), p2, orw.reshape(grp.M, W), gate.reshape(grp.M, W),
                        x2d, modop, wts)
    if valid != T:
        grp = _Group(B, valid, valid)
        modop = grp.seq_operand(mod)
        x1 = x1.reshape(B, T, D_MODEL)[:, :valid].reshape(grp.M, D_MODEL)
        h2t = h2t.reshape(D_MODEL, B, T)[:, :, :valid].reshape(D_MODEL, grp.M)
        qp = qp.reshape(B, T, -1)[:, :valid].reshape(grp.M, -1)
    sel = _peer_select(grp, qp, wts["peer_keys"])
    out = _peer(grp, h2t, sel, x1, modop, wts["g_post_ffn"], wts["peer_u"], wts["peer_vt"])
    ends = _Group(B, 1, 1)
    last = _projection(ends, x[:, valid - 1, :], ends.seq_operand(mod), wts["g_pre_mix"], wts["w_proj"], F32)
    shift_new = jnp.concatenate([last[:, :3 * W], last[:, COL_LORA:COL_LORA + 256]], axis=-1)
    return out.reshape(B, grp.T, D_MODEL)[:, :valid], s_gla, s_rw, shift_new


def kernel(x_prompt, x_sample, c_prompt, c_sample, state_gla, state_rwkv, state_shift, w_ada, b_ada, g_pre_mix, g_post_mix, g_pre_ffn, g_post_ffn, w_in, w_gla_gate, b_gla_gate, g_gla_norm, mu_shift, rw_w0, rw_w2, rw_a0, rw_a2, rw_g2, rw_k_k, rw_k_a, rw_r_k, rw_gn_w, rw_gn_b, w_b_gla, w_b_rw, w_out, peer_wq, peer_keys, peer_u, peer_v):
    depth = w_in.shape[0]
    bp = x_prompt.shape[0]
    xp, xs = x_prompt, x_sample
    outs = [[] for _ in range(6)]
    for l in range(depth):
        wts = _prepare_weights(
            w_in[l], w_gla_gate[l], b_gla_gate[l], g_gla_norm[l], mu_shift[l], rw_w0[l], rw_w2[l],
            rw_a0[l], rw_a2[l], rw_g2[l], rw_k_k[l], rw_k_a[l], rw_r_k[l], rw_gn_w[l], rw_gn_b[l],
            w_b_gla[l], w_b_rw[l], w_out[l], peer_wq[l], peer_keys[l], peer_u[l], peer_v[l],
            g_pre_mix[l], g_post_mix[l], g_pre_ffn[l], g_post_ffn[l])
        mod = _modulation(jnp.concatenate([c_prompt, c_sample], axis=0), w_ada[l], b_ada[l])
        zg = jnp.zeros((bp, GLA_HEADS, GLA_DK, GLA_DV), F32)
        zr = jnp.zeros((bp, RW_HEADS, RW_HEAD, RW_HEAD), F32)
        zs = jnp.zeros((bp, RW_SHIFT_W), F32)
        xp, g1, r1, s1 = _block(xp, mod[:bp], zg, zr, zs, wts)
        xs, g2, r2, s2 = _block(xs, mod[bp:], state_gla[l], state_rwkv[l], state_shift[l], wts)
        for acc, val in zip(outs, (g1, r1, s1, g2, r2, s2)):
            acc.append(val)
    stacked = [jnp.stack(o) for o in outs]
    return (xp, xs, *stacked)
```

```python
import functools

import jax
import jax.numpy as jnp
from jax import lax
from jax.experimental import pallas as pl
from jax.experimental.pallas import tpu as pltpu

F32 = jnp.float32
BF16 = jnp.bfloat16

D_MODEL = 1024
GLA_HEADS = 4
GLA_DK = 128
GLA_DV = 256
GLA_GATE_RANK = 16
GLA_GATE_TAU = 16.0
GLA_NORM_EPS = 1e-5
RW_HEAD = 64
RW_HEADS = 16
RW_PAIRS = RW_HEADS // 2
RW_GN_EPS = 64e-5
RW_SHIFT_W = 3328
PEER_HEADS = 8
PEER_NKEYS = 128
PEER_TOPK = 16
NORM_EPS = 1e-6
LANES = 128

COL_R, COL_K, COL_V = 0, 1024, 2048
COL_MG = 3072
COL_GG = 5120
COL_GV = 6144
COL_GQ = 7168
COL_GK = 7680
COL_LORA = 8192
COL_GA = 8448
PROJ_PAD_W = 8704
PROJ_TILE_N = 4352
PROJ_TILE_M = 1024

CHUNK = 64
TOKEN_TILE = 512
PEER_EXPERT_TILE = 512
PEER_SUBTILES = 4
VMEM_LIMIT = 56 * 1024 * 1024

NEG_INF = float("-inf")


def _mm(a, b):
    return jnp.dot(a.astype(BF16), b.astype(BF16), preferred_element_type=F32)


def _mm_nt(a, b):
    return lax.dot_general(a.astype(BF16), b.astype(BF16), (((1,), (1,)), ((), ())),
                           preferred_element_type=F32)


def _mm_tn(a, b):
    return lax.dot_general(a.astype(BF16), b.astype(BF16), (((0,), (0,)), ((), ())),
                           preferred_element_type=F32)


def _split(x):
    hi = x.astype(BF16)
    lo = (x - hi.astype(F32)).astype(BF16)
    return hi, lo


def _mm_exact_lhs(a, x):
    hi, lo = _split(x)
    a = a.astype(BF16)
    return (jnp.dot(a, hi, preferred_element_type=F32)
            + jnp.dot(a, lo, preferred_element_type=F32))


def _mm_exact_rhs(x, a):
    hi, lo = _split(x)
    a = a.astype(BF16)
    return (jnp.dot(hi, a, preferred_element_type=F32)
            + jnp.dot(lo, a, preferred_element_type=F32))


def _softplus(x):
    return jnp.maximum(x, 0.0) + jnp.log1p(jnp.exp(-jnp.abs(x)))


def _rms(x, g):
    return x * lax.rsqrt(jnp.mean(x * x, axis=-1, keepdims=True) + NORM_EPS) * g


def _params(sem, flags=None):
    return pltpu.CompilerParams(dimension_semantics=sem, vmem_limit_bytes=VMEM_LIMIT, flags=flags)


def _mod_kernel(c_ref, w_ref, b_ref, o_ref):
    c = c_ref[...]
    o_ref[...] = _mm(c * jax.nn.sigmoid(c), w_ref[...]) + b_ref[...]


def _modulation(c, w_ada, b_ada):
    rows = c.shape[0]
    n = w_ada.shape[1]
    tn = 1536
    return pl.pallas_call(
        _mod_kernel,
        grid=(n // tn,),
        in_specs=[pl.BlockSpec((rows, D_MODEL), lambda j: (0, 0)),
                  pl.BlockSpec((D_MODEL, tn), lambda j: (0, j)),
                  pl.BlockSpec((1, tn), lambda j: (0, j))],
        out_specs=pl.BlockSpec((rows, tn), lambda j: (0, j)),
        out_shape=jax.ShapeDtypeStruct((rows, n), F32),
        compiler_params=_params(("arbitrary",)),
        name="adaln_mod",
    )(c, w_ada, b_ada.reshape(1, n))


class _Group:
    def __init__(self, batch, seq, valid):
        self.B, self.T, self.Tv = batch, seq, valid
        self.M = batch * seq
        self.tm = min(TOKEN_TILE, self.M)
        self.per_batch_mod = seq % self.tm == 0

    def seq_operand(self, arr):
        if self.per_batch_mod:
            return arr.reshape(self.B, 1, arr.shape[-1])
        return jnp.repeat(arr, self.T, axis=0)

    def seq_spec(self, width, col, tm=None, tile_index=None):
        tm, T = tm or self.tm, self.T
        tile = tile_index or (lambda i, *_: i)
        if self.per_batch_mod:
            return pl.BlockSpec((None, 1, width), lambda *g: ((tile(*g) * tm) // T, 0, col))
        return pl.BlockSpec((tm, width), lambda *g: (tile(*g), col))

    def mod_spec(self, col, tm=None, tile_index=None):
        return self.seq_spec(D_MODEL, col, tm, tile_index)


def _proj_kernel(x_ref, shift_ref, scale_ref, g_ref, w_ref, p_ref, h_scr):
    @pl.when(pl.program_id(1) == 0)
    def _():
        h = _rms(x_ref[...], g_ref[...]) * (1.0 + scale_ref[...]) + shift_ref[...]
        h_scr[...] = h.astype(BF16)

    p_ref[...] = jnp.dot(h_scr[...], w_ref[...], preferred_element_type=F32).astype(p_ref.dtype)


def _projection(grp, x2d, modop, g_pre, w_proj, dtype):
    tm = PROJ_TILE_M if grp.per_batch_mod and grp.T % PROJ_TILE_M == 0 else grp.tm
    return pl.pallas_call(
        _proj_kernel,
        grid=(grp.M // tm, PROJ_PAD_W // PROJ_TILE_N),
        in_specs=[pl.BlockSpec((tm, D_MODEL), lambda i, j: (i, 0)),
                  grp.mod_spec(0, tm), grp.mod_spec(1, tm),
                  pl.BlockSpec((1, D_MODEL), lambda i, j: (0, 0)),
                  pl.BlockSpec((D_MODEL, PROJ_TILE_N), lambda i, j: (0, j))],
        out_specs=pl.BlockSpec((tm, PROJ_TILE_N), lambda i, j: (i, j)),
        out_shape=jax.ShapeDtypeStruct((grp.M, PROJ_PAD_W), dtype),
        scratch_shapes=[pltpu.VMEM((tm, D_MODEL), BF16)],
        compiler_params=_params(("arbitrary", "arbitrary")),
        name="in_proj",
    )(x2d, modop, modop, g_pre, w_proj)


def _gla_kernel(q_ref, k_ref, v_ref, ga_ref, wg_ref, bg_ref, s0_ref, og_ref, sout_ref, st_scr,
                *, chunk, n_chunks, tile, valid, seqs):
    t = pl.program_id(1)
    last = pl.num_programs(1) - 1
    row = lax.broadcasted_iota(jnp.int32, (tile, tile), 0)
    col = lax.broadcasted_iota(jnp.int32, (tile, tile), 1)
    shift = chunk.bit_length() - 1
    causal = (lax.shift_right_logical(row, shift) == lax.shift_right_logical(col, shift)) & (row >= col)
    tril = jnp.where(causal, 1.0, 0.0).astype(BF16)

    batch, heads, chunks = range(seqs), range(GLA_HEADS), range(n_chunks)
    pairs = [(b, h) for b in batch for h in heads]
    ks = [slice(h * GLA_DK, (h + 1) * GLA_DK) for h in heads]
    vs = [slice(h * GLA_DV, (h + 1) * GLA_DV) for h in heads]
    rows = [slice(c * chunk, (c + 1) * chunk) for c in chunks]
    ends = [slice((c + 1) * chunk - 1, (c + 1) * chunk) for c in chunks]

    @pl.when(t == 0)
    def _():
        for b, h in pairs:
            st_scr[b, h] = s0_ref[b, h].T

    z = [_mm(ga_ref[b], wg_ref[...]) + bg_ref[...] for b in batch]
    la = [-_softplus(-z[b]) / GLA_GATE_TAU for b in batch]
    k = [k_ref[b].astype(F32) for b in batch]
    v = {(b, h): v_ref[b, :, vs[h]] for b, h in pairs}
    if valid is not None:
        ok = (t * tile + lax.broadcasted_iota(jnp.int32, (tile, 1), 0)) < valid
        la = [jnp.where(ok, x, 0.0) for x in la]
        k = [jnp.where(ok, x, 0.0) for x in k]
        v = {key: jnp.where(ok, x, 0.0) for key, x in v.items()}
    cum = [_mm_exact_lhs(tril, la[b]) for b in batch]
    end = [jnp.concatenate([jnp.broadcast_to(cum[b][ends[c], :], (chunk, cum[b].shape[1])) for c in chunks], axis=0)
           for b in batch]
    qe = [q_ref[b].astype(F32) * (GLA_DK ** -0.5) * jnp.exp(cum[b]) for b in batch]
    ke = [k[b] * jnp.exp(-cum[b]) for b in batch]
    kd = [k[b] * jnp.exp(end[b] - cum[b]) for b in batch]
    decay = [[jnp.exp(cum[b][ends[c], :]) for c in chunks] for b in batch]
    a = {(b, h): jnp.where(causal, _mm_nt(qe[b][:, ks[h]], ke[b][:, ks[h]]), 0.0) for b, h in pairs}
    inside = {key: _mm(a[key], v[key]) for key in pairs}
    update = {(b, h, c): _mm_tn(v[b, h][rows[c], :], kd[b][rows[c], ks[h]]) for b, h in pairs for c in chunks}

    st = {(b, h, 0): st_scr[b, h] for b, h in pairs}
    for c in chunks:
        for b, h in pairs:
            st[b, h, c + 1] = st[b, h, c] * decay[b][c][:, ks[h]] + update[b, h, c]
    outer = {(b, h, c): _mm_nt(qe[b][rows[c], ks[h]], st[b, h, c]) for b, h in pairs for c in chunks}
    for b, h in pairs:
        st_scr[b, h] = st[b, h, n_chunks]
        for c in chunks:
            o = outer[b, h, c] + inside[b, h][rows[c], :]
            og = o * lax.rsqrt(jnp.mean(o * o, axis=-1, keepdims=True) + GLA_NORM_EPS)
            og_ref[b, rows[c], vs[h]] = og.astype(og_ref.dtype)

    @pl.when(t == last)
    def _():
        for b, h in pairs:
            sout_ref[b, h] = st_scr[b, h].T


def _seqs_per_step(batch, seq):
    n = max(1, min(8, 64 // seq))
    while batch % n:
        n //= 2
    return n


def _gla(grp, p3, wg_pad, bg, s0):
    B, T = grp.B, grp.T
    tile = min(T, 4 * CHUNK)
    chunk = min(CHUNK, tile)
    nb = max(_seqs_per_step(B, T), 4 if B % 4 == 0 else 2 - B % 2)
    kern = functools.partial(_gla_kernel, chunk=chunk, n_chunks=tile // chunk, tile=tile,
                             valid=None if grp.Tv == T else grp.Tv, seqs=nb)
    kw, vw = GLA_HEADS * GLA_DK, GLA_HEADS * GLA_DV
    st_spec = pl.BlockSpec((nb, GLA_HEADS, GLA_DK, GLA_DV), lambda i, t: (i, 0, 0, 0))
    return pl.pallas_call(
        kern,
        grid=(B // nb, T // tile),
        in_specs=[pl.BlockSpec((nb, tile, kw), lambda i, t: (i, t, COL_GQ // kw)),
                  pl.BlockSpec((nb, tile, kw), lambda i, t: (i, t, COL_GK // kw)),
                  pl.BlockSpec((nb, tile, vw), lambda i, t: (i, t, COL_GV // vw)),
                  pl.BlockSpec((nb, tile, LANES), lambda i, t: (i, t, COL_GA // LANES)),
                  pl.BlockSpec((LANES, kw), lambda i, t: (0, 0)),
                  pl.BlockSpec((1, kw), lambda i, t: (0, 0)),
                  st_spec],
        out_specs=[pl.BlockSpec((nb, tile, vw), lambda i, t: (i, t, 0)), st_spec],
        out_shape=[jax.ShapeDtypeStruct((B, T, vw), BF16),
                   jax.ShapeDtypeStruct((B, GLA_HEADS, GLA_DK, GLA_DV), F32)],
        scratch_shapes=[pltpu.VMEM((nb, GLA_HEADS, GLA_DV, GLA_DK), F32)],
        compiler_params=_params(("arbitrary", "arbitrary")),
        name="gla_scan",
    )(p3, p3, p3, p3, wg_pad, bg, s0)


def _rwkv_pairs_chunk(at, rt, bt, kt, v, states, *, chunk, same_head, strict, incl):
    n2 = 2 * chunk
    wide = n2 % LANES == 0
    pairs = range(len(states))

    def blocks(x):
        return jnp.where(same_head, jnp.concatenate([x, x], axis=0), 0.0).astype(BF16)

    a2 = [blocks(x) for x in at]
    r2 = [blocks(x) for x in rt]
    b2 = [blocks(x) for x in bt]
    k2 = [blocks(x) for x in kt]
    v2 = [blocks(x) for x in v]
    ar = [jnp.concatenate([a2[p], r2[p]], axis=0) for p in pairs]
    bk = [jnp.concatenate([b2[p], k2[p]], axis=0) for p in pairs]
    if wide:
        g = [_mm_nt(ar[p], bk[p]) for p in pairs]
        l_ab = [jnp.where(strict, g[p][:n2, :n2], 0.0) for p in pairs]
        l_ak = [jnp.where(strict, g[p][:n2, n2:], 0.0) for p in pairs]
        m_rb = [jnp.where(incl, g[p][n2:, :n2], 0.0) for p in pairs]
        m_rk = [jnp.where(incl, g[p][n2:, n2:], 0.0) for p in pairs]
    else:
        l_ab = [jnp.where(strict, _mm_nt(a2[p], b2[p]), 0.0) for p in pairs]
        l_ak = [jnp.where(strict, _mm_nt(a2[p], k2[p]), 0.0) for p in pairs]
        m_rb = [jnp.where(incl, _mm_nt(r2[p], b2[p]), 0.0) for p in pairs]
        m_rk = [jnp.where(incl, _mm_nt(r2[p], k2[p]), 0.0) for p in pairs]
    from_state = [_mm_nt(ar[p], states[p]) for p in pairs]
    x = [from_state[p][:n2] + _mm(l_ak[p], v2[p]) for p in pairs]
    pw = l_ab
    for _ in range(chunk.bit_length() - 2):
        if wide:
            y = [_mm(pw[p], jnp.concatenate([x[p], pw[p]], axis=1)) for p in pairs]
            x = [x[p] + y[p][:, :LANES] for p in pairs]
            pw = [y[p][:, LANES:] for p in pairs]
        else:
            x = [x[p] + _mm(pw[p], x[p]) for p in pairs]
            pw = [_mm(pw[p], pw[p]) for p in pairs]
    u = [x[p] + _mm(pw[p], x[p]) for p in pairs]
    uv = [jnp.concatenate([u[p].astype(BF16), v2[p]], axis=0) for p in pairs]
    if wide:
        o2 = [from_state[p][n2:] + _mm(jnp.concatenate([m_rb[p], m_rk[p]], axis=1), uv[p]) for p in pairs]
    else:
        o2 = [from_state[p][n2:] + _mm(m_rb[p], u[p]) + _mm(m_rk[p], v2[p]) for p in pairs]
    s_new = [states[p] + _mm_tn(uv[p], bk[p]) for p in pairs]
    return o2, s_new


def _rwkv_kernel(r_ref, k_ref, v_ref, l_ref, shift0_ref, mu_ref, mul_ref, w0_ref, w2_ref, a0_ref, a2_ref, g2_ref,
                 kk_ref, ka_ref, gw_ref, gb_ref, rk_ref, s0_ref,
                 o_ref, gate_ref, sout_ref, s_scr, last_scr, *, chunk, seqs, valid):
    t = pl.program_id(1)
    zeros = jnp.zeros((RW_HEAD, RW_HEAD), F32)
    W = RW_HEADS * RW_HEAD

    @pl.when(t == 0)
    def _():
        for q in range(seqs):
            last_scr[q] = shift0_ref[q]
            for p in range(RW_PAIRS):
                top = jnp.concatenate([s0_ref[q, 2 * p], zeros], axis=1)
                bot = jnp.concatenate([zeros, s0_ref[q, 2 * p + 1]], axis=1)
                s_scr[q, p] = jnp.concatenate([top, bot], axis=0)

    n2 = 2 * chunk
    row = lax.broadcasted_iota(jnp.int32, (n2, n2), 0)
    col = lax.broadcasted_iota(jnp.int32, (n2, n2), 1)

    def second(idx, size):
        return jnp.where(idx >= size, 1, 0)

    same = second(row, chunk) == second(col, chunk)
    strict = same & (row > col)
    incl = same & (row >= col)
    same_head = (second(lax.broadcasted_iota(jnp.int32, (n2, LANES), 0), chunk)
                 == second(lax.broadcasted_iota(jnp.int32, (n2, LANES), 1), RW_HEAD))
    trow = lax.broadcasted_iota(jnp.int32, (chunk, chunk), 0)
    tcol = lax.broadcasted_iota(jnp.int32, (chunk, chunk), 1)
    tril = jnp.where(trow >= tcol, 1.0, 0.0).astype(BF16)
    first_head = lax.broadcasted_iota(jnp.int32, (chunk, LANES), 1) < RW_HEAD

    def head_sum(x):
        lo = jnp.sum(jnp.where(first_head, x, 0.0), axis=-1, keepdims=True)
        hi = jnp.sum(jnp.where(first_head, 0.0, x), axis=-1, keepdims=True)
        return jnp.where(first_head, lo, hi)

    lanes = [slice(p * LANES, (p + 1) * LANES) for p in range(RW_PAIRS)]
    problems = [(q, p) for q in range(seqs) for p in range(RW_PAIRS)]
    batch = range(seqs)
    local = lax.broadcasted_iota(jnp.int32, (chunk, 1), 0)

    def shifted(ref, q, lo, hi, mu):
        cur = ref[q].astype(F32)
        prev = jnp.where(local == 0, last_scr[q, :, lo:hi], pltpu.roll(cur, 1, 0))
        last_scr[q, :, lo:hi] = cur[chunk - 1:chunk, :]
        return cur + (prev - cur) * mu

    rs = [shifted(r_ref, q, 0, W, mu_ref[:, 0:W]) for q in batch]
    kin = [shifted(k_ref, q, W, 2 * W, mu_ref[:, W:2 * W]) for q in batch]
    vs = [shifted(v_ref, q, 2 * W, 3 * W, mu_ref[:, 2 * W:3 * W]) for q in batch]
    lo = jnp.concatenate([shifted(l_ref, q, 3 * W, 3 * W + 256, mul_ref[...]) for q in batch], axis=0)
    wpre = w0_ref[...] + _mm(jnp.tanh(lo), w2_ref[...])
    lw_all = -jnp.exp(-_softplus(-wpre) - 0.5)
    a_all = jax.nn.sigmoid(a0_ref[...] + _mm(lo, a2_ref[...]))
    gate_all = _mm(jax.nn.sigmoid(lo), g2_ref[...])
    kk_all = []
    for q in batch:
        kk = kin[q] * kk_ref[...]
        sq = kk * kk
        norm2 = jnp.concatenate([head_sum(sq[:, ls]) for ls in lanes], axis=1)
        kk_all.append(kk / jnp.maximum(jnp.sqrt(norm2), 1e-12))
    if valid is not None:
        ok = (t * chunk + local) < valid

    ks, ats, rts, bts, kts, ends = [], [], [], [], [], []
    for q in batch:
        span = slice(q * chunk, (q + 1) * chunk)
        lw, a, kk = lw_all[span, :], a_all[span, :], kk_all[q]
        k = kin[q] * (1.0 + (a - 1.0) * ka_ref[...])
        an, bb = -kk, kk * a
        gate_ref[q] = gate_all[span, :].astype(gate_ref.dtype)
        if valid is not None:
            lw = jnp.where(ok, lw, 0.0)
            an = jnp.where(ok, an, 0.0)
            bb = jnp.where(ok, bb, 0.0)
            k = jnp.where(ok, k, 0.0)
            vs[q] = jnp.where(ok, vs[q], 0.0)
        cum = _mm_exact_lhs(tril, lw)
        w_incl = jnp.exp(cum)
        w_inv = jnp.exp(-cum)
        ks.append(k)
        ats.append(an * jnp.exp(cum - lw))
        rts.append(rs[q] * w_incl)
        bts.append(bb * w_inv)
        kts.append(k * w_inv)
        ends.append(w_incl[chunk - 1:chunk, :])

    def slabs(xs):
        return [xs[q][:, lanes[p]] for q, p in problems]

    o2, s_new = _rwkv_pairs_chunk(
        slabs(ats), slabs(rts), slabs(bts), slabs(kts), slabs(vs), [s_scr[q, p] for q, p in problems],
        chunk=chunk, same_head=same_head, strict=strict, incl=incl)
    for n, (q, p) in enumerate(problems):
        ls = lanes[p]
        s_scr[q, p] = s_new[n] * ends[q][:, ls]
        o = o2[n][:chunk] + o2[n][chunk:]
        mu = head_sum(o) * (1.0 / RW_HEAD)
        oc = o - mu
        var = head_sum(oc * oc) * (1.0 / RW_HEAD)
        on = oc * lax.rsqrt(var + RW_GN_EPS) * gw_ref[:, ls] + gb_ref[:, ls]
        bonus = head_sum(rs[q][:, ls] * ks[q][:, ls] * rk_ref[:, ls]) * vs[q][:, ls]
        o_ref[q, :, ls] = (on + bonus).astype(o_ref.dtype)

    @pl.when(t == pl.num_programs(1) - 1)
    def _():
        for q in range(seqs):
            for p in range(RW_PAIRS):
                s = s_scr[q, p]
                sout_ref[q, 2 * p] = s[:RW_HEAD, :RW_HEAD]
                sout_ref[q, 2 * p + 1] = s[RW_HEAD:, RW_HEAD:]


def _rwkv(grp, p3, shift0, wts, s0):
    B, T = grp.B, grp.T
    chunk = min(T, CHUNK)
    nb = max(_seqs_per_step(B, T), 4 if B % 4 == 0 else 2 - B % 2)
    W = RW_HEADS * RW_HEAD

    def cols(width, blk):
        return pl.BlockSpec((nb, chunk, width), lambda b, t: (b, t, blk))

    def full(shape):
        return pl.BlockSpec(shape, lambda b, t: (0,) * len(shape))

    seq_spec = pl.BlockSpec((nb, chunk, W), lambda b, t: (b, t, 0))
    st_spec = pl.BlockSpec((nb, RW_HEADS, RW_HEAD, RW_HEAD), lambda b, t: (b, 0, 0, 0))
    kern = functools.partial(_rwkv_kernel, chunk=chunk, seqs=nb, valid=None if grp.Tv == T else grp.Tv)
    return pl.pallas_call(
        kern,
        grid=(B // nb, T // chunk),
        in_specs=[cols(W, 0), cols(W, 1), cols(W, 2), cols(256, COL_LORA // 256),
                  pl.BlockSpec((nb, 1, RW_SHIFT_W), lambda b, t: (b, 0, 0)),
                  full((1, 3 * W)), full((1, 256)),
                  full((1, W)), full((256, W)), full((1, W)), full((256, W)), full((256, W)),
                  full((1, W)), full((1, W)),
                  full((1, W)), full((1, W)), full((1, W)), st_spec],
        out_specs=[seq_spec, seq_spec, st_spec],
        out_shape=[jax.ShapeDtypeStruct((B, T, W), BF16), jax.ShapeDtypeStruct((B, T, W), BF16),
                   jax.ShapeDtypeStruct((B, RW_HEADS, RW_HEAD, RW_HEAD), F32)],
        scratch_shapes=[pltpu.VMEM((nb, RW_PAIRS, 2 * RW_HEAD, 2 * RW_HEAD), F32),
                        pltpu.VMEM((nb, 1, RW_SHIFT_W), F32)],
        compiler_params=_params(("arbitrary", "arbitrary")),
        name="rwkv_scan",
    )(p3, p3, p3, p3, shift0.reshape(B, 1, RW_SHIFT_W),
      wts["mu_rkv"], wts["mu_lora"], wts["rw_w0"], wts["w2p"], wts["rw_a0"], wts["a2p"], wts["g2p"],
      wts["rw_k_k"], wts["rw_k_a"], wts["rw_gn_w"], wts["rw_gn_b"], wts["rw_r_k"], s0)


def _merge_kernel(og_ref, gg_ref, orw_ref, gate_ref, mga_ref, mgb_ref, x_ref,
                  gate1_ref, shift2_ref, scale2_ref,
                  ggn_ref, gpost_ref, gpre_ref, wbg_ref, wbr_ref, wout_ref, wq_ref,
                  x1_ref, h2t_ref, qp_ref):
    gg = gg_ref[...].astype(F32)
    gla_in = og_ref[...].astype(F32) * ggn_ref[...] * (gg * jax.nn.sigmoid(gg))
    br_gla = _mm(gla_in, wbg_ref[...])
    br_rw = _mm(orw_ref[...].astype(F32) * gate_ref[...].astype(F32), wbr_ref[...])
    merged = (jax.nn.sigmoid(mga_ref[...].astype(F32)) * br_gla
              + jax.nn.sigmoid(mgb_ref[...].astype(F32)) * br_rw)
    y = _mm(merged, wout_ref[...])
    x1 = x_ref[...] + gate1_ref[...] * _rms(y, gpost_ref[...])
    x1_ref[...] = x1
    h2 = _rms(x1, gpre_ref[...]) * (1.0 + scale2_ref[...]) + shift2_ref[...]
    h2t_ref[...] = h2.T.astype(BF16)
    qp_ref[...] = jnp.dot(h2.astype(BF16), wq_ref[...], preferred_element_type=F32).astype(BF16)


def _merge(grp, og, p2, orw, gate_rw, x2d, modop, wts):
    tm, M = grp.tm, grp.M
    W = D_MODEL

    def tok(blk):
        return pl.BlockSpec((tm, W), lambda i: (i, blk))

    def full(shape):
        return pl.BlockSpec(shape, lambda i: (0,) * len(shape), pipeline_mode=pl.Buffered(1))

    return pl.pallas_call(
        _merge_kernel,
        grid=(M // tm,),
        in_specs=[tok(0), tok(COL_GG // W), tok(0), tok(0), tok(COL_MG // W), tok(COL_MG // W + 1), tok(0),
                  grp.mod_spec(2, tm), grp.mod_spec(3, tm), grp.mod_spec(4, tm),
                  full((1, W)), full((1, W)), full((1, W)),
                  full((W, W)), full((W, W)), full((W, W)), full((W, 2 * W))],
        out_specs=[tok(0), pl.BlockSpec((W, tm), lambda i: (0, i)), pl.BlockSpec((tm, 2 * W), lambda i: (i, 0))],
        out_shape=[jax.ShapeDtypeStruct((M, W), F32), jax.ShapeDtypeStruct((W, M), BF16),
                   jax.ShapeDtypeStruct((M, 2 * W), BF16)],
        compiler_params=_params(("arbitrary",)),
        name="merge",
    )(og, p2, orw, gate_rw, p2, p2, x2d, modop, modop, modop,
      wts["g_gla_norm"], wts["g_post_mix"], wts["g_pre_ffn"],
      wts["w_b_gla"], wts["w_b_rw"], wts["w_out"], wts["peer_wq"])


def _pair_bits(x):
    bits = pltpu.bitcast(x.astype(BF16).astype(F32), jnp.uint32)
    return bits | (bits >> 16)


def _row_tile(row, rows):
    tile = pltpu.bitcast(jnp.broadcast_to(row, (8, row.shape[1])), BF16)
    return jnp.tile(tile, (rows // 16, 1))


def _batcher_network(n):
    def merge(lo, hi, r):
        step = r * 2
        if step < hi - lo:
            yield from merge(lo, hi, step)
            yield from merge(lo + r, hi, step)
            yield from [(i, i + r) for i in range(lo + r, hi - r, step)]
        else:
            yield (lo, lo + r)

    def sort(lo, hi):
        if hi - lo >= 1:
            mid = lo + (hi - lo) // 2
            yield from sort(lo, mid)
            yield from sort(mid + 1, hi)
            yield from merge(lo, hi, 1)

    return list(sort(0, n - 1))


def _top_values(s, n):
    sub = 8
    depth = s.shape[0] // sub
    cols = [s[i * sub:(i + 1) * sub, :] for i in range(depth)]
    for i, j in _batcher_network(depth):
        cols[i], cols[j] = jnp.maximum(cols[i], cols[j]), jnp.minimum(cols[i], cols[j])
    rows = lax.broadcasted_iota(jnp.int32, (n, s.shape[1]), 0)
    vals = jnp.zeros((n, s.shape[1]), F32)
    for it in range(n):
        m = jnp.max(cols[0], axis=0, keepdims=True)
        vals = jnp.where(rows == it, m, vals)
        hit = cols[0] == m
        for i in range(min(depth - 1, n - 1 - it)):
            cols[i] = jnp.where(hit, cols[i + 1], cols[i])
    return vals


def _count_prefix(vals, pred):
    def row(i):
        return vals[i:i + 1, :]

    c8 = pred(row(7))
    c4 = pred(jnp.where(c8, row(11), row(3)))
    c2 = pred(jnp.where(c8, jnp.where(c4, row(13), row(9)), jnp.where(c4, row(5), row(1))))
    upper = jnp.where(c4, jnp.where(c2, row(14), row(12)), jnp.where(c2, row(10), row(8)))
    lower = jnp.where(c4, jnp.where(c2, row(6), row(4)), jnp.where(c2, row(2), row(0)))
    c1 = pred(jnp.where(c8, upper, lower))
    c16 = pred(row(15))
    return (jnp.where(c8, 8.0, 0.0) + jnp.where(c4, 4.0, 0.0) + jnp.where(c2, 2.0, 0.0)
            + jnp.where(c1, 1.0, 0.0) + jnp.where(c16, 1.0, 0.0))


def _rank_among(s, vals):
    return _count_prefix(vals, lambda x: x > s)


def _candidate_groups(v1, v2):
    n, tm = v1.shape
    split = 4
    r_all = lax.broadcasted_iota(jnp.int32, (n, tm), 0)
    r_half = lax.broadcasted_iota(jnp.int32, (n // 2, tm), 0)
    groups = [v1[0:1, :] + v2]
    for a in range(1, split):
        groups.append(jnp.where(r_half < n // (a + 1), v1[a:a + 1, :] + v2[:n // 2, :], NEG_INF))
    groups.append(jnp.where(r_all >= split, v1 + v2[0:1, :], NEG_INF))
    for b in range(1, n // (split + 1)):
        keep = (r_half >= split) & (r_half < n // (b + 1))
        groups.append(jnp.where(keep, v1[:n // 2, :] + v2[b:b + 1, :], NEG_INF))
    return groups


def _group_max(groups, n):
    full = [g for g in groups if g.shape[0] == n]
    best = full[0]
    for g in full[1:]:
        best = jnp.maximum(best, g)
    best = jnp.maximum(best[:n // 2, :], best[n // 2:, :])
    for g in groups:
        if g.shape[0] != n:
            best = jnp.maximum(best, g)
    return jnp.max(best, axis=0, keepdims=True)


def _select_kernel(qp_ref, keys_ref, r2_ref, e2_ref, al_ref, n1_ref):
    tm = qp_ref.shape[0]
    for h in range(PEER_HEADS):
        q1 = qp_ref[:, (2 * h) * 128:(2 * h + 1) * 128]
        q2 = qp_ref[:, (2 * h + 1) * 128:(2 * h + 2) * 128]
        s1 = _mm_nt(keys_ref[2 * h], q1)
        s2 = _mm_nt(keys_ref[2 * h + 1], q2)
        v1 = _top_values(s1, PEER_TOPK)
        v2 = _top_values(s2, PEER_TOPK)
        rank2 = _rank_among(s2, v2)
        groups = _candidate_groups(v1, v2)
        cmax = v1[0:1, :] + v2[0:1, :]
        z = jnp.zeros((1, tm), F32)
        m = cmax
        for _ in range(PEER_TOPK):
            m = _group_max(groups, PEER_TOPK)
            z = z + jnp.exp(m - cmax)
            groups = [jnp.where(g == m, NEG_INF, g) for g in groups]
        th = m - s1
        n1 = _count_prefix(v2, lambda x: x >= th)
        r2_ref[h] = rank2.astype(BF16)
        e2_ref[h] = jnp.exp(s2 - v2[0:1, :]).astype(BF16)
        al_ref[h] = _pair_bits(jnp.exp(s1 - v1[0:1, :]) / z)
        n1_ref[h] = _pair_bits(n1)


def _peer_select(grp, qp, keys):
    tm, M = grp.tm, grp.M
    rows = pl.BlockSpec((PEER_HEADS, PEER_NKEYS, tm), lambda i: (0, 0, i))
    return pl.pallas_call(
        _select_kernel,
        grid=(M // tm,),
        in_specs=[pl.BlockSpec((tm, 2 * D_MODEL), lambda i: (i, 0)),
                  pl.BlockSpec((2 * PEER_HEADS, PEER_NKEYS, 128), lambda i: (0, 0, 0))],
        out_specs=[rows] * 4,
        out_shape=[jax.ShapeDtypeStruct((PEER_HEADS, PEER_NKEYS, M), BF16)] * 2
        + [jax.ShapeDtypeStruct((PEER_HEADS, PEER_NKEYS, M), jnp.uint32)] * 2,
        compiler_params=_params(("arbitrary",)),
        name="peer_select",
    )(qp, keys)


def _peer_weights(act_ref, y_ref, r2_ref, e2_ref, al_ref, n1_ref, row0):
    tm = act_ref.shape[1]
    half = PEER_NKEYS // 2
    for j in range(act_ref.shape[0] // PEER_NKEYS):
        i1 = slice(row0 + j, row0 + j + 1)
        w = [jnp.zeros((half, tm), BF16) for _ in range(2)]
        for h in range(PEER_HEADS):
            n1 = _row_tile(n1_ref[h, i1, :], half)
            al = _row_tile(al_ref[h, i1, :], half)
            for r in range(2):
                keys = slice(r * half, (r + 1) * half)
                w[r] = w[r] + jnp.where(r2_ref[h, keys, :] < n1, e2_ref[h, keys, :], 0.0) * al
        for r in range(2):
            rows = slice(j * PEER_NKEYS + r * half, j * PEER_NKEYS + (r + 1) * half)
            a = act_ref[rows, :].astype(BF16)
            y_ref[rows, :] = (0.5 * a * (1.0 + lax.erf(a * 0.7071067811865476))) * w[r]


def _peer_kernel(h2t_ref, u_ref, vt_ref, r2_ref, e2_ref, al_ref, n1_ref, x1_ref, gate2_ref, gpost_ref,
                 o_ref, acc_ref, act_0, act_1, y_0, y_1):
    e = pl.program_id(1)
    ts = act_0.shape[0]
    per = ts // PEER_NKEYS
    acts, ys = (act_0, act_1), (y_0, y_1)

    @pl.when(e == 0)
    def _():
        acc_ref[...] = jnp.zeros_like(acc_ref)

    def scores(k):
        acts[k % 2][...] = jnp.dot(u_ref[k], h2t_ref[...], preferred_element_type=F32)

    def mix(k):
        acc_ref[...] += jnp.dot(vt_ref[k], ys[k % 2][...], preferred_element_type=F32)

    scores(0)
    scores(1)
    for k in range(PEER_SUBTILES):
        _peer_weights(acts[k % 2], ys[k % 2], r2_ref, e2_ref, al_ref, n1_ref, k * per)
        if k + 2 < PEER_SUBTILES:
            scores(k + 2)
        if k >= 1:
            mix(k - 1)
    mix(PEER_SUBTILES - 1)

    @pl.when(e == pl.num_programs(1) - 1)
    def _():
        o_ref[...] = x1_ref[...] + gate2_ref[...] * _rms(acc_ref[...].T, gpost_ref[...])


def _peer(grp, h2t, sel, x1, modop, g_post, pu, pvt):
    tm, M = grp.tm, grp.M
    ts = PEER_EXPERT_TILE
    te = ts * PEER_SUBTILES
    r2, e2, al, n1 = sel
    packed = pl.BlockSpec((PEER_HEADS, PEER_NKEYS, tm), lambda i, e: (0, 0, i))
    rows = pl.BlockSpec((PEER_HEADS, te // PEER_NKEYS, tm), lambda i, e: (0, e, i))
    tok = pl.BlockSpec((tm, D_MODEL), lambda i, e: (i, 0))
    return pl.pallas_call(
        _peer_kernel,
        grid=(M // tm, pu.shape[0] // PEER_SUBTILES),
        in_specs=[pl.BlockSpec((D_MODEL, tm), lambda i, e: (0, i)),
                  pl.BlockSpec((PEER_SUBTILES, ts, D_MODEL), lambda i, e: (e, 0, 0)),
                  pl.BlockSpec((PEER_SUBTILES, D_MODEL, ts), lambda i, e: (e, 0, 0)),
                  packed, packed, rows, rows,
                  pl.BlockSpec((tm, D_MODEL), lambda i, e: (i, 0), pipeline_mode=pl.Buffered(1)),
                  grp.mod_spec(5),
                  pl.BlockSpec((1, D_MODEL), lambda i, e: (0, 0))],
        out_specs=tok,
        out_shape=jax.ShapeDtypeStruct((M, D_MODEL), F32),
        scratch_shapes=[pltpu.VMEM((D_MODEL, tm), F32),
                        pltpu.VMEM((ts, tm), F32), pltpu.VMEM((ts, tm), F32),
                        pltpu.VMEM((ts, tm), BF16), pltpu.VMEM((ts, tm), BF16)],
        compiler_params=_params(("arbitrary", "arbitrary")),
        name="peer_dense",
    )(h2t, pu, pvt, r2, e2, al, n1, x1, modop, g_post)


def _transpose_cast_kernel(x_ref, o_ref):
    rows = o_ref.shape[2]
    for t in range(o_ref.shape[0]):
        o_ref[t] = x_ref[t * rows:(t + 1) * rows, :].T.astype(o_ref.dtype)


def _transposed_tiles(x, rows, per_step=2):
    n, d = x.shape
    return pl.pallas_call(
        _transpose_cast_kernel,
        grid=(n // (rows * per_step),),
        in_specs=[pl.BlockSpec((rows * per_step, d), lambda k: (k, 0))],
        out_specs=pl.BlockSpec((per_step, d, rows), lambda k: (k, 0, 0)),
        out_shape=jax.ShapeDtypeStruct((n // rows, d, rows), BF16),
        compiler_params=_params(("arbitrary",)),
        name="transpose_cast",
    )(x)


def _prepare_weights(w_in, w_gla_gate, b_gla_gate, g_gla_norm, mu_shift, rw_w0, rw_w2, rw_a0, rw_a2,
                     rw_g2, rw_k_k, rw_k_a, rw_r_k, rw_gn_w, rw_gn_b, w_b_gla, w_b_rw, w_out,
                     peer_wq, peer_keys, peer_u, peer_v, g_pre_mix, g_post_mix, g_pre_ffn, g_post_ffn):
    W = RW_HEADS * RW_HEAD
    rw = w_in[:, 3088:6416]
    w_proj = jnp.concatenate(
        [rw[:, :3 * W], w_in[:, 6416:8464], w_in[:, 2048:3072], w_in[:, 1024:2048], w_in[:, 0:1024],
         rw[:, 3 * W:], w_in[:, 3072:3088], jnp.zeros((D_MODEL, PROJ_PAD_W - 8464), F32)],
        axis=1).astype(BF16)
    zeros64 = jnp.zeros((64, W), F32)
    return dict(
        w_proj=w_proj,
        wg_pad=jnp.concatenate([w_gla_gate, jnp.zeros((128 - GLA_GATE_RANK, 512), F32)], axis=0),
        bg=b_gla_gate.reshape(1, -1),
        g_gla_norm=g_gla_norm.reshape(1, -1),
        mu_rkv=mu_shift[:3 * W].reshape(1, -1),
        mu_lora=mu_shift[3 * W:].reshape(1, -1),
        rw_w0=rw_w0.reshape(1, -1),
        w2p=jnp.concatenate([rw_w2, jnp.zeros((192, W), F32)], axis=0),
        rw_a0=rw_a0.reshape(1, -1),
        a2p=jnp.concatenate([zeros64, rw_a2, jnp.zeros((128, W), F32)], axis=0),
        g2p=jnp.concatenate([jnp.zeros((128, W), F32), rw_g2], axis=0),
        rw_k_k=rw_k_k.reshape(1, -1),
        rw_k_a=rw_k_a.reshape(1, -1),
        rw_r_k=rw_r_k.reshape(1, -1),
        rw_gn_w=rw_gn_w.reshape(1, -1),
        rw_gn_b=rw_gn_b.reshape(1, -1),
        w_b_gla=w_b_gla.astype(BF16),
        w_b_rw=w_b_rw.astype(BF16),
        w_out=w_out.astype(BF16),
        peer_wq=peer_wq.astype(BF16),
        peer_keys=peer_keys.reshape(2 * PEER_HEADS, PEER_NKEYS, 128).astype(BF16),
        peer_u=peer_u.astype(BF16).reshape(-1, PEER_EXPERT_TILE, D_MODEL),
        peer_vt=_transposed_tiles(peer_v, PEER_EXPERT_TILE),
        g_pre_mix=g_pre_mix.reshape(1, -1),
        g_post_mix=g_post_mix.reshape(1, -1),
        g_pre_ffn=g_pre_ffn.reshape(1, -1),
        g_post_ffn=g_post_ffn.reshape(1, -1),
    )


def _block(x, mod, s_gla0, s_rw0, shift0, wts):
    B, valid, _ = x.shape
    T = -(-valid // 8) * 8
    x = jnp.pad(x, ((0, 0), (0, T - valid), (0, 0)))
    grp = _Group(B, T, valid)
    W = RW_HEADS * RW_HEAD
    x2d = x.reshape(grp.M, D_MODEL)
    modop = grp.seq_operand(mod)
    p2 = _projection(grp, x2d, modop, wts["g_pre_mix"], wts["w_proj"], BF16)
    p3 = p2.reshape(B, T, PROJ_PAD_W)
    og, s_gla = _gla(grp, p3, wts["wg_pad"], wts["bg"], s_gla0)
    orw, gate, s_rw = _rwkv(grp, p3, shift0, wts, s_rw0)
    x1, h2t, qp = _merge(grp, og.reshape(grp.M, -1), p2, orw.reshape(grp.M, W), gate.reshape(grp.M, W),
                        x2d, modop, wts)
    if valid != T:
        grp = _Group(B, valid, valid)
        modop = grp.seq_operand(mod)
        x1 = x1.reshape(B, T, D_MODEL)[:, :valid].reshape(grp.M, D_MODEL)
        h2t = h2t.reshape(D_MODEL, B, T)[:, :, :valid].reshape(D_MODEL, grp.M)
        qp = qp.reshape(B, T, -1)[:, :valid].reshape(grp.M, -1)
    sel = _peer_select(grp, qp, wts["peer_keys"])
    out = _peer(grp, h2t, sel, x1, modop, wts["g_post_ffn"], wts["peer_u"], wts["peer_vt"])
    ends = _Group(B, 1, 1)
    last = _projection(ends, x[:, valid - 1, :], ends.seq_operand(mod), wts["g_pre_mix"], wts["w_proj"], F32)
    shift_new = jnp.concatenate([last[:, :3 * W], last[:, COL_LORA:COL_LORA + 256]], axis=-1)
    return out.reshape(B, grp.T, D_MODEL)[:, :valid], s_gla, s_rw, shift_new


def kernel(x_prompt, x_sample, c_prompt, c_sample, state_gla, state_rwkv, state_shift, w_ada, b_ada, g_pre_mix, g_post_mix, g_pre_ffn, g_post_ffn, w_in, w_gla_gate, b_gla_gate, g_gla_norm, mu_shift, rw_w0, rw_w2, rw_a0, rw_a2, rw_g2, rw_k_k, rw_k_a, rw_r_k, rw_gn_w, rw_gn_b, w_b_gla, w_b_rw, w_out, peer_wq, peer_keys, peer_u, peer_v):
    depth = w_in.shape[0]
    bp = x_prompt.shape[0]
    xp, xs = x_prompt, x_sample
    outs = [[] for _ in range(6)]
    for l in range(depth):
        wts = _prepare_weights(
            w_in[l], w_gla_gate[l], b_gla_gate[l], g_gla_norm[l], mu_shift[l], rw_w0[l], rw_w2[l],
            rw_a0[l], rw_a2[l], rw_g2[l], rw_k_k[l], rw_k_a[l], rw_r_k[l], rw_gn_w[l], rw_gn_b[l],
            w_b_gla[l], w_b_rw[l], w_out[l], peer_wq[l], peer_keys[l], peer_u[l], peer_v[l],
            g_pre_mix[l], g_post_mix[l], g_pre_ffn[l], g_post_ffn[l])
        mod = _modulation(jnp.concatenate([c_prompt, c_sample], axis=0), w_ada[l], b_ada[l])
        zg = jnp.zeros((bp, GLA_HEADS, GLA_DK, GLA_DV), F32)
        zr = jnp.zeros((bp, RW_HEADS, RW_HEAD, RW_HEAD), F32)
        zs = jnp.zeros((bp, RW_SHIFT_W), F32)
        xp, g1, r1, s1 = _block(xp, mod[:bp], zg, zr, zs, wts)
        xs, g2, r2, s2 = _block(xs, mod[bp:], state_gla[l], state_rwkv[l], state_shift[l], wts)
        for acc, val in zip(outs, (g1, r1, s1, g2, r2, s2)):
            acc.append(val)
    stacked = [jnp.stack(o) for o in outs]
    return (xp, xs, *stacked)
```
